```python
import math
import jax, jax.numpy as jnp
from jax import lax
import numpy as np

D_MODEL = 2048
BATCH = 8
SEQ = 8192
DEPTH = 2

N_MIXERS = 2
N_HEADS = 16
HEAD_DIM = D_MODEL // N_HEADS
DILATED_GROUPS = ((128, 1), (512, 4), (2048, 16))
N_GROUPS = len(DILATED_GROUPS)
D_FF = 5632
CONV_W = 3
NORM_EPS = 1e-5
ALIBI_MAX = 8.0
NEG_INF = -1e30

kernel_name = "hybrid_shortconv_dilated_alibi_encoder"


def alibi_slopes(n_heads):
    return jnp.asarray(2.0 ** (-ALIBI_MAX * np.arange(1, n_heads + 1) / n_heads), dtype=jnp.float32)


def rmsnorm(x, g):
    xf = x.astype(jnp.float32)
    y = xf * lax.rsqrt(jnp.mean(xf * xf, axis=-1, keepdims=True) + NORM_EPS)
    return (y * g.astype(jnp.float32)).astype(x.dtype)


def dwconv3(u, w, b):
    up = jnp.pad(u, ((0, 0), (1, 1), (0, 0)))
    return up[:, :-2] * w[0] + up[:, 1:-1] * w[1] + up[:, 2:] * w[2] + b


def short_conv_mixer(h, w_in, conv_w, conv_b, w_out):
    z = h @ w_in
    u, gate_b, gate_c = jnp.split(z, 3, axis=-1)
    y = gate_b * dwconv3(gate_c * u, conv_w, conv_b)
    return y @ w_out


def to_strided(x, d):
    B, S = x.shape[:2]
    rest = x.shape[2:]
    y = x.reshape((B, S // d, d) + rest)
    y = jnp.swapaxes(y, 1, 2)
    return y.reshape((B * d, S // d) + rest)


def from_strided(y, d, B):
    N, L = y.shape[:2]
    rest = y.shape[2:]
    z = y.reshape((B, d, L) + rest)
    z = jnp.swapaxes(z, 1, 2)
    return z.reshape((B, L * d) + rest)


def banded_attention(q, k, v, slopes, dil, half):
    N, L, H, Dh = q.shape
    blk = half
    nb = -(-L // blk)
    Lp = nb * blk
    qb = jnp.pad(q, ((0, 0), (0, Lp - L), (0, 0), (0, 0))).reshape(N, nb, blk, H, Dh)
    kp = jnp.pad(k, ((0, 0), (blk, Lp - L + blk), (0, 0), (0, 0))).reshape(N, nb + 2, blk, H, Dh)
    vp = jnp.pad(v, ((0, 0), (blk, Lp - L + blk), (0, 0), (0, 0))).reshape(N, nb + 2, blk, H, Dh)
    kw = jnp.concatenate([kp[:, :-2], kp[:, 1:-1], kp[:, 2:]], axis=2)
    vw = jnp.concatenate([vp[:, :-2], vp[:, 1:-1], vp[:, 2:]], axis=2)
    qi = jnp.arange(Lp).reshape(nb, blk)
    kj = jnp.arange(nb)[:, None] * blk + jnp.arange(3 * blk)[None, :] - blk
    delta = jnp.abs(kj[:, None, :] - qi[:, :, None])
    valid = (delta <= half) & ((kj >= 0) & (kj < L))[:, None, :]
    s = jnp.einsum('nbqhd,nbkhd->nbhqk', qb, kw).astype(jnp.float32) * (Dh ** -0.5)
    bias = -slopes[None, :, None, None] * (dil * delta).astype(jnp.float32)[:, None, :, :]
    s = jnp.where(valid[:, None, :, :][None], s + bias[None], NEG_INF)
    m = jnp.max(s, axis=-1, keepdims=True)
    p = jnp.exp(s - m)
    den = jnp.sum(p, axis=-1)
    o = jnp.einsum('nbhqk,nbkhd->nbqhd', p, vw.astype(jnp.float32))
    den_t = jnp.swapaxes(den, 2, 3)
    o = o / den_t[..., None]
    lse = jnp.swapaxes(m[..., 0], 2, 3) + jnp.log(den_t)
    return o.reshape(N, Lp, H, Dh)[:, :L], lse.reshape(N, Lp, H)[:, :L]


def dilated_attention_mixer(h, w_qkv, w_out):
    B, S, D = h.shape
    qkv = (h @ w_qkv).reshape(B, S, N_GROUPS, 3, N_HEADS, HEAD_DIM)
    slopes = alibi_slopes(N_HEADS)
    outs, lses = [], []
    for g, (window, dil) in enumerate(DILATED_GROUPS):
        half = (window // 2) // dil
        q = to_strided(qkv[:, :, g, 0], dil)
        k = to_strided(qkv[:, :, g, 1], dil)
        v = to_strided(qkv[:, :, g, 2], dil)
        o, lse = banded_attention(q, k, v, slopes, dil, half)
        outs.append(from_strided(o, dil, B))
        lses.append(from_strided(lse, dil, B))
    wts = jax.nn.softmax(jnp.stack(lses, axis=0), axis=0)
    o = jnp.sum(wts[..., None] * jnp.stack(outs, axis=0), axis=0)
    return o.reshape(B, S, D).astype(h.dtype) @ w_out


def conv_ffn(h, w_up, conv_w, conv_b, w_down):
    u = dwconv3(h @ w_up, conv_w, conv_b)
    a, b = jnp.split(u, 2, axis=-1)
    return (jax.nn.silu(a) * b) @ w_down


def _fwd_setup_inputs(seed: int = 0) -> dict:
    key = jax.random.key(seed)
    ks = jax.random.split(key, 16)
    D, F = D_MODEL, D_FF
    n_a = (DEPTH + N_MIXERS - 1) // N_MIXERS
    n_b = DEPTH // N_MIXERS
    nrm = lambda k, shape, s: jax.random.normal(k, shape, jnp.float32) * s
    return {
        "x": nrm(ks[0], (BATCH, SEQ, D), 1.0),
        "mix_norm_g": 1.0 + nrm(ks[1], (DEPTH, D), 0.02),
        "ffn_norm_g": 1.0 + nrm(ks[2], (DEPTH, D), 0.02),
        "final_norm_g": 1.0 + nrm(ks[3], (D,), 0.02),
        "sc_w_in": nrm(ks[4], (n_a, D, 3 * D), D ** -0.5),
        "sc_conv_w": nrm(ks[5], (n_a, CONV_W, D), CONV_W ** -0.5),
        "sc_conv_b": nrm(ks[6], (n_a, D), 0.01),
        "sc_w_out": nrm(ks[7], (n_a, D, D), D ** -0.5),
        "attn_w_qkv": nrm(ks[8], (n_b, D, N_GROUPS * 3 * N_HEADS * HEAD_DIM), D ** -0.5),
        "attn_w_out": nrm(ks[9], (n_b, N_HEADS * HEAD_DIM, D), (N_HEADS * HEAD_DIM) ** -0.5),
        "ffn_w_up": nrm(ks[10], (DEPTH, D, 2 * F), D ** -0.5),
        "ffn_conv_w": nrm(ks[11], (DEPTH, CONV_W, 2 * F), CONV_W ** -0.5),
        "ffn_conv_b": nrm(ks[12], (DEPTH, 2 * F), 0.01),
        "ffn_w_down": nrm(ks[13], (DEPTH, F, D), F ** -0.5),
    }


def _fwd_reference(x, mix_norm_g, ffn_norm_g, final_norm_g, sc_w_in, sc_conv_w, sc_conv_b, sc_w_out,
              attn_w_qkv, attn_w_out, ffn_w_up, ffn_conv_w, ffn_conv_b, ffn_w_down):
    for i in range(DEPTH):
        h = rmsnorm(x, mix_norm_g[i])
        j = i // N_MIXERS
        if i % N_MIXERS == 0:
            x = x + short_conv_mixer(h, sc_w_in[j], sc_conv_w[j], sc_conv_b[j], sc_w_out[j])
        else:
            x = x + dilated_attention_mixer(h, attn_w_qkv[j], attn_w_out[j])
        h = rmsnorm(x, ffn_norm_g[i])
        x = x + conv_ffn(h, ffn_w_up[i], ffn_conv_w[i], ffn_conv_b[i], ffn_w_down[i])
    return rmsnorm(x, final_norm_g)


import jax as _jax
import jax.numpy as _jnp

TWIN_FORMAT = 'train_step'
FWD_PARAMS = ['x', 'mix_norm_g', 'ffn_norm_g', 'final_norm_g', 'sc_w_in', 'sc_conv_w', 'sc_conv_b', 'sc_w_out', 'attn_w_qkv', 'attn_w_out', 'ffn_w_up', 'ffn_conv_w', 'ffn_conv_b', 'ffn_w_down']
TWIN_WEIGHTS = ['mix_norm_g', 'ffn_norm_g', 'final_norm_g', 'sc_w_in', 'sc_conv_w', 'sc_conv_b', 'sc_w_out', 'attn_w_qkv', 'attn_w_out', 'ffn_w_up', 'ffn_conv_w', 'ffn_conv_b', 'ffn_w_down']
TWIN_DIFF_INPUT = 'x'
TWIN_INPUTS = ['x', 'mix_norm_g', 'ffn_norm_g', 'final_norm_g', 'sc_w_in', 'sc_conv_w', 'sc_conv_b', 'sc_w_out', 'attn_w_qkv', 'attn_w_out', 'ffn_w_up', 'ffn_conv_w', 'ffn_conv_b', 'ffn_w_down', 'loss_target', 'm_mix_norm_g', 'm_ffn_norm_g', 'm_final_norm_g', 'm_sc_w_in', 'm_sc_conv_w', 'm_sc_conv_b', 'm_sc_w_out', 'm_attn_w_qkv', 'm_attn_w_out', 'm_ffn_w_up', 'm_ffn_conv_w', 'm_ffn_conv_b', 'm_ffn_w_down', 'v_mix_norm_g', 'v_ffn_norm_g', 'v_final_norm_g', 'v_sc_w_in', 'v_sc_conv_w', 'v_sc_conv_b', 'v_sc_w_out', 'v_attn_w_qkv', 'v_attn_w_out', 'v_ffn_w_up', 'v_ffn_conv_w', 'v_ffn_conv_b', 'v_ffn_w_down']
TWIN_OUTPUTS = ['loss', 'grad_x', 'grad_mix_norm_g', 'grad_ffn_norm_g', 'grad_final_norm_g', 'grad_sc_w_in', 'grad_sc_conv_w', 'grad_sc_conv_b', 'grad_sc_w_out', 'grad_attn_w_qkv', 'grad_attn_w_out', 'grad_ffn_w_up', 'grad_ffn_conv_w', 'grad_ffn_conv_b', 'grad_ffn_w_down', 'delta_mix_norm_g', 'delta_ffn_norm_g', 'delta_final_norm_g', 'delta_sc_w_in', 'delta_sc_conv_w', 'delta_sc_conv_b', 'delta_sc_w_out', 'delta_attn_w_qkv', 'delta_attn_w_out', 'delta_ffn_w_up', 'delta_ffn_conv_w', 'delta_ffn_conv_b', 'delta_ffn_w_down', 'new_m_mix_norm_g', 'new_m_ffn_norm_g', 'new_m_final_norm_g', 'new_m_sc_w_in', 'new_m_sc_conv_w', 'new_m_sc_conv_b', 'new_m_sc_w_out', 'new_m_attn_w_qkv', 'new_m_attn_w_out', 'new_m_ffn_w_up', 'new_m_ffn_conv_w', 'new_m_ffn_conv_b', 'new_m_ffn_w_down', 'new_v_mix_norm_g', 'new_v_ffn_norm_g', 'new_v_final_norm_g', 'new_v_sc_w_in', 'new_v_sc_conv_w', 'new_v_sc_conv_b', 'new_v_sc_w_out', 'new_v_attn_w_qkv', 'new_v_attn_w_out', 'new_v_ffn_w_up', 'new_v_ffn_conv_w', 'new_v_ffn_conv_b', 'new_v_ffn_w_down']
TWIN_LEAF_KINDS = {'loss': 'loss', 'grad_x': 'grad_x', 'grad_mix_norm_g': 'grad_w', 'grad_ffn_norm_g': 'grad_w', 'grad_final_norm_g': 'grad_w', 'grad_sc_w_in': 'grad_w', 'grad_sc_conv_w': 'grad_w', 'grad_sc_conv_b': 'grad_w', 'grad_sc_w_out': 'grad_w', 'grad_attn_w_qkv': 'grad_w', 'grad_attn_w_out': 'grad_w', 'grad_ffn_w_up': 'grad_w', 'grad_ffn_conv_w': 'grad_w', 'grad_ffn_conv_b': 'grad_w', 'grad_ffn_w_down': 'grad_w', 'delta_mix_norm_g': 'delta_w', 'delta_ffn_norm_g': 'delta_w', 'delta_final_norm_g': 'delta_w', 'delta_sc_w_in': 'delta_w', 'delta_sc_conv_w': 'delta_w', 'delta_sc_conv_b': 'delta_w', 'delta_sc_w_out': 'delta_w', 'delta_attn_w_qkv': 'delta_w', 'delta_attn_w_out': 'delta_w', 'delta_ffn_w_up': 'delta_w', 'delta_ffn_conv_w': 'delta_w', 'delta_ffn_conv_b': 'delta_w', 'delta_ffn_w_down': 'delta_w', 'new_m_mix_norm_g': 'new_m', 'new_m_ffn_norm_g': 'new_m', 'new_m_final_norm_g': 'new_m', 'new_m_sc_w_in': 'new_m', 'new_m_sc_conv_w': 'new_m', 'new_m_sc_conv_b': 'new_m', 'new_m_sc_w_out': 'new_m', 'new_m_attn_w_qkv': 'new_m', 'new_m_attn_w_out': 'new_m', 'new_m_ffn_w_up': 'new_m', 'new_m_ffn_conv_w': 'new_m', 'new_m_ffn_conv_b': 'new_m', 'new_m_ffn_w_down': 'new_m', 'new_v_mix_norm_g': 'new_v', 'new_v_ffn_norm_g': 'new_v', 'new_v_final_norm_g': 'new_v', 'new_v_sc_w_in': 'new_v', 'new_v_sc_conv_w': 'new_v', 'new_v_sc_conv_b': 'new_v', 'new_v_sc_w_out': 'new_v', 'new_v_attn_w_qkv': 'new_v', 'new_v_attn_w_out': 'new_v', 'new_v_ffn_w_up': 'new_v', 'new_v_ffn_conv_w': 'new_v', 'new_v_ffn_conv_b': 'new_v', 'new_v_ffn_w_down': 'new_v'}


def _forward(args):
    return _fwd_reference(*[args[k] for k in FWD_PARAMS])


def _output_shape():
    def fwd():
        inp = _fwd_setup_inputs(0)
        return _fwd_reference(*[inp[k] for k in FWD_PARAMS])
    out = _jax.eval_shape(fwd)
    return out.shape, out.dtype

N_MICROBATCH = 1
ADAM_LR = 0.001
ADAM_B1 = 0.9
ADAM_B2 = 0.999
ADAM_EPS = 1e-08
ADAM_WD = 0.01
ADAM_STEP = 10
PER_EXAMPLE_BATCH_AXIS = {'x': 0, 'loss_target': 0}
SHARED_INPUTS = []
_WEIGHT_DTYPES = {'mix_norm_g': _jnp.float32, 'ffn_norm_g': _jnp.float32, 'final_norm_g': _jnp.float32, 'sc_w_in': _jnp.float32, 'sc_conv_w': _jnp.float32, 'sc_conv_b': _jnp.float32, 'sc_w_out': _jnp.float32, 'attn_w_qkv': _jnp.float32, 'attn_w_out': _jnp.float32, 'ffn_w_up': _jnp.float32, 'ffn_conv_w': _jnp.float32, 'ffn_conv_b': _jnp.float32, 'ffn_w_down': _jnp.float32}
MOMENT_SCALE = {'mix_norm_g': 1.370660e-01, 'ffn_norm_g': 7.253793e-02, 'final_norm_g': 3.190343e+01, 'sc_w_in': 1.038445e-01, 'sc_conv_w': 1.060383e-01, 'sc_conv_b': 1.046584e-01, 'sc_w_out': 1.039846e-01, 'attn_w_qkv': 1.103683e-02, 'attn_w_out': 2.148307e-02, 'ffn_w_up': 3.076684e-02, 'ffn_conv_w': 3.098473e-02, 'ffn_conv_b': 3.027496e-02, 'ffn_w_down': 5.030291e-02}


def _to_microbatches(a, axis):
    t = _jnp.moveaxis(a, axis, 0)
    t = t.reshape((N_MICROBATCH, t.shape[0] // N_MICROBATCH) + t.shape[1:])
    return _jnp.moveaxis(t, 1, axis + 1)


def setup_inputs(seed: int = 0) -> dict:
    inp = _fwd_setup_inputs(seed)
    key = _jax.random.fold_in(_jax.random.key(seed), 7919)
    shape, _ = _output_shape()
    out = dict(inp)
    out["loss_target"] = _jax.random.normal(_jax.random.fold_in(key, 0), shape, _jnp.float32)
    for i, name in enumerate(TWIN_WEIGHTS):
        w = inp[name].astype(_jnp.float32)
        if MOMENT_SCALE is None:
            s = _jnp.sqrt(_jnp.mean(_jnp.square(w)) + 1e-30)
        else:
            s = MOMENT_SCALE[name]
        km, kv = _jax.random.split(_jax.random.fold_in(key, i + 1))
        out[name] = w
        out["m_" + name] = s * _jax.random.normal(km, w.shape, _jnp.float32)
        out["v_" + name] = (s * s) * _jax.random.uniform(kv, w.shape, _jnp.float32, 0.5, 1.5)
    if N_MICROBATCH > 1:
        for name, axis in PER_EXAMPLE_BATCH_AXIS.items():
            out[name] = _to_microbatches(out[name], axis)
    return {'x': out['x'], 'mix_norm_g': out['mix_norm_g'], 'ffn_norm_g': out['ffn_norm_g'], 'final_norm_g': out['final_norm_g'], 'sc_w_in': out['sc_w_in'], 'sc_conv_w': out['sc_conv_w'], 'sc_conv_b': out['sc_conv_b'], 'sc_w_out': out['sc_w_out'], 'attn_w_qkv': out['attn_w_qkv'], 'attn_w_out': out['attn_w_out'], 'ffn_w_up': out['ffn_w_up'], 'ffn_conv_w': out['ffn_conv_w'], 'ffn_conv_b': out['ffn_conv_b'], 'ffn_w_down': out['ffn_w_down'], 'loss_target': out['loss_target'], 'm_mix_norm_g': out['m_mix_norm_g'], 'm_ffn_norm_g': out['m_ffn_norm_g'], 'm_final_norm_g': out['m_final_norm_g'], 'm_sc_w_in': out['m_sc_w_in'], 'm_sc_conv_w': out['m_sc_conv_w'], 'm_sc_conv_b': out['m_sc_conv_b'], 'm_sc_w_out': out['m_sc_w_out'], 'm_attn_w_qkv': out['m_attn_w_qkv'], 'm_attn_w_out': out['m_attn_w_out'], 'm_ffn_w_up': out['m_ffn_w_up'], 'm_ffn_conv_w': out['m_ffn_conv_w'], 'm_ffn_conv_b': out['m_ffn_conv_b'], 'm_ffn_w_down': out['m_ffn_w_down'], 'v_mix_norm_g': out['v_mix_norm_g'], 'v_ffn_norm_g': out['v_ffn_norm_g'], 'v_final_norm_g': out['v_final_norm_g'], 'v_sc_w_in': out['v_sc_w_in'], 'v_sc_conv_w': out['v_sc_conv_w'], 'v_sc_conv_b': out['v_sc_conv_b'], 'v_sc_w_out': out['v_sc_w_out'], 'v_attn_w_qkv': out['v_attn_w_qkv'], 'v_attn_w_out': out['v_attn_w_out'], 'v_ffn_w_up': out['v_ffn_w_up'], 'v_ffn_conv_w': out['v_ffn_conv_w'], 'v_ffn_conv_b': out['v_ffn_conv_b'], 'v_ffn_w_down': out['v_ffn_w_down']}


def _loss(weights, diff, rest, loss_target):
    with _jax.named_scope("forward"):
        args = {**rest, TWIN_DIFF_INPUT: diff, **{k: w.astype(_WEIGHT_DTYPES[k]) for k, w in weights.items()}}
        y = _forward(args)
    with _jax.named_scope("loss_head"):
        err = _jnp.square(y.astype(_jnp.float32) - loss_target)
        return 0.5 * _jnp.sum(_jnp.mean(err, axis=-1)) if err.ndim else 0.5 * err


def _adamw(w, g, m, v):
    m = ADAM_B1 * m + (1.0 - ADAM_B1) * g
    v = ADAM_B2 * v + (1.0 - ADAM_B2) * _jnp.square(g)
    m_hat = m / (1.0 - ADAM_B1 ** ADAM_STEP)
    v_hat = v / (1.0 - ADAM_B2 ** ADAM_STEP)
    delta = -ADAM_LR * (m_hat / (_jnp.sqrt(v_hat) + ADAM_EPS) + ADAM_WD * w)
    return delta, m, v


def reference(x, mix_norm_g, ffn_norm_g, final_norm_g, sc_w_in, sc_conv_w, sc_conv_b, sc_w_out, attn_w_qkv, attn_w_out, ffn_w_up, ffn_conv_w, ffn_conv_b, ffn_w_down, loss_target, m_mix_norm_g, m_ffn_norm_g, m_final_norm_g, m_sc_w_in, m_sc_conv_w, m_sc_conv_b, m_sc_w_out, m_attn_w_qkv, m_attn_w_out, m_ffn_w_up, m_ffn_conv_w, m_ffn_conv_b, m_ffn_w_down, v_mix_norm_g, v_ffn_norm_g, v_final_norm_g, v_sc_w_in, v_sc_conv_w, v_sc_conv_b, v_sc_w_out, v_attn_w_qkv, v_attn_w_out, v_ffn_w_up, v_ffn_conv_w, v_ffn_conv_b, v_ffn_w_down):
    given = dict(x=x, mix_norm_g=mix_norm_g, ffn_norm_g=ffn_norm_g, final_norm_g=final_norm_g, sc_w_in=sc_w_in, sc_conv_w=sc_conv_w, sc_conv_b=sc_conv_b, sc_w_out=sc_w_out, attn_w_qkv=attn_w_qkv, attn_w_out=attn_w_out, ffn_w_up=ffn_w_up, ffn_conv_w=ffn_conv_w, ffn_conv_b=ffn_conv_b, ffn_w_down=ffn_w_down, loss_target=loss_target, m_mix_norm_g=m_mix_norm_g, m_ffn_norm_g=m_ffn_norm_g, m_final_norm_g=m_final_norm_g, m_sc_w_in=m_sc_w_in, m_sc_conv_w=m_sc_conv_w, m_sc_conv_b=m_sc_conv_b, m_sc_w_out=m_sc_w_out, m_attn_w_qkv=m_attn_w_qkv, m_attn_w_out=m_attn_w_out, m_ffn_w_up=m_ffn_w_up, m_ffn_conv_w=m_ffn_conv_w, m_ffn_conv_b=m_ffn_conv_b, m_ffn_w_down=m_ffn_w_down, v_mix_norm_g=v_mix_norm_g, v_ffn_norm_g=v_ffn_norm_g, v_final_norm_g=v_final_norm_g, v_sc_w_in=v_sc_w_in, v_sc_conv_w=v_sc_conv_w, v_sc_conv_b=v_sc_conv_b, v_sc_w_out=v_sc_w_out, v_attn_w_qkv=v_attn_w_qkv, v_attn_w_out=v_attn_w_out, v_ffn_w_up=v_ffn_w_up, v_ffn_conv_w=v_ffn_conv_w, v_ffn_conv_b=v_ffn_conv_b, v_ffn_w_down=v_ffn_w_down)
    weights = {n: given[n] for n in TWIN_WEIGHTS}
    shared = {n: given[n] for n in SHARED_INPUTS}
    per_example = {n: given[n] for n in ['x']}
    grad_fn = _jax.value_and_grad(_loss, argnums=(0, 1))

    def one_microbatch(ex, loss_target):
        ex = dict(ex)
        diff = ex.pop(TWIN_DIFF_INPUT)
        return grad_fn(weights, diff, {**shared, **ex}, loss_target)

    if N_MICROBATCH == 1:
        loss, (grad_w, grad_x) = one_microbatch(per_example, given["loss_target"])
    else:
        def body(carry, xs):
            loss_sum, grad_sum = carry
            l_k, (gw_k, gx_k) = one_microbatch(xs[0], xs[1])
            with _jax.named_scope("update"):
                return (loss_sum + l_k, _jax.tree.map(_jnp.add, grad_sum, gw_k)), gx_k

        init = (_jnp.zeros((), _jnp.float32), _jax.tree.map(_jnp.zeros_like, weights))
        (loss, grad_w), grad_x = _jax.lax.scan(body, init, (per_example, given["loss_target"]))
    with _jax.named_scope("update"):
        delta_w, new_m, new_v = {}, {}, {}
        for n in TWIN_WEIGHTS:
            delta_w[n], new_m[n], new_v[n] = _adamw(weights[n], grad_w[n], given["m_" + n], given["v_" + n])
    return (loss, grad_x, *[grad_w[n] for n in TWIN_WEIGHTS], *[delta_w[n] for n in TWIN_WEIGHTS],
            *[new_m[n] for n in TWIN_WEIGHTS], *[new_v[n] for n in TWIN_WEIGHTS])
```

```python
import functools
import math

import numpy as np
import jax
import jax.numpy as jnp
from jax import lax
from jax.experimental import pallas as pl
from jax.experimental.pallas import tpu as pltpu

F32 = jnp.float32
BF16 = jnp.bfloat16
MESH = pl.DeviceIdType.MESH

HEAD_DIM = 128
DILATED_GROUPS = ((128, 1), (512, 4), (2048, 16))
DILATIONS = tuple(d for _, d in DILATED_GROUPS)
BAND = (DILATED_GROUPS[0][0] // 2) // DILATED_GROUPS[0][1]
assert all((w // 2) // d == BAND for w, d in DILATED_GROUPS)
NORM_EPS = 1e-5
ALIBI_MAX = 8.0
NEG_INF = -1e30
ADAM_LR, ADAM_B1, ADAM_B2, ADAM_EPS, ADAM_WD, ADAM_STEP = 0.001, 0.9, 0.999, 1e-08, 0.01, 10

N_DEV = 8
LANES = 128
HALO = 16
VMEM_LIMIT = 56 * 1024 * 1024


def _cp(*sem):
    return pltpu.CompilerParams(dimension_semantics=sem, vmem_limit_bytes=VMEM_LIMIT)


def _tile(n, pref, mult):
    t = (min(n, pref) // mult) * mult
    while t >= mult:
        if n % t == 0:
            return t
        t -= mult
    return n


def _sds(shape, dtype):
    return jax.ShapeDtypeStruct(shape, dtype)


NN = (((1,), (0,)), ((), ()))
NT = (((1,), (1,)), ((), ()))
TN = (((0,), (0,)), ((), ()))


def _mm(name, operands, in_specs, out_sds, o_spec, grid, dims, acc_shape, has_res=False, aliases=None):
    nk = grid[2]
    n_in = len(operands)

    def body(*refs):
        a_ref, b_ref = refs[0], refs[1]
        r_ref = refs[2] if has_res else None
        o_ref = refs[n_in]
        acc = refs[n_in + 1]
        part = lax.dot_general(a_ref[...], b_ref[...], dims, preferred_element_type=F32)

        def finish(total):
            if has_res:
                total = total + r_ref[...]
            o_ref[...] = total.astype(o_ref.dtype)

        if nk == 1:
            finish(part)
        else:
            k = pl.program_id(2)

            @pl.when(k == 0)
            def _():
                acc[...] = part

            @pl.when(k > 0)
            def _():
                acc[...] += part

            @pl.when(k == nk - 1)
            def _():
                finish(acc[...])

    return pl.pallas_call(
        body, grid=grid, in_specs=in_specs, out_specs=o_spec, out_shape=out_sds,
        scratch_shapes=[pltpu.VMEM(acc_shape if nk > 1 else (8, LANES), F32)],
        input_output_aliases=aliases or {}, name=name,
        compiler_params=_cp("parallel", "parallel", "arbitrary"),
    )(*operands)


def _mm_nn_cols(name, a, wg, off, n_tiles, tn, out_dtype):
    M, K = a.shape
    per = wg.shape[2] // tn
    tm = _tile(M, 1024, 16)
    return _mm(name, (a, wg),
               [pl.BlockSpec((tm, K), lambda i, j, k: (i, 0)),
                pl.BlockSpec((None, K, tn), lambda i, j, k: ((off + j) // per, 0, (off + j) % per))],
               _sds((M, n_tiles * tn), out_dtype), pl.BlockSpec((tm, tn), lambda i, j, k: (i, j)),
               (M // tm, n_tiles, 1), NN, (tm, tn))


def _mm_nn(name, a, w, res, out_dtype):
    M, K = a.shape
    N = w.shape[1]
    tm, tn, tk = _tile(M, 1024, 16), _tile(N, 1024, LANES), _tile(K, 512, LANES)
    ops = [a, w]
    specs = [pl.BlockSpec((tm, tk), lambda i, j, k: (i, k)), pl.BlockSpec((tk, tn), lambda i, j, k: (k, j))]
    if res is not None:
        ops.append(res)
        specs.append(pl.BlockSpec((tm, tn), lambda i, j, k: (i, j)))
    return _mm(name, tuple(ops), specs, _sds((M, N), out_dtype), pl.BlockSpec((tm, tn), lambda i, j, k: (i, j)),
               (M // tm, N // tn, K // tk), NN, (tm, tn), has_res=res is not None)


def _mm_nt_cols(name, dy, wg, off, n_tiles, tkc, out_dtype):
    M = dy.shape[0]
    K = wg.shape[1]
    per = wg.shape[2] // tkc
    tm, tn = _tile(M, 1024, 16), _tile(K, 1024, LANES)
    return _mm(name, (dy, wg),
               [pl.BlockSpec((tm, tkc), lambda i, j, k: (i, k)),
                pl.BlockSpec((None, tn, tkc), lambda i, j, k: ((off + k) // per, j, (off + k) % per))],
               _sds((M, K), out_dtype), pl.BlockSpec((tm, tn), lambda i, j, k: (i, j)),
               (M // tm, K // tn, n_tiles), NT, (tm, tn))


def _mm_nt(name, dy, w, out_dtype, tn_pref=1024):
    M, N = dy.shape
    Kw = w.shape[0]
    tm, tn, tk = _tile(M, 1024, 16), _tile(Kw, tn_pref, LANES), _tile(N, 2048, LANES)
    return _mm(name, (dy, w),
               [pl.BlockSpec((tm, tk), lambda i, j, k: (i, k)), pl.BlockSpec((tn, tk), lambda i, j, k: (j, k))],
               _sds((M, Kw), out_dtype), pl.BlockSpec((tm, tn), lambda i, j, k: (i, j)),
               (M // tm, Kw // tn, N // tk), NT, (tm, tn))


def _mm_tn_cols(name, a, dy, ns, cw, off, n_tiles, tn, prev=None):
    M, K = a.shape
    per = cw // tn
    tkr, tk = _tile(K, 1024, LANES), _tile(M, 1024, 16)
    ops = [a, dy]
    specs = [pl.BlockSpec((tk, tkr), lambda i, j, k: (k, i)), pl.BlockSpec((tk, tn), lambda i, j, k: (k, j))]
    aliases = None
    if prev is not None:
        ops.append(prev)
        specs.append(pl.BlockSpec(memory_space=pl.ANY))
        aliases = {2: 0}
    return _mm(name, tuple(ops), specs, _sds((ns, K, cw), BF16),
               pl.BlockSpec((None, tkr, tn), lambda i, j, k: ((off + j) // per, i, (off + j) % per)),
               (K // tkr, n_tiles, M // tk), TN, (tkr, tn), aliases=aliases)


def _mm_tn(name, a, dy, tkr_pref=1024):
    M, Kw = a.shape
    N = dy.shape[1]
    tkr, tn, tk = _tile(Kw, tkr_pref, LANES), _tile(N, 1024, LANES), _tile(M, 1024, 16)
    return _mm(name, (a, dy),
               [pl.BlockSpec((tk, tkr), lambda i, j, k: (k, i)), pl.BlockSpec((tk, tn), lambda i, j, k: (k, j))],
               _sds((Kw, N), BF16), pl.BlockSpec((tkr, tn), lambda i, j, k: (i, j)),
               (Kw // tkr, N // tn, M // tk), TN, (tkr, tn))


def _rmsnorm_fwd(name, x, g):
    T, D = x.shape
    tr = _tile(T, 512, 16)

    def body(x_ref, g_ref, h_ref):
        xf = x_ref[...]
        r = lax.rsqrt(jnp.mean(xf * xf, axis=-1, keepdims=True) + NORM_EPS)
        h_ref[...] = (xf * r * g_ref[...]).astype(h_ref.dtype)

    return pl.pallas_call(
        body, grid=(T // tr,),
        in_specs=[pl.BlockSpec((tr, D), lambda i: (i, 0)), pl.BlockSpec((1, D), lambda i: (0, 0))],
        out_specs=pl.BlockSpec((tr, D), lambda i: (i, 0)), out_shape=_sds((T, D), BF16),
        name=name, compiler_params=_cp("parallel"))(x, g.reshape(1, D))


def _rmsnorm_bwd(name, x, g, dhs, dres):
    T, D = x.shape
    tr = _tile(T, 256, 16)
    n_dh = len(dhs)

    def body(*refs):
        x_ref, g_ref = refs[0], refs[1]
        dh_refs = refs[2:2 + n_dh]
        dres_ref = refs[2 + n_dh]
        dx_ref, dxb_ref, dg_ref = refs[3 + n_dh], refs[4 + n_dh], refs[5 + n_dh]
        xf = x_ref[...]
        r = lax.rsqrt(jnp.mean(xf * xf, axis=-1, keepdims=True) + NORM_EPS)
        xhat = xf * r
        dh = dh_refs[0][...].astype(F32)
        for q in dh_refs[1:]:
            dh = dh + q[...].astype(F32)
        dy = dh * g_ref[...]
        c = jnp.mean(dy * xhat, axis=-1, keepdims=True)
        dx = dres_ref[...] + r * (dy - xhat * c)
        dx_ref[...] = dx
        dxb_ref[...] = dx.astype(BF16)

        @pl.when(pl.program_id(0) == 0)
        def _():
            dg_ref[...] = jnp.zeros_like(dg_ref)

        dg_ref[...] += jnp.sum(dh * xhat, axis=0, keepdims=True)

    row = pl.BlockSpec((tr, D), lambda i: (i, 0))
    vec = pl.BlockSpec((1, D), lambda i: (0, 0))
    dx, dx_b, dg = pl.pallas_call(
        body, grid=(T // tr,), in_specs=[row, vec] + [row] * n_dh + [row],
        out_specs=[row, row, vec], out_shape=[_sds((T, D), F32), _sds((T, D), BF16), _sds((1, D), F32)],
        name=name, compiler_params=_cp("arbitrary"))(x, g.reshape(1, D), *dhs, dres)
    return dx, dx_b, dg[0]


def _final_loss(name, x, g, tgt):
    T, D = x.shape
    tr = _tile(T, 256, 16)

    def body(x_ref, g_ref, t_ref, dx_ref, dxb_ref, dg_ref, loss_ref):
        xf = x_ref[...]
        r = lax.rsqrt(jnp.mean(xf * xf, axis=-1, keepdims=True) + NORM_EPS)
        xhat = xf * r
        err = xhat * g_ref[...] - t_ref[...]
        dy = err * (1.0 / D)
        dxh = dy * g_ref[...]
        c = jnp.mean(dxh * xhat, axis=-1, keepdims=True)
        dx = r * (dxh - xhat * c)
        dx_ref[...] = dx
        dxb_ref[...] = dx.astype(BF16)

        @pl.when(pl.program_id(0) == 0)
        def _():
            dg_ref[...] = jnp.zeros_like(dg_ref)
            loss_ref[...] = jnp.zeros_like(loss_ref)

        dg_ref[...] += jnp.sum(dy * xhat, axis=0, keepdims=True)
        loss_ref[...] += 0.5 * jnp.sum(jnp.mean(err * err, axis=-1, keepdims=True), axis=0, keepdims=True)

    row = pl.BlockSpec((tr, D), lambda i: (i, 0))
    vec = pl.BlockSpec((1, D), lambda i: (0, 0))
    dx, dx_b, dg, loss = pl.pallas_call(
        body, grid=(T // tr,), in_specs=[row, vec, row],
        out_specs=[row, row, vec, pl.BlockSpec((1, 1), lambda i: (0, 0))],
        out_shape=[_sds((T, D), F32), _sds((T, D), BF16), _sds((1, D), F32), _sds((1, 1), F32)],
        name=name, compiler_params=_cp("arbitrary"))(x, g.reshape(1, D), tgt)
    return loss[0, 0], dx, dx_b, dg[0]


def _halo_specs(tr, tc, n_rows, col):
    rb = tr // HALO
    last = n_rows // HALO - 1
    return [pl.BlockSpec((tr, tc), lambda *g: (g[-1], col(*g))),
            pl.BlockSpec((HALO, tc), lambda *g: (jnp.maximum(g[-1] * rb - 1, 0), col(*g))),
            pl.BlockSpec((HALO, tc), lambda *g: (jnp.minimum((g[-1] + 1) * rb, last), col(*g)))]


def _ext(cur_ref, prev_ref, next_ref, i, n_i):
    p = prev_ref[...].astype(F32) * (i > 0).astype(F32)
    n = next_ref[...].astype(F32) * (i < n_i - 1).astype(F32)
    return jnp.concatenate([p, cur_ref[...].astype(F32), n], axis=0)


def _shift_dn(x):
    return pltpu.roll(x, 1, axis=0)


def _shift_up(x):
    return pltpu.roll(x, x.shape[0] - 1, axis=0)


def _conv(x, w_ref, b_ref):
    return w_ref[0:1, :] * _shift_dn(x) + w_ref[1:2, :] * x + w_ref[2:3, :] * _shift_up(x) + b_ref[...]


def _conv_t(g, w_ref):
    return w_ref[0:1, :] * _shift_up(g) + w_ref[1:2, :] * g + w_ref[2:3, :] * _shift_dn(g)


def _mid(x, tr):
    return x[HALO:HALO + tr, :]


def _conv_wgrad(acc_ref, g, x, tr, first):
    gm = _mid(g, tr)

    @pl.when(first)
    def _():
        acc_ref[...] = jnp.zeros_like(acc_ref)

    acc_ref[0:1, :] += jnp.sum(gm * _mid(_shift_dn(x), tr), axis=0, keepdims=True)
    acc_ref[1:2, :] += jnp.sum(gm * _mid(x, tr), axis=0, keepdims=True)
    acc_ref[2:3, :] += jnp.sum(gm * _mid(_shift_up(x), tr), axis=0, keepdims=True)
    acc_ref[3:4, :] += jnp.sum(gm, axis=0, keepdims=True)


def _sigmoid(a):
    return 1.0 / (1.0 + jnp.exp(-a))


def _ffn_gate_fwd(name, up, cw, cb):
    T, F2 = up.shape
    F = F2 // 2
    tc, tr = _tile(F, 512, LANES), _tile(T, 512, HALO)
    nF, n_i = F // tc, T // tr

    def body(ac, ap, an, bc, bp, bn, wa, wb, ba, bb, o_ref):
        i = pl.program_id(1)
        ua = _mid(_conv(_ext(ac, ap, an, i, n_i), wa, ba), tr)
        ub = _mid(_conv(_ext(bc, bp, bn, i, n_i), wb, bb), tr)
        o_ref[...] = (ua * _sigmoid(ua) * ub).astype(o_ref.dtype)

    wspec = lambda o: pl.BlockSpec((3, tc), lambda j, i: (0, j + o))
    bspec = lambda o: pl.BlockSpec((1, tc), lambda j, i: (0, j + o))
    return pl.pallas_call(
        body, grid=(nF, n_i),
        in_specs=_halo_specs(tr, tc, T, lambda j, i: j) + _halo_specs(tr, tc, T, lambda j, i: j + nF)
        + [wspec(0), wspec(nF), bspec(0), bspec(nF)],
        out_specs=pl.BlockSpec((tr, tc), lambda j, i: (i, j)), out_shape=_sds((T, F), BF16),
        name=name, compiler_params=_cp("parallel", "parallel"))(up, up, up, up, up, up, cw, cw, cb, cb)


def _ffn_gate_bwd(name, up, dact, cw, cb):
    T, F2 = up.shape
    F = F2 // 2
    tc, tr = _tile(F, 512, LANES), _tile(T, 256, HALO)
    nF, n_i = F // tc, T // tr

    def body(ac, ap, an, bc, bp, bn, dc, dp, dn, wa, wb, ba, bb, o_ref, wg_ref):
        j, i = pl.program_id(0), pl.program_id(1)
        xa = _ext(ac, ap, an, i, n_i)
        xb = _ext(bc, bp, bn, i, n_i)
        da = _ext(dc, dp, dn, i, n_i)
        ua = _conv(xa, wa, ba)
        sig = _sigmoid(ua)

        @pl.when(j < nF)
        def _():
            g = da * _conv(xb, wb, bb) * (sig * (1.0 + ua * (1.0 - sig)))
            o_ref[...] = _mid(_conv_t(g, wa), tr).astype(o_ref.dtype)
            _conv_wgrad(wg_ref, g, xa, tr, i == 0)

        @pl.when(j >= nF)
        def _():
            g = da * (ua * sig)
            o_ref[...] = _mid(_conv_t(g, wb), tr).astype(o_ref.dtype)
            _conv_wgrad(wg_ref, g, xb, tr, i == 0)

    wspec = lambda o: pl.BlockSpec((3, tc), lambda j, i: (0, j % nF + o))
    bspec = lambda o: pl.BlockSpec((1, tc), lambda j, i: (0, j % nF + o))
    return pl.pallas_call(
        body, grid=(2 * nF, n_i),
        in_specs=_halo_specs(tr, tc, T, lambda j, i: j % nF) + _halo_specs(tr, tc, T, lambda j, i: j % nF + nF)
        + _halo_specs(tr, tc, T, lambda j, i: j % nF) + [wspec(0), wspec(nF), bspec(0), bspec(nF)],
        out_specs=[pl.BlockSpec((tr, tc), lambda j, i: (i, j)), pl.BlockSpec((8, tc), lambda j, i: (0, j))],
        out_shape=[_sds((T, F2), BF16), _sds((8, F2), F32)],
        name=name, compiler_params=_cp("parallel", "arbitrary"),
    )(up, up, up, up, up, up, dact, dact, dact, cw, cw, cb, cb)


def _sc_gate_fwd(name, z, cw, cb):
    T, D3 = z.shape
    D = D3 // 3
    tc, tr = _tile(D, 512, LANES), _tile(T, 512, HALO)
    nD, n_i = D // tc, T // tr

    def body(uc, up_, un, gb, cc, cp, cn, w, b, o_ref):
        i = pl.program_id(1)
        cu = _ext(cc, cp, cn, i, n_i) * _ext(uc, up_, un, i, n_i)
        o_ref[...] = (gb[...].astype(F32) * _mid(_conv(cu, w, b), tr)).astype(o_ref.dtype)

    return pl.pallas_call(
        body, grid=(nD, n_i),
        in_specs=_halo_specs(tr, tc, T, lambda j, i: j) + [pl.BlockSpec((tr, tc), lambda j, i: (i, j + nD))]
        + _halo_specs(tr, tc, T, lambda j, i: j + 2 * nD)
        + [pl.BlockSpec((3, tc), lambda j, i: (0, j)), pl.BlockSpec((1, tc), lambda j, i: (0, j))],
        out_specs=pl.BlockSpec((tr, tc), lambda j, i: (i, j)), out_shape=_sds((T, D), BF16),
        name=name, compiler_params=_cp("parallel", "parallel"))(z, z, z, z, z, z, z, cw, cb)


def _sc_gate_bwd(name, z, dy, cw, cb):
    T, D3 = z.shape
    D = D3 // 3
    tc, tr = _tile(D, 512, LANES), _tile(T, 256, HALO)
    nD, n_i = D // tc, T // tr

    def body(uc, up_, un, bc, bp, bn, cc, cp, cn, yc, yp, yn, w, b, o_ref, wg_ref):
        kind, i = pl.program_id(0), pl.program_id(2)
        u = _ext(uc, up_, un, i, n_i)
        gc = _ext(cc, cp, cn, i, n_i)
        cu = gc * u
        g = _ext(yc, yp, yn, i, n_i) * _ext(bc, bp, bn, i, n_i)

        @pl.when(kind == 0)
        def _():
            o_ref[...] = (_mid(_conv_t(g, w), tr) * _mid(gc, tr)).astype(o_ref.dtype)
            _conv_wgrad(wg_ref, g, cu, tr, i == 0)

        @pl.when(kind == 1)
        def _():
            o_ref[...] = (yc[...].astype(F32) * _mid(_conv(cu, w, b), tr)).astype(o_ref.dtype)

            @pl.when(i == 0)
            def _():
                wg_ref[...] = jnp.zeros_like(wg_ref)

        @pl.when(kind == 2)
        def _():
            o_ref[...] = (_mid(_conv_t(g, w), tr) * _mid(u, tr)).astype(o_ref.dtype)

            @pl.when(i == 0)
            def _():
                wg_ref[...] = jnp.zeros_like(wg_ref)

    hs = lambda o: _halo_specs(tr, tc, T, lambda k, j, i: j + o)
    out, wg = pl.pallas_call(
        body, grid=(3, nD, n_i),
        in_specs=hs(0) + hs(nD) + hs(2 * nD) + hs(0)
        + [pl.BlockSpec((3, tc), lambda k, j, i: (0, j)), pl.BlockSpec((1, tc), lambda k, j, i: (0, j))],
        out_specs=[pl.BlockSpec((tr, tc), lambda k, j, i: (i, k * nD + j)),
                   pl.BlockSpec((None, 8, tc), lambda k, j, i: (k, 0, j))],
        out_shape=[_sds((T, D3), BF16), _sds((3, 8, D), F32)],
        name=name, compiler_params=_cp("parallel", "parallel", "arbitrary"),
    )(z, z, z, z, z, z, z, z, z, dy, dy, dy, cw, cb)
    return out, wg[0]


def _slopes(n_heads):
    return jnp.asarray(2.0 ** (-ALIBI_MAX * np.arange(1, n_heads + 1) / n_heads), dtype=F32)


def _rows(ref, r, n, d):
    if d == 1:
        return ref[...]
    return ref[pl.ds(r, n, stride=d), :]


def _band(n, L, d, slope, q_rows, k_rows, q0, k0):
    qi = lax.broadcasted_iota(jnp.int32, (q_rows, k_rows), 0) + q0
    kj = lax.broadcasted_iota(jnp.int32, (q_rows, k_rows), 1) + k0
    dist = jnp.abs(kj - qi)
    base = n * LANES
    ok = (dist <= BAND) & (kj + base >= 0) & (kj + base < L) & (qi + base >= 0) & (qi + base < L)
    return ok, -slope * (dist * d).astype(F32)


def _win_specs(d, H, col, nblk_half):
    return [pl.BlockSpec((d, BAND, LANES), lambda h, n: (0, jnp.maximum(2 * n - 1, 0), col * H + h)),
            pl.BlockSpec((d, LANES, LANES), lambda h, n: (0, n, col * H + h)),
            pl.BlockSpec((d, BAND, LANES), lambda h, n: (0, jnp.minimum(2 * n + 2, nblk_half - 1), col * H + h))]


def _nat_specs(d, n_half):
    return [pl.BlockSpec((BAND * d, LANES), lambda h, n: (jnp.maximum(2 * n - 1, 0), h)),
            pl.BlockSpec((LANES * d, LANES), lambda h, n: (n, h)),
            pl.BlockSpec((BAND * d, LANES), lambda h, n: (jnp.minimum(2 * n + 2, n_half - 1), h))]


def _attn_fwd(name, qkv, d, H):
    T = qkv.shape[0]
    D = H * HEAD_DIM
    L = T // d
    nb = L // LANES
    scale = HEAD_DIM ** -0.5
    q3 = qkv.reshape(d, L, 3 * D)

    def body(s_ref, q_ref, kp, kc, kn, vp, vc, vn, o_ref, l_ref):
        h, n = pl.program_id(0), pl.program_id(1)
        ok, bias = _band(n, L, d, s_ref[h], LANES, 2 * LANES, 0, -BAND)

        def per_r(r, carry):
            k = jnp.concatenate([kp[r], kc[r], kn[r]], axis=0)
            v = jnp.concatenate([vp[r], vc[r], vn[r]], axis=0)
            s = lax.dot_general(q_ref[r], k, NT, preferred_element_type=F32) * scale
            s = jnp.where(ok, s + bias, NEG_INF)
            m = jnp.max(s, axis=1, keepdims=True)
            p = jnp.exp(s - m)
            den = jnp.sum(p, axis=1, keepdims=True)
            o = lax.dot_general(p.astype(BF16), v, NN, preferred_element_type=F32) / den
            lse = jnp.broadcast_to(m + jnp.log(den), (LANES, LANES))
            if d == 1:
                o_ref[...] = o
                l_ref[...] = lse
            else:
                o_ref[pl.ds(r, LANES, stride=d), :] = o
                l_ref[pl.ds(r, LANES, stride=d), :] = lse
            return carry

        lax.fori_loop(0, d, per_r, 0)

    out = pl.BlockSpec((LANES * d, LANES), lambda h, n: (n, h))
    return pl.pallas_call(
        body, grid=(H, nb),
        in_specs=[pl.BlockSpec(memory_space=pltpu.SMEM), pl.BlockSpec((d, LANES, LANES), lambda h, n: (0, n, h))]
        + _win_specs(d, H, 1, L // BAND) + _win_specs(d, H, 2, L // BAND),
        out_specs=[out, out], out_shape=[_sds((T, D), F32), _sds((T, D), F32)],
        name=name, compiler_params=_cp("parallel", "parallel"))(_slopes(H), q3, q3, q3, q3, q3, q3, q3)


def _attn_combine(name, outs, lses):
    T, D = outs[0].shape
    tr, tc = _tile(T, 512, 16), _tile(D, 512, LANES)

    def body(o0, o1, o2, l0, l1, l2, ob_ref, of_ref, l_ref):
        a0, a1, a2 = l0[...], l1[...], l2[...]
        m = jnp.maximum(jnp.maximum(a0, a1), a2)
        e0, e1, e2 = jnp.exp(a0 - m), jnp.exp(a1 - m), jnp.exp(a2 - m)
        z = e0 + e1 + e2
        o = (e0 * o0[...] + e1 * o1[...] + e2 * o2[...]) / z
        of_ref[...] = o
        ob_ref[...] = o.astype(BF16)
        l_ref[...] = m + jnp.log(z)

    blk = pl.BlockSpec((tr, tc), lambda i, j: (i, j))
    return pl.pallas_call(
        body, grid=(T // tr, D // tc), in_specs=[blk] * 6, out_specs=[blk] * 3,
        out_shape=[_sds((T, D), BF16), _sds((T, D), F32), _sds((T, D), F32)],
        name=name, compiler_params=_cp("parallel", "parallel"))(*outs, *lses)


def _attn_delta(name, do, o):
    T, D = do.shape
    tr = _tile(T, 1024, 16)

    def body(a, b, o_ref):
        o_ref[...] = jnp.broadcast_to(jnp.sum(a[...] * b[...], axis=1, keepdims=True), o_ref.shape)

    blk = pl.BlockSpec((tr, LANES), lambda i, j: (i, j))
    return pl.pallas_call(body, grid=(T // tr, D // LANES), in_specs=[blk, blk], out_specs=blk,
                          out_shape=_sds((T, D), F32), name=name, compiler_params=_cp("parallel", "parallel"))(do, o)


def _attn_bwd(name, qkv, do, lse, delta, d, H):
    T = qkv.shape[0]
    D = H * HEAD_DIM
    L = T // d
    nb = L // LANES
    scale = HEAD_DIM ** -0.5
    q3 = qkv.reshape(d, L, 3 * D)

    def body(s_ref, qp, qc, qn, kp, kc, kn, vp, vc, vn, gp, gc, gn, lp, lc, ln, dp_, dc_, dn_, dq_ref, dk_ref, dv_ref):
        h, n = pl.program_id(0), pl.program_id(1)
        ok_q, bias_q = _band(n, L, d, s_ref[h], LANES, 2 * LANES, 0, -BAND)
        ok_k, bias_k = _band(n, L, d, s_ref[h], 2 * LANES, LANES, -BAND, 0)

        def win(p_ref, c_val, n_ref, r):
            return jnp.concatenate([_rows(p_ref, r, BAND, d), c_val, _rows(n_ref, r, BAND, d)], axis=0)

        def per_r(r, carry):
            q_c, k_c, v_c = qc[r], kc[r], vc[r]
            g_c, l_c, t_c = _rows(gc, r, LANES, d), _rows(lc, r, LANES, d), _rows(dc_, r, LANES, d)
            k_w = jnp.concatenate([kp[r], k_c, kn[r]], axis=0)
            v_w = jnp.concatenate([vp[r], v_c, vn[r]], axis=0)
            s = lax.dot_general(q_c, k_w, NT, preferred_element_type=F32) * scale + bias_q
            p = jnp.where(ok_q, jnp.exp(s - l_c[:, 0:1]), 0.0)
            dp = lax.dot_general(g_c.astype(BF16), v_w, NT, preferred_element_type=F32)
            ds = jnp.where(ok_q, p * (dp - t_c[:, 0:1]), 0.0)
            dq_ref[r] = (lax.dot_general(ds.astype(BF16), k_w, NN, preferred_element_type=F32) * scale).astype(BF16)
            q_w = jnp.concatenate([qp[r], q_c, qn[r]], axis=0)
            g_w = win(gp, g_c, gn, r).astype(BF16)
            l_w = win(lp, l_c, ln, r)
            t_w = win(dp_, t_c, dn_, r)
            s2 = lax.dot_general(q_w, k_c, NT, preferred_element_type=F32) * scale + bias_k
            p2 = jnp.where(ok_k, jnp.exp(s2 - l_w[:, 0:1]), 0.0)
            dv_ref[r] = lax.dot_general(p2.astype(BF16), g_w, TN, preferred_element_type=F32).astype(BF16)
            dp2 = lax.dot_general(g_w, v_c, NT, preferred_element_type=F32)
            ds2 = jnp.where(ok_k, p2 * (dp2 - t_w[:, 0:1]), 0.0)
            dk_ref[r] = (lax.dot_general(ds2.astype(BF16), q_w, TN, preferred_element_type=F32) * scale).astype(BF16)
            return carry

        lax.fori_loop(0, d, per_r, 0)

    nh = L // BAND
    out = pl.BlockSpec((d, LANES, LANES), lambda h, n: (0, n, h))
    dq, dk, dv = pl.pallas_call(
        body, grid=(H, nb),
        in_specs=[pl.BlockSpec(memory_space=pltpu.SMEM)]
        + _win_specs(d, H, 0, nh) + _win_specs(d, H, 1, nh) + _win_specs(d, H, 2, nh)
        + _nat_specs(d, nh) + _nat_specs(d, nh) + _nat_specs(d, nh),
        out_specs=[out, out, out], out_shape=[_sds((d, L, D), BF16)] * 3,
        name=name, compiler_params=_cp("parallel", "parallel"),
    )(_slopes(H), *([q3] * 9), *([do] * 3), *([lse] * 3), *([delta] * 3))
    return dq.reshape(T, D), dk.reshape(T, D), dv.reshape(T, D)


def _to_group_order(a, d):
    if d == 1:
        return a
    T, C = a.shape
    return a.reshape(T // d, d, C).swapaxes(0, 1).reshape(T, C)


def _from_group_order(a, d):
    if d == 1:
        return a
    T, C = a.shape
    return a.reshape(d, T // d, C).swapaxes(0, 1).reshape(T, C)


def _fwd_bwd(x, tgt, W, S):
    T, D = x.shape
    H = D // HEAD_DIM
    ns = W["in"].shape[0]
    cw_in, cw_qkv, cw_up = W["in"].shape[2], W["qkv"].shape[2], W["up"][0].shape[2]
    tq = math.gcd(3 * D, cw_qkv)
    ntg = 3 * D // tq

    def ffn_fwd(l, xin):
        hf = _rmsnorm_fwd(f"ffn_norm{l}", xin, S["ffn_g"][l])
        up = _mm_nn_cols(f"ffn_up{l}", hf, W["up"][l], 0, ns, cw_up, BF16)
        act = _ffn_gate_fwd(f"ffn_gate{l}", up, S["ffn_cw"][l], S["ffn_cb"][l][None])
        return hf, up, act, _mm_nn(f"ffn_down{l}", act, W["dn"][l], xin, F32)

    def ffn_bwd(l, xin, hf, up, act, dxo, dxo_b):
        dact = _mm_nt(f"ffn_down_dx{l}", dxo_b, W["dn"][l], BF16, tn_pref=512)
        dw_dn = _mm_tn(f"ffn_down_dw{l}", act, dxo_b, tkr_pref=1408)
        dup, cg = _ffn_gate_bwd(f"ffn_gate_bwd{l}", up, dact, S["ffn_cw"][l], S["ffn_cb"][l][None])
        dhf = _mm_nt_cols(f"ffn_up_dx{l}", dup, W["up"][l], 0, ns, cw_up, F32)
        dw_up = _mm_tn_cols(f"ffn_up_dw{l}", hf, dup, ns, cw_up, 0, ns, cw_up)
        dx, dx_b, dg = _rmsnorm_bwd(f"ffn_norm_bwd{l}", xin, S["ffn_g"][l], [dhf], dxo)
        return dx, dx_b, dg, cg, dw_up, dw_dn

    h0 = _rmsnorm_fwd("mix_norm0", x, S["mix_g"][0])
    z = _mm_nn_cols("sc_in", h0, W["in"], 0, ns, cw_in, BF16)
    y = _sc_gate_fwd("sc_gate", z, S["sc_cw"], S["sc_cb"][None])
    x1 = _mm_nn("sc_out", y, W["sco"], x, F32)
    hf0, up0, act0, x2 = ffn_fwd(0, x1)
    h1 = _rmsnorm_fwd("mix_norm1", x2, S["mix_g"][1])
    hd, qkv, outs, lses = [], [], [], []
    for g, d in enumerate(DILATIONS):
        hd.append(_to_group_order(h1, d))
        qkv.append(_mm_nn_cols(f"attn_qkv{g}", hd[g], W["qkv"], g * ntg, ntg, tq, BF16))
        o_g, l_g = _attn_fwd(f"attn_fwd{g}", qkv[g], d, H)
        outs.append(o_g)
        lses.append(l_g)
    o_b, o_f, lse = _attn_combine("attn_combine", outs, lses)
    x3 = _mm_nn("attn_out", o_b, W["ao"], x2, F32)
    hf1, up1, act1, x4 = ffn_fwd(1, x3)
    loss, dx4, dx4_b, dg_fin = _final_loss("final_loss", x4, S["fin_g"], tgt)

    dx3, dx3_b, dg_f1, cg1, dw_up1, dw_dn1 = ffn_bwd(1, x3, hf1, up1, act1, dx4, dx4_b)
    do = _mm_nt("attn_out_dx", dx3_b, W["ao"], F32)
    dw_ao = _mm_tn("attn_out_dw", o_b, dx3_b)
    delta = _attn_delta("attn_delta", do, o_f)
    dhs, dw_qkv = [], None
    for g, d in enumerate(DILATIONS):
        dq, dk, dv = _attn_bwd(f"attn_bwd{g}", qkv[g], do, lse, delta, d, H)
        dqkv = jnp.concatenate([dq, dk, dv], axis=1)
        dhs.append(_from_group_order(_mm_nt_cols(f"attn_qkv_dx{g}", dqkv, W["qkv"], g * ntg, ntg, tq, F32), d))
        dw_qkv = _mm_tn_cols(f"attn_qkv_dw{g}", hd[g], dqkv, ns, cw_qkv, g * ntg, ntg, tq, prev=dw_qkv)
    dx2, dx2_b, dg_m1 = _rmsnorm_bwd("mix_norm_bwd1", x2, S["mix_g"][1], dhs, dx3)
    dx1, dx1_b, dg_f0, cg0, dw_up0, dw_dn0 = ffn_bwd(0, x1, hf0, up0, act0, dx2, dx2_b)
    dy = _mm_nt("sc_out_dx", dx1_b, W["sco"], BF16)
    dw_sco = _mm_tn("sc_out_dw", y, dx1_b)
    dz, cg_sc = _sc_gate_bwd("sc_gate_bwd", z, dy, S["sc_cw"], S["sc_cb"][None])
    dh0 = _mm_nt_cols("sc_in_dx", dz, W["in"], 0, ns, cw_in, F32)
    dw_in = _mm_tn_cols("sc_in_dw", h0, dz, ns, cw_in, 0, ns, cw_in)
    dx0, _, dg_m0 = _rmsnorm_bwd("mix_norm_bwd0", x, S["mix_g"][0], [dh0], dx1)

    dW = {"in": dw_in, "sco": dw_sco, "qkv": dw_qkv, "ao": dw_ao, "up": [dw_up0, dw_up1], "dn": [dw_dn0, dw_dn1]}
    dS = {"mix_g": jnp.stack([dg_m0, dg_m1]), "ffn_g": jnp.stack([dg_f0, dg_f1]), "fin_g": dg_fin,
          "sc_cw": cg_sc[0:3], "sc_cb": cg_sc[3], "ffn_cw": jnp.stack([cg0[0:3], cg1[0:3]]),
          "ffn_cb": jnp.stack([cg0[3], cg1[3]])}
    return loss, dx0, dW, dS


def _place():
    x, y, c = lax.axis_index("x"), lax.axis_index("y"), lax.axis_index("c")
    return x, y, c, [(1 - x, y), (x, 1 - y), (1 - x, 1 - y)]


def _any_specs(n):
    return [pl.BlockSpec(memory_space=pl.ANY)] * n


def _all_gather(name, arrs):
    n = len(arrs)

    def body(*refs):
        ins, outs = refs[:n], refs[n:2 * n]
        send_sems, recv_sems, local_sems = refs[2 * n:]
        x, y, c, chips = _place()
        me, sibling = 4 * x + 2 * y + c, (x, y, 1 - c)

        def copy(a, k, blk, to, src=None):
            dst = outs[a].at[blk]
            return pltpu.make_async_remote_copy(src_ref=dst if src is None else src, dst_ref=dst,
                                                send_sem=send_sems.at[a, k], recv_sem=recv_sems.at[a, k],
                                                device_id=to, device_id_type=MESH)

        mine = [pltpu.make_async_copy(ins[a], outs[a].at[me], local_sems.at[a]) for a in range(n)]
        first = []
        for a in range(n):
            first.append(copy(a, 0, me, sibling, src=ins[a]))
            first += [copy(a, 1 + j, me, (*chip, c), src=ins[a]) for j, chip in enumerate(chips)]
        for cp in mine + first:
            cp.start()
        passed = []
        for j, (px, py) in enumerate(chips):
            for a in range(n):
                blk = 4 * px + 2 * py + c
                copy(a, 1 + j, blk, sibling).wait_recv()
                passed.append(copy(a, 4 + j, blk, sibling))
                passed[-1].start()
        for a in range(n):
            copy(a, 0, 4 * x + 2 * y + 1 - c, sibling).wait_recv()
            for j, (px, py) in enumerate(chips):
                copy(a, 4 + j, 4 * px + 2 * py + 1 - c, sibling).wait_recv()
        for cp in first + passed:
            cp.wait_send()
        for cp in mine:
            cp.wait()

    return pl.pallas_call(
        body, in_specs=_any_specs(n), out_specs=_any_specs(n),
        out_shape=[_sds((N_DEV,) + a.shape, a.dtype) for a in arrs],
        scratch_shapes=[pltpu.SemaphoreType.DMA((n, 7)), pltpu.SemaphoreType.DMA((n, 7)), pltpu.SemaphoreType.DMA((n,))],
        name=name)(*arrs)


def _rs_pair_exchange(name, slabs):
    n = len(slabs)

    def body(*refs):
        ins, outs = refs[:n], refs[n:2 * n]
        send_sems, recv_sems = refs[2 * n:]
        x, y, c, _ = _place()
        copies = [pltpu.make_async_remote_copy(src_ref=ins[a].at[2 * k + 1 - c], dst_ref=outs[a].at[k],
                                               send_sem=send_sems.at[a, k], recv_sem=recv_sems.at[a, k],
                                               device_id=(x, y, 1 - c), device_id_type=MESH)
                  for a in range(n) for k in range(4)]
        for cp in copies:
            cp.start()
        for cp in copies:
            cp.wait()

    return pl.pallas_call(
        body, in_specs=_any_specs(n), out_specs=_any_specs(n),
        out_shape=[_sds((4,) + a.shape[1:], a.dtype) for a in slabs],
        scratch_shapes=[pltpu.SemaphoreType.DMA((n, 4)), pltpu.SemaphoreType.DMA((n, 4))], name=name)(*slabs)


def _rs_pair_add(name, slabs, got, c_arr):
    _, R, C = slabs.shape
    tr, tc = _tile(R, 512, 16), _tile(C, 1536, LANES)

    def body(c_ref, a_ref, b_ref, o_ref):
        o_ref[...] = (a_ref[...].astype(F32) + b_ref[...].astype(F32)).astype(o_ref.dtype)

    return pl.pallas_call(
        body,
        grid_spec=pltpu.PrefetchScalarGridSpec(
            num_scalar_prefetch=1, grid=(4, R // tr, C // tc),
            in_specs=[pl.BlockSpec((None, tr, tc), lambda k, i, j, c_ref: (2 * k + c_ref[0], i, j)),
                      pl.BlockSpec((None, tr, tc), lambda k, i, j, c_ref: (k, i, j))],
            out_specs=pl.BlockSpec((None, tr, tc), lambda k, i, j, c_ref: (k, i, j))),
        out_shape=_sds((4, R, C), BF16), name=name,
        compiler_params=_cp("parallel", "parallel", "parallel"))(c_arr, slabs, got)


def _rs_chip_exchange(name, parts):
    n = len(parts)

    def body(*refs):
        ins, outs = refs[:n], refs[n:2 * n]
        send_sems, recv_sems, local_sems = refs[2 * n:]
        x, y, c, chips = _place()
        my_chip = 2 * x + y
        mine = [pltpu.make_async_copy(ins[a].at[my_chip], outs[a].at[my_chip], local_sems.at[a]) for a in range(n)]
        sends = [pltpu.make_async_remote_copy(src_ref=ins[a].at[2 * px + py], dst_ref=outs[a].at[my_chip],
                                              send_sem=send_sems.at[a, j], recv_sem=recv_sems.at[a, j],
                                              device_id=(px, py, c), device_id_type=MESH)
                 for a in range(n) for j, (px, py) in enumerate(chips)]
        for cp in mine + sends:
            cp.start()
        for a in range(n):
            for j, (px, py) in enumerate(chips):
                pltpu.make_async_remote_copy(src_ref=ins[a].at[my_chip], dst_ref=outs[a].at[2 * px + py],
                                             send_sem=send_sems.at[a, j], recv_sem=recv_sems.at[a, j],
                                             device_id=(px, py, c), device_id_type=MESH).wait_recv()
        for cp in sends:
            cp.wait_send()
        for cp in mine:
            cp.wait()

    return pl.pallas_call(
        body, in_specs=_any_specs(n), out_specs=_any_specs(n),
        out_shape=[_sds(a.shape, a.dtype) for a in parts],
        scratch_shapes=[pltpu.SemaphoreType.DMA((n, 3)), pltpu.SemaphoreType.DMA((n, 3)), pltpu.SemaphoreType.DMA((n,))],
        name=name)(*parts)


def _sum_slots(name, a):
    _, rows, _ = a.shape

    def body(a_ref, o_ref):
        s = a_ref[0]
        for k in range(1, N_DEV):
            s = s + a_ref[k]
        o_ref[...] = s

    return pl.pallas_call(body, out_shape=_sds((rows, LANES), F32), name=name)(a)


def _cast_bf16(name, w3, l):
    _, R, C = w3.shape
    tr, tc = _tile(R, 512, 16), _tile(C, 1536, LANES)

    def body(w_ref, o_ref):
        o_ref[...] = w_ref[...].astype(BF16)

    return pl.pallas_call(
        body, grid=(R // tr, C // tc), in_specs=[pl.BlockSpec((None, tr, tc), lambda i, j: (l, i, j))],
        out_specs=pl.BlockSpec((tr, tc), lambda i, j: (i, j)), out_shape=_sds((R, C), BF16),
        name=name, compiler_params=_cp("parallel", "parallel"))(w3)


def _adamw(name, g_slots, w3, m3, v3, l, prev):
    n_slots, R, C = g_slots.shape
    tr, tc = _tile(R, 256, 8), _tile(C, 1536, LANES)
    c1, c2 = 1.0 - ADAM_B1 ** ADAM_STEP, 1.0 - ADAM_B2 ** ADAM_STEP

    def body(g_ref, w_ref, m_ref, v_ref, *rest):
        og, od, om, ov = rest[-4:]
        g = g_ref[0].astype(F32)
        for k in range(1, n_slots):
            g = g + g_ref[k].astype(F32)
        m = ADAM_B1 * m_ref[...] + (1.0 - ADAM_B1) * g
        v = ADAM_B2 * v_ref[...] + (1.0 - ADAM_B2) * (g * g)
        og[...] = g
        om[...] = m
        ov[...] = v
        od[...] = -ADAM_LR * ((m / c1) / (jnp.sqrt(v / c2) + ADAM_EPS) + ADAM_WD * w_ref[...])

    lay = pl.BlockSpec((None, tr, tc), lambda i, j: (l, i, j))
    ops = [g_slots, w3, m3, v3]
    specs = [pl.BlockSpec((n_slots, tr, tc), lambda i, j: (0, i, j)), lay, lay, lay]
    aliases = {}
    if prev is not None:
        ops += list(prev)
        specs += _any_specs(4)
        aliases = {4 + k: k for k in range(4)}
    return pl.pallas_call(
        body, grid=(R // tr, C // tc), in_specs=specs, out_specs=[lay] * 4,
        out_shape=[_sds(w3.shape, F32)] * 4, input_output_aliases=aliases,
        name=name, compiler_params=_cp("parallel", "parallel"))(*ops)


def _pack(parts):
    flat = jnp.concatenate([p.reshape(-1) for p in parts])
    pad = (-flat.shape[0]) % LANES
    return jnp.pad(flat, (0, pad)).reshape(-1, LANES)


def _unpack(packed, shapes):
    flat = packed.reshape(-1)
    out, at = [], 0
    for s in shapes:
        n = int(np.prod(s))
        out.append(flat[at:at + n].reshape(s))
        at += n
    return out


def kernel(x, mix_norm_g, ffn_norm_g, final_norm_g, sc_w_in, sc_conv_w, sc_conv_b, sc_w_out, attn_w_qkv, attn_w_out, ffn_w_up, ffn_conv_w, ffn_conv_b, ffn_w_down, loss_target, m_mix_norm_g, m_ffn_norm_g, m_final_norm_g, m_sc_w_in, m_sc_conv_w, m_sc_conv_b, m_sc_w_out, m_attn_w_qkv, m_attn_w_out, m_ffn_w_up, m_ffn_conv_w, m_ffn_conv_b, m_ffn_w_down, v_mix_norm_g, v_ffn_norm_g, v_final_norm_g, v_sc_w_in, v_sc_conv_w, v_sc_conv_b, v_sc_w_out, v_attn_w_qkv, v_attn_w_out, v_ffn_w_up, v_ffn_conv_w, v_ffn_conv_b, v_ffn_w_down):
    _, T, D = x.shape
    n_layers = ffn_w_up.shape[0]
    me = 4 * lax.axis_index("x") + 2 * lax.axis_index("y") + lax.axis_index("c")
    c_arr = lax.axis_index("c").astype(jnp.int32).reshape(1)

    big = [("in", sc_w_in, 0), ("sco", sc_w_out, 0), ("qkv", attn_w_qkv, 0), ("ao", attn_w_out, 0)]
    big += [(f"up{l}", ffn_w_up, l) for l in range(n_layers)] + [(f"dn{l}", ffn_w_down, l) for l in range(n_layers)]
    gathered = _all_gather("gather_weights", [_cast_bf16(f"cast_{nm}", w, l) for nm, w, l in big])
    G = {nm: g for (nm, _, _), g in zip(big, gathered)}
    rows_major = lambda g: g.reshape(g.shape[0] * g.shape[1], g.shape[2])
    W = {"in": G["in"], "sco": rows_major(G["sco"]), "qkv": G["qkv"], "ao": rows_major(G["ao"]),
         "up": [G[f"up{l}"] for l in range(n_layers)], "dn": [rows_major(G[f"dn{l}"]) for l in range(n_layers)]}

    cw_shapes = [sc_conv_w.shape, ffn_conv_w.shape]
    cw_all = _all_gather("gather_conv_w", [_pack([sc_conv_w, ffn_conv_w])])[0]
    sc_cw_all, ffn_cw_all = zip(*[_unpack(cw_all[s], cw_shapes) for s in range(N_DEV)])
    sc_cw = jnp.concatenate(sc_cw_all, axis=-1)[0]
    ffn_cw = jnp.concatenate(ffn_cw_all, axis=-1)
    S = {"mix_g": mix_norm_g, "ffn_g": ffn_norm_g, "fin_g": final_norm_g, "sc_cw": sc_cw, "sc_cb": sc_conv_b[0],
         "ffn_cw": ffn_cw, "ffn_cb": ffn_conv_b}

    loss_part, grad_x, dW, dS = _fwd_bwd(x[0], loss_target[0], W, S)

    small_names = ["mix_g", "ffn_g", "fin_g", "sc_cb", "ffn_cb", "sc_cw", "ffn_cw"]
    small_parts = [dS[k] for k in small_names] + [loss_part.reshape(1)]
    small_sum = _sum_slots("sum_small", _all_gather("gather_small", [_pack(small_parts)])[0])
    g_mix, g_ffn, g_fin, g_scb, g_fcb, g_scw, g_fcw, loss = _unpack(small_sum, [p.shape for p in small_parts])
    g_scw = lax.dynamic_slice_in_dim(g_scw, me * sc_conv_w.shape[-1], sc_conv_w.shape[-1], axis=-1)[None]
    g_fcw = lax.dynamic_slice_in_dim(g_fcw, me * ffn_conv_w.shape[-1], ffn_conv_w.shape[-1], axis=-1)
    g_scb = g_scb[None]
    small_g = [g_mix, g_ffn, g_fin, g_scw, g_scb, g_fcw, g_fcb]
    small_w = [mix_norm_g, ffn_norm_g, final_norm_g, sc_conv_w, sc_conv_b, ffn_conv_w, ffn_conv_b]
    small_m = [m_mix_norm_g, m_ffn_norm_g, m_final_norm_g, m_sc_conv_w, m_sc_conv_b, m_ffn_conv_w, m_ffn_conv_b]
    small_v = [v_mix_norm_g, v_ffn_norm_g, v_final_norm_g, v_sc_conv_w, v_sc_conv_b, v_ffn_conv_w, v_ffn_conv_b]
    small_out = _adamw("adamw_small", _pack(small_g)[None], _pack(small_w)[None], _pack(small_m)[None],
                       _pack(small_v)[None], 0, None)
    small_shapes = [w.shape for w in small_w]
    sg, sd, sm, sv = [_unpack(o[0], small_shapes) for o in small_out]

    slab = lambda g, like: g if g.ndim == 3 else g.reshape(N_DEV, like.shape[1], like.shape[2])
    slabs = [slab(dW["in"], sc_w_in), slab(dW["sco"], sc_w_out), slab(dW["qkv"], attn_w_qkv), slab(dW["ao"], attn_w_out)]
    slabs += [slab(dW["up"][l], ffn_w_up) for l in range(n_layers)] + [slab(dW["dn"][l], ffn_w_down) for l in range(n_layers)]
    got = _rs_pair_exchange("rs_pair", slabs)
    parts = [_rs_pair_add(f"rs_add_{nm}", s, g, c_arr) for (nm, _, _), s, g in zip(big, slabs, got)]
    sums = _rs_chip_exchange("rs_chip", parts)
    moments = {"in": (m_sc_w_in, v_sc_w_in), "sco": (m_sc_w_out, v_sc_w_out), "qkv": (m_attn_w_qkv, v_attn_w_qkv),
               "ao": (m_attn_w_out, v_attn_w_out), "up": (m_ffn_w_up, v_ffn_w_up), "dn": (m_ffn_w_down, v_ffn_w_down)}
    upd = {}
    for (nm, w, l), s in zip(big, sums):
        key = nm.rstrip("0123456789")
        upd[key] = _adamw(f"adamw_{nm}", s, w, moments[key][0], moments[key][1], l, upd.get(key))

    def leaves(k):
        return [sg, sd, sm, sv][k][0:3] + [upd["in"][k], [sg, sd, sm, sv][k][3], [sg, sd, sm, sv][k][4], upd["sco"][k],
                                          upd["qkv"][k], upd["ao"][k], upd["up"][k], [sg, sd, sm, sv][k][5],
                                          [sg, sd, sm, sv][k][6], upd["dn"][k]]

    return (loss.reshape(()), grad_x[None], *leaves(0), *leaves(1), *leaves(2), *leaves(3))
```

```python
import functools
import math

import numpy as np
import jax
import jax.numpy as jnp
from jax import lax
from jax.experimental import pallas as pl
from jax.experimental.pallas import tpu as pltpu

F32 = jnp.float32
BF16 = jnp.bfloat16
MESH = pl.DeviceIdType.MESH

HEAD_DIM = 128
DILATED_GROUPS = ((128, 1), (512, 4), (2048, 16))
DILATIONS = tuple(d for _, d in DILATED_GROUPS)
BAND = (DILATED_GROUPS[0][0] // 2) // DILATED_GROUPS[0][1]
assert all((w // 2) // d == BAND for w, d in DILATED_GROUPS)
NORM_EPS = 1e-5
ALIBI_MAX = 8.0
NEG_INF = -1e30
ADAM_LR, ADAM_B1, ADAM_B2, ADAM_EPS, ADAM_WD, ADAM_STEP = 0.001, 0.9, 0.999, 1e-08, 0.01, 10

N_DEV = 8
LANES = 128
HALO = 16
VMEM_LIMIT = 56 * 1024 * 1024


def _cp(*sem):
    return pltpu.CompilerParams(dimension_semantics=sem, vmem_limit_bytes=VMEM_LIMIT)


def _tile(n, pref, mult):
    t = (min(n, pref) // mult) * mult
    while t >= mult:
        if n % t == 0:
            return t
        t -= mult
    return n


def _sds(shape, dtype):
    return jax.ShapeDtypeStruct(shape, dtype)


NN = (((1,), (0,)), ((), ()))
NT = (((1,), (1,)), ((), ()))
TN = (((0,), (0,)), ((), ()))


def _mm(name, operands, in_specs, out_sds, o_spec, grid, dims, acc_shape, has_res=False, aliases=None):
    nk = grid[2]
    n_in = len(operands)

    def body(*refs):
        a_ref, b_ref = refs[0], refs[1]
        r_ref = refs[2] if has_res else None
        o_ref = refs[n_in]
        acc = refs[n_in + 1]
        part = lax.dot_general(a_ref[...], b_ref[...], dims, preferred_element_type=F32)

        def finish(total):
            if has_res:
                total = total + r_ref[...]
            o_ref[...] = total.astype(o_ref.dtype)

        if nk == 1:
            finish(part)
        else:
            k = pl.program_id(2)

            @pl.when(k == 0)
            def _():
                acc[...] = part

            @pl.when(k > 0)
            def _():
                acc[...] += part

            @pl.when(k == nk - 1)
            def _():
                finish(acc[...])

    return pl.pallas_call(
        body, grid=grid, in_specs=in_specs, out_specs=o_spec, out_shape=out_sds,
        scratch_shapes=[pltpu.VMEM(acc_shape if nk > 1 else (8, LANES), F32)],
        input_output_aliases=aliases or {}, name=name,
        compiler_params=_cp("parallel", "parallel", "arbitrary"),
    )(*operands)


def _mm_nn_cols(name, a, wg, off, n_tiles, tn, out_dtype):
    M, K = a.shape
    per = wg.shape[2] // tn
    tm = _tile(M, 1024, 16)
    return _mm(name, (a, wg),
               [pl.BlockSpec((tm, K), lambda i, j, k: (i, 0)),
                pl.BlockSpec((None, K, tn), lambda i, j, k: ((off + j) // per, 0, (off + j) % per))],
               _sds((M, n_tiles * tn), out_dtype), pl.BlockSpec((tm, tn), lambda i, j, k: (i, j)),
               (M // tm, n_tiles, 1), NN, (tm, tn))


def _mm_nn(name, a, w, res, out_dtype):
    M, K = a.shape
    N = w.shape[1]
    tm, tn, tk = _tile(M, 1024, 16), _tile(N, 1024, LANES), _tile(K, 2048, LANES)
    ops = [a, w]
    specs = [pl.BlockSpec((tm, tk), lambda i, j, k: (i, k)), pl.BlockSpec((tk, tn), lambda i, j, k: (k, j))]
    if res is not None:
        ops.append(res)
        specs.append(pl.BlockSpec((tm, tn), lambda i, j, k: (i, j)))
    return _mm(name, tuple(ops), specs, _sds((M, N), out_dtype), pl.BlockSpec((tm, tn), lambda i, j, k: (i, j)),
               (M // tm, N // tn, K // tk), NN, (tm, tn), has_res=res is not None)


def _mm_nt_cols(name, dy, wg, off, n_tiles, tkc, out_dtype):
    M = dy.shape[0]
    K = wg.shape[1]
    per = wg.shape[2] // tkc
    tm, tn = _tile(M, 1024, 16), _tile(K, 1024, LANES)
    return _mm(name, (dy, wg),
               [pl.BlockSpec((tm, tkc), lambda i, j, k: (i, k)),
                pl.BlockSpec((None, tn, tkc), lambda i, j, k: ((off + k) // per, j, (off + k) % per))],
               _sds((M, K), out_dtype), pl.BlockSpec((tm, tn), lambda i, j, k: (i, j)),
               (M // tm, K // tn, n_tiles), NT, (tm, tn))


def _mm_nt(name, dy, w, out_dtype, tn_pref=1024):
    M, N = dy.shape
    Kw = w.shape[0]
    tm, tn, tk = _tile(M, 1024, 16), _tile(Kw, tn_pref, LANES), _tile(N, 2048, LANES)
    return _mm(name, (dy, w),
               [pl.BlockSpec((tm, tk), lambda i, j, k: (i, k)), pl.BlockSpec((tn, tk), lambda i, j, k: (j, k))],
               _sds((M, Kw), out_dtype), pl.BlockSpec((tm, tn), lambda i, j, k: (i, j)),
               (M // tm, Kw // tn, N // tk), NT, (tm, tn))


def _mm_tn_cols(name, a, dy, ns, cw, off, n_tiles, tn, prev=None):
    M, K = a.shape
    per = cw // tn
    tkr, tk = _tile(K, 1024, LANES), _tile(M, 1024, 16)
    ops = [a, dy]
    specs = [pl.BlockSpec((tk, tkr), lambda i, j, k: (k, i)), pl.BlockSpec((tk, tn), lambda i, j, k: (k, j))]
    aliases = None
    if prev is not None:
        ops.append(prev)
        specs.append(pl.BlockSpec(memory_space=pl.ANY))
        aliases = {2: 0}
    return _mm(name, tuple(ops), specs, _sds((ns, K, cw), BF16),
               pl.BlockSpec((None, tkr, tn), lambda i, j, k: ((off + j) // per, i, (off + j) % per)),
               (K // tkr, n_tiles, M // tk), TN, (tkr, tn), aliases=aliases)


def _mm_tn(name, a, dy, tkr_pref=1024):
    M, Kw = a.shape
    N = dy.shape[1]
    tkr, tn, tk = _tile(Kw, tkr_pref, LANES), _tile(N, 1024, LANES), _tile(M, 1024, 16)
    return _mm(name, (a, dy),
               [pl.BlockSpec((tk, tkr), lambda i, j, k: (k, i)), pl.BlockSpec((tk, tn), lambda i, j, k: (k, j))],
               _sds((Kw, N), BF16), pl.BlockSpec((tkr, tn), lambda i, j, k: (i, j)),
               (Kw // tkr, N // tn, M // tk), TN, (tkr, tn))


def _rmsnorm_fwd(name, x, g):
    T, D = x.shape
    tr = _tile(T, 512, 16)

    def body(x_ref, g_ref, h_ref):
        xf = x_ref[...]
        r = lax.rsqrt(jnp.mean(xf * xf, axis=-1, keepdims=True) + NORM_EPS)
        h_ref[...] = (xf * r * g_ref[...]).astype(h_ref.dtype)

    return pl.pallas_call(
        body, grid=(T // tr,),
        in_specs=[pl.BlockSpec((tr, D), lambda i: (i, 0)), pl.BlockSpec((1, D), lambda i: (0, 0))],
        out_specs=pl.BlockSpec((tr, D), lambda i: (i, 0)), out_shape=_sds((T, D), BF16),
        name=name, compiler_params=_cp("parallel"))(x, g.reshape(1, D))


def _rmsnorm_bwd(name, x, g, dhs, dres):
    T, D = x.shape
    tr = _tile(T, 256, 16)
    n_dh = len(dhs)

    def body(*refs):
        x_ref, g_ref = refs[0], refs[1]
        dh_refs = refs[2:2 + n_dh]
        dres_ref = refs[2 + n_dh]
        dx_ref, dxb_ref, dg_ref = refs[3 + n_dh], refs[4 + n_dh], refs[5 + n_dh]
        xf = x_ref[...]
        r = lax.rsqrt(jnp.mean(xf * xf, axis=-1, keepdims=True) + NORM_EPS)
        xhat = xf * r
        dh = dh_refs[0][...].astype(F32)
        for q in dh_refs[1:]:
            dh = dh + q[...].astype(F32)
        dy = dh * g_ref[...]
        c = jnp.mean(dy * xhat, axis=-1, keepdims=True)
        dx = dres_ref[...] + r * (dy - xhat * c)
        dx_ref[...] = dx
        dxb_ref[...] = dx.astype(BF16)

        @pl.when(pl.program_id(0) == 0)
        def _():
            dg_ref[...] = jnp.zeros_like(dg_ref)

        dg_ref[...] += jnp.sum(dh * xhat, axis=0, keepdims=True)

    row = pl.BlockSpec((tr, D), lambda i: (i, 0))
    vec = pl.BlockSpec((1, D), lambda i: (0, 0))
    dx, dx_b, dg = pl.pallas_call(
        body, grid=(T // tr,), in_specs=[row, vec] + [row] * n_dh + [row],
        out_specs=[row, row, vec], out_shape=[_sds((T, D), F32), _sds((T, D), BF16), _sds((1, D), F32)],
        name=name, compiler_params=_cp("arbitrary"))(x, g.reshape(1, D), *dhs, dres)
    return dx, dx_b, dg[0]


def _final_loss(name, x, g, tgt):
    T, D = x.shape
    tr = _tile(T, 256, 16)

    def body(x_ref, g_ref, t_ref, dx_ref, dxb_ref, dg_ref, loss_ref):
        xf = x_ref[...]
        r = lax.rsqrt(jnp.mean(xf * xf, axis=-1, keepdims=True) + NORM_EPS)
        xhat = xf * r
        err = xhat * g_ref[...] - t_ref[...]
        dy = err * (1.0 / D)
        dxh = dy * g_ref[...]
        c = jnp.mean(dxh * xhat, axis=-1, keepdims=True)
        dx = r * (dxh - xhat * c)
        dx_ref[...] = dx
        dxb_ref[...] = dx.astype(BF16)

        @pl.when(pl.program_id(0) == 0)
        def _():
            dg_ref[...] = jnp.zeros_like(dg_ref)
            loss_ref[...] = jnp.zeros_like(loss_ref)

        dg_ref[...] += jnp.sum(dy * xhat, axis=0, keepdims=True)
        loss_ref[...] += 0.5 * jnp.sum(jnp.mean(err * err, axis=-1, keepdims=True), axis=0, keepdims=True)

    row = pl.BlockSpec((tr, D), lambda i: (i, 0))
    vec = pl.BlockSpec((1, D), lambda i: (0, 0))
    dx, dx_b, dg, loss = pl.pallas_call(
        body, grid=(T // tr,), in_specs=[row, vec, row],
        out_specs=[row, row, vec, pl.BlockSpec((1, 1), lambda i: (0, 0))],
        out_shape=[_sds((T, D), F32), _sds((T, D), BF16), _sds((1, D), F32), _sds((1, 1), F32)],
        name=name, compiler_params=_cp("arbitrary"))(x, g.reshape(1, D), tgt)
    return loss[0, 0], dx, dx_b, dg[0]


def _halo_specs(tr, tc, n_rows, col):
    rb = tr // HALO
    last = n_rows // HALO - 1
    return [pl.BlockSpec((tr, tc), lambda *g: (g[-1], col(*g))),
            pl.BlockSpec((HALO, tc), lambda *g: (jnp.maximum(g[-1] * rb - 1, 0), col(*g))),
            pl.BlockSpec((HALO, tc), lambda *g: (jnp.minimum((g[-1] + 1) * rb, last), col(*g)))]


def _ext(cur_ref, prev_ref, next_ref, i, n_i):
    p = prev_ref[...].astype(F32) * (i > 0).astype(F32)
    n = next_ref[...].astype(F32) * (i < n_i - 1).astype(F32)
    return jnp.concatenate([p, cur_ref[...].astype(F32), n], axis=0)


def _shift_dn(x):
    return pltpu.roll(x, 1, axis=0)


def _shift_up(x):
    return pltpu.roll(x, x.shape[0] - 1, axis=0)


def _conv(x, w_ref, b_ref):
    return w_ref[0:1, :] * _shift_dn(x) + w_ref[1:2, :] * x + w_ref[2:3, :] * _shift_up(x) + b_ref[...]


def _conv_t(g, w_ref):
    return w_ref[0:1, :] * _shift_up(g) + w_ref[1:2, :] * g + w_ref[2:3, :] * _shift_dn(g)


def _mid(x, tr):
    return x[HALO:HALO + tr, :]


def _conv_wgrad(acc_ref, g, x, tr, first):
    gm = _mid(g, tr)

    @pl.when(first)
    def _():
        acc_ref[...] = jnp.zeros_like(acc_ref)

    acc_ref[0:1, :] += jnp.sum(gm * _mid(_shift_dn(x), tr), axis=0, keepdims=True)
    acc_ref[1:2, :] += jnp.sum(gm * _mid(x, tr), axis=0, keepdims=True)
    acc_ref[2:3, :] += jnp.sum(gm * _mid(_shift_up(x), tr), axis=0, keepdims=True)
    acc_ref[3:4, :] += jnp.sum(gm, axis=0, keepdims=True)


def _sigmoid(a):
    return 1.0 / (1.0 + jnp.exp(-a))


def _ffn_gate_fwd(name, up, cw, cb):
    T, F2 = up.shape
    F = F2 // 2
    tc, tr = _tile(F, 512, LANES), _tile(T, 512, HALO)
    nF, n_i = F // tc, T // tr

    def body(ac, ap, an, bc, bp, bn, wa, wb, ba, bb, o_ref):
        i = pl.program_id(1)
        ua = _mid(_conv(_ext(ac, ap, an, i, n_i), wa, ba), tr)
        ub = _mid(_conv(_ext(bc, bp, bn, i, n_i), wb, bb), tr)
        o_ref[...] = (ua * _sigmoid(ua) * ub).astype(o_ref.dtype)

    wspec = lambda o: pl.BlockSpec((3, tc), lambda j, i: (0, j + o))
    bspec = lambda o: pl.BlockSpec((1, tc), lambda j, i: (0, j + o))
    return pl.pallas_call(
        body, grid=(nF, n_i),
        in_specs=_halo_specs(tr, tc, T, lambda j, i: j) + _halo_specs(tr, tc, T, lambda j, i: j + nF)
        + [wspec(0), wspec(nF), bspec(0), bspec(nF)],
        out_specs=pl.BlockSpec((tr, tc), lambda j, i: (i, j)), out_shape=_sds((T, F), BF16),
        name=name, compiler_params=_cp("parallel", "parallel"))(up, up, up, up, up, up, cw, cw, cb, cb)


def _ffn_gate_bwd(name, up, dact, cw, cb):
    T, F2 = up.shape
    F = F2 // 2
    tc, tr = _tile(F, 512, LANES), _tile(T, 256, HALO)
    nF, n_i = F // tc, T // tr

    def body(ac, ap, an, bc, bp, bn, dc, dp, dn, wa, wb, ba, bb, o_ref, wg_ref):
        j, i = pl.program_id(0), pl.program_id(1)
        xa = _ext(ac, ap, an, i, n_i)
        xb = _ext(bc, bp, bn, i, n_i)
        da = _ext(dc, dp, dn, i, n_i)
        ua = _conv(xa, wa, ba)
        sig = _sigmoid(ua)

        @pl.when(j < nF)
        def _():
            g = da * _conv(xb, wb, bb) * (sig * (1.0 + ua * (1.0 - sig)))
            o_ref[...] = _mid(_conv_t(g, wa), tr).astype(o_ref.dtype)
            _conv_wgrad(wg_ref, g, xa, tr, i == 0)

        @pl.when(j >= nF)
        def _():
            g = da * (ua * sig)
            o_ref[...] = _mid(_conv_t(g, wb), tr).astype(o_ref.dtype)
            _conv_wgrad(wg_ref, g, xb, tr, i == 0)

    wspec = lambda o: pl.BlockSpec((3, tc), lambda j, i: (0, j % nF + o))
    bspec = lambda o: pl.BlockSpec((1, tc), lambda j, i: (0, j % nF + o))
    return pl.pallas_call(
        body, grid=(2 * nF, n_i),
        in_specs=_halo_specs(tr, tc, T, lambda j, i: j % nF) + _halo_specs(tr, tc, T, lambda j, i: j % nF + nF)
        + _halo_specs(tr, tc, T, lambda j, i: j % nF) + [wspec(0), wspec(nF), bspec(0), bspec(nF)],
        out_specs=[pl.BlockSpec((tr, tc), lambda j, i: (i, j)), pl.BlockSpec((8, tc), lambda j, i: (0, j))],
        out_shape=[_sds((T, F2), BF16), _sds((8, F2), F32)],
        name=name, compiler_params=_cp("parallel", "arbitrary"),
    )(up, up, up, up, up, up, dact, dact, dact, cw, cw, cb, cb)


def _sc_gate_fwd(name, z, cw, cb):
    T, D3 = z.shape
    D = D3 // 3
    tc, tr = _tile(D, 512, LANES), _tile(T, 512, HALO)
    nD, n_i = D // tc, T // tr

    def body(uc, up_, un, gb, cc, cp, cn, w, b, o_ref):
        i = pl.program_id(1)
        cu = _ext(cc, cp, cn, i, n_i) * _ext(uc, up_, un, i, n_i)
        o_ref[...] = (gb[...].astype(F32) * _mid(_conv(cu, w, b), tr)).astype(o_ref.dtype)

    return pl.pallas_call(
        body, grid=(nD, n_i),
        in_specs=_halo_specs(tr, tc, T, lambda j, i: j) + [pl.BlockSpec((tr, tc), lambda j, i: (i, j + nD))]
        + _halo_specs(tr, tc, T, lambda j, i: j + 2 * nD)
        + [pl.BlockSpec((3, tc), lambda j, i: (0, j)), pl.BlockSpec((1, tc), lambda j, i: (0, j))],
        out_specs=pl.BlockSpec((tr, tc), lambda j, i: (i, j)), out_shape=_sds((T, D), BF16),
        name=name, compiler_params=_cp("parallel", "parallel"))(z, z, z, z, z, z, z, cw, cb)


def _sc_gate_bwd(name, z, dy, cw, cb):
    T, D3 = z.shape
    D = D3 // 3
    tc, tr = _tile(D, 512, LANES), _tile(T, 256, HALO)
    nD, n_i = D // tc, T // tr

    def body(uc, up_, un, bc, bp, bn, cc, cp, cn, yc, yp, yn, w, b, o_ref, wg_ref):
        kind, i = pl.program_id(0), pl.program_id(2)
        u = _ext(uc, up_, un, i, n_i)
        gc = _ext(cc, cp, cn, i, n_i)
        cu = gc * u
        g = _ext(yc, yp, yn, i, n_i) * _ext(bc, bp, bn, i, n_i)

        @pl.when(kind == 0)
        def _():
            o_ref[...] = (_mid(_conv_t(g, w), tr) * _mid(gc, tr)).astype(o_ref.dtype)
            _conv_wgrad(wg_ref, g, cu, tr, i == 0)

        @pl.when(kind == 1)
        def _():
            o_ref[...] = (yc[...].astype(F32) * _mid(_conv(cu, w, b), tr)).astype(o_ref.dtype)

            @pl.when(i == 0)
            def _():
                wg_ref[...] = jnp.zeros_like(wg_ref)

        @pl.when(kind == 2)
        def _():
            o_ref[...] = (_mid(_conv_t(g, w), tr) * _mid(u, tr)).astype(o_ref.dtype)

            @pl.when(i == 0)
            def _():
                wg_ref[...] = jnp.zeros_like(wg_ref)

    hs = lambda o: _halo_specs(tr, tc, T, lambda k, j, i: j + o)
    out, wg = pl.pallas_call(
        body, grid=(3, nD, n_i),
        in_specs=hs(0) + hs(nD) + hs(2 * nD) + hs(0)
        + [pl.BlockSpec((3, tc), lambda k, j, i: (0, j)), pl.BlockSpec((1, tc), lambda k, j, i: (0, j))],
        out_specs=[pl.BlockSpec((tr, tc), lambda k, j, i: (i, k * nD + j)),
                   pl.BlockSpec((None, 8, tc), lambda k, j, i: (k, 0, j))],
        out_shape=[_sds((T, D3), BF16), _sds((3, 8, D), F32)],
        name=name, compiler_params=_cp("parallel", "parallel", "arbitrary"),
    )(z, z, z, z, z, z, z, z, z, dy, dy, dy, cw, cb)
    return out, wg[0]


def _slopes(n_heads):
    return jnp.asarray(2.0 ** (-ALIBI_MAX * np.arange(1, n_heads + 1) / n_heads), dtype=F32)


def _nq(L):
    return max(1, min(4, L // LANES // 2))


def _srows(ref, r, start, n, d):
    if d == 1:
        return ref[start:start + n, :]
    return ref[pl.ds(start * d + r, n, stride=d), :]


def _win(p_ref, c_ref, n_ref, r, b, nq):
    lo, hi, top = b * LANES - BAND, b * LANES + LANES + BAND, nq * LANES
    parts = [p_ref[r]] if lo < 0 else []
    parts.append(c_ref[r, max(lo, 0):min(hi, top), :])
    if hi > top:
        parts.append(n_ref[r])
    return parts[0] if len(parts) == 1 else jnp.concatenate(parts, axis=0)


def _nat_win(p_ref, c_ref, n_ref, r, b, nq, d):
    lo, hi, top = b * LANES - BAND, b * LANES + LANES + BAND, nq * LANES
    parts = [_srows(p_ref, r, 0, BAND, d)] if lo < 0 else []
    parts.append(_srows(c_ref, r, max(lo, 0), min(hi, top) - max(lo, 0), d))
    if hi > top:
        parts.append(_srows(n_ref, r, 0, BAND, d))
    return parts[0] if len(parts) == 1 else jnp.concatenate(parts, axis=0)


def _biases(slope, d, n, n_steps, nq, q_rows, k_rows, q0, k0):
    qi = lax.broadcasted_iota(jnp.int32, (q_rows, k_rows), 0) + q0
    kj = lax.broadcasted_iota(jnp.int32, (q_rows, k_rows), 1) + k0
    dist = jnp.abs(kj - qi)
    base = jnp.where(dist <= BAND, -slope * (dist * d).astype(F32), NEG_INF)
    out = []
    for b in range(nq):
        t = base
        if b == 0:
            t = jnp.where((n == 0) & ((kj < 0) | (qi < 0)), NEG_INF, t)
        if b == nq - 1:
            t = jnp.where((n == n_steps - 1) & ((kj >= LANES) | (qi >= LANES)), NEG_INF, t)
        out.append(t)
    return out


def _win_specs(d, H, col, nq, L):
    return [pl.BlockSpec((d, BAND, LANES), lambda h, n: (0, jnp.maximum(2 * nq * n - 1, 0), col * H + h)),
            pl.BlockSpec((d, nq * LANES, LANES), lambda h, n: (0, n, col * H + h)),
            pl.BlockSpec((d, BAND, LANES), lambda h, n: (0, jnp.minimum(2 * nq * (n + 1), L // BAND - 1), col * H + h))]


def _nat_specs(d, nq, L):
    return [pl.BlockSpec((BAND * d, LANES), lambda h, n: (jnp.maximum(2 * nq * n - 1, 0), h)),
            pl.BlockSpec((nq * LANES * d, LANES), lambda h, n: (n, h)),
            pl.BlockSpec((BAND * d, LANES), lambda h, n: (jnp.minimum(2 * nq * (n + 1), L // BAND - 1), h))]


def _over_residues(d, per_r):
    if d == 1:
        per_r(0, 0)
    else:
        lax.fori_loop(0, d, per_r, 0, unroll=2)


def _attn_fwd(name, qkv, d, H):
    T = qkv.shape[0]
    D = H * HEAD_DIM
    L = T // d
    nq = _nq(L)
    n_steps = L // (nq * LANES)
    scale = HEAD_DIM ** -0.5
    q3 = qkv.reshape(d, L, 3 * D)

    def body(s_ref, q_ref, kp, kc, kn, vp, vc, vn, o_ref, l_ref):
        h, n = pl.program_id(0), pl.program_id(1)
        bias = _biases(s_ref[h], d, n, n_steps, nq, LANES, 2 * LANES, 0, -BAND)

        def per_r(r, carry):
            for b in range(nq):
                k, v = _win(kp, kc, kn, r, b, nq), _win(vp, vc, vn, r, b, nq)
                s = lax.dot_general(q_ref[r, b * LANES:(b + 1) * LANES, :], k, NT, preferred_element_type=F32) * scale + bias[b]
                m = jnp.max(s, axis=1, keepdims=True)
                p = jnp.exp(s - m)
                den = jnp.sum(p, axis=1, keepdims=True)
                o = lax.dot_general(p.astype(BF16), v, NN, preferred_element_type=F32) / den
                lse = jnp.broadcast_to(m + jnp.log(den), (LANES, LANES))
                if d == 1:
                    o_ref[b * LANES:(b + 1) * LANES, :] = o
                    l_ref[b * LANES:(b + 1) * LANES, :] = lse
                else:
                    o_ref[pl.ds(b * LANES * d + r, LANES, stride=d), :] = o
                    l_ref[pl.ds(b * LANES * d + r, LANES, stride=d), :] = lse
            return carry

        _over_residues(d, per_r)

    out = pl.BlockSpec((nq * LANES * d, LANES), lambda h, n: (n, h))
    return pl.pallas_call(
        body, grid=(H, n_steps),
        in_specs=[pl.BlockSpec(memory_space=pltpu.SMEM), pl.BlockSpec((d, nq * LANES, LANES), lambda h, n: (0, n, h))]
        + _win_specs(d, H, 1, nq, L) + _win_specs(d, H, 2, nq, L),
        out_specs=[out, out], out_shape=[_sds((T, D), F32), _sds((T, D), F32)],
        name=name, compiler_params=_cp("parallel", "parallel"))(_slopes(H), q3, q3, q3, q3, q3, q3, q3)


def _attn_combine(name, outs, lses):
    T, D = outs[0].shape
    tr, tc = _tile(T, 512, 16), _tile(D, 512, LANES)

    def body(o0, o1, o2, l0, l1, l2, ob_ref, of_ref, l_ref):
        a0, a1, a2 = l0[...], l1[...], l2[...]
        m = jnp.maximum(jnp.maximum(a0, a1), a2)
        e0, e1, e2 = jnp.exp(a0 - m), jnp.exp(a1 - m), jnp.exp(a2 - m)
        z = e0 + e1 + e2
        o = (e0 * o0[...] + e1 * o1[...] + e2 * o2[...]) / z
        of_ref[...] = o
        ob_ref[...] = o.astype(BF16)
        l_ref[...] = m + jnp.log(z)

    blk = pl.BlockSpec((tr, tc), lambda i, j: (i, j))
    return pl.pallas_call(
        body, grid=(T // tr, D // tc), in_specs=[blk] * 6, out_specs=[blk] * 3,
        out_shape=[_sds((T, D), BF16), _sds((T, D), F32), _sds((T, D), F32)],
        name=name, compiler_params=_cp("parallel", "parallel"))(*outs, *lses)


def _attn_stats(name, do, o, lse):
    T, D = do.shape
    tr = _tile(T, 1024, 16)

    def body(a, b, l, o_ref):
        delta = jnp.broadcast_to(jnp.sum(a[...] * b[...], axis=1, keepdims=True), o_ref.shape)
        lane = lax.broadcasted_iota(jnp.int32, o_ref.shape, 1)
        o_ref[...] = jnp.where(lane < BAND, l[...], delta)

    blk = pl.BlockSpec((tr, LANES), lambda i, j: (i, j))
    return pl.pallas_call(body, grid=(T // tr, D // LANES), in_specs=[blk, blk, blk], out_specs=blk,
                          out_shape=_sds((T, D), F32), name=name, compiler_params=_cp("parallel", "parallel"))(do, o, lse)


def _attn_bwd(name, qkv, do, stats, d, H):
    T = qkv.shape[0]
    D = H * HEAD_DIM
    L = T // d
    nq = _nq(L)
    n_steps = L // (nq * LANES)
    scale = HEAD_DIM ** -0.5
    q3 = qkv.reshape(d, L, 3 * D)
    mid = slice(BAND, BAND + LANES)

    def body(s_ref, qp, qc, qn, kp, kc, kn, vp, vc, vn, gp, gc, gn, tp, tc_, tn_, dq_ref, dk_ref, dv_ref):
        h, n = pl.program_id(0), pl.program_id(1)
        bias_q = _biases(s_ref[h], d, n, n_steps, nq, LANES, 2 * LANES, 0, -BAND)
        bias_k = _biases(s_ref[h], d, n, n_steps, nq, 2 * LANES, LANES, -BAND, 0)

        def per_r(r, carry):
            for b in range(nq):
                rows = slice(b * LANES, (b + 1) * LANES)
                q_w, k_w, v_w = _win(qp, qc, qn, r, b, nq), _win(kp, kc, kn, r, b, nq), _win(vp, vc, vn, r, b, nq)
                g_w = _nat_win(gp, gc, gn, r, b, nq, d)
                t_w = _nat_win(tp, tc_, tn_, r, b, nq, d)
                g_b = g_w.astype(BF16)
                q_c, k_c, v_c, g_c, t_c = q_w[mid], k_w[mid], v_w[mid], g_b[mid], t_w[mid]
                s = lax.dot_general(q_c, k_w, NT, preferred_element_type=F32) * scale + bias_q[b]
                p = jnp.exp(s - t_c[:, 0:1])
                dp = lax.dot_general(g_c, v_w, NT, preferred_element_type=F32)
                ds = p * (dp - t_c[:, BAND:BAND + 1])
                dq_ref[r, rows, :] = (lax.dot_general(ds.astype(BF16), k_w, NN, preferred_element_type=F32) * scale).astype(BF16)
                s2 = lax.dot_general(q_w, k_c, NT, preferred_element_type=F32) * scale + bias_k[b]
                p2 = jnp.exp(s2 - t_w[:, 0:1])
                dv_ref[r, rows, :] = lax.dot_general(p2.astype(BF16), g_b, TN, preferred_element_type=F32).astype(BF16)
                dp2 = lax.dot_general(g_b, v_c, NT, preferred_element_type=F32)
                ds2 = p2 * (dp2 - t_w[:, BAND:BAND + 1])
                dk_ref[r, rows, :] = (lax.dot_general(ds2.astype(BF16), q_w, TN, preferred_element_type=F32) * scale).astype(BF16)
            return carry

        _over_residues(d, per_r)

    out = pl.BlockSpec((d, nq * LANES, LANES), lambda h, n: (0, n, h))
    dq, dk, dv = pl.pallas_call(
        body, grid=(H, n_steps),
        in_specs=[pl.BlockSpec(memory_space=pltpu.SMEM)]
        + _win_specs(d, H, 0, nq, L) + _win_specs(d, H, 1, nq, L) + _win_specs(d, H, 2, nq, L)
        + _nat_specs(d, nq, L) + _nat_specs(d, nq, L),
        out_specs=[out, out, out], out_shape=[_sds((d, L, D), BF16)] * 3,
        name=name, compiler_params=_cp("parallel", "parallel"),
    )(_slopes(H), *([q3] * 9), *([do] * 3), *([stats] * 3))
    return dq.reshape(T, D), dk.reshape(T, D), dv.reshape(T, D)


def _to_group_order(a, d):
    if d == 1:
        return a
    T, C = a.shape
    return a.reshape(T // d, d, C).swapaxes(0, 1).reshape(T, C)


def _from_group_order(a, d):
    if d == 1:
        return a
    T, C = a.shape
    return a.reshape(d, T // d, C).swapaxes(0, 1).reshape(T, C)


def _fwd_bwd(x, tgt, W, S):
    T, D = x.shape
    H = D // HEAD_DIM
    ns = W["in"].shape[0]
    cw_in, cw_qkv, cw_up = W["in"].shape[2], W["qkv"].shape[2], W["up"][0].shape[2]
    tq = math.gcd(3 * D, cw_qkv)
    ntg = 3 * D // tq

    def ffn_fwd(l, xin):
        hf = _rmsnorm_fwd(f"ffn_norm{l}", xin, S["ffn_g"][l])
        up = _mm_nn_cols(f"ffn_up{l}", hf, W["up"][l], 0, ns, cw_up, BF16)
        act = _ffn_gate_fwd(f"ffn_gate{l}", up, S["ffn_cw"][l], S["ffn_cb"][l][None])
        return hf, up, act, _mm_nn(f"ffn_down{l}", act, W["dn"][l], xin, F32)

    def ffn_bwd(l, xin, hf, up, act, dxo, dxo_b):
        dact = _mm_nt(f"ffn_down_dx{l}", dxo_b, W["dn"][l], BF16, tn_pref=512)
        dw_dn = _mm_tn(f"ffn_down_dw{l}", act, dxo_b, tkr_pref=1408)
        dup, cg = _ffn_gate_bwd(f"ffn_gate_bwd{l}", up, dact, S["ffn_cw"][l], S["ffn_cb"][l][None])
        dhf = _mm_nt_cols(f"ffn_up_dx{l}", dup, W["up"][l], 0, ns, cw_up, F32)
        dw_up = _mm_tn_cols(f"ffn_up_dw{l}", hf, dup, ns, cw_up, 0, ns, cw_up)
        dx, dx_b, dg = _rmsnorm_bwd(f"ffn_norm_bwd{l}", xin, S["ffn_g"][l], [dhf], dxo)
        return dx, dx_b, dg, cg, dw_up, dw_dn

    h0 = _rmsnorm_fwd("mix_norm0", x, S["mix_g"][0])
    z = _mm_nn_cols("sc_in", h0, W["in"], 0, ns, cw_in, BF16)
    y = _sc_gate_fwd("sc_gate", z, S["sc_cw"], S["sc_cb"][None])
    x1 = _mm_nn("sc_out", y, W["sco"], x, F32)
    hf0, up0, act0, x2 = ffn_fwd(0, x1)
    h1 = _rmsnorm_fwd("mix_norm1", x2, S["mix_g"][1])
    hd, qkv, outs, lses = [], [], [], []
    for g, d in enumerate(DILATIONS):
        hd.append(_to_group_order(h1, d))
        qkv.append(_mm_nn_cols(f"attn_qkv{g}", hd[g], W["qkv"], g * ntg, ntg, tq, BF16))
        o_g, l_g = _attn_fwd(f"attn_fwd{g}", qkv[g], d, H)
        outs.append(o_g)
        lses.append(l_g)
    o_b, o_f, lse = _attn_combine("attn_combine", outs, lses)
    x3 = _mm_nn("attn_out", o_b, W["ao"], x2, F32)
    hf1, up1, act1, x4 = ffn_fwd(1, x3)
    loss, dx4, dx4_b, dg_fin = _final_loss("final_loss", x4, S["fin_g"], tgt)

    dx3, dx3_b, dg_f1, cg1, dw_up1, dw_dn1 = ffn_bwd(1, x3, hf1, up1, act1, dx4, dx4_b)
    do = _mm_nt("attn_out_dx", dx3_b, W["ao"], F32)
    dw_ao = _mm_tn("attn_out_dw", o_b, dx3_b)
    stats = _attn_stats("attn_stats", do, o_f, lse)
    dhs, dw_qkv = [], None
    for g, d in enumerate(DILATIONS):
        dq, dk, dv = _attn_bwd(f"attn_bwd{g}", qkv[g], do, stats, d, H)
        dqkv = jnp.concatenate([dq, dk, dv], axis=1)
        dhs.append(_from_group_order(_mm_nt_cols(f"attn_qkv_dx{g}", dqkv, W["qkv"], g * ntg, ntg, tq, F32), d))
        dw_qkv = _mm_tn_cols(f"attn_qkv_dw{g}", hd[g], dqkv, ns, cw_qkv, g * ntg, ntg, tq, prev=dw_qkv)
    dx2, dx2_b, dg_m1 = _rmsnorm_bwd("mix_norm_bwd1", x2, S["mix_g"][1], dhs, dx3)
    dx1, dx1_b, dg_f0, cg0, dw_up0, dw_dn0 = ffn_bwd(0, x1, hf0, up0, act0, dx2, dx2_b)
    dy = _mm_nt("sc_out_dx", dx1_b, W["sco"], BF16)
    dw_sco = _mm_tn("sc_out_dw", y, dx1_b)
    dz, cg_sc = _sc_gate_bwd("sc_gate_bwd", z, dy, S["sc_cw"], S["sc_cb"][None])
    dh0 = _mm_nt_cols("sc_in_dx", dz, W["in"], 0, ns, cw_in, F32)
    dw_in = _mm_tn_cols("sc_in_dw", h0, dz, ns, cw_in, 0, ns, cw_in)
    dx0, _, dg_m0 = _rmsnorm_bwd("mix_norm_bwd0", x, S["mix_g"][0], [dh0], dx1)

    dW = {"in": dw_in, "sco": dw_sco, "qkv": dw_qkv, "ao": dw_ao, "up": [dw_up0, dw_up1], "dn": [dw_dn0, dw_dn1]}
    dS = {"mix_g": jnp.stack([dg_m0, dg_m1]), "ffn_g": jnp.stack([dg_f0, dg_f1]), "fin_g": dg_fin,
          "sc_cw": cg_sc[0:3], "sc_cb": cg_sc[3], "ffn_cw": jnp.stack([cg0[0:3], cg1[0:3]]),
          "ffn_cb": jnp.stack([cg0[3], cg1[3]])}
    return loss, dx0, dW, dS


def _place():
    x, y, c = lax.axis_index("x"), lax.axis_index("y"), lax.axis_index("c")
    return x, y, c, [(1 - x, y), (x, 1 - y), (1 - x, 1 - y)]


def _any_specs(n):
    return [pl.BlockSpec(memory_space=pl.ANY)] * n


def _all_gather(name, arrs):
    n = len(arrs)

    def body(*refs):
        ins, outs = refs[:n], refs[n:2 * n]
        send_sems, recv_sems, local_sems = refs[2 * n:]
        x, y, c, chips = _place()
        me, sibling = 4 * x + 2 * y + c, (x, y, 1 - c)

        def copy(a, k, blk, to, src=None):
            dst = outs[a].at[blk]
            return pltpu.make_async_remote_copy(src_ref=dst if src is None else src, dst_ref=dst,
                                                send_sem=send_sems.at[a, k], recv_sem=recv_sems.at[a, k],
                                                device_id=to, device_id_type=MESH)

        mine = [pltpu.make_async_copy(ins[a], outs[a].at[me], local_sems.at[a]) for a in range(n)]
        first = []
        for a in range(n):
            first.append(copy(a, 0, me, sibling, src=ins[a]))
            first += [copy(a, 1 + j, me, (*chip, c), src=ins[a]) for j, chip in enumerate(chips)]
        for cp in mine + first:
            cp.start()
        passed = []
        for j, (px, py) in enumerate(chips):
            for a in range(n):
                blk = 4 * px + 2 * py + c
                copy(a, 1 + j, blk, sibling).wait_recv()
                passed.append(copy(a, 4 + j, blk, sibling))
                passed[-1].start()
        for a in range(n):
            copy(a, 0, 4 * x + 2 * y + 1 - c, sibling).wait_recv()
            for j, (px, py) in enumerate(chips):
                copy(a, 4 + j, 4 * px + 2 * py + 1 - c, sibling).wait_recv()
        for cp in first + passed:
            cp.wait_send()
        for cp in mine:
            cp.wait()

    return pl.pallas_call(
        body, in_specs=_any_specs(n), out_specs=_any_specs(n),
        out_shape=[_sds((N_DEV,) + a.shape, a.dtype) for a in arrs],
        scratch_shapes=[pltpu.SemaphoreType.DMA((n, 7)), pltpu.SemaphoreType.DMA((n, 7)), pltpu.SemaphoreType.DMA((n,))],
        name=name)(*arrs)


def _rs_pair_exchange(name, slabs):
    n = len(slabs)

    def body(*refs):
        ins, outs = refs[:n], refs[n:2 * n]
        send_sems, recv_sems = refs[2 * n:]
        x, y, c, _ = _place()
        copies = [pltpu.make_async_remote_copy(src_ref=ins[a].at[2 * k + 1 - c], dst_ref=outs[a].at[k],
                                               send_sem=send_sems.at[a, k], recv_sem=recv_sems.at[a, k],
                                               device_id=(x, y, 1 - c), device_id_type=MESH)
                  for a in range(n) for k in range(4)]
        for cp in copies:
            cp.start()
        for cp in copies:
            cp.wait()

    return pl.pallas_call(
        body, in_specs=_any_specs(n), out_specs=_any_specs(n),
        out_shape=[_sds((4,) + a.shape[1:], a.dtype) for a in slabs],
        scratch_shapes=[pltpu.SemaphoreType.DMA((n, 4)), pltpu.SemaphoreType.DMA((n, 4))], name=name)(*slabs)


def _rs_pair_add(name, slabs, got, c_arr):
    _, R, C = slabs.shape
    tr, tc = _tile(R, 512, 16), _tile(C, 1536, LANES)

    def body(c_ref, a_ref, b_ref, o_ref):
        o_ref[...] = (a_ref[...].astype(F32) + b_ref[...].astype(F32)).astype(o_ref.dtype)

    return pl.pallas_call(
        body,
        grid_spec=pltpu.PrefetchScalarGridSpec(
            num_scalar_prefetch=1, grid=(4, R // tr, C // tc),
            in_specs=[pl.BlockSpec((None, tr, tc), lambda k, i, j, c_ref: (2 * k + c_ref[0], i, j)),
                      pl.BlockSpec((None, tr, tc), lambda k, i, j, c_ref: (k, i, j))],
            out_specs=pl.BlockSpec((None, tr, tc), lambda k, i, j, c_ref: (k, i, j))),
        out_shape=_sds((4, R, C), BF16), name=name,
        compiler_params=_cp("parallel", "parallel", "parallel"))(c_arr, slabs, got)


def _rs_chip_exchange(name, parts):
    n = len(parts)

    def body(*refs):
        ins, outs = refs[:n], refs[n:2 * n]
        send_sems, recv_sems, local_sems = refs[2 * n:]
        x, y, c, chips = _place()
        my_chip = 2 * x + y
        mine = [pltpu.make_async_copy(ins[a].at[my_chip], outs[a].at[my_chip], local_sems.at[a]) for a in range(n)]
        sends = [pltpu.make_async_remote_copy(src_ref=ins[a].at[2 * px + py], dst_ref=outs[a].at[my_chip],
                                              send_sem=send_sems.at[a, j], recv_sem=recv_sems.at[a, j],
                                              device_id=(px, py, c), device_id_type=MESH)
                 for a in range(n) for j, (px, py) in enumerate(chips)]
        for cp in mine + sends:
            cp.start()
        for a in range(n):
            for j, (px, py) in enumerate(chips):
                pltpu.make_async_remote_copy(src_ref=ins[a].at[my_chip], dst_ref=outs[a].at[2 * px + py],
                                             send_sem=send_sems.at[a, j], recv_sem=recv_sems.at[a, j],
                                             device_id=(px, py, c), device_id_type=MESH).wait_recv()
        for cp in sends:
            cp.wait_send()
        for cp in mine:
            cp.wait()

    return pl.pallas_call(
        body, in_specs=_any_specs(n), out_specs=_any_specs(n),
        out_shape=[_sds(a.shape, a.dtype) for a in parts],
        scratch_shapes=[pltpu.SemaphoreType.DMA((n, 3)), pltpu.SemaphoreType.DMA((n, 3)), pltpu.SemaphoreType.DMA((n,))],
        name=name)(*parts)


def _sum_slots(name, a):
    _, rows, _ = a.shape

    def body(a_ref, o_ref):
        s = a_ref[0]
        for k in range(1, N_DEV):
            s = s + a_ref[k]
        o_ref[...] = s

    return pl.pallas_call(body, out_shape=_sds((rows, LANES), F32), name=name)(a)


def _cast_bf16(name, w3, l):
    _, R, C = w3.shape
    tr, tc = _tile(R, 512, 16), _tile(C, 1536, LANES)

    def body(w_ref, o_ref):
        o_ref[...] = w_ref[...].astype(BF16)

    return pl.pallas_call(
        body, grid=(R // tr, C // tc), in_specs=[pl.BlockSpec((None, tr, tc), lambda i, j: (l, i, j))],
        out_specs=pl.BlockSpec((tr, tc), lambda i, j: (i, j)), out_shape=_sds((R, C), BF16),
        name=name, compiler_params=_cp("parallel", "parallel"))(w3)


def _adamw(name, g_slots, w3, m3, v3, l, prev):
    n_slots, R, C = g_slots.shape
    tr, tc = _tile(R, 256, 8), _tile(C, 1536, LANES)
    c1, c2 = 1.0 - ADAM_B1 ** ADAM_STEP, 1.0 - ADAM_B2 ** ADAM_STEP

    def body(g_ref, w_ref, m_ref, v_ref, *rest):
        og, od, om, ov = rest[-4:]
        g = g_ref[0].astype(F32)
        for k in range(1, n_slots):
            g = g + g_ref[k].astype(F32)
        m = ADAM_B1 * m_ref[...] + (1.0 - ADAM_B1) * g
        v = ADAM_B2 * v_ref[...] + (1.0 - ADAM_B2) * (g * g)
        og[...] = g
        om[...] = m
        ov[...] = v
        od[...] = -ADAM_LR * ((m / c1) / (jnp.sqrt(v / c2) + ADAM_EPS) + ADAM_WD * w_ref[...])

    lay = pl.BlockSpec((None, tr, tc), lambda i, j: (l, i, j))
    ops = [g_slots, w3, m3, v3]
    specs = [pl.BlockSpec((n_slots, tr, tc), lambda i, j: (0, i, j)), lay, lay, lay]
    aliases = {}
    if prev is not None:
        ops += list(prev)
        specs += _any_specs(4)
        aliases = {4 + k: k for k in range(4)}
    return pl.pallas_call(
        body, grid=(R // tr, C // tc), in_specs=specs, out_specs=[lay] * 4,
        out_shape=[_sds(w3.shape, F32)] * 4, input_output_aliases=aliases,
        name=name, compiler_params=_cp("parallel", "parallel"))(*ops)


def _pack(parts):
    flat = jnp.concatenate([p.reshape(-1) for p in parts])
    pad = (-flat.shape[0]) % LANES
    return jnp.pad(flat, (0, pad)).reshape(-1, LANES)


def _unpack(packed, shapes):
    flat = packed.reshape(-1)
    out, at = [], 0
    for s in shapes:
        n = int(np.prod(s))
        out.append(flat[at:at + n].reshape(s))
        at += n
    return out


def kernel(x, mix_norm_g, ffn_norm_g, final_norm_g, sc_w_in, sc_conv_w, sc_conv_b, sc_w_out, attn_w_qkv, attn_w_out, ffn_w_up, ffn_conv_w, ffn_conv_b, ffn_w_down, loss_target, m_mix_norm_g, m_ffn_norm_g, m_final_norm_g, m_sc_w_in, m_sc_conv_w, m_sc_conv_b, m_sc_w_out, m_attn_w_qkv, m_attn_w_out, m_ffn_w_up, m_ffn_conv_w, m_ffn_conv_b, m_ffn_w_down, v_mix_norm_g, v_ffn_norm_g, v_final_norm_g, v_sc_w_in, v_sc_conv_w, v_sc_conv_b, v_sc_w_out, v_attn_w_qkv, v_attn_w_out, v_ffn_w_up, v_ffn_conv_w, v_ffn_conv_b, v_ffn_w_down):
    _, T, D = x.shape
    n_layers = ffn_w_up.shape[0]
    me = 4 * lax.axis_index("x") + 2 * lax.axis_index("y") + lax.axis_index("c")
    c_arr = lax.axis_index("c").astype(jnp.int32).reshape(1)

    big = [("in", sc_w_in, 0), ("sco", sc_w_out, 0), ("qkv", attn_w_qkv, 0), ("ao", attn_w_out, 0)]
    big += [(f"up{l}", ffn_w_up, l) for l in range(n_layers)] + [(f"dn{l}", ffn_w_down, l) for l in range(n_layers)]
    gathered = _all_gather("gather_weights", [_cast_bf16(f"cast_{nm}", w, l) for nm, w, l in big])
    G = {nm: g for (nm, _, _), g in zip(big, gathered)}
    rows_major = lambda g: g.reshape(g.shape[0] * g.shape[1], g.shape[2])
    W = {"in": G["in"], "sco": rows_major(G["sco"]), "qkv": G["qkv"], "ao": rows_major(G["ao"]),
         "up": [G[f"up{l}"] for l in range(n_layers)], "dn": [rows_major(G[f"dn{l}"]) for l in range(n_layers)]}

    cw_shapes = [sc_conv_w.shape, ffn_conv_w.shape]
    cw_all = _all_gather("gather_conv_w", [_pack([sc_conv_w, ffn_conv_w])])[0]
    sc_cw_all, ffn_cw_all = zip(*[_unpack(cw_all[s], cw_shapes) for s in range(N_DEV)])
    sc_cw = jnp.concatenate(sc_cw_all, axis=-1)[0]
    ffn_cw = jnp.concatenate(ffn_cw_all, axis=-1)
    S = {"mix_g": mix_norm_g, "ffn_g": ffn_norm_g, "fin_g": final_norm_g, "sc_cw": sc_cw, "sc_cb": sc_conv_b[0],
         "ffn_cw": ffn_cw, "ffn_cb": ffn_conv_b}

    loss_part, grad_x, dW, dS = _fwd_bwd(x[0], loss_target[0], W, S)

    small_names = ["mix_g", "ffn_g", "fin_g", "sc_cb", "ffn_cb", "sc_cw", "ffn_cw"]
    small_parts = [dS[k] for k in small_names] + [loss_part.reshape(1)]
    small_sum = _sum_slots("sum_small", _all_gather("gather_small", [_pack(small_parts)])[0])
    g_mix, g_ffn, g_fin, g_scb, g_fcb, g_scw, g_fcw, loss = _unpack(small_sum, [p.shape for p in small_parts])
    g_scw = lax.dynamic_slice_in_dim(g_scw, me * sc_conv_w.shape[-1], sc_conv_w.shape[-1], axis=-1)[None]
    g_fcw = lax.dynamic_slice_in_dim(g_fcw, me * ffn_conv_w.shape[-1], ffn_conv_w.shape[-1], axis=-1)
    g_scb = g_scb[None]
    small_g = [g_mix, g_ffn, g_fin, g_scw, g_scb, g_fcw, g_fcb]
    small_w = [mix_norm_g, ffn_norm_g, final_norm_g, sc_conv_w, sc_conv_b, ffn_conv_w, ffn_conv_b]
    small_m = [m_mix_norm_g, m_ffn_norm_g, m_final_norm_g, m_sc_conv_w, m_sc_conv_b, m_ffn_conv_w, m_ffn_conv_b]
    small_v = [v_mix_norm_g, v_ffn_norm_g, v_final_norm_g, v_sc_conv_w, v_sc_conv_b, v_ffn_conv_w, v_ffn_conv_b]
    small_out = _adamw("adamw_small", _pack(small_g)[None], _pack(small_w)[None], _pack(small_m)[None],
                       _pack(small_v)[None], 0, None)
    small_shapes = [w.shape for w in small_w]
    sg, sd, sm, sv = [_unpack(o[0], small_shapes) for o in small_out]

    slab = lambda g, like: g if g.ndim == 3 else g.reshape(N_DEV, like.shape[1], like.shape[2])
    slabs = [slab(dW["in"], sc_w_in), slab(dW["sco"], sc_w_out), slab(dW["qkv"], attn_w_qkv), slab(dW["ao"], attn_w_out)]
    slabs += [slab(dW["up"][l], ffn_w_up) for l in range(n_layers)] + [slab(dW["dn"][l], ffn_w_down) for l in range(n_layers)]
    got = _rs_pair_exchange("rs_pair", slabs)
    parts = [_rs_pair_add(f"rs_add_{nm}", s, g, c_arr) for (nm, _, _), s, g in zip(big, slabs, got)]
    sums = _rs_chip_exchange("rs_chip", parts)
    moments = {"in": (m_sc_w_in, v_sc_w_in), "sco": (m_sc_w_out, v_sc_w_out), "qkv": (m_attn_w_qkv, v_attn_w_qkv),
               "ao": (m_attn_w_out, v_attn_w_out), "up": (m_ffn_w_up, v_ffn_w_up), "dn": (m_ffn_w_down, v_ffn_w_down)}
    upd = {}
    for (nm, w, l), s in zip(big, sums):
        key = nm.rstrip("0123456789")
        upd[key] = _adamw(f"adamw_{nm}", s, w, moments[key][0], moments[key][1], l, upd.get(key))

    def leaves(k):
        return [sg, sd, sm, sv][k][0:3] + [upd["in"][k], [sg, sd, sm, sv][k][3], [sg, sd, sm, sv][k][4], upd["sco"][k],
                                          upd["qkv"][k], upd["ao"][k], upd["up"][k], [sg, sd, sm, sv][k][5],
                                          [sg, sd, sm, sv][k][6], upd["dn"][k]]

    return (loss.reshape(()), grad_x[None], *leaves(0), *leaves(1), *leaves(2), *leaves(3))
```

```python
import functools
import math

import numpy as np
import jax
import jax.numpy as jnp
from jax import lax
from jax.experimental import pallas as pl
from jax.experimental.pallas import tpu as pltpu

F32 = jnp.float32
BF16 = jnp.bfloat16
MESH = pl.DeviceIdType.MESH

HEAD_DIM = 128
DILATED_GROUPS = ((128, 1), (512, 4), (2048, 16))
DILATIONS = tuple(d for _, d in DILATED_GROUPS)
BAND = (DILATED_GROUPS[0][0] // 2) // DILATED_GROUPS[0][1]
assert all((w // 2) // d == BAND for w, d in DILATED_GROUPS)
NORM_EPS = 1e-5
ALIBI_MAX = 8.0
NEG_INF = -1e30
ADAM_LR, ADAM_B1, ADAM_B2, ADAM_EPS, ADAM_WD, ADAM_STEP = 0.001, 0.9, 0.999, 1e-08, 0.01, 10

N_DEV = 8
LANES = 128
HALO = 16
VMEM_LIMIT = 56 * 1024 * 1024


def _cp(*sem):
    return pltpu.CompilerParams(dimension_semantics=sem, vmem_limit_bytes=VMEM_LIMIT)


def _tile(n, pref, mult):
    t = (min(n, pref) // mult) * mult
    while t >= mult:
        if n % t == 0:
            return t
        t -= mult
    return n


def _sds(shape, dtype):
    return jax.ShapeDtypeStruct(shape, dtype)


NN = (((1,), (0,)), ((), ()))
NT = (((1,), (1,)), ((), ()))
TN = (((0,), (0,)), ((), ()))


class _Stage:
    def __init__(self, arrays, out_shapes, sems, start, finish):
        self.arrays, self.out_shapes, self.sems, self.start, self.finish = arrays, out_shapes, sems, start, finish


def _join(stages):
    stages = [s for s in stages if s is not None]
    if not stages:
        return None

    def split(refs, count):
        out, at = [], 0
        for s in stages:
            out.append(refs[at:at + count(s)])
            at += count(s)
        return out

    def each(which):
        def run(ins, outs, sems):
            parts = zip(split(ins, lambda s: len(s.arrays)), split(outs, lambda s: len(s.out_shapes)),
                        split(sems, lambda s: len(s.sems)))
            for s, (i, o, m) in zip(stages, parts):
                getattr(s, which)(i, o, m)
        return run

    return _Stage(sum([s.arrays for s in stages], []), sum([s.out_shapes for s in stages], []),
                  sum([s.sems for s in stages], []), each("start"), each("finish"))


def _any_specs(n):
    return [pl.BlockSpec(memory_space=pl.ANY)] * n


def _run_stage(name, st):
    n, m = len(st.arrays), len(st.out_shapes)

    def body(*refs):
        ins, outs, sems = refs[:n], refs[n:n + m], refs[n + m:]
        st.start(ins, outs, sems)
        st.finish(ins, outs, sems)

    return pl.pallas_call(body, in_specs=_any_specs(n), out_specs=_any_specs(m), out_shape=st.out_shapes,
                          scratch_shapes=st.sems, name=name)(*st.arrays)


def _mm(name, operands, in_specs, out_sds, o_spec, grid, dims, acc_shape, has_res=False, aliases=None, comm=None):
    nk = grid[2]
    n_in = len(operands)
    n_ci, n_co = (len(comm.arrays), len(comm.out_shapes)) if comm else (0, 0)

    def body(*refs):
        a_ref, b_ref = refs[0], refs[1]
        r_ref = refs[2] if has_res else None
        o_ref = refs[n_in + n_ci]
        acc = refs[n_in + n_ci + 1 + n_co]
        c_refs = (refs[n_in:n_in + n_ci], refs[n_in + n_ci + 1:n_in + n_ci + 1 + n_co], refs[n_in + n_ci + 2 + n_co:])
        ids = [pl.program_id(q) for q in range(3)]
        if comm:
            @pl.when((ids[0] == 0) & (ids[1] == 0) & (ids[2] == 0))
            def _():
                comm.start(*c_refs)

        part = lax.dot_general(a_ref[...], b_ref[...], dims, preferred_element_type=F32)

        def finish(total):
            if has_res:
                total = total + r_ref[...]
            o_ref[...] = total.astype(o_ref.dtype)

        if nk == 1:
            finish(part)
        else:
            k = ids[2]

            @pl.when(k == 0)
            def _():
                acc[...] = part

            @pl.when(k > 0)
            def _():
                acc[...] += part

            @pl.when(k == nk - 1)
            def _():
                finish(acc[...])

        if comm:
            @pl.when((ids[0] == grid[0] - 1) & (ids[1] == grid[1] - 1) & (ids[2] == nk - 1))
            def _():
                comm.finish(*c_refs)

    scratch = [pltpu.VMEM(acc_shape if nk > 1 else (8, LANES), F32)]
    if not comm:
        out = pl.pallas_call(
            body, grid=grid, in_specs=in_specs, out_specs=o_spec, out_shape=out_sds, scratch_shapes=scratch,
            input_output_aliases=aliases or {}, name=name,
            compiler_params=_cp("parallel", "parallel", "arbitrary"))(*operands)
        return out, []
    outs = pl.pallas_call(
        body, grid=grid, in_specs=list(in_specs) + _any_specs(n_ci), out_specs=[o_spec] + _any_specs(n_co),
        out_shape=[out_sds] + comm.out_shapes, scratch_shapes=scratch + comm.sems,
        input_output_aliases=aliases or {}, name=name,
        compiler_params=_cp("arbitrary", "arbitrary", "arbitrary"))(*operands, *comm.arrays)
    return outs[0], list(outs[1:])


def _mm_nn_cols(name, a, wg, off, n_tiles, tn, out_dtype, comm=None):
    M, K = a.shape
    per = wg.shape[2] // tn
    tm = _tile(M, 1024, 16)
    return _mm(name, (a, wg),
               [pl.BlockSpec((tm, K), lambda i, j, k: (i, 0)),
                pl.BlockSpec((None, K, tn), lambda i, j, k: ((off + j) // per, 0, (off + j) % per))],
               _sds((M, n_tiles * tn), out_dtype), pl.BlockSpec((tm, tn), lambda i, j, k: (i, j)),
               (M // tm, n_tiles, 1), NN, (tm, tn), comm=comm)


def _mm_nn(name, a, w, res, out_dtype, comm=None):
    M, K = a.shape
    N = w.shape[1]
    tm, tn, tk = _tile(M, 1024, 16), _tile(N, 1024, LANES), _tile(K, 2048, LANES)
    ops = [a, w]
    specs = [pl.BlockSpec((tm, tk), lambda i, j, k: (i, k)), pl.BlockSpec((tk, tn), lambda i, j, k: (k, j))]
    if res is not None:
        ops.append(res)
        specs.append(pl.BlockSpec((tm, tn), lambda i, j, k: (i, j)))
    return _mm(name, tuple(ops), specs, _sds((M, N), out_dtype), pl.BlockSpec((tm, tn), lambda i, j, k: (i, j)),
               (M // tm, N // tn, K // tk), NN, (tm, tn), has_res=res is not None, comm=comm)


def _mm_nt_cols(name, dy, wg, off, n_tiles, tkc, out_dtype, comm=None):
    M = dy.shape[0]
    K = wg.shape[1]
    per = wg.shape[2] // tkc
    tm, tn = _tile(M, 1024, 16), _tile(K, 1024, LANES)
    return _mm(name, (dy, wg),
               [pl.BlockSpec((tm, tkc), lambda i, j, k: (i, k)),
                pl.BlockSpec((None, tn, tkc), lambda i, j, k: ((off + k) // per, j, (off + k) % per))],
               _sds((M, K), out_dtype), pl.BlockSpec((tm, tn), lambda i, j, k: (i, j)),
               (M // tm, K // tn, n_tiles), NT, (tm, tn), comm=comm)


def _mm_nt(name, dy, w, out_dtype, tn_pref=1024, comm=None):
    M, N = dy.shape
    Kw = w.shape[0]
    tm, tn, tk = _tile(M, 1024, 16), _tile(Kw, tn_pref, LANES), _tile(N, 2048, LANES)
    return _mm(name, (dy, w),
               [pl.BlockSpec((tm, tk), lambda i, j, k: (i, k)), pl.BlockSpec((tn, tk), lambda i, j, k: (j, k))],
               _sds((M, Kw), out_dtype), pl.BlockSpec((tm, tn), lambda i, j, k: (i, j)),
               (M // tm, Kw // tn, N // tk), NT, (tm, tn), comm=comm)


def _mm_tn_cols(name, a, dy, ns, cw, off, n_tiles, tn, prev=None, comm=None):
    M, K = a.shape
    per = cw // tn
    tkr, tk = _tile(K, 1024, LANES), _tile(M, 1024, 16)
    ops = [a, dy]
    specs = [pl.BlockSpec((tk, tkr), lambda i, j, k: (k, i)), pl.BlockSpec((tk, tn), lambda i, j, k: (k, j))]
    aliases = None
    if prev is not None:
        ops.append(prev)
        specs.append(pl.BlockSpec(memory_space=pl.ANY))
        aliases = {2: 0}
    return _mm(name, tuple(ops), specs, _sds((ns, K, cw), BF16),
               pl.BlockSpec((None, tkr, tn), lambda i, j, k: ((off + j) // per, i, (off + j) % per)),
               (K // tkr, n_tiles, M // tk), TN, (tkr, tn), aliases=aliases, comm=comm)


def _mm_tn(name, a, dy, tkr_pref=1024, comm=None):
    M, Kw = a.shape
    N = dy.shape[1]
    tkr, tn, tk = _tile(Kw, tkr_pref, LANES), _tile(N, 1024, LANES), _tile(M, 1024, 16)
    return _mm(name, (a, dy),
               [pl.BlockSpec((tk, tkr), lambda i, j, k: (k, i)), pl.BlockSpec((tk, tn), lambda i, j, k: (k, j))],
               _sds((Kw, N), BF16), pl.BlockSpec((tkr, tn), lambda i, j, k: (i, j)),
               (Kw // tkr, N // tn, M // tk), TN, (tkr, tn), comm=comm)


def _rmsnorm_fwd(name, x, g):
    T, D = x.shape
    tr = _tile(T, 512, 16)

    def body(x_ref, g_ref, h_ref):
        xf = x_ref[...]
        r = lax.rsqrt(jnp.mean(xf * xf, axis=-1, keepdims=True) + NORM_EPS)
        h_ref[...] = (xf * r * g_ref[...]).astype(h_ref.dtype)

    return pl.pallas_call(
        body, grid=(T // tr,),
        in_specs=[pl.BlockSpec((tr, D), lambda i: (i, 0)), pl.BlockSpec((1, D), lambda i: (0, 0))],
        out_specs=pl.BlockSpec((tr, D), lambda i: (i, 0)), out_shape=_sds((T, D), BF16),
        name=name, compiler_params=_cp("parallel"))(x, g.reshape(1, D))


def _rmsnorm_bwd(name, x, g, dhs, dres):
    T, D = x.shape
    tr = _tile(T, 256, 16)
    n_dh = len(dhs)

    def body(*refs):
        x_ref, g_ref = refs[0], refs[1]
        dh_refs = refs[2:2 + n_dh]
        dres_ref = refs[2 + n_dh]
        dx_ref, dxb_ref, dg_ref = refs[3 + n_dh], refs[4 + n_dh], refs[5 + n_dh]
        xf = x_ref[...]
        r = lax.rsqrt(jnp.mean(xf * xf, axis=-1, keepdims=True) + NORM_EPS)
        xhat = xf * r
        dh = dh_refs[0][...].astype(F32)
        for q in dh_refs[1:]:
            dh = dh + q[...].astype(F32)
        dy = dh * g_ref[...]
        c = jnp.mean(dy * xhat, axis=-1, keepdims=True)
        dx = dres_ref[...] + r * (dy - xhat * c)
        dx_ref[...] = dx
        dxb_ref[...] = dx.astype(BF16)

        @pl.when(pl.program_id(0) == 0)
        def _():
            dg_ref[...] = jnp.zeros_like(dg_ref)

        dg_ref[...] += jnp.sum(dh * xhat, axis=0, keepdims=True)

    row = pl.BlockSpec((tr, D), lambda i: (i, 0))
    vec = pl.BlockSpec((1, D), lambda i: (0, 0))
    dx, dx_b, dg = pl.pallas_call(
        body, grid=(T // tr,), in_specs=[row, vec] + [row] * n_dh + [row],
        out_specs=[row, row, vec], out_shape=[_sds((T, D), F32), _sds((T, D), BF16), _sds((1, D), F32)],
        name=name, compiler_params=_cp("arbitrary"))(x, g.reshape(1, D), *dhs, dres)
    return dx, dx_b, dg[0]


def _final_loss(name, x, g, tgt):
    T, D = x.shape
    tr = _tile(T, 256, 16)

    def body(x_ref, g_ref, t_ref, dx_ref, dxb_ref, dg_ref, loss_ref):
        xf = x_ref[...]
        r = lax.rsqrt(jnp.mean(xf * xf, axis=-1, keepdims=True) + NORM_EPS)
        xhat = xf * r
        err = xhat * g_ref[...] - t_ref[...]
        dy = err * (1.0 / D)
        dxh = dy * g_ref[...]
        c = jnp.mean(dxh * xhat, axis=-1, keepdims=True)
        dx = r * (dxh - xhat * c)
        dx_ref[...] = dx
        dxb_ref[...] = dx.astype(BF16)

        @pl.when(pl.program_id(0) == 0)
        def _():
            dg_ref[...] = jnp.zeros_like(dg_ref)
            loss_ref[...] = jnp.zeros_like(loss_ref)

        dg_ref[...] += jnp.sum(dy * xhat, axis=0, keepdims=True)
        loss_ref[...] += 0.5 * jnp.sum(jnp.mean(err * err, axis=-1, keepdims=True), axis=0, keepdims=True)

    row = pl.BlockSpec((tr, D), lambda i: (i, 0))
    vec = pl.BlockSpec((1, D), lambda i: (0, 0))
    dx, dx_b, dg, loss = pl.pallas_call(
        body, grid=(T // tr,), in_specs=[row, vec, row],
        out_specs=[row, row, vec, pl.BlockSpec((1, 1), lambda i: (0, 0))],
        out_shape=[_sds((T, D), F32), _sds((T, D), BF16), _sds((1, D), F32), _sds((1, 1), F32)],
        name=name, compiler_params=_cp("arbitrary"))(x, g.reshape(1, D), tgt)
    return loss[0, 0], dx, dx_b, dg[0]


def _halo_specs(tr, tc, n_rows, col):
    rb = tr // HALO
    last = n_rows // HALO - 1
    return [pl.BlockSpec((tr, tc), lambda *g: (g[-1], col(*g))),
            pl.BlockSpec((HALO, tc), lambda *g: (jnp.maximum(g[-1] * rb - 1, 0), col(*g))),
            pl.BlockSpec((HALO, tc), lambda *g: (jnp.minimum((g[-1] + 1) * rb, last), col(*g)))]


def _ext(cur_ref, prev_ref, next_ref, i, n_i):
    p = prev_ref[...].astype(F32) * (i > 0).astype(F32)
    n = next_ref[...].astype(F32) * (i < n_i - 1).astype(F32)
    return jnp.concatenate([p, cur_ref[...].astype(F32), n], axis=0)


def _shift_dn(x):
    return pltpu.roll(x, 1, axis=0)


def _shift_up(x):
    return pltpu.roll(x, x.shape[0] - 1, axis=0)


def _conv(x, w_ref, b_ref):
    return w_ref[0:1, :] * _shift_dn(x) + w_ref[1:2, :] * x + w_ref[2:3, :] * _shift_up(x) + b_ref[...]


def _conv_t(g, w_ref):
    return w_ref[0:1, :] * _shift_up(g) + w_ref[1:2, :] * g + w_ref[2:3, :] * _shift_dn(g)


def _mid(x, tr):
    return x[HALO:HALO + tr, :]


def _conv_wgrad(acc_ref, g, x, tr, first):
    gm = _mid(g, tr)

    @pl.when(first)
    def _():
        acc_ref[...] = jnp.zeros_like(acc_ref)

    acc_ref[0:1, :] += jnp.sum(gm * _mid(_shift_dn(x), tr), axis=0, keepdims=True)
    acc_ref[1:2, :] += jnp.sum(gm * _mid(x, tr), axis=0, keepdims=True)
    acc_ref[2:3, :] += jnp.sum(gm * _mid(_shift_up(x), tr), axis=0, keepdims=True)
    acc_ref[3:4, :] += jnp.sum(gm, axis=0, keepdims=True)


def _sigmoid(a):
    return 1.0 / (1.0 + jnp.exp(-a))


def _ffn_gate_fwd(name, up, cw, cb):
    T, F2 = up.shape
    F = F2 // 2
    tc, tr = _tile(F, 512, LANES), _tile(T, 512, HALO)
    nF, n_i = F // tc, T // tr

    def body(ac, ap, an, bc, bp, bn, wa, wb, ba, bb, o_ref):
        i = pl.program_id(1)
        ua = _mid(_conv(_ext(ac, ap, an, i, n_i), wa, ba), tr)
        ub = _mid(_conv(_ext(bc, bp, bn, i, n_i), wb, bb), tr)
        o_ref[...] = (ua * _sigmoid(ua) * ub).astype(o_ref.dtype)

    wspec = lambda o: pl.BlockSpec((3, tc), lambda j, i: (0, j + o))
    bspec = lambda o: pl.BlockSpec((1, tc), lambda j, i: (0, j + o))
    return pl.pallas_call(
        body, grid=(nF, n_i),
        in_specs=_halo_specs(tr, tc, T, lambda j, i: j) + _halo_specs(tr, tc, T, lambda j, i: j + nF)
        + [wspec(0), wspec(nF), bspec(0), bspec(nF)],
        out_specs=pl.BlockSpec((tr, tc), lambda j, i: (i, j)), out_shape=_sds((T, F), BF16),
        name=name, compiler_params=_cp("parallel", "parallel"))(up, up, up, up, up, up, cw, cw, cb, cb)


def _ffn_gate_bwd(name, up, dact, cw, cb):
    T, F2 = up.shape
    F = F2 // 2
    tc, tr = _tile(F, 512, LANES), _tile(T, 256, HALO)
    nF, n_i = F // tc, T // tr

    def body(ac, ap, an, bc, bp, bn, dc, dp, dn, wa, wb, ba, bb, o_ref, wg_ref):
        j, i = pl.program_id(0), pl.program_id(1)
        xa = _ext(ac, ap, an, i, n_i)
        xb = _ext(bc, bp, bn, i, n_i)
        da = _ext(dc, dp, dn, i, n_i)
        ua = _conv(xa, wa, ba)
        sig = _sigmoid(ua)

        @pl.when(j < nF)
        def _():
            g = da * _conv(xb, wb, bb) * (sig * (1.0 + ua * (1.0 - sig)))
            o_ref[...] = _mid(_conv_t(g, wa), tr).astype(o_ref.dtype)
            _conv_wgrad(wg_ref, g, xa, tr, i == 0)

        @pl.when(j >= nF)
        def _():
            g = da * (ua * sig)
            o_ref[...] = _mid(_conv_t(g, wb), tr).astype(o_ref.dtype)
            _conv_wgrad(wg_ref, g, xb, tr, i == 0)

    wspec = lambda o: pl.BlockSpec((3, tc), lambda j, i: (0, j % nF + o))
    bspec = lambda o: pl.BlockSpec((1, tc), lambda j, i: (0, j % nF + o))
    return pl.pallas_call(
        body, grid=(2 * nF, n_i),
        in_specs=_halo_specs(tr, tc, T, lambda j, i: j % nF) + _halo_specs(tr, tc, T, lambda j, i: j % nF + nF)
        + _halo_specs(tr, tc, T, lambda j, i: j % nF) + [wspec(0), wspec(nF), bspec(0), bspec(nF)],
        out_specs=[pl.BlockSpec((tr, tc), lambda j, i: (i, j)), pl.BlockSpec((8, tc), lambda j, i: (0, j))],
        out_shape=[_sds((T, F2), BF16), _sds((8, F2), F32)],
        name=name, compiler_params=_cp("parallel", "arbitrary"),
    )(up, up, up, up, up, up, dact, dact, dact, cw, cw, cb, cb)


def _sc_gate_fwd(name, z, cw, cb):
    T, D3 = z.shape
    D = D3 // 3
    tc, tr = _tile(D, 512, LANES), _tile(T, 512, HALO)
    nD, n_i = D // tc, T // tr

    def body(uc, up_, un, gb, cc, cp, cn, w, b, o_ref):
        i = pl.program_id(1)
        cu = _ext(cc, cp, cn, i, n_i) * _ext(uc, up_, un, i, n_i)
        o_ref[...] = (gb[...].astype(F32) * _mid(_conv(cu, w, b), tr)).astype(o_ref.dtype)

    return pl.pallas_call(
        body, grid=(nD, n_i),
        in_specs=_halo_specs(tr, tc, T, lambda j, i: j) + [pl.BlockSpec((tr, tc), lambda j, i: (i, j + nD))]
        + _halo_specs(tr, tc, T, lambda j, i: j + 2 * nD)
        + [pl.BlockSpec((3, tc), lambda j, i: (0, j)), pl.BlockSpec((1, tc), lambda j, i: (0, j))],
        out_specs=pl.BlockSpec((tr, tc), lambda j, i: (i, j)), out_shape=_sds((T, D), BF16),
        name=name, compiler_params=_cp("parallel", "parallel"))(z, z, z, z, z, z, z, cw, cb)


def _sc_gate_bwd(name, z, dy, cw, cb):
    T, D3 = z.shape
    D = D3 // 3
    tc, tr = _tile(D, 512, LANES), _tile(T, 256, HALO)
    nD, n_i = D // tc, T // tr

    def body(uc, up_, un, bc, bp, bn, cc, cp, cn, yc, yp, yn, w, b, o_ref, wg_ref):
        kind, i = pl.program_id(0), pl.program_id(2)
        u = _ext(uc, up_, un, i, n_i)
        gc = _ext(cc, cp, cn, i, n_i)
        cu = gc * u
        g = _ext(yc, yp, yn, i, n_i) * _ext(bc, bp, bn, i, n_i)

        @pl.when(kind == 0)
        def _():
            o_ref[...] = (_mid(_conv_t(g, w), tr) * _mid(gc, tr)).astype(o_ref.dtype)
            _conv_wgrad(wg_ref, g, cu, tr, i == 0)

        @pl.when(kind == 1)
        def _():
            o_ref[...] = (yc[...].astype(F32) * _mid(_conv(cu, w, b), tr)).astype(o_ref.dtype)

            @pl.when(i == 0)
            def _():
                wg_ref[...] = jnp.zeros_like(wg_ref)

        @pl.when(kind == 2)
        def _():
            o_ref[...] = (_mid(_conv_t(g, w), tr) * _mid(u, tr)).astype(o_ref.dtype)

            @pl.when(i == 0)
            def _():
                wg_ref[...] = jnp.zeros_like(wg_ref)

    hs = lambda o: _halo_specs(tr, tc, T, lambda k, j, i: j + o)
    out, wg = pl.pallas_call(
        body, grid=(3, nD, n_i),
        in_specs=hs(0) + hs(nD) + hs(2 * nD) + hs(0)
        + [pl.BlockSpec((3, tc), lambda k, j, i: (0, j)), pl.BlockSpec((1, tc), lambda k, j, i: (0, j))],
        out_specs=[pl.BlockSpec((tr, tc), lambda k, j, i: (i, k * nD + j)),
                   pl.BlockSpec((None, 8, tc), lambda k, j, i: (k, 0, j))],
        out_shape=[_sds((T, D3), BF16), _sds((3, 8, D), F32)],
        name=name, compiler_params=_cp("parallel", "parallel", "arbitrary"),
    )(z, z, z, z, z, z, z, z, z, dy, dy, dy, cw, cb)
    return out, wg[0]


def _slopes(n_heads):
    return jnp.asarray(2.0 ** (-ALIBI_MAX * np.arange(1, n_heads + 1) / n_heads), dtype=F32)


def _nq(L):
    return max(1, min(4, L // LANES // 2))


def _srows(ref, r, start, n, d):
    if d == 1:
        return ref[start:start + n, :]
    return ref[pl.ds(start * d + r, n, stride=d), :]


def _win(p_ref, c_ref, n_ref, r, b, nq):
    lo, hi, top = b * LANES - BAND, b * LANES + LANES + BAND, nq * LANES
    parts = [p_ref[r]] if lo < 0 else []
    parts.append(c_ref[r, max(lo, 0):min(hi, top), :])
    if hi > top:
        parts.append(n_ref[r])
    return parts[0] if len(parts) == 1 else jnp.concatenate(parts, axis=0)


def _nat_win(p_ref, c_ref, n_ref, r, b, nq, d):
    lo, hi, top = b * LANES - BAND, b * LANES + LANES + BAND, nq * LANES
    parts = [_srows(p_ref, r, 0, BAND, d)] if lo < 0 else []
    parts.append(_srows(c_ref, r, max(lo, 0), min(hi, top) - max(lo, 0), d))
    if hi > top:
        parts.append(_srows(n_ref, r, 0, BAND, d))
    return parts[0] if len(parts) == 1 else jnp.concatenate(parts, axis=0)


def _biases(slope, d, n, n_steps, nq, q_rows, k_rows, q0, k0):
    qi = lax.broadcasted_iota(jnp.int32, (q_rows, k_rows), 0) + q0
    kj = lax.broadcasted_iota(jnp.int32, (q_rows, k_rows), 1) + k0
    dist = jnp.abs(kj - qi)
    base = jnp.where(dist <= BAND, -slope * (dist * d).astype(F32), NEG_INF)
    out = []
    for b in range(nq):
        t = base
        if b == 0:
            t = jnp.where((n == 0) & ((kj < 0) | (qi < 0)), NEG_INF, t)
        if b == nq - 1:
            t = jnp.where((n == n_steps - 1) & ((kj >= LANES) | (qi >= LANES)), NEG_INF, t)
        out.append(t)
    return out


def _win_specs(d, H, col, nq, L):
    return [pl.BlockSpec((d, BAND, LANES), lambda h, n: (0, jnp.maximum(2 * nq * n - 1, 0), col * H + h)),
            pl.BlockSpec((d, nq * LANES, LANES), lambda h, n: (0, n, col * H + h)),
            pl.BlockSpec((d, BAND, LANES), lambda h, n: (0, jnp.minimum(2 * nq * (n + 1), L // BAND - 1), col * H + h))]


def _nat_specs(d, nq, L):
    return [pl.BlockSpec((BAND * d, LANES), lambda h, n: (jnp.maximum(2 * nq * n - 1, 0), h)),
            pl.BlockSpec((nq * LANES * d, LANES), lambda h, n: (n, h)),
            pl.BlockSpec((BAND * d, LANES), lambda h, n: (jnp.minimum(2 * nq * (n + 1), L // BAND - 1), h))]


def _over_residues(d, per_r):
    if d == 1:
        per_r(0, 0)
    else:
        lax.fori_loop(0, d, per_r, 0, unroll=2)


def _attn_fwd(name, qkv, d, H):
    T = qkv.shape[0]
    D = H * HEAD_DIM
    L = T // d
    nq = _nq(L)
    n_steps = L // (nq * LANES)
    scale = HEAD_DIM ** -0.5
    q3 = qkv.reshape(d, L, 3 * D)

    def body(s_ref, q_ref, kp, kc, kn, vp, vc, vn, o_ref, l_ref):
        h, n = pl.program_id(0), pl.program_id(1)
        bias = _biases(s_ref[h], d, n, n_steps, nq, LANES, 2 * LANES, 0, -BAND)

        def per_r(r, carry):
            for b in range(nq):
                k, v = _win(kp, kc, kn, r, b, nq), _win(vp, vc, vn, r, b, nq)
                s = lax.dot_general(q_ref[r, b * LANES:(b + 1) * LANES, :], k, NT, preferred_element_type=F32) * scale + bias[b]
                m = jnp.max(s, axis=1, keepdims=True)
                p = jnp.exp(s - m)
                den = jnp.sum(p, axis=1, keepdims=True)
                o = lax.dot_general(p.astype(BF16), v, NN, preferred_element_type=F32) / den
                lse = jnp.broadcast_to(m + jnp.log(den), (LANES, LANES))
                if d == 1:
                    o_ref[b * LANES:(b + 1) * LANES, :] = o
                    l_ref[b * LANES:(b + 1) * LANES, :] = lse
                else:
                    o_ref[pl.ds(b * LANES * d + r, LANES, stride=d), :] = o
                    l_ref[pl.ds(b * LANES * d + r, LANES, stride=d), :] = lse
            return carry

        _over_residues(d, per_r)

    out = pl.BlockSpec((nq * LANES * d, LANES), lambda h, n: (n, h))
    return pl.pallas_call(
        body, grid=(H, n_steps),
        in_specs=[pl.BlockSpec(memory_space=pltpu.SMEM), pl.BlockSpec((d, nq * LANES, LANES), lambda h, n: (0, n, h))]
        + _win_specs(d, H, 1, nq, L) + _win_specs(d, H, 2, nq, L),
        out_specs=[out, out], out_shape=[_sds((T, D), F32), _sds((T, D), F32)],
        name=name, compiler_params=_cp("parallel", "parallel"))(_slopes(H), q3, q3, q3, q3, q3, q3, q3)


def _attn_combine(name, outs, lses):
    T, D = outs[0].shape
    tr, tc = _tile(T, 512, 16), _tile(D, 512, LANES)

    def body(o0, o1, o2, l0, l1, l2, ob_ref, of_ref, l_ref):
        a0, a1, a2 = l0[...], l1[...], l2[...]
        m = jnp.maximum(jnp.maximum(a0, a1), a2)
        e0, e1, e2 = jnp.exp(a0 - m), jnp.exp(a1 - m), jnp.exp(a2 - m)
        z = e0 + e1 + e2
        o = (e0 * o0[...] + e1 * o1[...] + e2 * o2[...]) / z
        of_ref[...] = o
        ob_ref[...] = o.astype(BF16)
        l_ref[...] = m + jnp.log(z)

    blk = pl.BlockSpec((tr, tc), lambda i, j: (i, j))
    return pl.pallas_call(
        body, grid=(T // tr, D // tc), in_specs=[blk] * 6, out_specs=[blk] * 3,
        out_shape=[_sds((T, D), BF16), _sds((T, D), F32), _sds((T, D), F32)],
        name=name, compiler_params=_cp("parallel", "parallel"))(*outs, *lses)


def _attn_stats(name, do, o, lse):
    T, D = do.shape
    tr = _tile(T, 1024, 16)

    def body(a, b, l, o_ref):
        delta = jnp.broadcast_to(jnp.sum(a[...] * b[...], axis=1, keepdims=True), o_ref.shape)
        lane = lax.broadcasted_iota(jnp.int32, o_ref.shape, 1)
        o_ref[...] = jnp.where(lane < BAND, l[...], delta)

    blk = pl.BlockSpec((tr, LANES), lambda i, j: (i, j))
    return pl.pallas_call(body, grid=(T // tr, D // LANES), in_specs=[blk, blk, blk], out_specs=blk,
                          out_shape=_sds((T, D), F32), name=name, compiler_params=_cp("parallel", "parallel"))(do, o, lse)


def _attn_bwd(name, qkv, do, stats, d, H):
    T = qkv.shape[0]
    D = H * HEAD_DIM
    L = T // d
    nq = _nq(L)
    n_steps = L // (nq * LANES)
    scale = HEAD_DIM ** -0.5
    q3 = qkv.reshape(d, L, 3 * D)
    mid = slice(BAND, BAND + LANES)

    def body(s_ref, qp, qc, qn, kp, kc, kn, vp, vc, vn, gp, gc, gn, tp, tc_, tn_, dq_ref, dk_ref, dv_ref):
        h, n = pl.program_id(0), pl.program_id(1)
        bias_q = _biases(s_ref[h], d, n, n_steps, nq, LANES, 2 * LANES, 0, -BAND)
        bias_k = _biases(s_ref[h], d, n, n_steps, nq, 2 * LANES, LANES, -BAND, 0)

        def per_r(r, carry):
            for b in range(nq):
                rows = slice(b * LANES, (b + 1) * LANES)
                q_w, k_w, v_w = _win(qp, qc, qn, r, b, nq), _win(kp, kc, kn, r, b, nq), _win(vp, vc, vn, r, b, nq)
                g_w = _nat_win(gp, gc, gn, r, b, nq, d)
                t_w = _nat_win(tp, tc_, tn_, r, b, nq, d)
                g_b = g_w.astype(BF16)
                q_c, k_c, v_c, g_c, t_c = q_w[mid], k_w[mid], v_w[mid], g_b[mid], t_w[mid]
                s = lax.dot_general(q_c, k_w, NT, preferred_element_type=F32) * scale + bias_q[b]
                p = jnp.exp(s - t_c[:, 0:1])
                dp = lax.dot_general(g_c, v_w, NT, preferred_element_type=F32)
                ds = p * (dp - t_c[:, BAND:BAND + 1])
                dq_ref[r, rows, :] = (lax.dot_general(ds.astype(BF16), k_w, NN, preferred_element_type=F32) * scale).astype(BF16)
                s2 = lax.dot_general(q_w, k_c, NT, preferred_element_type=F32) * scale + bias_k[b]
                p2 = jnp.exp(s2 - t_w[:, 0:1])
                dv_ref[r, rows, :] = lax.dot_general(p2.astype(BF16), g_b, TN, preferred_element_type=F32).astype(BF16)
                dp2 = lax.dot_general(g_b, v_c, NT, preferred_element_type=F32)
                ds2 = p2 * (dp2 - t_w[:, BAND:BAND + 1])
                dk_ref[r, rows, :] = (lax.dot_general(ds2.astype(BF16), q_w, TN, preferred_element_type=F32) * scale).astype(BF16)
            return carry

        _over_residues(d, per_r)

    out = pl.BlockSpec((d, nq * LANES, LANES), lambda h, n: (0, n, h))
    dq, dk, dv = pl.pallas_call(
        body, grid=(H, n_steps),
        in_specs=[pl.BlockSpec(memory_space=pltpu.SMEM)]
        + _win_specs(d, H, 0, nq, L) + _win_specs(d, H, 1, nq, L) + _win_specs(d, H, 2, nq, L)
        + _nat_specs(d, nq, L) + _nat_specs(d, nq, L),
        out_specs=[out, out, out], out_shape=[_sds((d, L, D), BF16)] * 3,
        name=name, compiler_params=_cp("parallel", "parallel"),
    )(_slopes(H), *([q3] * 9), *([do] * 3), *([stats] * 3))
    return dq.reshape(T, D), dk.reshape(T, D), dv.reshape(T, D)


def _to_group_order(a, d):
    if d == 1:
        return a
    T, C = a.shape
    return a.reshape(T // d, d, C).swapaxes(0, 1).reshape(T, C)


def _from_group_order(a, d):
    if d == 1:
        return a
    T, C = a.shape
    return a.reshape(d, T // d, C).swapaxes(0, 1).reshape(T, C)


def _fwd_bwd(x, tgt, S, ex):
    T, D = x.shape
    H = D // HEAD_DIM
    ns = ex.w("in").shape[0]
    cw_in, cw_qkv, cw_up = ex.cw("in"), ex.cw("qkv"), ex.cw("up0")
    tq = math.gcd(3 * D, cw_qkv)
    ntg = 3 * D // tq

    def mm(fn, *args, rides=(), **kw):
        out, extra = fn(*args, comm=_join([getattr(ex, kind)(keys) for kind, keys in rides]), **kw)
        at = 0
        for kind, keys in rides:
            getattr(ex, kind + "_done")(keys, extra[at:at + len(keys)])
            at += len(keys)
        return out

    def ffn_fwd(l, xin, rides_up, rides_dn):
        hf = _rmsnorm_fwd(f"ffn_norm{l}", xin, S["ffn_g"][l])
        up = mm(_mm_nn_cols, f"ffn_up{l}", hf, ex.w(f"up{l}"), 0, ns, cw_up, BF16, rides=rides_up)
        act = _ffn_gate_fwd(f"ffn_gate{l}", up, S["ffn_cw"][l], S["ffn_cb"][l][None])
        return hf, up, act, mm(_mm_nn, f"ffn_down{l}", act, ex.w(f"dn{l}"), xin, F32, rides=rides_dn)

    def ffn_bwd(l, xin, hf, up, act, dxo, dxo_b, rides):
        dact = mm(_mm_nt, f"ffn_down_dx{l}", dxo_b, ex.w(f"dn{l}"), BF16, tn_pref=512, rides=rides[0])
        ex.grad(f"dn{l}", mm(_mm_tn, f"ffn_down_dw{l}", act, dxo_b, tkr_pref=1408))
        dup, cg = _ffn_gate_bwd(f"ffn_gate_bwd{l}", up, dact, S["ffn_cw"][l], S["ffn_cb"][l][None])
        dhf = mm(_mm_nt_cols, f"ffn_up_dx{l}", dup, ex.w(f"up{l}"), 0, ns, cw_up, F32, rides=rides[1])
        ex.grad(f"up{l}", mm(_mm_tn_cols, f"ffn_up_dw{l}", hf, dup, ns, cw_up, 0, ns, cw_up, rides=rides[2]))
        dx, dx_b, dg = _rmsnorm_bwd(f"ffn_norm_bwd{l}", xin, S["ffn_g"][l], [dhf], dxo)
        return dx, dx_b, dg, cg

    h0 = _rmsnorm_fwd("mix_norm0", x, S["mix_g"][0])
    z = mm(_mm_nn_cols, "sc_in", h0, ex.w("in"), 0, ns, cw_in, BF16, rides=[("ag", ["sco", "up0"])])
    y = _sc_gate_fwd("sc_gate", z, S["sc_cw"], S["sc_cb"][None])
    x1 = mm(_mm_nn, "sc_out", y, ex.w("sco"), x, F32)
    hf0, up0, act0, x2 = ffn_fwd(0, x1, [("ag", ["dn0", "qkv"])], [("ag", ["ao", "up1"])])
    h1 = _rmsnorm_fwd("mix_norm1", x2, S["mix_g"][1])
    hd, qkv, outs, lses = [], [], [], []
    for g, d in enumerate(DILATIONS):
        hd.append(_to_group_order(h1, d))
        qkv.append(mm(_mm_nn_cols, f"attn_qkv{g}", hd[g], ex.w("qkv"), g * ntg, ntg, tq, BF16,
                      rides=[("ag", ["dn1"])] if g == 0 else ()))
        o_g, l_g = _attn_fwd(f"attn_fwd{g}", qkv[g], d, H)
        outs.append(o_g)
        lses.append(l_g)
    o_b, o_f, lse = _attn_combine("attn_combine", outs, lses)
    x3 = mm(_mm_nn, "attn_out", o_b, ex.w("ao"), x2, F32)
    hf1, up1, act1, x4 = ffn_fwd(1, x3, (), ())
    loss, dx4, dx4_b, dg_fin = _final_loss("final_loss", x4, S["fin_g"], tgt)

    dx3, dx3_b, dg_f1, cg1 = ffn_bwd(1, x3, hf1, up1, act1, dx4, dx4_b,
                                     [(), [("pair", ["dn1"])], [("chip", ["dn1"])]])
    do = mm(_mm_nt, "attn_out_dx", dx3_b, ex.w("ao"), F32, rides=[("pair", ["up1"])])
    ex.grad("ao", mm(_mm_tn, "attn_out_dw", o_b, dx3_b))
    stats = _attn_stats("attn_stats", do, o_f, lse)
    dhs, dw_qkv = [], None
    qkv_rides = [[("chip", ["up1"]), ("pair", ["ao"])], [("chip", ["ao"])], ()]
    for g, d in enumerate(DILATIONS):
        dq, dk, dv = _attn_bwd(f"attn_bwd{g}", qkv[g], do, stats, d, H)
        dqkv = jnp.concatenate([dq, dk, dv], axis=1)
        dhs.append(_from_group_order(mm(_mm_nt_cols, f"attn_qkv_dx{g}", dqkv, ex.w("qkv"), g * ntg, ntg, tq, F32,
                                        rides=qkv_rides[g]), d))
        dw_qkv = mm(_mm_tn_cols, f"attn_qkv_dw{g}", hd[g], dqkv, ns, cw_qkv, g * ntg, ntg, tq, prev=dw_qkv)
    ex.grad("qkv", dw_qkv)
    dx2, dx2_b, dg_m1 = _rmsnorm_bwd("mix_norm_bwd1", x2, S["mix_g"][1], dhs, dx3)
    dx1, dx1_b, dg_f0, cg0 = ffn_bwd(0, x1, hf0, up0, act0, dx2, dx2_b,
                                     [[("pair", ["qkv"])], [("chip", ["qkv"]), ("pair", ["dn0"])], [("chip", ["dn0"])]])
    dy = mm(_mm_nt, "sc_out_dx", dx1_b, ex.w("sco"), BF16, rides=[("pair", ["up0"])])
    ex.grad("sco", mm(_mm_tn, "sc_out_dw", y, dx1_b))
    dz, cg_sc = _sc_gate_bwd("sc_gate_bwd", z, dy, S["sc_cw"], S["sc_cb"][None])
    dh0 = mm(_mm_nt_cols, "sc_in_dx", dz, ex.w("in"), 0, ns, cw_in, F32, rides=[("chip", ["up0"]), ("pair", ["sco"])])
    ex.grad("in", mm(_mm_tn_cols, "sc_in_dw", h0, dz, ns, cw_in, 0, ns, cw_in, rides=[("chip", ["sco"])]))
    dx0, _, dg_m0 = _rmsnorm_bwd("mix_norm_bwd0", x, S["mix_g"][0], [dh0], dx1)

    dS = {"mix_g": jnp.stack([dg_m0, dg_m1]), "ffn_g": jnp.stack([dg_f0, dg_f1]), "fin_g": dg_fin,
          "sc_cw": cg_sc[0:3], "sc_cb": cg_sc[3], "ffn_cw": jnp.stack([cg0[0:3], cg1[0:3]]),
          "ffn_cb": jnp.stack([cg0[3], cg1[3]])}
    return loss, dx0, dS


def _place():
    x, y, c = lax.axis_index("x"), lax.axis_index("y"), lax.axis_index("c")
    return x, y, c, [(1 - x, y), (x, 1 - y), (1 - x, 1 - y)]


def _ag_stage(arrs):
    n = len(arrs)

    def plan(ins, outs, sems):
        send_sems, recv_sems, local_sems = sems
        x, y, c, chips = _place()
        me, sibling = 4 * x + 2 * y + c, (x, y, 1 - c)

        def copy(a, k, blk, to, src=None):
            dst = outs[a].at[blk]
            return pltpu.make_async_remote_copy(src_ref=dst if src is None else src, dst_ref=dst,
                                                send_sem=send_sems.at[a, k], recv_sem=recv_sems.at[a, k],
                                                device_id=to, device_id_type=MESH)

        mine = [pltpu.make_async_copy(ins[a], outs[a].at[me], local_sems.at[a]) for a in range(n)]
        first = []
        for a in range(n):
            first.append(copy(a, 0, me, sibling, src=ins[a]))
            first += [copy(a, 1 + j, me, (*chip, c), src=ins[a]) for j, chip in enumerate(chips)]
        return x, y, c, chips, sibling, copy, mine, first

    def start(ins, outs, sems):
        *_, mine, first = plan(ins, outs, sems)
        for cp in mine + first:
            cp.start()

    def finish(ins, outs, sems):
        x, y, c, chips, sibling, copy, mine, first = plan(ins, outs, sems)
        passed = []
        for j, (px, py) in enumerate(chips):
            for a in range(n):
                blk = 4 * px + 2 * py + c
                copy(a, 1 + j, blk, sibling).wait_recv()
                passed.append(copy(a, 4 + j, blk, sibling))
                passed[-1].start()
        for a in range(n):
            copy(a, 0, 4 * x + 2 * y + 1 - c, sibling).wait_recv()
            for j, (px, py) in enumerate(chips):
                copy(a, 4 + j, 4 * px + 2 * py + 1 - c, sibling).wait_recv()
        for cp in first + passed:
            cp.wait_send()
        for cp in mine:
            cp.wait()

    return _Stage(list(arrs), [_sds((N_DEV,) + a.shape, a.dtype) for a in arrs],
                  [pltpu.SemaphoreType.DMA((n, 7)), pltpu.SemaphoreType.DMA((n, 7)), pltpu.SemaphoreType.DMA((n,))],
                  start, finish)


def _pair_stage(slabs):
    n = len(slabs)

    def copies(ins, outs, sems):
        send_sems, recv_sems = sems
        x, y, c, _ = _place()
        return [pltpu.make_async_remote_copy(src_ref=ins[a].at[2 * k + 1 - c], dst_ref=outs[a].at[k],
                                             send_sem=send_sems.at[a, k], recv_sem=recv_sems.at[a, k],
                                             device_id=(x, y, 1 - c), device_id_type=MESH)
                for a in range(n) for k in range(4)]

    def start(ins, outs, sems):
        for cp in copies(ins, outs, sems):
            cp.start()

    def finish(ins, outs, sems):
        for cp in copies(ins, outs, sems):
            cp.wait()

    return _Stage(list(slabs), [_sds((4,) + a.shape[1:], a.dtype) for a in slabs],
                  [pltpu.SemaphoreType.DMA((n, 4)), pltpu.SemaphoreType.DMA((n, 4))], start, finish)


def _rs_pair_add(name, slabs, got, c_arr):
    _, R, C = slabs.shape
    tr, tc = _tile(R, 512, 16), _tile(C, 1536, LANES)

    def body(c_ref, a_ref, b_ref, o_ref):
        o_ref[...] = (a_ref[...].astype(F32) + b_ref[...].astype(F32)).astype(o_ref.dtype)

    return pl.pallas_call(
        body,
        grid_spec=pltpu.PrefetchScalarGridSpec(
            num_scalar_prefetch=1, grid=(4, R // tr, C // tc),
            in_specs=[pl.BlockSpec((None, tr, tc), lambda k, i, j, c_ref: (2 * k + c_ref[0], i, j)),
                      pl.BlockSpec((None, tr, tc), lambda k, i, j, c_ref: (k, i, j))],
            out_specs=pl.BlockSpec((None, tr, tc), lambda k, i, j, c_ref: (k, i, j))),
        out_shape=_sds((4, R, C), BF16), name=name,
        compiler_params=_cp("parallel", "parallel", "parallel"))(c_arr, slabs, got)


def _chip_stage(parts):
    n = len(parts)

    def plan(ins, outs, sems):
        send_sems, recv_sems, local_sems = sems
        x, y, c, chips = _place()
        my_chip = 2 * x + y
        mine = [pltpu.make_async_copy(ins[a].at[my_chip], outs[a].at[my_chip], local_sems.at[a]) for a in range(n)]
        sends = [pltpu.make_async_remote_copy(src_ref=ins[a].at[2 * px + py], dst_ref=outs[a].at[my_chip],
                                              send_sem=send_sems.at[a, j], recv_sem=recv_sems.at[a, j],
                                              device_id=(px, py, c), device_id_type=MESH)
                 for a in range(n) for j, (px, py) in enumerate(chips)]
        arrivals = lambda: [pltpu.make_async_remote_copy(src_ref=ins[a].at[my_chip], dst_ref=outs[a].at[2 * px + py],
                                                         send_sem=send_sems.at[a, j], recv_sem=recv_sems.at[a, j],
                                                         device_id=(px, py, c), device_id_type=MESH)
                            for a in range(n) for j, (px, py) in enumerate(chips)]
        return mine, sends, arrivals

    def start(ins, outs, sems):
        mine, sends, _ = plan(ins, outs, sems)
        for cp in mine + sends:
            cp.start()

    def finish(ins, outs, sems):
        mine, sends, arrivals = plan(ins, outs, sems)
        for cp in arrivals():
            cp.wait_recv()
        for cp in sends:
            cp.wait_send()
        for cp in mine:
            cp.wait()

    return _Stage(list(parts), [_sds(a.shape, a.dtype) for a in parts],
                  [pltpu.SemaphoreType.DMA((n, 3)), pltpu.SemaphoreType.DMA((n, 3)), pltpu.SemaphoreType.DMA((n,))],
                  start, finish)


class _Exchange:
    ROW_SHARDED = ("sco", "ao", "dn0", "dn1")

    def __init__(self, shards, c_arr):
        self.sh, self.c_arr = shards, c_arr
        self.W, self.dw, self.parts, self.sums = {}, {}, {}, {}

    def cw(self, key):
        return self.sh[key].shape[1]

    def w(self, key):
        return self.W[key]

    def ag(self, keys):
        return _ag_stage([self.sh[k] for k in keys])

    def ag_done(self, keys, outs):
        for k, g in zip(keys, outs):
            self.W[k] = g.reshape(g.shape[0] * g.shape[1], g.shape[2]) if k in self.ROW_SHARDED else g

    def grad(self, key, dw):
        self.dw[key] = dw if dw.ndim == 3 else dw.reshape((N_DEV,) + self.sh[key].shape)

    def pair(self, keys):
        return _pair_stage([self.dw[k] for k in keys])

    def pair_done(self, keys, outs):
        for k, got in zip(keys, outs):
            self.parts[k] = _rs_pair_add(f"rs_add_{k}", self.dw[k], got, self.c_arr)

    def chip(self, keys):
        return _chip_stage([self.parts[k] for k in keys])

    def chip_done(self, keys, outs):
        self.sums.update(zip(keys, outs))


def _sum_slots(name, a):
    _, rows, _ = a.shape

    def body(a_ref, o_ref):
        s = a_ref[0]
        for k in range(1, N_DEV):
            s = s + a_ref[k]
        o_ref[...] = s

    return pl.pallas_call(body, out_shape=_sds((rows, LANES), F32), name=name)(a)


def _cast_bf16(name, w3, l):
    _, R, C = w3.shape
    tr, tc = _tile(R, 512, 16), _tile(C, 1536, LANES)

    def body(w_ref, o_ref):
        o_ref[...] = w_ref[...].astype(BF16)

    return pl.pallas_call(
        body, grid=(R // tr, C // tc), in_specs=[pl.BlockSpec((None, tr, tc), lambda i, j: (l, i, j))],
        out_specs=pl.BlockSpec((tr, tc), lambda i, j: (i, j)), out_shape=_sds((R, C), BF16),
        name=name, compiler_params=_cp("parallel", "parallel"))(w3)


def _adamw(name, g_slots, w3, m3, v3, l, prev):
    n_slots, R, C = g_slots.shape
    tr, tc = _tile(R, 256, 8), _tile(C, 1536, LANES)
    c1, c2 = 1.0 - ADAM_B1 ** ADAM_STEP, 1.0 - ADAM_B2 ** ADAM_STEP

    def body(g_ref, w_ref, m_ref, v_ref, *rest):
        og, od, om, ov = rest[-4:]
        g = g_ref[0].astype(F32)
        for k in range(1, n_slots):
            g = g + g_ref[k].astype(F32)
        m = ADAM_B1 * m_ref[...] + (1.0 - ADAM_B1) * g
        v = ADAM_B2 * v_ref[...] + (1.0 - ADAM_B2) * (g * g)
        og[...] = g
        om[...] = m
        ov[...] = v
        od[...] = -ADAM_LR * ((m / c1) / (jnp.sqrt(v / c2) + ADAM_EPS) + ADAM_WD * w_ref[...])

    lay = pl.BlockSpec((None, tr, tc), lambda i, j: (l, i, j))
    ops = [g_slots, w3, m3, v3]
    specs = [pl.BlockSpec((n_slots, tr, tc), lambda i, j: (0, i, j)), lay, lay, lay]
    aliases = {}
    if prev is not None:
        ops += list(prev)
        specs += _any_specs(4)
        aliases = {4 + k: k for k in range(4)}
    return pl.pallas_call(
        body, grid=(R // tr, C // tc), in_specs=specs, out_specs=[lay] * 4,
        out_shape=[_sds(w3.shape, F32)] * 4, input_output_aliases=aliases,
        name=name, compiler_params=_cp("parallel", "parallel"))(*ops)


def _pack(parts):
    flat = jnp.concatenate([p.reshape(-1) for p in parts])
    pad = (-flat.shape[0]) % LANES
    return jnp.pad(flat, (0, pad)).reshape(-1, LANES)


def _unpack(packed, shapes):
    flat = packed.reshape(-1)
    out, at = [], 0
    for s in shapes:
        n = int(np.prod(s))
        out.append(flat[at:at + n].reshape(s))
        at += n
    return out


def kernel(x, mix_norm_g, ffn_norm_g, final_norm_g, sc_w_in, sc_conv_w, sc_conv_b, sc_w_out, attn_w_qkv, attn_w_out, ffn_w_up, ffn_conv_w, ffn_conv_b, ffn_w_down, loss_target, m_mix_norm_g, m_ffn_norm_g, m_final_norm_g, m_sc_w_in, m_sc_conv_w, m_sc_conv_b, m_sc_w_out, m_attn_w_qkv, m_attn_w_out, m_ffn_w_up, m_ffn_conv_w, m_ffn_conv_b, m_ffn_w_down, v_mix_norm_g, v_ffn_norm_g, v_final_norm_g, v_sc_w_in, v_sc_conv_w, v_sc_conv_b, v_sc_w_out, v_attn_w_qkv, v_attn_w_out, v_ffn_w_up, v_ffn_conv_w, v_ffn_conv_b, v_ffn_w_down):
    _, T, D = x.shape
    n_layers = ffn_w_up.shape[0]
    me = 4 * lax.axis_index("x") + 2 * lax.axis_index("y") + lax.axis_index("c")
    c_arr = lax.axis_index("c").astype(jnp.int32).reshape(1)

    big = [("in", sc_w_in, 0), ("sco", sc_w_out, 0), ("qkv", attn_w_qkv, 0), ("ao", attn_w_out, 0)]
    big += [(f"up{l}", ffn_w_up, l) for l in range(n_layers)] + [(f"dn{l}", ffn_w_down, l) for l in range(n_layers)]
    ex = _Exchange({nm: _cast_bf16(f"cast_{nm}", w, l) for nm, w, l in big}, c_arr)

    cw_shapes = [sc_conv_w.shape, ffn_conv_w.shape]
    w_in_all, cw_all = _run_stage("gather_first", _join([ex.ag(["in"]), _ag_stage([_pack([sc_conv_w, ffn_conv_w])])]))
    ex.ag_done(["in"], [w_in_all])
    sc_cw_all, ffn_cw_all = zip(*[_unpack(cw_all[s], cw_shapes) for s in range(N_DEV)])
    sc_cw = jnp.concatenate(sc_cw_all, axis=-1)[0]
    ffn_cw = jnp.concatenate(ffn_cw_all, axis=-1)
    S = {"mix_g": mix_norm_g, "ffn_g": ffn_norm_g, "fin_g": final_norm_g, "sc_cw": sc_cw, "sc_cb": sc_conv_b[0],
         "ffn_cw": ffn_cw, "ffn_cb": ffn_conv_b}

    loss_part, grad_x, dS = _fwd_bwd(x[0], loss_target[0], S, ex)

    small_names = ["mix_g", "ffn_g", "fin_g", "sc_cb", "ffn_cb", "sc_cw", "ffn_cw"]
    small_parts = [dS[k] for k in small_names] + [loss_part.reshape(1)]
    got_in, small_all = _run_stage("rs_pair_last", _join([ex.pair(["in"]), _ag_stage([_pack(small_parts)])]))
    ex.pair_done(["in"], [got_in])
    ex.chip_done(["in"], _run_stage("rs_chip_last", ex.chip(["in"])))
    small_sum = _sum_slots("sum_small", small_all)
    g_mix, g_ffn, g_fin, g_scb, g_fcb, g_scw, g_fcw, loss = _unpack(small_sum, [p.shape for p in small_parts])
    g_scw = lax.dynamic_slice_in_dim(g_scw, me * sc_conv_w.shape[-1], sc_conv_w.shape[-1], axis=-1)[None]
    g_fcw = lax.dynamic_slice_in_dim(g_fcw, me * ffn_conv_w.shape[-1], ffn_conv_w.shape[-1], axis=-1)
    g_scb = g_scb[None]
    small_g = [g_mix, g_ffn, g_fin, g_scw, g_scb, g_fcw, g_fcb]
    small_w = [mix_norm_g, ffn_norm_g, final_norm_g, sc_conv_w, sc_conv_b, ffn_conv_w, ffn_conv_b]
    small_m = [m_mix_norm_g, m_ffn_norm_g, m_final_norm_g, m_sc_conv_w, m_sc_conv_b, m_ffn_conv_w, m_ffn_conv_b]
    small_v = [v_mix_norm_g, v_ffn_norm_g, v_final_norm_g, v_sc_conv_w, v_sc_conv_b, v_ffn_conv_w, v_ffn_conv_b]
    small_out = _adamw("adamw_small", _pack(small_g)[None], _pack(small_w)[None], _pack(small_m)[None],
                       _pack(small_v)[None], 0, None)
    small_shapes = [w.shape for w in small_w]
    sg, sd, sm, sv = [_unpack(o[0], small_shapes) for o in small_out]

    moments = {"in": (m_sc_w_in, v_sc_w_in), "sco": (m_sc_w_out, v_sc_w_out), "qkv": (m_attn_w_qkv, v_attn_w_qkv),
               "ao": (m_attn_w_out, v_attn_w_out), "up": (m_ffn_w_up, v_ffn_w_up), "dn": (m_ffn_w_down, v_ffn_w_down)}
    upd = {}
    for nm, w, l in big:
        key = nm.rstrip("0123456789")
        upd[key] = _adamw(f"adamw_{nm}", ex.sums[nm], w, moments[key][0], moments[key][1], l, upd.get(key))

    def leaves(k):
        return [sg, sd, sm, sv][k][0:3] + [upd["in"][k], [sg, sd, sm, sv][k][3], [sg, sd, sm, sv][k][4], upd["sco"][k],
                                          upd["qkv"][k], upd["ao"][k], upd["up"][k], [sg, sd, sm, sv][k][5],
                                          [sg, sd, sm, sv][k][6], upd["dn"][k]]

    return (loss.reshape(()), grad_x[None], *leaves(0), *leaves(1), *leaves(2), *leaves(3))
```

```python
import math

import numpy as np
import jax
import jax.numpy as jnp
from jax import lax
from jax.experimental import pallas as pl
from jax.experimental.pallas import tpu as pltpu

F32 = jnp.float32
BF16 = jnp.bfloat16
MESH = pl.DeviceIdType.MESH

HEAD_DIM = 128
DILATED_GROUPS = ((128, 1), (512, 4), (2048, 16))
DILATIONS = tuple(d for _, d in DILATED_GROUPS)
BAND = (DILATED_GROUPS[0][0] // 2) // DILATED_GROUPS[0][1]
assert all((w // 2) // d == BAND for w, d in DILATED_GROUPS)
NORM_EPS = 1e-5
ALIBI_MAX = 8.0
NEG_INF = -1e30
ADAM_LR, ADAM_B1, ADAM_B2, ADAM_EPS, ADAM_WD, ADAM_STEP = 0.001, 0.9, 0.999, 1e-08, 0.01, 10

N_DEV = 8
LANES = 128
HALO = 16
VMEM_LIMIT = 56 * 1024 * 1024


def _cp(*sem):
    return pltpu.CompilerParams(dimension_semantics=sem, vmem_limit_bytes=VMEM_LIMIT)


def _tile(n, pref, mult):
    t = (min(n, pref) // mult) * mult
    while t >= mult:
        if n % t == 0:
            return t
        t -= mult
    return n


def _sds(shape, dtype):
    return jax.ShapeDtypeStruct(shape, dtype)


def _any_specs(n):
    return [pl.BlockSpec(memory_space=pl.ANY)] * n


class _Stage:
    def __init__(self, arrays, out_shapes, sems, start, finish):
        self.arrays, self.out_shapes, self.sems, self.start, self.finish = arrays, out_shapes, sems, start, finish


def _join(stages):
    stages = [s for s in stages if s is not None]
    if not stages:
        return None

    def split(refs, count):
        out, at = [], 0
        for s in stages:
            out.append(refs[at:at + count(s)])
            at += count(s)
        return out

    def each(which):
        def run(ins, outs, sems):
            parts = zip(split(ins, lambda s: len(s.arrays)), split(outs, lambda s: len(s.out_shapes)),
                        split(sems, lambda s: len(s.sems)))
            for s, (i, o, m) in zip(stages, parts):
                getattr(s, which)(i, o, m)
        return run

    return _Stage(sum([s.arrays for s in stages], []), sum([s.out_shapes for s in stages], []),
                  sum([s.sems for s in stages], []), each("start"), each("finish"))


def _run_stage(name, st):
    n, m = len(st.arrays), len(st.out_shapes)

    def body(*refs):
        ins, outs, sems = refs[:n], refs[n:n + m], refs[n + m:]
        st.start(ins, outs, sems)
        st.finish(ins, outs, sems)

    return pl.pallas_call(body, in_specs=_any_specs(n), out_specs=_any_specs(m), out_shape=st.out_shapes,
                          scratch_shapes=st.sems, name=name)(*st.arrays)


NN = (((1,), (0,)), ((), ()))
NT = (((1,), (1,)), ((), ()))
TN = (((0,), (0,)), ((), ()))


def _mm(name, operands, in_specs, out_sds, o_spec, grid, dims, acc_shape, has_res=False, aliases=None, comm=None):
    nk = grid[2]
    n_in = len(operands)
    n_ci, n_co = (len(comm.arrays), len(comm.out_shapes)) if comm else (0, 0)

    def body(*refs):
        a_ref, b_ref = refs[0], refs[1]
        r_ref = refs[2] if has_res else None
        o_ref = refs[n_in + n_ci]
        acc = refs[n_in + n_ci + 1 + n_co]
        c_refs = (refs[n_in:n_in + n_ci], refs[n_in + n_ci + 1:n_in + n_ci + 1 + n_co], refs[n_in + n_ci + 2 + n_co:])
        ids = [pl.program_id(q) for q in range(3)]
        if comm:
            @pl.when((ids[0] == 0) & (ids[1] == 0) & (ids[2] == 0))
            def _():
                comm.start(*c_refs)

        part = lax.dot_general(a_ref[...], b_ref[...], dims, preferred_element_type=F32)

        def finish(total):
            if has_res:
                total = total + r_ref[...]
            o_ref[...] = total.astype(o_ref.dtype)

        if nk == 1:
            finish(part)
        else:
            k = ids[2]

            @pl.when(k == 0)
            def _():
                acc[...] = part

            @pl.when((k > 0) & (k < nk - 1))
            def _():
                acc[...] += part

            @pl.when(k == nk - 1)
            def _():
                finish(acc[...] + part)

        if comm:
            @pl.when((ids[0] == grid[0] - 1) & (ids[1] == grid[1] - 1) & (ids[2] == nk - 1))
            def _():
                comm.finish(*c_refs)

    scratch = [pltpu.VMEM(acc_shape if nk > 1 else (8, LANES), F32)]
    if not comm:
        out = pl.pallas_call(
            body, grid=grid, in_specs=in_specs, out_specs=o_spec, out_shape=out_sds, scratch_shapes=scratch,
            input_output_aliases=aliases or {}, name=name,
            compiler_params=_cp("parallel", "parallel", "arbitrary"))(*operands)
        return out, []
    outs = pl.pallas_call(
        body, grid=grid, in_specs=list(in_specs) + _any_specs(n_ci), out_specs=[o_spec] + _any_specs(n_co),
        out_shape=[out_sds] + comm.out_shapes, scratch_shapes=scratch + comm.sems,
        input_output_aliases=aliases or {}, name=name,
        compiler_params=_cp("arbitrary", "arbitrary", "arbitrary"))(*operands, *comm.arrays)
    return outs[0], list(outs[1:])


def _stack(a):
    return a if a.ndim == 3 else a[None]


def _mm_nn(name, a, w, res, out_dtype, col0=0, n_cols=None, comm=None):
    M, K = a.shape
    N = n_cols or w.shape[1]
    tm, tn, tk = _tile(M, 1024, 16), _tile(N, 1024, LANES), _tile(K, 2048, LANES)
    c0 = col0 // tn
    ops = [a, w]
    specs = [pl.BlockSpec((tm, tk), lambda i, j, k: (i, k)), pl.BlockSpec((tk, tn), lambda i, j, k: (k, c0 + j))]
    if res is not None:
        ops.append(res)
        specs.append(pl.BlockSpec((tm, tn), lambda i, j, k: (i, j)))
    return _mm(name, tuple(ops), specs, _sds((M, N), out_dtype), pl.BlockSpec((tm, tn), lambda i, j, k: (i, j)),
               (M // tm, N // tn, K // tk), NN, (tm, tn), has_res=res is not None, comm=comm)


def _mm_nt(name, dy, w, out_dtype, col0=0, tn_pref=1024, tk_pref=2048, comm=None):
    dy = _stack(dy)
    _, M, Np = dy.shape
    Kw = w.shape[0]
    tm, tn, tk = _tile(M, 1024, 16), _tile(Kw, tn_pref, LANES), _tile(Np, tk_pref, LANES)
    per, c0 = Np // tk, col0 // tk
    return _mm(name, (dy, w),
               [pl.BlockSpec((None, tm, tk), lambda i, j, k: (k // per, i, k % per)),
                pl.BlockSpec((tn, tk), lambda i, j, k: (j, c0 + k))],
               _sds((M, Kw), out_dtype), pl.BlockSpec((tm, tn), lambda i, j, k: (i, j)),
               (M // tm, Kw // tn, dy.shape[0] * per), NT, (tm, tn), comm=comm)


def _mm_tn(name, a, dy, n_total=None, col0=0, tkr_pref=1024, tn_pref=1024, prev=None, comm=None):
    dy = _stack(dy)
    P, M, Np = dy.shape
    Kw = a.shape[1]
    tkr, tn, tk = _tile(Kw, tkr_pref, LANES), _tile(Np, tn_pref, LANES), _tile(M, 2048, 16)
    per, c0 = Np // tn, col0 // tn
    ops = [a, dy]
    specs = [pl.BlockSpec((tk, tkr), lambda i, j, k: (k, i)),
             pl.BlockSpec((None, tk, tn), lambda i, j, k: (j // per, k, j % per))]
    aliases = None
    if prev is not None:
        ops.append(prev)
        specs.append(pl.BlockSpec(memory_space=pl.ANY))
        aliases = {2: 0}
    return _mm(name, tuple(ops), specs, _sds((Kw, n_total or P * Np), BF16),
               pl.BlockSpec((tkr, tn), lambda i, j, k: (i, c0 + j)),
               (Kw // tkr, P * per, M // tk), TN, (tkr, tn), aliases=aliases, comm=comm)


def _rmsnorm_fwd(name, x, g):
    T, D = x.shape
    tr = _tile(T, 512, 16)

    def body(x_ref, g_ref, h_ref):
        xf = x_ref[...]
        r = lax.rsqrt(jnp.mean(xf * xf, axis=-1, keepdims=True) + NORM_EPS)
        h_ref[...] = (xf * r * g_ref[...]).astype(h_ref.dtype)

    return pl.pallas_call(
        body, grid=(T // tr,),
        in_specs=[pl.BlockSpec((tr, D), lambda i: (i, 0)), pl.BlockSpec((1, D), lambda i: (0, 0))],
        out_specs=pl.BlockSpec((tr, D), lambda i: (i, 0)), out_shape=_sds((T, D), BF16),
        name=name, compiler_params=_cp("parallel"))(x, g.reshape(1, D))


def _rmsnorm_bwd(name, x, g, dhs, dres):
    T, D = x.shape
    tr = _tile(T, 256, 16)
    n_dh = len(dhs)

    def body(*refs):
        x_ref, g_ref = refs[0], refs[1]
        dh_refs = refs[2:2 + n_dh]
        dres_ref = refs[2 + n_dh]
        dx_ref, dxb_ref, dg_ref = refs[3 + n_dh], refs[4 + n_dh], refs[5 + n_dh]
        xf = x_ref[...]
        r = lax.rsqrt(jnp.mean(xf * xf, axis=-1, keepdims=True) + NORM_EPS)
        xhat = xf * r
        dh = dh_refs[0][...].astype(F32)
        for q in dh_refs[1:]:
            dh = dh + q[...].astype(F32)
        dy = dh * g_ref[...]
        c = jnp.mean(dy * xhat, axis=-1, keepdims=True)
        dx = dres_ref[...] + r * (dy - xhat * c)
        dx_ref[...] = dx
        dxb_ref[...] = dx.astype(BF16)

        @pl.when(pl.program_id(0) == 0)
        def _():
            dg_ref[...] = jnp.zeros_like(dg_ref)

        dg_ref[...] += jnp.sum(dh * xhat, axis=0, keepdims=True)

    row = pl.BlockSpec((tr, D), lambda i: (i, 0))
    vec = pl.BlockSpec((1, D), lambda i: (0, 0))
    dx, dx_b, dg = pl.pallas_call(
        body, grid=(T // tr,), in_specs=[row, vec] + [row] * n_dh + [row],
        out_specs=[row, row, vec], out_shape=[_sds((T, D), F32), _sds((T, D), BF16), _sds((1, D), F32)],
        name=name, compiler_params=_cp("arbitrary"))(x, g.reshape(1, D), *dhs, dres)
    return dx, dx_b, dg[0]


def _final_loss(name, x, g, tgt):
    T, D = x.shape
    tr = _tile(T, 256, 16)

    def body(x_ref, g_ref, t_ref, dx_ref, dxb_ref, dg_ref, loss_ref):
        xf = x_ref[...]
        r = lax.rsqrt(jnp.mean(xf * xf, axis=-1, keepdims=True) + NORM_EPS)
        xhat = xf * r
        err = xhat * g_ref[...] - t_ref[...]
        dy = err * (1.0 / D)
        dxh = dy * g_ref[...]
        c = jnp.mean(dxh * xhat, axis=-1, keepdims=True)
        dx = r * (dxh - xhat * c)
        dx_ref[...] = dx
        dxb_ref[...] = dx.astype(BF16)

        @pl.when(pl.program_id(0) == 0)
        def _():
            dg_ref[...] = jnp.zeros_like(dg_ref)
            loss_ref[...] = jnp.zeros_like(loss_ref)

        dg_ref[...] += jnp.sum(dy * xhat, axis=0, keepdims=True)
        loss_ref[...] += 0.5 * jnp.sum(jnp.mean(err * err, axis=-1, keepdims=True), axis=0, keepdims=True)

    row = pl.BlockSpec((tr, D), lambda i: (i, 0))
    vec = pl.BlockSpec((1, D), lambda i: (0, 0))
    dx, dx_b, dg, loss = pl.pallas_call(
        body, grid=(T // tr,), in_specs=[row, vec, row],
        out_specs=[row, row, vec, pl.BlockSpec((1, 1), lambda i: (0, 0))],
        out_shape=[_sds((T, D), F32), _sds((T, D), BF16), _sds((1, D), F32), _sds((1, 1), F32)],
        name=name, compiler_params=_cp("arbitrary"))(x, g.reshape(1, D), tgt)
    return loss[0, 0], dx, dx_b, dg[0]


def _halo_specs(tr, tc, n_rows, col):
    rb = tr // HALO
    last = n_rows // HALO - 1
    return [pl.BlockSpec((tr, tc), lambda *g: (g[-1], col(*g))),
            pl.BlockSpec((HALO, tc), lambda *g: (jnp.maximum(g[-1] * rb - 1, 0), col(*g))),
            pl.BlockSpec((HALO, tc), lambda *g: (jnp.minimum((g[-1] + 1) * rb, last), col(*g)))]


def _ext(cur_ref, prev_ref, next_ref, i, n_i):
    p = prev_ref[...].astype(F32) * (i > 0).astype(F32)
    n = next_ref[...].astype(F32) * (i < n_i - 1).astype(F32)
    return jnp.concatenate([p, cur_ref[...].astype(F32), n], axis=0)


def _shift_dn(x):
    return pltpu.roll(x, 1, axis=0)


def _shift_up(x):
    return pltpu.roll(x, x.shape[0] - 1, axis=0)


def _conv(x, w_ref, b_ref):
    return w_ref[0:1, :] * _shift_dn(x) + w_ref[1:2, :] * x + w_ref[2:3, :] * _shift_up(x) + b_ref[...]


def _conv_t(g, w_ref):
    return w_ref[0:1, :] * _shift_up(g) + w_ref[1:2, :] * g + w_ref[2:3, :] * _shift_dn(g)


def _mid(x, tr):
    return x[HALO:HALO + tr, :]


def _conv_wgrad(acc_ref, g, x, tr, first):
    gm = _mid(g, tr)

    @pl.when(first)
    def _():
        acc_ref[...] = jnp.zeros_like(acc_ref)

    acc_ref[0:1, :] += jnp.sum(gm * _mid(_shift_dn(x), tr), axis=0, keepdims=True)
    acc_ref[1:2, :] += jnp.sum(gm * _mid(x, tr), axis=0, keepdims=True)
    acc_ref[2:3, :] += jnp.sum(gm * _mid(_shift_up(x), tr), axis=0, keepdims=True)
    acc_ref[3:4, :] += jnp.sum(gm, axis=0, keepdims=True)


def _sigmoid(a):
    return 1.0 / (1.0 + jnp.exp(-a))


def _ffn_gate_fwd(name, up, cw, cb):
    T, F2 = up.shape
    F = F2 // 2
    tc, tr = _tile(F, 512, LANES), _tile(T, 512, HALO)
    nF, n_i = F // tc, T // tr

    def body(ac, ap, an, bc, bp, bn, wa, wb, ba, bb, o_ref):
        i = pl.program_id(1)
        ua = _mid(_conv(_ext(ac, ap, an, i, n_i), wa, ba), tr)
        ub = _mid(_conv(_ext(bc, bp, bn, i, n_i), wb, bb), tr)
        o_ref[...] = (ua * _sigmoid(ua) * ub).astype(o_ref.dtype)

    wspec = lambda o: pl.BlockSpec((3, tc), lambda j, i: (0, j + o))
    bspec = lambda o: pl.BlockSpec((1, tc), lambda j, i: (0, j + o))
    return pl.pallas_call(
        body, grid=(nF, n_i),
        in_specs=_halo_specs(tr, tc, T, lambda j, i: j) + _halo_specs(tr, tc, T, lambda j, i: j + nF)
        + [wspec(0), wspec(nF), bspec(0), bspec(nF)],
        out_specs=pl.BlockSpec((tr, tc), lambda j, i: (i, j)), out_shape=_sds((T, F), BF16),
        name=name, compiler_params=_cp("parallel", "parallel"))(up, up, up, up, up, up, cw, cw, cb, cb)


def _ffn_gate_bwd(name, up, dact, cw, cb):
    T, F2 = up.shape
    F = F2 // 2
    tc, tr = _tile(F, 512, LANES), _tile(T, 256, HALO)
    nF, n_i = F // tc, T // tr

    def body(ac, ap, an, bc, bp, bn, dc, dp, dn, wa, wb, ba, bb, o_ref, wga_ref, wgb_ref):
        i = pl.program_id(1)
        xa = _ext(ac, ap, an, i, n_i)
        xb = _ext(bc, bp, bn, i, n_i)
        da = _ext(dc, dp, dn, i, n_i)
        ua = _conv(xa, wa, ba)
        sig = _sigmoid(ua)
        ga = da * _conv(xb, wb, bb) * (sig * (1.0 + ua * (1.0 - sig)))
        o_ref[0] = _mid(_conv_t(ga, wa), tr).astype(o_ref.dtype)
        _conv_wgrad(wga_ref, ga, xa, tr, i == 0)
        gb = da * (ua * sig)
        o_ref[1] = _mid(_conv_t(gb, wb), tr).astype(o_ref.dtype)
        _conv_wgrad(wgb_ref, gb, xb, tr, i == 0)

    wspec = lambda o: pl.BlockSpec((3, tc), lambda j, i: (0, j + o))
    bspec = lambda o: pl.BlockSpec((1, tc), lambda j, i: (0, j + o))
    wg = pl.BlockSpec((8, tc), lambda j, i: (0, j))
    dup, wga, wgb = pl.pallas_call(
        body, grid=(nF, n_i),
        in_specs=_halo_specs(tr, tc, T, lambda j, i: j) + _halo_specs(tr, tc, T, lambda j, i: j + nF)
        + _halo_specs(tr, tc, T, lambda j, i: j) + [wspec(0), wspec(nF), bspec(0), bspec(nF)],
        out_specs=[pl.BlockSpec((2, tr, tc), lambda j, i: (0, i, j)), wg, wg],
        out_shape=[_sds((2, T, F), BF16), _sds((8, F), F32), _sds((8, F), F32)],
        name=name, compiler_params=_cp("parallel", "arbitrary"),
    )(up, up, up, up, up, up, dact, dact, dact, cw, cw, cb, cb)
    return dup, jnp.concatenate([wga, wgb], axis=1)


def _sc_gate_fwd(name, z, cw, cb):
    T, D3 = z.shape
    D = D3 // 3
    tc, tr = _tile(D, 512, LANES), _tile(T, 512, HALO)
    nD, n_i = D // tc, T // tr

    def body(uc, up_, un, gb, cc, cp, cn, w, b, o_ref):
        i = pl.program_id(1)
        cu = _ext(cc, cp, cn, i, n_i) * _ext(uc, up_, un, i, n_i)
        o_ref[...] = (gb[...].astype(F32) * _mid(_conv(cu, w, b), tr)).astype(o_ref.dtype)

    return pl.pallas_call(
        body, grid=(nD, n_i),
        in_specs=_halo_specs(tr, tc, T, lambda j, i: j) + [pl.BlockSpec((tr, tc), lambda j, i: (i, j + nD))]
        + _halo_specs(tr, tc, T, lambda j, i: j + 2 * nD)
        + [pl.BlockSpec((3, tc), lambda j, i: (0, j)), pl.BlockSpec((1, tc), lambda j, i: (0, j))],
        out_specs=pl.BlockSpec((tr, tc), lambda j, i: (i, j)), out_shape=_sds((T, D), BF16),
        name=name, compiler_params=_cp("parallel", "parallel"))(z, z, z, z, z, z, z, cw, cb)


def _sc_gate_bwd(name, z, dy, cw, cb):
    T, D3 = z.shape
    D = D3 // 3
    tc, tr = _tile(D, 512, LANES), _tile(T, 256, HALO)
    nD, n_i = D // tc, T // tr

    def body(uc, up_, un, bc, bp, bn, cc, cp, cn, yc, yp, yn, w, b, o_ref, wg_ref):
        i = pl.program_id(1)
        u = _ext(uc, up_, un, i, n_i)
        gc = _ext(cc, cp, cn, i, n_i)
        cu = gc * u
        g = _ext(yc, yp, yn, i, n_i) * _ext(bc, bp, bn, i, n_i)
        dcu = _mid(_conv_t(g, w), tr)
        o_ref[0] = (dcu * _mid(gc, tr)).astype(o_ref.dtype)
        o_ref[1] = (yc[...].astype(F32) * _mid(_conv(cu, w, b), tr)).astype(o_ref.dtype)
        o_ref[2] = (dcu * _mid(u, tr)).astype(o_ref.dtype)
        _conv_wgrad(wg_ref, g, cu, tr, i == 0)

    hs = lambda o: _halo_specs(tr, tc, T, lambda j, i: j + o)
    return pl.pallas_call(
        body, grid=(nD, n_i),
        in_specs=hs(0) + hs(nD) + hs(2 * nD) + hs(0)
        + [pl.BlockSpec((3, tc), lambda j, i: (0, j)), pl.BlockSpec((1, tc), lambda j, i: (0, j))],
        out_specs=[pl.BlockSpec((3, tr, tc), lambda j, i: (0, i, j)), pl.BlockSpec((8, tc), lambda j, i: (0, j))],
        out_shape=[_sds((3, T, D), BF16), _sds((8, D), F32)],
        name=name, compiler_params=_cp("parallel", "arbitrary"),
    )(z, z, z, z, z, z, z, z, z, dy, dy, dy, cw, cb)


def _slopes(n_heads):
    return jnp.asarray(2.0 ** (-ALIBI_MAX * np.arange(1, n_heads + 1) / n_heads), dtype=F32)


def _nq(L):
    return max(1, min(4, L // LANES // 2))


def _srows(ref, r, start, n, d):
    if d == 1:
        return ref[start:start + n, :]
    return ref[pl.ds(start * d + r, n, stride=d), :]


def _win(p_ref, c_ref, n_ref, r, b, nq):
    lo, hi, top = b * LANES - BAND, b * LANES + LANES + BAND, nq * LANES
    parts = [p_ref[r]] if lo < 0 else []
    parts.append(c_ref[r, max(lo, 0):min(hi, top), :])
    if hi > top:
        parts.append(n_ref[r])
    return parts[0] if len(parts) == 1 else jnp.concatenate(parts, axis=0)


def _nat_win(p_ref, c_ref, n_ref, r, b, nq, d):
    lo, hi, top = b * LANES - BAND, b * LANES + LANES + BAND, nq * LANES
    parts = [_srows(p_ref, r, 0, BAND, d)] if lo < 0 else []
    parts.append(_srows(c_ref, r, max(lo, 0), min(hi, top) - max(lo, 0), d))
    if hi > top:
        parts.append(_srows(n_ref, r, 0, BAND, d))
    return parts[0] if len(parts) == 1 else jnp.concatenate(parts, axis=0)


def _biases(slope, d, n, n_steps, nq, q_rows, k_rows, q0, k0):
    qi = lax.broadcasted_iota(jnp.int32, (q_rows, k_rows), 0) + q0
    kj = lax.broadcasted_iota(jnp.int32, (q_rows, k_rows), 1) + k0
    dist = jnp.abs(kj - qi)
    base = jnp.where(dist <= BAND, -slope * (dist * d).astype(F32), NEG_INF)
    out = []
    for b in range(nq):
        t = base
        if b == 0:
            t = jnp.where((n == 0) & ((kj < 0) | (qi < 0)), NEG_INF, t)
        if b == nq - 1:
            t = jnp.where((n == n_steps - 1) & ((kj >= LANES) | (qi >= LANES)), NEG_INF, t)
        out.append(t)
    return out


def _win_specs(d, H, col, nq, L):
    return [pl.BlockSpec((d, BAND, LANES), lambda h, n: (0, jnp.maximum(2 * nq * n - 1, 0), col * H + h)),
            pl.BlockSpec((d, nq * LANES, LANES), lambda h, n: (0, n, col * H + h)),
            pl.BlockSpec((d, BAND, LANES), lambda h, n: (0, jnp.minimum(2 * nq * (n + 1), L // BAND - 1), col * H + h))]


def _nat_specs(d, nq, L):
    return [pl.BlockSpec((BAND * d, LANES), lambda h, n: (jnp.maximum(2 * nq * n - 1, 0), h)),
            pl.BlockSpec((nq * LANES * d, LANES), lambda h, n: (n, h)),
            pl.BlockSpec((BAND * d, LANES), lambda h, n: (jnp.minimum(2 * nq * (n + 1), L // BAND - 1), h))]


def _over_residues(d, per_r):
    if d == 1:
        per_r(0, 0)
    else:
        lax.fori_loop(0, d, per_r, 0, unroll=2)


def _attn_fwd(name, qkv, d, H):
    T = qkv.shape[0]
    D = H * HEAD_DIM
    L = T // d
    nq = _nq(L)
    n_steps = L // (nq * LANES)
    scale = HEAD_DIM ** -0.5
    q3 = qkv.reshape(d, L, 3 * D)

    def body(s_ref, q_ref, kp, kc, kn, vp, vc, vn, o_ref, l_ref):
        h, n = pl.program_id(0), pl.program_id(1)
        bias = _biases(s_ref[h], d, n, n_steps, nq, LANES, 2 * LANES, 0, -BAND)

        def per_r(r, carry):
            for b in range(nq):
                k, v = _win(kp, kc, kn, r, b, nq), _win(vp, vc, vn, r, b, nq)
                s = lax.dot_general(q_ref[r, b * LANES:(b + 1) * LANES, :], k, NT, preferred_element_type=F32) * scale + bias[b]
                m = jnp.max(s, axis=1, keepdims=True)
                p = jnp.exp(s - m)
                den = jnp.sum(p, axis=1, keepdims=True)
                o = lax.dot_general(p.astype(BF16), v, NN, preferred_element_type=F32) / den
                lse = jnp.broadcast_to(m + jnp.log(den), (LANES, LANES))
                if d == 1:
                    o_ref[b * LANES:(b + 1) * LANES, :] = o
                    l_ref[b * LANES:(b + 1) * LANES, :] = lse
                else:
                    o_ref[pl.ds(b * LANES * d + r, LANES, stride=d), :] = o
                    l_ref[pl.ds(b * LANES * d + r, LANES, stride=d), :] = lse
            return carry

        _over_residues(d, per_r)

    out = pl.BlockSpec((nq * LANES * d, LANES), lambda h, n: (n, h))
    return pl.pallas_call(
        body, grid=(H, n_steps),
        in_specs=[pl.BlockSpec(memory_space=pltpu.SMEM), pl.BlockSpec((d, nq * LANES, LANES), lambda h, n: (0, n, h))]
        + _win_specs(d, H, 1, nq, L) + _win_specs(d, H, 2, nq, L),
        out_specs=[out, out], out_shape=[_sds((T, D), F32), _sds((T, D), F32)],
        name=name, compiler_params=_cp("parallel", "parallel"))(_slopes(H), q3, q3, q3, q3, q3, q3, q3)


def _attn_combine(name, outs, lses):
    T, D = outs[0].shape
    tr, tc = _tile(T, 512, 16), _tile(D, 512, LANES)

    def body(o0, o1, o2, l0, l1, l2, ob_ref, of_ref, l_ref):
        a0, a1, a2 = l0[...], l1[...], l2[...]
        m = jnp.maximum(jnp.maximum(a0, a1), a2)
        e0, e1, e2 = jnp.exp(a0 - m), jnp.exp(a1 - m), jnp.exp(a2 - m)
        z = e0 + e1 + e2
        o = (e0 * o0[...] + e1 * o1[...] + e2 * o2[...]) / z
        of_ref[...] = o
        ob_ref[...] = o.astype(BF16)
        l_ref[...] = m + jnp.log(z)

    blk = pl.BlockSpec((tr, tc), lambda i, j: (i, j))
    return pl.pallas_call(
        body, grid=(T // tr, D // tc), in_specs=[blk] * 6, out_specs=[blk] * 3,
        out_shape=[_sds((T, D), BF16), _sds((T, D), F32), _sds((T, D), F32)],
        name=name, compiler_params=_cp("parallel", "parallel"))(*outs, *lses)


def _attn_stats(name, do, o, lse):
    T, D = do.shape
    tr = _tile(T, 1024, 16)

    def body(a, b, l, o_ref):
        delta = jnp.broadcast_to(jnp.sum(a[...] * b[...], axis=1, keepdims=True), o_ref.shape)
        lane = lax.broadcasted_iota(jnp.int32, o_ref.shape, 1)
        o_ref[...] = jnp.where(lane < BAND, l[...], delta)

    blk = pl.BlockSpec((tr, LANES), lambda i, j: (i, j))
    return pl.pallas_call(body, grid=(T // tr, D // LANES), in_specs=[blk, blk, blk], out_specs=blk,
                          out_shape=_sds((T, D), F32), name=name, compiler_params=_cp("parallel", "parallel"))(do, o, lse)


def _attn_bwd(name, qkv, do, stats, d, H):
    T = qkv.shape[0]
    D = H * HEAD_DIM
    L = T // d
    nq = _nq(L)
    n_steps = L // (nq * LANES)
    scale = HEAD_DIM ** -0.5
    q3 = qkv.reshape(d, L, 3 * D)
    mid = slice(BAND, BAND + LANES)

    def body(s_ref, qp, qc, qn, kp, kc, kn, vp, vc, vn, gp, gc, gn, tp, tc_, tn_, o_ref):
        h, n = pl.program_id(0), pl.program_id(1)
        bias_q = _biases(s_ref[h], d, n, n_steps, nq, LANES, 2 * LANES, 0, -BAND)
        bias_k = _biases(s_ref[h], d, n, n_steps, nq, 2 * LANES, LANES, -BAND, 0)

        def per_r(r, carry):
            for b in range(nq):
                rows = slice(b * LANES, (b + 1) * LANES)
                q_w, k_w, v_w = _win(qp, qc, qn, r, b, nq), _win(kp, kc, kn, r, b, nq), _win(vp, vc, vn, r, b, nq)
                g_w = _nat_win(gp, gc, gn, r, b, nq, d)
                t_w = _nat_win(tp, tc_, tn_, r, b, nq, d)
                g_b = g_w.astype(BF16)
                q_c, k_c, v_c, g_c, t_c = q_w[mid], k_w[mid], v_w[mid], g_b[mid], t_w[mid]
                s = lax.dot_general(q_c, k_w, NT, preferred_element_type=F32) * scale + bias_q[b]
                p = jnp.exp(s - t_c[:, 0:1])
                dp = lax.dot_general(g_c, v_w, NT, preferred_element_type=F32)
                ds = p * (dp - t_c[:, BAND:BAND + 1])
                o_ref[0, r, rows, :] = (lax.dot_general(ds.astype(BF16), k_w, NN, preferred_element_type=F32) * scale).astype(BF16)
                s2 = lax.dot_general(q_w, k_c, NT, preferred_element_type=F32) * scale + bias_k[b]
                p2 = jnp.exp(s2 - t_w[:, 0:1])
                o_ref[2, r, rows, :] = lax.dot_general(p2.astype(BF16), g_b, TN, preferred_element_type=F32).astype(BF16)
                dp2 = lax.dot_general(g_b, v_c, NT, preferred_element_type=F32)
                ds2 = p2 * (dp2 - t_w[:, BAND:BAND + 1])
                o_ref[1, r, rows, :] = (lax.dot_general(ds2.astype(BF16), q_w, TN, preferred_element_type=F32) * scale).astype(BF16)
            return carry

        _over_residues(d, per_r)

    dqkv = pl.pallas_call(
        body, grid=(H, n_steps),
        in_specs=[pl.BlockSpec(memory_space=pltpu.SMEM)]
        + _win_specs(d, H, 0, nq, L) + _win_specs(d, H, 1, nq, L) + _win_specs(d, H, 2, nq, L)
        + _nat_specs(d, nq, L) + _nat_specs(d, nq, L),
        out_specs=pl.BlockSpec((3, d, nq * LANES, LANES), lambda h, n: (0, 0, n, h)),
        out_shape=_sds((3, d, L, D), BF16),
        name=name, compiler_params=_cp("parallel", "parallel"),
    )(_slopes(H), *([q3] * 9), *([do] * 3), *([stats] * 3))
    return dqkv.reshape(3, T, D)


def _to_group_order(a, d):
    if d == 1:
        return a
    T, C = a.shape
    return a.reshape(T // d, d, C).swapaxes(0, 1).reshape(T, C)


def _from_group_order(a, d):
    if d == 1:
        return a
    T, C = a.shape
    return a.reshape(d, T // d, C).swapaxes(0, 1).reshape(T, C)


def _fwd_bwd(x, tgt, S, ex):
    T, D = x.shape
    H = D // HEAD_DIM
    G3 = 3 * D

    def mm(fn, *args, rides=(), **kw):
        out, extra = fn(*args, comm=_join([getattr(ex, kind)(keys) for kind, keys in rides]), **kw)
        at = 0
        for kind, keys in rides:
            getattr(ex, kind + "_done")(keys, extra[at:at + len(keys)])
            at += len(keys)
        return out

    def ffn_fwd(l, xin, rides_up, rides_dn):
        hf = _rmsnorm_fwd(f"ffn_norm{l}", xin, S["ffn_g"][l])
        up = mm(_mm_nn, f"ffn_up{l}", hf, ex.w(f"up{l}"), None, BF16, rides=rides_up)
        act = _ffn_gate_fwd(f"ffn_gate{l}", up, S["ffn_cw"][l], S["ffn_cb"][l][None])
        return hf, up, act, mm(_mm_nn, f"ffn_down{l}", act, ex.w(f"dn{l}"), xin, F32, rides=rides_dn)

    def ffn_bwd(l, xin, hf, up, act, dxo, dxo_b, rides):
        dact = mm(_mm_nt, f"ffn_down_dx{l}", dxo_b, ex.w(f"dn{l}"), BF16, tn_pref=1408, rides=rides[0])
        ex.grad(f"dn{l}", mm(_mm_tn, f"ffn_down_dw{l}", act, dxo_b, tkr_pref=1408))
        dup, cg = _ffn_gate_bwd(f"ffn_gate_bwd{l}", up, dact, S["ffn_cw"][l], S["ffn_cb"][l][None])
        dhf = mm(_mm_nt, f"ffn_up_dx{l}", dup, ex.w(f"up{l}"), F32, tk_pref=2816, rides=rides[1])
        ex.grad(f"up{l}", mm(_mm_tn, f"ffn_up_dw{l}", hf, dup, tn_pref=1408, rides=rides[2]))
        dx, dx_b, dg = _rmsnorm_bwd(f"ffn_norm_bwd{l}", xin, S["ffn_g"][l], [dhf], dxo)
        return dx, dx_b, dg, cg

    h0 = _rmsnorm_fwd("mix_norm0", x, S["mix_g"][0])
    z = mm(_mm_nn, "sc_in", h0, ex.w("in"), None, BF16, rides=[("ag", ["sco", "up0"])])
    y = _sc_gate_fwd("sc_gate", z, S["sc_cw"], S["sc_cb"][None])
    x1 = mm(_mm_nn, "sc_out", y, ex.w("sco"), x, F32)
    hf0, up0, act0, x2 = ffn_fwd(0, x1, [("ag", ["dn0", "qkv"])], [("ag", ["ao", "up1"])])
    h1 = _rmsnorm_fwd("mix_norm1", x2, S["mix_g"][1])
    hd, qkv, outs, lses = [], [], [], []
    for g, d in enumerate(DILATIONS):
        hd.append(_to_group_order(h1, d))
        qkv.append(mm(_mm_nn, f"attn_qkv{g}", hd[g], ex.w("qkv"), None, BF16, col0=g * G3, n_cols=G3,
                      rides=[("ag", ["dn1"])] if g == 0 else ()))
        o_g, l_g = _attn_fwd(f"attn_fwd{g}", qkv[g], d, H)
        outs.append(o_g)
        lses.append(l_g)
    o_b, o_f, lse = _attn_combine("attn_combine", outs, lses)
    x3 = mm(_mm_nn, "attn_out", o_b, ex.w("ao"), x2, F32)
    hf1, up1, act1, x4 = ffn_fwd(1, x3, (), ())
    loss, dx4, dx4_b, dg_fin = _final_loss("final_loss", x4, S["fin_g"], tgt)

    dx3, dx3_b, dg_f1, cg1 = ffn_bwd(1, x3, hf1, up1, act1, dx4, dx4_b,
                                     [(), [("pair", ["dn1"])], [("chip", ["dn1"])]])
    do = mm(_mm_nt, "attn_out_dx", dx3_b, ex.w("ao"), F32, rides=[("pair", ["up1"])])
    ex.grad("ao", mm(_mm_tn, "attn_out_dw", o_b, dx3_b))
    stats = _attn_stats("attn_stats", do, o_f, lse)
    dhs, dw_qkv = [], None
    qkv_rides = [[("chip", ["up1"]), ("pair", ["ao"])], [("chip", ["ao"])], ()]
    for g, d in enumerate(DILATIONS):
        dqkv = _attn_bwd(f"attn_bwd{g}", qkv[g], do, stats, d, H)
        dhs.append(_from_group_order(mm(_mm_nt, f"attn_qkv_dx{g}", dqkv, ex.w("qkv"), F32, col0=g * G3,
                                        rides=qkv_rides[g]), d))
        dw_qkv = mm(_mm_tn, f"attn_qkv_dw{g}", hd[g], dqkv, n_total=len(DILATIONS) * G3, col0=g * G3, prev=dw_qkv)
    ex.grad("qkv", dw_qkv)
    dx2, dx2_b, dg_m1 = _rmsnorm_bwd("mix_norm_bwd1", x2, S["mix_g"][1], dhs, dx3)
    dx1, dx1_b, dg_f0, cg0 = ffn_bwd(0, x1, hf0, up0, act0, dx2, dx2_b,
                                     [[("pair", ["qkv"])], [("chip", ["qkv"]), ("pair", ["dn0"])], [("chip", ["dn0"])]])
    dy = mm(_mm_nt, "sc_out_dx", dx1_b, ex.w("sco"), BF16, rides=[("pair", ["up0"])])
    ex.grad("sco", mm(_mm_tn, "sc_out_dw", y, dx1_b))
    dz, cg_sc = _sc_gate_bwd("sc_gate_bwd", z, dy, S["sc_cw"], S["sc_cb"][None])
    dh0 = mm(_mm_nt, "sc_in_dx", dz, ex.w("in"), F32, rides=[("chip", ["up0"]), ("pair", ["sco"])])
    ex.grad("in", mm(_mm_tn, "sc_in_dw", h0, dz, rides=[("chip", ["sco"])]))
    dx0, _, dg_m0 = _rmsnorm_bwd("mix_norm_bwd0", x, S["mix_g"][0], [dh0], dx1)

    dS = {"mix_g": jnp.stack([dg_m0, dg_m1]), "ffn_g": jnp.stack([dg_f0, dg_f1]), "fin_g": dg_fin,
          "sc_cw": cg_sc[0:3], "sc_cb": cg_sc[3], "ffn_cw": jnp.stack([cg0[0:3], cg1[0:3]]),
          "ffn_cb": jnp.stack([cg0[3], cg1[3]])}
    return loss, dx0, dS


def _place():
    x, y, c = lax.axis_index("x"), lax.axis_index("y"), lax.axis_index("c")
    return x, y, c, [(1 - x, y), (x, 1 - y), (1 - x, 1 - y)]


def _block(ref, s, shape, axis):
    R, C = shape
    if axis == 0:
        return ref.at[pl.ds(pl.multiple_of(s * R, HALO), R), :]
    return ref.at[:, pl.ds(pl.multiple_of(s * C, LANES), C)]


def _whole(shape, axis):
    return (shape[0] * N_DEV, shape[1]) if axis == 0 else (shape[0], shape[1] * N_DEV)


def _ag_stage(arrs, axes):
    n = len(arrs)

    def plan(ins, outs, sems):
        send_sems, recv_sems, local_sems = sems
        x, y, c, chips = _place()
        me, sibling = 4 * x + 2 * y + c, (x, y, 1 - c)

        def copy(a, k, blk, to, src=None):
            dst = _block(outs[a], blk, arrs[a].shape, axes[a])
            return pltpu.make_async_remote_copy(src_ref=dst if src is None else src, dst_ref=dst,
                                                send_sem=send_sems.at[a, k], recv_sem=recv_sems.at[a, k],
                                                device_id=to, device_id_type=MESH)

        mine = [pltpu.make_async_copy(ins[a], _block(outs[a], me, arrs[a].shape, axes[a]), local_sems.at[a])
                for a in range(n)]
        first = []
        for a in range(n):
            first.append(copy(a, 0, me, sibling, src=ins[a]))
            first += [copy(a, 1 + j, me, (*chip, c), src=ins[a]) for j, chip in enumerate(chips)]
        return x, y, c, chips, sibling, copy, mine, first

    def start(ins, outs, sems):
        *_, mine, first = plan(ins, outs, sems)
        for cp in mine + first:
            cp.start()

    def finish(ins, outs, sems):
        x, y, c, chips, sibling, copy, mine, first = plan(ins, outs, sems)
        passed = []
        for j, (px, py) in enumerate(chips):
            for a in range(n):
                blk = 4 * px + 2 * py + c
                copy(a, 1 + j, blk, sibling).wait_recv()
                passed.append(copy(a, 4 + j, blk, sibling))
                passed[-1].start()
        for a in range(n):
            copy(a, 0, 4 * x + 2 * y + 1 - c, sibling).wait_recv()
            for j, (px, py) in enumerate(chips):
                copy(a, 4 + j, 4 * px + 2 * py + 1 - c, sibling).wait_recv()
        for cp in first + passed:
            cp.wait_send()
        for cp in mine:
            cp.wait()

    return _Stage(list(arrs), [_sds(_whole(a.shape, ax), a.dtype) for a, ax in zip(arrs, axes)],
                  [pltpu.SemaphoreType.DMA((n, 7)), pltpu.SemaphoreType.DMA((n, 7)), pltpu.SemaphoreType.DMA((n,))],
                  start, finish)


def _pair_stage(dws, shapes, axes):
    n = len(dws)

    def copies(ins, outs, sems):
        send_sems, recv_sems = sems
        x, y, c, _ = _place()
        return [pltpu.make_async_remote_copy(src_ref=_block(ins[a], 2 * k + 1 - c, shapes[a], axes[a]),
                                             dst_ref=outs[a].at[k],
                                             send_sem=send_sems.at[a, k], recv_sem=recv_sems.at[a, k],
                                             device_id=(x, y, 1 - c), device_id_type=MESH)
                for a in range(n) for k in range(4)]

    def start(ins, outs, sems):
        for cp in copies(ins, outs, sems):
            cp.start()

    def finish(ins, outs, sems):
        for cp in copies(ins, outs, sems):
            cp.wait()

    return _Stage(list(dws), [_sds((4,) + tuple(s), a.dtype) for a, s in zip(dws, shapes)],
                  [pltpu.SemaphoreType.DMA((n, 4)), pltpu.SemaphoreType.DMA((n, 4))], start, finish)


def _rs_pair_add(name, dw, got, c_arr, axis):
    _, R, C = got.shape
    tr, tc = _tile(R, 512, 16), _tile(C, 1536, LANES)
    per = R // tr if axis == 0 else C // tc

    def body(c_ref, a_ref, b_ref, o_ref):
        o_ref[...] = (a_ref[...].astype(F32) + b_ref[...].astype(F32)).astype(o_ref.dtype)

    if axis == 0:
        mine = pl.BlockSpec((tr, tc), lambda k, i, j, c_ref: ((2 * k + c_ref[0]) * per + i, j))
    else:
        mine = pl.BlockSpec((tr, tc), lambda k, i, j, c_ref: (i, (2 * k + c_ref[0]) * per + j))
    return pl.pallas_call(
        body,
        grid_spec=pltpu.PrefetchScalarGridSpec(
            num_scalar_prefetch=1, grid=(4, R // tr, C // tc),
            in_specs=[mine, pl.BlockSpec((None, tr, tc), lambda k, i, j, c_ref: (k, i, j))],
            out_specs=pl.BlockSpec((None, tr, tc), lambda k, i, j, c_ref: (k, i, j))),
        out_shape=_sds((4, R, C), BF16), name=name,
        compiler_params=_cp("parallel", "parallel", "parallel"))(c_arr, dw, got)


def _chip_stage(parts):
    n = len(parts)

    def plan(ins, outs, sems):
        send_sems, recv_sems, local_sems = sems
        x, y, c, chips = _place()
        my_chip = 2 * x + y
        mine = [pltpu.make_async_copy(ins[a].at[my_chip], outs[a].at[my_chip], local_sems.at[a]) for a in range(n)]
        sends = [pltpu.make_async_remote_copy(src_ref=ins[a].at[2 * px + py], dst_ref=outs[a].at[my_chip],
                                              send_sem=send_sems.at[a, j], recv_sem=recv_sems.at[a, j],
                                              device_id=(px, py, c), device_id_type=MESH)
                 for a in range(n) for j, (px, py) in enumerate(chips)]
        arrivals = lambda: [pltpu.make_async_remote_copy(src_ref=ins[a].at[my_chip], dst_ref=outs[a].at[2 * px + py],
                                                         send_sem=send_sems.at[a, j], recv_sem=recv_sems.at[a, j],
                                                         device_id=(px, py, c), device_id_type=MESH)
                            for a in range(n) for j, (px, py) in enumerate(chips)]
        return mine, sends, arrivals

    def start(ins, outs, sems):
        mine, sends, _ = plan(ins, outs, sems)
        for cp in mine + sends:
            cp.start()

    def finish(ins, outs, sems):
        mine, sends, arrivals = plan(ins, outs, sems)
        for cp in arrivals():
            cp.wait_recv()
        for cp in sends:
            cp.wait_send()
        for cp in mine:
            cp.wait()

    return _Stage(list(parts), [_sds(a.shape, a.dtype) for a in parts],
                  [pltpu.SemaphoreType.DMA((n, 3)), pltpu.SemaphoreType.DMA((n, 3)), pltpu.SemaphoreType.DMA((n,))],
                  start, finish)


class _Exchange:
    ROW_SHARDED = ("sco", "ao", "dn0", "dn1")

    def __init__(self, shards, c_arr):
        self.sh, self.c_arr = shards, c_arr
        self.W, self.dw, self.parts, self.sums = {}, {}, {}, {}

    def axis(self, key):
        return 0 if key in self.ROW_SHARDED else 1

    def w(self, key):
        return self.W[key]

    def ag(self, keys):
        return _ag_stage([self.sh[k] for k in keys], [self.axis(k) for k in keys])

    def ag_done(self, keys, outs):
        self.W.update(zip(keys, outs))

    def grad(self, key, dw):
        self.dw[key] = dw

    def pair(self, keys):
        return _pair_stage([self.dw[k] for k in keys], [self.sh[k].shape for k in keys], [self.axis(k) for k in keys])

    def pair_done(self, keys, outs):
        for k, got in zip(keys, outs):
            self.parts[k] = _rs_pair_add(f"rs_add_{k}", self.dw[k], got, self.c_arr, self.axis(k))

    def chip(self, keys):
        return _chip_stage([self.parts[k] for k in keys])

    def chip_done(self, keys, outs):
        self.sums.update(zip(keys, outs))


def _sum_slots(name, a):
    _, rows, _ = a.shape

    def body(a_ref, o_ref):
        s = a_ref[0]
        for k in range(1, N_DEV):
            s = s + a_ref[k]
        o_ref[...] = s

    return pl.pallas_call(body, out_shape=_sds((rows, LANES), F32), name=name)(a)


def _cast_bf16(name, w3, l):
    _, R, C = w3.shape
    tr, tc = _tile(R, 512, 16), _tile(C, 1536, LANES)

    def body(w_ref, o_ref):
        o_ref[...] = w_ref[...].astype(BF16)

    return pl.pallas_call(
        body, grid=(R // tr, C // tc), in_specs=[pl.BlockSpec((None, tr, tc), lambda i, j: (l, i, j))],
        out_specs=pl.BlockSpec((tr, tc), lambda i, j: (i, j)), out_shape=_sds((R, C), BF16),
        name=name, compiler_params=_cp("parallel", "parallel"))(w3)


def _adamw(name, g_slots, w3, m3, v3, l, prev):
    n_slots, R, C = g_slots.shape
    tr, tc = _tile(R, 256, 8), _tile(C, 1536, LANES)
    c1, c2 = 1.0 - ADAM_B1 ** ADAM_STEP, 1.0 - ADAM_B2 ** ADAM_STEP

    def body(g_ref, w_ref, m_ref, v_ref, *rest):
        og, od, om, ov = rest[-4:]
        g = g_ref[0].astype(F32)
        for k in range(1, n_slots):
            g = g + g_ref[k].astype(F32)
        m = ADAM_B1 * m_ref[...] + (1.0 - ADAM_B1) * g
        v = ADAM_B2 * v_ref[...] + (1.0 - ADAM_B2) * (g * g)
        og[...] = g
        om[...] = m
        ov[...] = v
        od[...] = -ADAM_LR * ((m / c1) / (jnp.sqrt(v / c2) + ADAM_EPS) + ADAM_WD * w_ref[...])

    lay = pl.BlockSpec((None, tr, tc), lambda i, j: (l, i, j))
    ops = [g_slots, w3, m3, v3]
    specs = [pl.BlockSpec((n_slots, tr, tc), lambda i, j: (0, i, j)), lay, lay, lay]
    aliases = {}
    if prev is not None:
        ops += list(prev)
        specs += _any_specs(4)
        aliases = {4 + k: k for k in range(4)}
    return pl.pallas_call(
        body, grid=(R // tr, C // tc), in_specs=specs, out_specs=[lay] * 4,
        out_shape=[_sds(w3.shape, F32)] * 4, input_output_aliases=aliases,
        name=name, compiler_params=_cp("parallel", "parallel"))(*ops)


def _pack(parts):
    flat = jnp.concatenate([p.reshape(-1) for p in parts])
    pad = (-flat.shape[0]) % (HALO * LANES)
    return jnp.pad(flat, (0, pad)).reshape(-1, LANES)


def _unpack(packed, shapes):
    flat = packed.reshape(-1)
    out, at = [], 0
    for s in shapes:
        n = int(np.prod(s))
        out.append(flat[at:at + n].reshape(s))
        at += n
    return out


def kernel(x, mix_norm_g, ffn_norm_g, final_norm_g, sc_w_in, sc_conv_w, sc_conv_b, sc_w_out, attn_w_qkv, attn_w_out, ffn_w_up, ffn_conv_w, ffn_conv_b, ffn_w_down, loss_target, m_mix_norm_g, m_ffn_norm_g, m_final_norm_g, m_sc_w_in, m_sc_conv_w, m_sc_conv_b, m_sc_w_out, m_attn_w_qkv, m_attn_w_out, m_ffn_w_up, m_ffn_conv_w, m_ffn_conv_b, m_ffn_w_down, v_mix_norm_g, v_ffn_norm_g, v_final_norm_g, v_sc_w_in, v_sc_conv_w, v_sc_conv_b, v_sc_w_out, v_attn_w_qkv, v_attn_w_out, v_ffn_w_up, v_ffn_conv_w, v_ffn_conv_b, v_ffn_w_down):
    n_layers = ffn_w_up.shape[0]
    me = 4 * lax.axis_index("x") + 2 * lax.axis_index("y") + lax.axis_index("c")
    c_arr = lax.axis_index("c").astype(jnp.int32).reshape(1)

    big = [("in", sc_w_in, 0), ("sco", sc_w_out, 0), ("qkv", attn_w_qkv, 0), ("ao", attn_w_out, 0)]
    big += [(f"up{l}", ffn_w_up, l) for l in range(n_layers)] + [(f"dn{l}", ffn_w_down, l) for l in range(n_layers)]
    ex = _Exchange({nm: _cast_bf16(f"cast_{nm}", w, l) for nm, w, l in big}, c_arr)

    cw_shapes = [sc_conv_w.shape, ffn_conv_w.shape]
    cw_mine = _pack([sc_conv_w, ffn_conv_w])
    w_in_all, cw_all = _run_stage("gather_first", _join([ex.ag(["in"]), _ag_stage([cw_mine], [0])]))
    ex.ag_done(["in"], [w_in_all])
    cw_all = cw_all.reshape((N_DEV,) + cw_mine.shape)
    sc_cw_all, ffn_cw_all = zip(*[_unpack(cw_all[s], cw_shapes) for s in range(N_DEV)])
    sc_cw = jnp.concatenate(sc_cw_all, axis=-1)[0]
    ffn_cw = jnp.concatenate(ffn_cw_all, axis=-1)
    S = {"mix_g": mix_norm_g, "ffn_g": ffn_norm_g, "fin_g": final_norm_g, "sc_cw": sc_cw, "sc_cb": sc_conv_b[0],
         "ffn_cw": ffn_cw, "ffn_cb": ffn_conv_b}

    loss_part, grad_x, dS = _fwd_bwd(x[0], loss_target[0], S, ex)

    small_names = ["mix_g", "ffn_g", "fin_g", "sc_cb", "ffn_cb", "sc_cw", "ffn_cw"]
    small_parts = [dS[k] for k in small_names] + [loss_part.reshape(1)]
    small_mine = _pack(small_parts)
    got_in, small_all = _run_stage("rs_pair_last", _join([ex.pair(["in"]), _ag_stage([small_mine], [0])]))
    ex.pair_done(["in"], [got_in])
    ex.chip_done(["in"], _run_stage("rs_chip_last", ex.chip(["in"])))
    small_sum = _sum_slots("sum_small", small_all.reshape((N_DEV,) + small_mine.shape))
    g_mix, g_ffn, g_fin, g_scb, g_fcb, g_scw, g_fcw, loss = _unpack(small_sum, [p.shape for p in small_parts])
    g_scw = lax.dynamic_slice_in_dim(g_scw, me * sc_conv_w.shape[-1], sc_conv_w.shape[-1], axis=-1)[None]
    g_fcw = lax.dynamic_slice_in_dim(g_fcw, me * ffn_conv_w.shape[-1], ffn_conv_w.shape[-1], axis=-1)
    g_scb = g_scb[None]
    small_g = [g_mix, g_ffn, g_fin, g_scw, g_scb, g_fcw, g_fcb]
    small_w = [mix_norm_g, ffn_norm_g, final_norm_g, sc_conv_w, sc_conv_b, ffn_conv_w, ffn_conv_b]
    small_m = [m_mix_norm_g, m_ffn_norm_g, m_final_norm_g, m_sc_conv_w, m_sc_conv_b, m_ffn_conv_w, m_ffn_conv_b]
    small_v = [v_mix_norm_g, v_ffn_norm_g, v_final_norm_g, v_sc_conv_w, v_sc_conv_b, v_ffn_conv_w, v_ffn_conv_b]
    small_out = _adamw("adamw_small", _pack(small_g)[None], _pack(small_w)[None], _pack(small_m)[None],
                       _pack(small_v)[None], 0, None)
    small_shapes = [w.shape for w in small_w]
    sg, sd, sm, sv = [_unpack(o[0], small_shapes) for o in small_out]

    moments = {"in": (m_sc_w_in, v_sc_w_in), "sco": (m_sc_w_out, v_sc_w_out), "qkv": (m_attn_w_qkv, v_attn_w_qkv),
               "ao": (m_attn_w_out, v_attn_w_out), "up": (m_ffn_w_up, v_ffn_w_up), "dn": (m_ffn_w_down, v_ffn_w_down)}
    upd = {}
    for nm, w, l in big:
        key = nm.rstrip("0123456789")
        upd[key] = _adamw(f"adamw_{nm}", ex.sums[nm], w, moments[key][0], moments[key][1], l, upd.get(key))

    def leaves(k):
        return [sg, sd, sm, sv][k][0:3] + [upd["in"][k], [sg, sd, sm, sv][k][3], [sg, sd, sm, sv][k][4], upd["sco"][k],
                                          upd["qkv"][k], upd["ao"][k], upd["up"][k], [sg, sd, sm, sv][k][5],
                                          [sg, sd, sm, sv][k][6], upd["dn"][k]]

    return (loss.reshape(()), grad_x[None], *leaves(0), *leaves(1), *leaves(2), *leaves(3))
```

```python
import math

import numpy as np
import jax
import jax.numpy as jnp
from jax import lax
from jax.experimental import pallas as pl
from jax.experimental.pallas import tpu as pltpu

F32 = jnp.float32
BF16 = jnp.bfloat16
MESH = pl.DeviceIdType.MESH

HEAD_DIM = 128
DILATED_GROUPS = ((128, 1), (512, 4), (2048, 16))
DILATIONS = tuple(d for _, d in DILATED_GROUPS)
BAND = (DILATED_GROUPS[0][0] // 2) // DILATED_GROUPS[0][1]
assert all((w // 2) // d == BAND for w, d in DILATED_GROUPS)
NORM_EPS = 1e-5
ALIBI_MAX = 8.0
NEG_INF = -1e30
ADAM_LR, ADAM_B1, ADAM_B2, ADAM_EPS, ADAM_WD, ADAM_STEP = 0.001, 0.9, 0.999, 1e-08, 0.01, 10

N_DEV = 8
LANES = 128
HALO = 16
VMEM_LIMIT = 56 * 1024 * 1024


def _cp(*sem):
    return pltpu.CompilerParams(dimension_semantics=sem, vmem_limit_bytes=VMEM_LIMIT)


def _tile(n, pref, mult):
    t = (min(n, pref) // mult) * mult
    while t >= mult:
        if n % t == 0:
            return t
        t -= mult
    return n


def _sds(shape, dtype):
    return jax.ShapeDtypeStruct(shape, dtype)


def _any_specs(n):
    return [pl.BlockSpec(memory_space=pl.ANY)] * n


class _Stage:
    def __init__(self, arrays, out_shapes, sems, start, finish):
        self.arrays, self.out_shapes, self.sems, self.start, self.finish = arrays, out_shapes, sems, start, finish


def _join(stages):
    stages = [s for s in stages if s is not None]
    if not stages:
        return None

    def split(refs, count):
        out, at = [], 0
        for s in stages:
            out.append(refs[at:at + count(s)])
            at += count(s)
        return out

    def each(which):
        def run(ins, outs, sems):
            parts = zip(split(ins, lambda s: len(s.arrays)), split(outs, lambda s: len(s.out_shapes)),
                        split(sems, lambda s: len(s.sems)))
            for s, (i, o, m) in zip(stages, parts):
                getattr(s, which)(i, o, m)
        return run

    return _Stage(sum([s.arrays for s in stages], []), sum([s.out_shapes for s in stages], []),
                  sum([s.sems for s in stages], []), each("start"), each("finish"))


def _run_stage(name, st):
    n, m = len(st.arrays), len(st.out_shapes)

    def body(*refs):
        ins, outs, sems = refs[:n], refs[n:n + m], refs[n + m:]
        st.start(ins, outs, sems)
        st.finish(ins, outs, sems)

    return pl.pallas_call(body, in_specs=_any_specs(n), out_specs=_any_specs(m), out_shape=st.out_shapes,
                          scratch_shapes=st.sems, name=name)(*st.arrays)


NN = (((1,), (0,)), ((), ()))
NT = (((1,), (1,)), ((), ()))
TN = (((0,), (0,)), ((), ()))


def _mm(name, operands, in_specs, out_sds, o_spec, grid, dims, acc_shape, has_res=False, aliases=None, comm=None):
    nk = grid[2]
    n_in = len(operands)
    n_ci, n_co = (len(comm.arrays), len(comm.out_shapes)) if comm else (0, 0)

    def body(*refs):
        a_ref, b_ref = refs[0], refs[1]
        r_ref = refs[2] if has_res else None
        o_ref = refs[n_in + n_ci]
        acc = refs[n_in + n_ci + 1 + n_co]
        c_refs = (refs[n_in:n_in + n_ci], refs[n_in + n_ci + 1:n_in + n_ci + 1 + n_co], refs[n_in + n_ci + 2 + n_co:])
        ids = [pl.program_id(q) for q in range(3)]
        if comm:
            @pl.when((ids[0] == 0) & (ids[1] == 0) & (ids[2] == 0))
            def _():
                comm.start(*c_refs)

        part = lax.dot_general(a_ref[...], b_ref[...], dims, preferred_element_type=F32)

        def finish(total):
            if has_res:
                total = total + r_ref[...]
            o_ref[...] = total.astype(o_ref.dtype)

        if nk == 1:
            finish(part)
        else:
            k = ids[2]

            @pl.when(k == 0)
            def _():
                acc[...] = part

            @pl.when((k > 0) & (k < nk - 1))
            def _():
                acc[...] += part

            @pl.when(k == nk - 1)
            def _():
                finish(acc[...] + part)

        if comm:
            @pl.when((ids[0] == grid[0] - 1) & (ids[1] == grid[1] - 1) & (ids[2] == nk - 1))
            def _():
                comm.finish(*c_refs)

    scratch = [pltpu.VMEM(acc_shape if nk > 1 else (8, LANES), F32)]
    if not comm:
        out = pl.pallas_call(
            body, grid=grid, in_specs=in_specs, out_specs=o_spec, out_shape=out_sds, scratch_shapes=scratch,
            input_output_aliases=aliases or {}, name=name,
            compiler_params=_cp("parallel", "parallel", "arbitrary"))(*operands)
        return out, []
    outs = pl.pallas_call(
        body, grid=grid, in_specs=list(in_specs) + _any_specs(n_ci), out_specs=[o_spec] + _any_specs(n_co),
        out_shape=[out_sds] + comm.out_shapes, scratch_shapes=scratch + comm.sems,
        input_output_aliases=aliases or {}, name=name,
        compiler_params=_cp("arbitrary", "arbitrary", "arbitrary"))(*operands, *comm.arrays)
    return outs[0], list(outs[1:])


def _stack(a):
    return a if a.ndim == 3 else a[None]


def _mm_nn(name, a, w, res, out_dtype, col0=0, n_cols=None, tk_pref=2048, comm=None):
    M, K = a.shape
    N = n_cols or w.shape[1]
    tm, tn, tk = _tile(M, 1024, 16), _tile(N, 1024, LANES), _tile(K, tk_pref, LANES)
    c0 = col0 // tn
    ops = [a, w]
    specs = [pl.BlockSpec((tm, tk), lambda i, j, k: (i, k)), pl.BlockSpec((tk, tn), lambda i, j, k: (k, c0 + j))]
    if res is not None:
        ops.append(res)
        specs.append(pl.BlockSpec((tm, tn), lambda i, j, k: (i, j)))
    return _mm(name, tuple(ops), specs, _sds((M, N), out_dtype), pl.BlockSpec((tm, tn), lambda i, j, k: (i, j)),
               (M // tm, N // tn, K // tk), NN, (tm, tn), has_res=res is not None, comm=comm)


def _mm_nt(name, dy, w, out_dtype, col0=0, tn_pref=1024, tk_pref=2048, comm=None):
    dy = _stack(dy)
    _, M, Np = dy.shape
    Kw = w.shape[0]
    tm, tn, tk = _tile(M, 1024, 16), _tile(Kw, tn_pref, LANES), _tile(Np, tk_pref, LANES)
    per, c0 = Np // tk, col0 // tk
    return _mm(name, (dy, w),
               [pl.BlockSpec((None, tm, tk), lambda i, j, k: (k // per, i, k % per)),
                pl.BlockSpec((tn, tk), lambda i, j, k: (j, c0 + k))],
               _sds((M, Kw), out_dtype), pl.BlockSpec((tm, tn), lambda i, j, k: (i, j)),
               (M // tm, Kw // tn, dy.shape[0] * per), NT, (tm, tn), comm=comm)


def _mm_tn(name, a, dy, n_total=None, col0=0, tkr_pref=1024, tn_pref=1024, prev=None, comm=None):
    dy = _stack(dy)
    P, M, Np = dy.shape
    Kw = a.shape[1]
    tkr, tn, tk = _tile(Kw, tkr_pref, LANES), _tile(Np, tn_pref, LANES), _tile(M, 2048, 16)
    per, c0 = Np // tn, col0 // tn
    ops = [a, dy]
    specs = [pl.BlockSpec((tk, tkr), lambda i, j, k: (k, i)),
             pl.BlockSpec((None, tk, tn), lambda i, j, k: (j // per, k, j % per))]
    aliases = None
    if prev is not None:
        ops.append(prev)
        specs.append(pl.BlockSpec(memory_space=pl.ANY))
        aliases = {2: 0}
    return _mm(name, tuple(ops), specs, _sds((Kw, n_total or P * Np), BF16),
               pl.BlockSpec((tkr, tn), lambda i, j, k: (i, c0 + j)),
               (Kw // tkr, P * per, M // tk), TN, (tkr, tn), aliases=aliases, comm=comm)


def _rmsnorm_fwd(name, x, g):
    T, D = x.shape
    tr = _tile(T, 512, 16)

    def body(x_ref, g_ref, h_ref):
        xf = x_ref[...]
        r = lax.rsqrt(jnp.mean(xf * xf, axis=-1, keepdims=True) + NORM_EPS)
        h_ref[...] = (xf * r * g_ref[...]).astype(h_ref.dtype)

    return pl.pallas_call(
        body, grid=(T // tr,),
        in_specs=[pl.BlockSpec((tr, D), lambda i: (i, 0)), pl.BlockSpec((1, D), lambda i: (0, 0))],
        out_specs=pl.BlockSpec((tr, D), lambda i: (i, 0)), out_shape=_sds((T, D), BF16),
        name=name, compiler_params=_cp("parallel"))(x, g.reshape(1, D))


def _rmsnorm_bwd(name, x, g, dhs, dres, comm=None):
    T, D = x.shape
    tr = _tile(T, 256, 16)
    n_dh = len(dhs)
    n_in = 3 + n_dh
    n_ci, n_co = (len(comm.arrays), len(comm.out_shapes)) if comm else (0, 0)

    def body(*refs):
        x_ref, g_ref = refs[0], refs[1]
        dh_refs = refs[2:2 + n_dh]
        dres_ref = refs[2 + n_dh]
        dx_ref, dxb_ref, dg_ref = refs[n_in + n_ci:n_in + n_ci + 3]
        c_refs = (refs[n_in:n_in + n_ci], refs[n_in + n_ci + 3:n_in + n_ci + 3 + n_co], refs[n_in + n_ci + 3 + n_co:])
        if comm:
            @pl.when(pl.program_id(0) == 0)
            def _():
                comm.start(*c_refs)

        xf = x_ref[...]
        r = lax.rsqrt(jnp.mean(xf * xf, axis=-1, keepdims=True) + NORM_EPS)
        xhat = xf * r
        dh = dh_refs[0][...].astype(F32)
        for q in dh_refs[1:]:
            dh = dh + q[...].astype(F32)
        dy = dh * g_ref[...]
        c = jnp.mean(dy * xhat, axis=-1, keepdims=True)
        dx = dres_ref[...] + r * (dy - xhat * c)
        dx_ref[...] = dx
        dxb_ref[...] = dx.astype(BF16)

        @pl.when(pl.program_id(0) == 0)
        def _():
            dg_ref[...] = jnp.zeros_like(dg_ref)

        dg_ref[...] += jnp.sum(dh * xhat, axis=0, keepdims=True)

        if comm:
            @pl.when(pl.program_id(0) == T // tr - 1)
            def _():
                comm.finish(*c_refs)

    row = pl.BlockSpec((tr, D), lambda i: (i, 0))
    vec = pl.BlockSpec((1, D), lambda i: (0, 0))
    outs = pl.pallas_call(
        body, grid=(T // tr,), in_specs=[row, vec] + [row] * n_dh + [row] + _any_specs(n_ci),
        out_specs=[row, row, vec] + _any_specs(n_co),
        out_shape=[_sds((T, D), F32), _sds((T, D), BF16), _sds((1, D), F32)] + (comm.out_shapes if comm else []),
        scratch_shapes=comm.sems if comm else [],
        name=name, compiler_params=_cp("arbitrary"))(x, g.reshape(1, D), *dhs, dres, *(comm.arrays if comm else []))
    return (outs[0], outs[1], outs[2][0]), list(outs[3:])


def _final_loss(name, x, g, tgt):
    T, D = x.shape
    tr = _tile(T, 256, 16)

    def body(x_ref, g_ref, t_ref, dx_ref, dxb_ref, dg_ref, loss_ref):
        xf = x_ref[...]
        r = lax.rsqrt(jnp.mean(xf * xf, axis=-1, keepdims=True) + NORM_EPS)
        xhat = xf * r
        err = xhat * g_ref[...] - t_ref[...]
        dy = err * (1.0 / D)
        dxh = dy * g_ref[...]
        c = jnp.mean(dxh * xhat, axis=-1, keepdims=True)
        dx = r * (dxh - xhat * c)
        dx_ref[...] = dx
        dxb_ref[...] = dx.astype(BF16)

        @pl.when(pl.program_id(0) == 0)
        def _():
            dg_ref[...] = jnp.zeros_like(dg_ref)
            loss_ref[...] = jnp.zeros_like(loss_ref)

        dg_ref[...] += jnp.sum(dy * xhat, axis=0, keepdims=True)
        loss_ref[...] += 0.5 * jnp.sum(jnp.mean(err * err, axis=-1, keepdims=True), axis=0, keepdims=True)

    row = pl.BlockSpec((tr, D), lambda i: (i, 0))
    vec = pl.BlockSpec((1, D), lambda i: (0, 0))
    dx, dx_b, dg, loss = pl.pallas_call(
        body, grid=(T // tr,), in_specs=[row, vec, row],
        out_specs=[row, row, vec, pl.BlockSpec((1, 1), lambda i: (0, 0))],
        out_shape=[_sds((T, D), F32), _sds((T, D), BF16), _sds((1, D), F32), _sds((1, 1), F32)],
        name=name, compiler_params=_cp("arbitrary"))(x, g.reshape(1, D), tgt)
    return loss[0, 0], dx, dx_b, dg[0]


def _halo_specs(tr, tc, n_rows, col):
    rb = tr // HALO
    last = n_rows // HALO - 1
    return [pl.BlockSpec((tr, tc), lambda *g: (g[-1], col(*g))),
            pl.BlockSpec((HALO, tc), lambda *g: (jnp.maximum(g[-1] * rb - 1, 0), col(*g))),
            pl.BlockSpec((HALO, tc), lambda *g: (jnp.minimum((g[-1] + 1) * rb, last), col(*g)))]


def _ext(cur_ref, prev_ref, next_ref, i, n_i):
    p = prev_ref[...].astype(F32) * (i > 0).astype(F32)
    n = next_ref[...].astype(F32) * (i < n_i - 1).astype(F32)
    return jnp.concatenate([p, cur_ref[...].astype(F32), n], axis=0)


def _shift_dn(x):
    return pltpu.roll(x, 1, axis=0)


def _shift_up(x):
    return pltpu.roll(x, x.shape[0] - 1, axis=0)


def _conv(x, w_ref, b_ref):
    return w_ref[0:1, :] * _shift_dn(x) + w_ref[1:2, :] * x + w_ref[2:3, :] * _shift_up(x) + b_ref[...]


def _mid(x, tr):
    return x[HALO:HALO + tr, :]


def _conv_t(g, w_ref):
    return w_ref[0:1, :] * _shift_up(g) + w_ref[1:2, :] * g + w_ref[2:3, :] * _shift_dn(g)


def _conv_wgrad(acc_ref, g, x, tr, first):
    gm = _mid(g, tr)

    @pl.when(first)
    def _():
        acc_ref[...] = jnp.zeros_like(acc_ref)

    acc_ref[0:1, :] += jnp.sum(gm * _mid(_shift_dn(x), tr), axis=0, keepdims=True)
    acc_ref[1:2, :] += jnp.sum(gm * _mid(x, tr), axis=0, keepdims=True)
    acc_ref[2:3, :] += jnp.sum(gm * _mid(_shift_up(x), tr), axis=0, keepdims=True)
    acc_ref[3:4, :] += jnp.sum(gm, axis=0, keepdims=True)


def _sigmoid(a):
    return 1.0 / (1.0 + jnp.exp(-a))


def _ffn_gate_fwd(name, up, cw, cb):
    T, F2 = up.shape
    F = F2 // 2
    tc, tr = _tile(F, 512, LANES), _tile(T, 512, HALO)
    nF, n_i = F // tc, T // tr

    def body(ac, ap, an, bc, bp, bn, wa, wb, ba, bb, o_ref):
        i = pl.program_id(1)
        ua = _mid(_conv(_ext(ac, ap, an, i, n_i), wa, ba), tr)
        ub = _mid(_conv(_ext(bc, bp, bn, i, n_i), wb, bb), tr)
        o_ref[...] = (ua * _sigmoid(ua) * ub).astype(o_ref.dtype)

    wspec = lambda o: pl.BlockSpec((3, tc), lambda j, i: (0, j + o))
    bspec = lambda o: pl.BlockSpec((1, tc), lambda j, i: (0, j + o))
    return pl.pallas_call(
        body, grid=(nF, n_i),
        in_specs=_halo_specs(tr, tc, T, lambda j, i: j) + _halo_specs(tr, tc, T, lambda j, i: j + nF)
        + [wspec(0), wspec(nF), bspec(0), bspec(nF)],
        out_specs=pl.BlockSpec((tr, tc), lambda j, i: (i, j)), out_shape=_sds((T, F), BF16),
        name=name, compiler_params=_cp("parallel", "parallel"))(up, up, up, up, up, up, cw, cw, cb, cb)


def _ffn_gate_bwd(name, up, dact, cw, cb):
    T, F2 = up.shape
    F = F2 // 2
    tc, tr = _tile(F, 512, LANES), _tile(T, 512, HALO)
    nF, n_i = F // tc, T // tr

    def body(ac, ap, an, bc, bp, bn, dc, dp, dn, wa, wb, ba, bb, o_ref, wga_ref, wgb_ref):
        i = pl.program_id(1)
        xa = _ext(ac, ap, an, i, n_i)
        xb = _ext(bc, bp, bn, i, n_i)
        da = _ext(dc, dp, dn, i, n_i)
        ua = _conv(xa, wa, ba)
        sig = _sigmoid(ua)
        ga = da * _conv(xb, wb, bb) * (sig * (1.0 + ua * (1.0 - sig)))
        o_ref[0] = _mid(_conv_t(ga, wa), tr).astype(o_ref.dtype)
        _conv_wgrad(wga_ref, ga, xa, tr, i == 0)
        gb = da * (ua * sig)
        o_ref[1] = _mid(_conv_t(gb, wb), tr).astype(o_ref.dtype)
        _conv_wgrad(wgb_ref, gb, xb, tr, i == 0)

    wspec = lambda o: pl.BlockSpec((3, tc), lambda j, i: (0, j + o))
    bspec = lambda o: pl.BlockSpec((1, tc), lambda j, i: (0, j + o))
    wg = pl.BlockSpec((8, tc), lambda j, i: (0, j))
    dup, wga, wgb = pl.pallas_call(
        body, grid=(nF, n_i),
        in_specs=_halo_specs(tr, tc, T, lambda j, i: j) + _halo_specs(tr, tc, T, lambda j, i: j + nF)
        + _halo_specs(tr, tc, T, lambda j, i: j) + [wspec(0), wspec(nF), bspec(0), bspec(nF)],
        out_specs=[pl.BlockSpec((2, tr, tc), lambda j, i: (0, i, j)), wg, wg],
        out_shape=[_sds((2, T, F), BF16), _sds((8, F), F32), _sds((8, F), F32)],
        name=name, compiler_params=_cp("parallel", "arbitrary"),
    )(up, up, up, up, up, up, dact, dact, dact, cw, cw, cb, cb)
    return dup, jnp.concatenate([wga, wgb], axis=1)


def _sc_gate_fwd(name, z, cw, cb):
    T, D3 = z.shape
    D = D3 // 3
    tc, tr = _tile(D, 512, LANES), _tile(T, 512, HALO)
    nD, n_i = D // tc, T // tr

    def body(uc, up_, un, gb, cc, cp, cn, w, b, o_ref):
        i = pl.program_id(1)
        cu = _ext(cc, cp, cn, i, n_i) * _ext(uc, up_, un, i, n_i)
        o_ref[...] = (gb[...].astype(F32) * _mid(_conv(cu, w, b), tr)).astype(o_ref.dtype)

    return pl.pallas_call(
        body, grid=(nD, n_i),
        in_specs=_halo_specs(tr, tc, T, lambda j, i: j) + [pl.BlockSpec((tr, tc), lambda j, i: (i, j + nD))]
        + _halo_specs(tr, tc, T, lambda j, i: j + 2 * nD)
        + [pl.BlockSpec((3, tc), lambda j, i: (0, j)), pl.BlockSpec((1, tc), lambda j, i: (0, j))],
        out_specs=pl.BlockSpec((tr, tc), lambda j, i: (i, j)), out_shape=_sds((T, D), BF16),
        name=name, compiler_params=_cp("parallel", "parallel"))(z, z, z, z, z, z, z, cw, cb)


def _sc_gate_bwd(name, z, dy, cw, cb):
    T, D3 = z.shape
    D = D3 // 3
    tc, tr = _tile(D, 512, LANES), _tile(T, 512, HALO)
    nD, n_i = D // tc, T // tr

    def body(uc, up_, un, bc, bp, bn, cc, cp, cn, yc, yp, yn, w, b, o_ref, wg_ref):
        i = pl.program_id(1)
        u = _ext(uc, up_, un, i, n_i)
        gc = _ext(cc, cp, cn, i, n_i)
        cu = gc * u
        g = _ext(yc, yp, yn, i, n_i) * _ext(bc, bp, bn, i, n_i)
        dcu = _mid(_conv_t(g, w), tr)
        o_ref[0] = (dcu * _mid(gc, tr)).astype(o_ref.dtype)
        o_ref[1] = (yc[...].astype(F32) * _mid(_conv(cu, w, b), tr)).astype(o_ref.dtype)
        o_ref[2] = (dcu * _mid(u, tr)).astype(o_ref.dtype)
        _conv_wgrad(wg_ref, g, cu, tr, i == 0)

    hs = lambda o: _halo_specs(tr, tc, T, lambda j, i: j + o)
    return pl.pallas_call(
        body, grid=(nD, n_i),
        in_specs=hs(0) + hs(nD) + hs(2 * nD) + hs(0)
        + [pl.BlockSpec((3, tc), lambda j, i: (0, j)), pl.BlockSpec((1, tc), lambda j, i: (0, j))],
        out_specs=[pl.BlockSpec((3, tr, tc), lambda j, i: (0, i, j)), pl.BlockSpec((8, tc), lambda j, i: (0, j))],
        out_shape=[_sds((3, T, D), BF16), _sds((8, D), F32)],
        name=name, compiler_params=_cp("parallel", "arbitrary"),
    )(z, z, z, z, z, z, z, z, z, dy, dy, dy, cw, cb)


def _slopes(n_heads):
    return jnp.asarray(2.0 ** (-ALIBI_MAX * np.arange(1, n_heads + 1) / n_heads), dtype=F32)


CHAINS = 8


def _nq(L, d):
    return max(1, min(CHAINS if d == 1 else CHAINS // 2, L // LANES // 2))


def _srows(ref, r, start, n, d):
    if d == 1:
        return ref[start:start + n, :]
    return ref[pl.ds(start * d + r, n, stride=d), :]


def _win(p_ref, c_ref, n_ref, r, b, nq):
    lo, hi, top = b * LANES - BAND, b * LANES + LANES + BAND, nq * LANES
    parts = [p_ref[r]] if lo < 0 else []
    parts.append(c_ref[r, max(lo, 0):min(hi, top), :])
    if hi > top:
        parts.append(n_ref[r])
    return parts[0] if len(parts) == 1 else jnp.concatenate(parts, axis=0)


def _nat_win(p_ref, c_ref, n_ref, r, b, nq, d):
    lo, hi, top = b * LANES - BAND, b * LANES + LANES + BAND, nq * LANES
    parts = [_srows(p_ref, r, 0, BAND, d)] if lo < 0 else []
    parts.append(_srows(c_ref, r, max(lo, 0), min(hi, top) - max(lo, 0), d))
    if hi > top:
        parts.append(_srows(n_ref, r, 0, BAND, d))
    return parts[0] if len(parts) == 1 else jnp.concatenate(parts, axis=0)


def _biases(slope, d, n, n_steps, nq, q_rows, k_rows, q0, k0):
    qi = lax.broadcasted_iota(jnp.int32, (q_rows, k_rows), 0) + q0
    kj = lax.broadcasted_iota(jnp.int32, (q_rows, k_rows), 1) + k0
    dist = jnp.abs(kj - qi)
    base = jnp.where(dist <= BAND, -slope * (dist * d).astype(F32), NEG_INF)
    out = []
    for b in range(nq):
        t = base
        if b == 0:
            t = jnp.where((n == 0) & ((kj < 0) | (qi < 0)), NEG_INF, t)
        if b == nq - 1:
            t = jnp.where((n == n_steps - 1) & ((kj >= LANES) | (qi >= LANES)), NEG_INF, t)
        out.append(t)
    return out


def _win_specs(d, H, col, nq, L):
    return [pl.BlockSpec((d, BAND, LANES), lambda h, n: (0, jnp.maximum(2 * nq * n - 1, 0), col * H + h)),
            pl.BlockSpec((d, nq * LANES, LANES), lambda h, n: (0, n, col * H + h)),
            pl.BlockSpec((d, BAND, LANES), lambda h, n: (0, jnp.minimum(2 * nq * (n + 1), L // BAND - 1), col * H + h))]


def _nat_specs(d, nq, L):
    return [pl.BlockSpec((BAND * d, LANES), lambda h, n: (jnp.maximum(2 * nq * n - 1, 0), h)),
            pl.BlockSpec((nq * LANES * d, LANES), lambda h, n: (n, h)),
            pl.BlockSpec((BAND * d, LANES), lambda h, n: (jnp.minimum(2 * nq * (n + 1), L // BAND - 1), h))]


def _over_residues(d, nq, per_r):
    if d == 1:
        per_r(0, 0)
    else:
        lax.fori_loop(0, d, per_r, 0, unroll=min(d, max(1, CHAINS // nq)))


def _attn_fwd(name, qkv, d, H):
    T = qkv.shape[0]
    D = H * HEAD_DIM
    L = T // d
    nq = _nq(L, d)
    n_steps = L // (nq * LANES)
    scale = HEAD_DIM ** -0.5
    q3 = qkv.reshape(d, L, 3 * D)

    def body(s_ref, q_ref, kp, kc, kn, vp, vc, vn, o_ref, l_ref):
        h, n = pl.program_id(0), pl.program_id(1)
        bias = _biases(s_ref[h], d, n, n_steps, nq, LANES, 2 * LANES, 0, -BAND)

        def per_r(r, carry):
            for b in range(nq):
                k, v = _win(kp, kc, kn, r, b, nq), _win(vp, vc, vn, r, b, nq)
                s = lax.dot_general(q_ref[r, b * LANES:(b + 1) * LANES, :], k, NT, preferred_element_type=F32) * scale + bias[b]
                m = jnp.max(s, axis=1, keepdims=True)
                p = jnp.exp(s - m)
                den = jnp.sum(p, axis=1, keepdims=True)
                o = lax.dot_general(p.astype(BF16), v, NN, preferred_element_type=F32) / den
                lse = jnp.broadcast_to(m + jnp.log(den), (LANES, LANES))
                if d == 1:
                    o_ref[b * LANES:(b + 1) * LANES, :] = o
                    l_ref[b * LANES:(b + 1) * LANES, :] = lse
                else:
                    o_ref[pl.ds(b * LANES * d + r, LANES, stride=d), :] = o
                    l_ref[pl.ds(b * LANES * d + r, LANES, stride=d), :] = lse
            return carry

        _over_residues(d, nq, per_r)

    out = pl.BlockSpec((nq * LANES * d, LANES), lambda h, n: (n, h))
    return pl.pallas_call(
        body, grid=(H, n_steps),
        in_specs=[pl.BlockSpec(memory_space=pltpu.SMEM), pl.BlockSpec((d, nq * LANES, LANES), lambda h, n: (0, n, h))]
        + _win_specs(d, H, 1, nq, L) + _win_specs(d, H, 2, nq, L),
        out_specs=[out, out], out_shape=[_sds((T, D), F32), _sds((T, D), F32)],
        name=name, compiler_params=_cp("parallel", "parallel"))(_slopes(H), q3, q3, q3, q3, q3, q3, q3)


def _attn_combine(name, outs, lses):
    T, D = outs[0].shape
    tr, tc = _tile(T, 512, 16), _tile(D, 512, LANES)

    def body(o0, o1, o2, l0, l1, l2, ob_ref, of_ref, l_ref):
        a0, a1, a2 = l0[...], l1[...], l2[...]
        m = jnp.maximum(jnp.maximum(a0, a1), a2)
        e0, e1, e2 = jnp.exp(a0 - m), jnp.exp(a1 - m), jnp.exp(a2 - m)
        z = e0 + e1 + e2
        o = (e0 * o0[...] + e1 * o1[...] + e2 * o2[...]) / z
        of_ref[...] = o
        ob_ref[...] = o.astype(BF16)
        l_ref[...] = m + jnp.log(z)

    blk = pl.BlockSpec((tr, tc), lambda i, j: (i, j))
    return pl.pallas_call(
        body, grid=(T // tr, D // tc), in_specs=[blk] * 6, out_specs=[blk] * 3,
        out_shape=[_sds((T, D), BF16), _sds((T, D), F32), _sds((T, D), F32)],
        name=name, compiler_params=_cp("parallel", "parallel"))(*outs, *lses)


def _attn_stats(name, do, o, lse):
    T, D = do.shape
    tr = _tile(T, 1024, 16)

    def body(a, b, l, o_ref):
        delta = jnp.broadcast_to(jnp.sum(a[...] * b[...], axis=1, keepdims=True), o_ref.shape)
        lane = lax.broadcasted_iota(jnp.int32, o_ref.shape, 1)
        o_ref[...] = jnp.where(lane < BAND, l[...], delta)

    blk = pl.BlockSpec((tr, LANES), lambda i, j: (i, j))
    return pl.pallas_call(body, grid=(T // tr, D // LANES), in_specs=[blk, blk, blk], out_specs=blk,
                          out_shape=_sds((T, D), F32), name=name, compiler_params=_cp("parallel", "parallel"))(do, o, lse)


def _attn_bwd(name, qkv, do, stats, d, H):
    T = qkv.shape[0]
    D = H * HEAD_DIM
    L = T // d
    nq = _nq(L, d)
    n_steps = L // (nq * LANES)
    scale = HEAD_DIM ** -0.5
    q3 = qkv.reshape(d, L, 3 * D)
    mid = slice(BAND, BAND + LANES)

    def body(s_ref, qp, qc, qn, kp, kc, kn, vp, vc, vn, gp, gc, gn, tp, tc_, tn_, o_ref):
        h, n = pl.program_id(0), pl.program_id(1)
        bias_q = _biases(s_ref[h], d, n, n_steps, nq, LANES, 2 * LANES, 0, -BAND)
        bias_k = _biases(s_ref[h], d, n, n_steps, nq, 2 * LANES, LANES, -BAND, 0)

        def per_r(r, carry):
            for b in range(nq):
                rows = slice(b * LANES, (b + 1) * LANES)
                q_w, k_w, v_w = _win(qp, qc, qn, r, b, nq), _win(kp, kc, kn, r, b, nq), _win(vp, vc, vn, r, b, nq)
                g_w = _nat_win(gp, gc, gn, r, b, nq, d)
                t_w = _nat_win(tp, tc_, tn_, r, b, nq, d)
                g_b = g_w.astype(BF16)
                q_c, k_c, v_c, g_c, t_c = q_w[mid], k_w[mid], v_w[mid], g_b[mid], t_w[mid]
                s = lax.dot_general(q_c, k_w, NT, preferred_element_type=F32) * scale + bias_q[b]
                p = jnp.exp(s - t_c[:, 0:1])
                dp = lax.dot_general(g_c, v_w, NT, preferred_element_type=F32)
                ds = p * (dp - t_c[:, BAND:BAND + 1])
                o_ref[0, r, rows, :] = (lax.dot_general(ds.astype(BF16), k_w, NN, preferred_element_type=F32) * scale).astype(BF16)
                s2 = lax.dot_general(q_w, k_c, NT, preferred_element_type=F32) * scale + bias_k[b]
                p2 = jnp.exp(s2 - t_w[:, 0:1])
                o_ref[2, r, rows, :] = lax.dot_general(p2.astype(BF16), g_b, TN, preferred_element_type=F32).astype(BF16)
                dp2 = lax.dot_general(g_b, v_c, NT, preferred_element_type=F32)
                ds2 = p2 * (dp2 - t_w[:, BAND:BAND + 1])
                o_ref[1, r, rows, :] = (lax.dot_general(ds2.astype(BF16), q_w, TN, preferred_element_type=F32) * scale).astype(BF16)
            return carry

        _over_residues(d, nq, per_r)

    dqkv = pl.pallas_call(
        body, grid=(H, n_steps),
        in_specs=[pl.BlockSpec(memory_space=pltpu.SMEM)]
        + _win_specs(d, H, 0, nq, L) + _win_specs(d, H, 1, nq, L) + _win_specs(d, H, 2, nq, L)
        + _nat_specs(d, nq, L) + _nat_specs(d, nq, L),
        out_specs=pl.BlockSpec((3, d, nq * LANES, LANES), lambda h, n: (0, 0, n, h)),
        out_shape=_sds((3, d, L, D), BF16),
        name=name, compiler_params=_cp("parallel", "parallel"),
    )(_slopes(H), *([q3] * 9), *([do] * 3), *([stats] * 3))
    return dqkv.reshape(3, T, D)


def _to_group_order(a, d):
    if d == 1:
        return a
    T, C = a.shape
    return a.reshape(T // d, d, C).swapaxes(0, 1).reshape(T, C)


def _from_group_order(a, d):
    if d == 1:
        return a
    T, C = a.shape
    return a.reshape(d, T // d, C).swapaxes(0, 1).reshape(T, C)


def _fwd_bwd(x, tgt, S, ex):
    T, D = x.shape
    H = D // HEAD_DIM
    G3 = 3 * D

    def mm(fn, *args, rides=(), **kw):
        out, extra = fn(*args, comm=_join([getattr(ex, kind)(keys) for kind, keys in rides]), **kw)
        at = 0
        for kind, keys in rides:
            getattr(ex, kind + "_done")(keys, extra[at:at + len(keys)])
            at += len(keys)
        return out

    def ffn_fwd(l, xin, rides_up, rides_dn):
        hf = _rmsnorm_fwd(f"ffn_norm{l}", xin, S["ffn_g"][l])
        up = mm(_mm_nn, f"ffn_up{l}", hf, ex.w(f"up{l}"), None, BF16, rides=rides_up)
        act = _ffn_gate_fwd(f"ffn_gate{l}", up, S["ffn_cw"][l], S["ffn_cb"][l][None])
        return hf, up, act, mm(_mm_nn, f"ffn_down{l}", act, ex.w(f"dn{l}"), xin, F32, tk_pref=2816, rides=rides_dn)

    def ffn_bwd(l, xin, hf, up, act, dxo, dxo_b, rides):
        dact = mm(_mm_nt, f"ffn_down_dx{l}", dxo_b, ex.w(f"dn{l}"), BF16, tn_pref=1408, rides=rides[0])
        ex.grad(f"dn{l}", mm(_mm_tn, f"ffn_down_dw{l}", act, dxo_b, tkr_pref=1408))
        dup, cg = _ffn_gate_bwd(f"ffn_gate_bwd{l}", up, dact, S["ffn_cw"][l], S["ffn_cb"][l][None])
        dhf = mm(_mm_nt, f"ffn_up_dx{l}", dup, ex.w(f"up{l}"), F32, tk_pref=2816, rides=rides[1])
        ex.grad(f"up{l}", mm(_mm_tn, f"ffn_up_dw{l}", hf, dup, tn_pref=1408, rides=rides[2]))
        dx, dx_b, dg = mm(_rmsnorm_bwd, f"ffn_norm_bwd{l}", xin, S["ffn_g"][l], [dhf], dxo)
        return dx, dx_b, dg, cg

    h0 = _rmsnorm_fwd("mix_norm0", x, S["mix_g"][0])
    z = mm(_mm_nn, "sc_in", h0, ex.w("in"), None, BF16, rides=[("ag", ["sco", "up0"])])
    y = _sc_gate_fwd("sc_gate", z, S["sc_cw"], S["sc_cb"][None])
    x1 = mm(_mm_nn, "sc_out", y, ex.w("sco"), x, F32, rides=[("ag", ["dn0"])])
    hf0, up0, act0, x2 = ffn_fwd(0, x1, [("ag", ["qkv"])], [("ag", ["ao", "up1"])])
    h1 = _rmsnorm_fwd("mix_norm1", x2, S["mix_g"][1])
    hd, qkv, outs, lses = [], [], [], []
    for g, d in enumerate(DILATIONS):
        hd.append(_to_group_order(h1, d))
        qkv.append(mm(_mm_nn, f"attn_qkv{g}", hd[g], ex.w("qkv"), None, BF16, col0=g * G3, n_cols=G3,
                      rides=[("ag", ["dn1"])] if g == 0 else ()))
        o_g, l_g = _attn_fwd(f"attn_fwd{g}", qkv[g], d, H)
        outs.append(o_g)
        lses.append(l_g)
    o_b, o_f, lse = _attn_combine("attn_combine", outs, lses)
    x3 = mm(_mm_nn, "attn_out", o_b, ex.w("ao"), x2, F32)
    hf1, up1, act1, x4 = ffn_fwd(1, x3, (), ())
    loss, dx4, dx4_b, dg_fin = _final_loss("final_loss", x4, S["fin_g"], tgt)

    dx3, dx3_b, dg_f1, cg1 = ffn_bwd(1, x3, hf1, up1, act1, dx4, dx4_b,
                                     [(), [("pair", ["dn1"])], [("chip", ["dn1"])]])
    do = mm(_mm_nt, "attn_out_dx", dx3_b, ex.w("ao"), F32, rides=[("pair", ["up1"])])
    ex.grad("ao", mm(_mm_tn, "attn_out_dw", o_b, dx3_b))
    stats = _attn_stats("attn_stats", do, o_f, lse)
    dhs, dw_qkv = [], None
    qkv_rides = [[("chip", ["up1"]), ("pair", ["ao"])], [("chip", ["ao"])], ()]
    for g, d in enumerate(DILATIONS):
        dqkv = _attn_bwd(f"attn_bwd{g}", qkv[g], do, stats, d, H)
        dhs.append(_from_group_order(mm(_mm_nt, f"attn_qkv_dx{g}", dqkv, ex.w("qkv"), F32, col0=g * G3,
                                        rides=qkv_rides[g]), d))
        dw_qkv = mm(_mm_tn, f"attn_qkv_dw{g}", hd[g], dqkv, n_total=len(DILATIONS) * G3, col0=g * G3, prev=dw_qkv)
    ex.grad("qkv", dw_qkv)
    dx2, dx2_b, dg_m1 = mm(_rmsnorm_bwd, "mix_norm_bwd1", x2, S["mix_g"][1], dhs, dx3)
    dx1, dx1_b, dg_f0, cg0 = ffn_bwd(0, x1, hf0, up0, act0, dx2, dx2_b,
                                     [[("pair", ["qkv"])], [("chip", ["qkv"]), ("pair", ["dn0"])], [("chip", ["dn0"])]])
    dy = mm(_mm_nt, "sc_out_dx", dx1_b, ex.w("sco"), BF16, rides=[("pair", ["up0"])])
    ex.grad("sco", mm(_mm_tn, "sc_out_dw", y, dx1_b))
    dz, cg_sc = _sc_gate_bwd("sc_gate_bwd", z, dy, S["sc_cw"], S["sc_cb"][None])
    ex.grad("in", mm(_mm_tn, "sc_in_dw", h0, dz, rides=[("chip", ["up0"]), ("pair", ["sco"])]))
    dh0 = mm(_mm_nt, "sc_in_dx", dz, ex.w("in"), F32, rides=[("chip", ["sco"]), ("pair", ["in"])])
    dx0, _, dg_m0 = mm(_rmsnorm_bwd, "mix_norm_bwd0", x, S["mix_g"][0], [dh0], dx1, rides=[("chip", ["in"])])

    dS = {"mix_g": jnp.stack([dg_m0, dg_m1]), "ffn_g": jnp.stack([dg_f0, dg_f1]), "fin_g": dg_fin,
          "sc_cw": cg_sc[0:3], "sc_cb": cg_sc[3], "ffn_cw": jnp.stack([cg0[0:3], cg1[0:3]]),
          "ffn_cb": jnp.stack([cg0[3], cg1[3]])}
    return loss, dx0, dS


def _place():
    x, y, c = lax.axis_index("x"), lax.axis_index("y"), lax.axis_index("c")
    return x, y, c, [(1 - x, y), (x, 1 - y), (1 - x, 1 - y)]


def _block(ref, s, shape, axis):
    R, C = shape
    if axis == 0:
        return ref.at[pl.ds(pl.multiple_of(s * R, HALO), R), :]
    return ref.at[:, pl.ds(pl.multiple_of(s * C, LANES), C)]


def _whole(shape, axis):
    return (shape[0] * N_DEV, shape[1]) if axis == 0 else (shape[0], shape[1] * N_DEV)


def _ag_stage(arrs, axes):
    n = len(arrs)

    def plan(ins, outs, sems):
        send_sems, recv_sems, local_sems = sems
        x, y, c, chips = _place()
        me, sibling = 4 * x + 2 * y + c, (x, y, 1 - c)

        def copy(a, k, blk, to, src=None):
            dst = _block(outs[a], blk, arrs[a].shape, axes[a])
            return pltpu.make_async_remote_copy(src_ref=dst if src is None else src, dst_ref=dst,
                                                send_sem=send_sems.at[a, k], recv_sem=recv_sems.at[a, k],
                                                device_id=to, device_id_type=MESH)

        mine = [pltpu.make_async_copy(ins[a], _block(outs[a], me, arrs[a].shape, axes[a]), local_sems.at[a])
                for a in range(n)]
        first = []
        for a in range(n):
            first.append(copy(a, 0, me, sibling, src=ins[a]))
            first += [copy(a, 1 + j, me, (*chip, c), src=ins[a]) for j, chip in enumerate(chips)]
        return x, y, c, chips, sibling, copy, mine, first

    def start(ins, outs, sems):
        *_, mine, first = plan(ins, outs, sems)
        for cp in mine + first:
            cp.start()

    def finish(ins, outs, sems):
        x, y, c, chips, sibling, copy, mine, first = plan(ins, outs, sems)
        passed = []
        for j, (px, py) in enumerate(chips):
            for a in range(n):
                blk = 4 * px + 2 * py + c
                copy(a, 1 + j, blk, sibling).wait_recv()
                passed.append(copy(a, 4 + j, blk, sibling))
                passed[-1].start()
        for a in range(n):
            copy(a, 0, 4 * x + 2 * y + 1 - c, sibling).wait_recv()
            for j, (px, py) in enumerate(chips):
                copy(a, 4 + j, 4 * px + 2 * py + 1 - c, sibling).wait_recv()
        for cp in first + passed:
            cp.wait_send()
        for cp in mine:
            cp.wait()

    return _Stage(list(arrs), [_sds(_whole(a.shape, ax), a.dtype) for a, ax in zip(arrs, axes)],
                  [pltpu.SemaphoreType.DMA((n, 7)), pltpu.SemaphoreType.DMA((n, 7)), pltpu.SemaphoreType.DMA((n,))],
                  start, finish)


def _pair_stage(dws, shapes, axes):
    n = len(dws)

    def copies(ins, outs, sems):
        send_sems, recv_sems = sems
        x, y, c, _ = _place()
        return [pltpu.make_async_remote_copy(src_ref=_block(ins[a], 2 * k + 1 - c, shapes[a], axes[a]),
                                             dst_ref=outs[a].at[k],
                                             send_sem=send_sems.at[a, k], recv_sem=recv_sems.at[a, k],
                                             device_id=(x, y, 1 - c), device_id_type=MESH)
                for a in range(n) for k in range(4)]

    def start(ins, outs, sems):
        for cp in copies(ins, outs, sems):
            cp.start()

    def finish(ins, outs, sems):
        for cp in copies(ins, outs, sems):
            cp.wait()

    return _Stage(list(dws), [_sds((4,) + tuple(s), a.dtype) for a, s in zip(dws, shapes)],
                  [pltpu.SemaphoreType.DMA((n, 4)), pltpu.SemaphoreType.DMA((n, 4))], start, finish)


def _rs_pair_add(name, dw, got, c_arr, axis):
    _, R, C = got.shape
    tr, tc = _tile(R, 512, 16), _tile(C, 1536, LANES)
    per = R // tr if axis == 0 else C // tc

    def body(c_ref, a_ref, b_ref, o_ref):
        o_ref[...] = (a_ref[...].astype(F32) + b_ref[...].astype(F32)).astype(o_ref.dtype)

    if axis == 0:
        mine = pl.BlockSpec((tr, tc), lambda k, i, j, c_ref: ((2 * k + c_ref[0]) * per + i, j))
    else:
        mine = pl.BlockSpec((tr, tc), lambda k, i, j, c_ref: (i, (2 * k + c_ref[0]) * per + j))
    return pl.pallas_call(
        body,
        grid_spec=pltpu.PrefetchScalarGridSpec(
            num_scalar_prefetch=1, grid=(4, R // tr, C // tc),
            in_specs=[mine, pl.BlockSpec((None, tr, tc), lambda k, i, j, c_ref: (k, i, j))],
            out_specs=pl.BlockSpec((None, tr, tc), lambda k, i, j, c_ref: (k, i, j))),
        out_shape=_sds((4, R, C), BF16), name=name,
        compiler_params=_cp("parallel", "parallel", "parallel"))(c_arr, dw, got)


def _chip_stage(parts):
    n = len(parts)

    def plan(ins, outs, sems):
        send_sems, recv_sems, local_sems = sems
        x, y, c, chips = _place()
        my_chip = 2 * x + y
        mine = [pltpu.make_async_copy(ins[a].at[my_chip], outs[a].at[my_chip], local_sems.at[a]) for a in range(n)]
        sends = [pltpu.make_async_remote_copy(src_ref=ins[a].at[2 * px + py], dst_ref=outs[a].at[my_chip],
                                              send_sem=send_sems.at[a, j], recv_sem=recv_sems.at[a, j],
                                              device_id=(px, py, c), device_id_type=MESH)
                 for a in range(n) for j, (px, py) in enumerate(chips)]
        arrivals = lambda: [pltpu.make_async_remote_copy(src_ref=ins[a].at[my_chip], dst_ref=outs[a].at[2 * px + py],
                                                         send_sem=send_sems.at[a, j], recv_sem=recv_sems.at[a, j],
                                                         device_id=(px, py, c), device_id_type=MESH)
                            for a in range(n) for j, (px, py) in enumerate(chips)]
        return mine, sends, arrivals

    def start(ins, outs, sems):
        mine, sends, _ = plan(ins, outs, sems)
        for cp in mine + sends:
            cp.start()

    def finish(ins, outs, sems):
        mine, sends, arrivals = plan(ins, outs, sems)
        for cp in arrivals():
            cp.wait_recv()
        for cp in sends:
            cp.wait_send()
        for cp in mine:
            cp.wait()

    return _Stage(list(parts), [_sds(a.shape, a.dtype) for a in parts],
                  [pltpu.SemaphoreType.DMA((n, 3)), pltpu.SemaphoreType.DMA((n, 3)), pltpu.SemaphoreType.DMA((n,))],
                  start, finish)


class _Exchange:
    ROW_SHARDED = ("sco", "ao", "dn0", "dn1")

    def __init__(self, shards, c_arr):
        self.sh, self.c_arr = shards, c_arr
        self.W, self.dw, self.parts, self.sums = {}, {}, {}, {}

    def axis(self, key):
        return 0 if key in self.ROW_SHARDED else 1

    def w(self, key):
        return self.W[key]

    def ag(self, keys):
        return _ag_stage([self.sh[k] for k in keys], [self.axis(k) for k in keys])

    def ag_done(self, keys, outs):
        self.W.update(zip(keys, outs))

    def grad(self, key, dw):
        self.dw[key] = dw

    def pair(self, keys):
        return _pair_stage([self.dw[k] for k in keys], [self.sh[k].shape for k in keys], [self.axis(k) for k in keys])

    def pair_done(self, keys, outs):
        for k, got in zip(keys, outs):
            self.parts[k] = _rs_pair_add(f"rs_add_{k}", self.dw[k], got, self.c_arr, self.axis(k))

    def chip(self, keys):
        return _chip_stage([self.parts[k] for k in keys])

    def chip_done(self, keys, outs):
        self.sums.update(zip(keys, outs))


def _sum_slots(name, a):
    _, rows, _ = a.shape

    def body(a_ref, o_ref):
        s = a_ref[0]
        for k in range(1, N_DEV):
            s = s + a_ref[k]
        o_ref[...] = s

    return pl.pallas_call(body, out_shape=_sds((rows, LANES), F32), name=name)(a)


def _cast_bf16(name, w3, l):
    _, R, C = w3.shape
    tr, tc = _tile(R, 512, 16), _tile(C, 1536, LANES)

    def body(w_ref, o_ref):
        o_ref[...] = w_ref[...].astype(BF16)

    return pl.pallas_call(
        body, grid=(R // tr, C // tc), in_specs=[pl.BlockSpec((None, tr, tc), lambda i, j: (l, i, j))],
        out_specs=pl.BlockSpec((tr, tc), lambda i, j: (i, j)), out_shape=_sds((R, C), BF16),
        name=name, compiler_params=_cp("parallel", "parallel"))(w3)


def _adamw(name, g_slots, w3, m3, v3, l, prev):
    n_slots, R, C = g_slots.shape
    tr, tc = _tile(R, 256, 8), _tile(C, 1536, LANES)
    c1, c2 = 1.0 - ADAM_B1 ** ADAM_STEP, 1.0 - ADAM_B2 ** ADAM_STEP

    def body(g_ref, w_ref, m_ref, v_ref, *rest):
        og, od, om, ov = rest[-4:]
        g = g_ref[0].astype(F32)
        for k in range(1, n_slots):
            g = g + g_ref[k].astype(F32)
        m = ADAM_B1 * m_ref[...] + (1.0 - ADAM_B1) * g
        v = ADAM_B2 * v_ref[...] + (1.0 - ADAM_B2) * (g * g)
        og[...] = g
        om[...] = m
        ov[...] = v
        od[...] = -ADAM_LR * ((m / c1) / (jnp.sqrt(v / c2) + ADAM_EPS) + ADAM_WD * w_ref[...])

    lay = pl.BlockSpec((None, tr, tc), lambda i, j: (l, i, j))
    ops = [g_slots, w3, m3, v3]
    specs = [pl.BlockSpec((n_slots, tr, tc), lambda i, j: (0, i, j)), lay, lay, lay]
    aliases = {}
    if prev is not None:
        ops += list(prev)
        specs += _any_specs(4)
        aliases = {4 + k: k for k in range(4)}
    return pl.pallas_call(
        body, grid=(R // tr, C // tc), in_specs=specs, out_specs=[lay] * 4,
        out_shape=[_sds(w3.shape, F32)] * 4, input_output_aliases=aliases,
        name=name, compiler_params=_cp("parallel", "parallel"))(*ops)


def _pack(parts):
    flat = jnp.concatenate([p.reshape(-1) for p in parts])
    pad = (-flat.shape[0]) % (HALO * LANES)
    return jnp.pad(flat, (0, pad)).reshape(-1, LANES)


def _unpack(packed, shapes):
    flat = packed.reshape(-1)
    out, at = [], 0
    for s in shapes:
        n = int(np.prod(s))
        out.append(flat[at:at + n].reshape(s))
        at += n
    return out


def kernel(x, mix_norm_g, ffn_norm_g, final_norm_g, sc_w_in, sc_conv_w, sc_conv_b, sc_w_out, attn_w_qkv, attn_w_out, ffn_w_up, ffn_conv_w, ffn_conv_b, ffn_w_down, loss_target, m_mix_norm_g, m_ffn_norm_g, m_final_norm_g, m_sc_w_in, m_sc_conv_w, m_sc_conv_b, m_sc_w_out, m_attn_w_qkv, m_attn_w_out, m_ffn_w_up, m_ffn_conv_w, m_ffn_conv_b, m_ffn_w_down, v_mix_norm_g, v_ffn_norm_g, v_final_norm_g, v_sc_w_in, v_sc_conv_w, v_sc_conv_b, v_sc_w_out, v_attn_w_qkv, v_attn_w_out, v_ffn_w_up, v_ffn_conv_w, v_ffn_conv_b, v_ffn_w_down):
    n_layers = ffn_w_up.shape[0]
    me = 4 * lax.axis_index("x") + 2 * lax.axis_index("y") + lax.axis_index("c")
    c_arr = lax.axis_index("c").astype(jnp.int32).reshape(1)

    big = [("in", sc_w_in, 0), ("sco", sc_w_out, 0), ("qkv", attn_w_qkv, 0), ("ao", attn_w_out, 0)]
    big += [(f"up{l}", ffn_w_up, l) for l in range(n_layers)] + [(f"dn{l}", ffn_w_down, l) for l in range(n_layers)]
    ex = _Exchange({nm: _cast_bf16(f"cast_{nm}", w, l) for nm, w, l in big}, c_arr)

    cw_shapes = [sc_conv_w.shape, ffn_conv_w.shape]
    cw_mine = _pack([sc_conv_w, ffn_conv_w])
    w_in_all, cw_all = _run_stage("gather_first", _join([ex.ag(["in"]), _ag_stage([cw_mine], [0])]))
    ex.ag_done(["in"], [w_in_all])
    cw_all = cw_all.reshape((N_DEV,) + cw_mine.shape)
    sc_cw_all, ffn_cw_all = zip(*[_unpack(cw_all[s], cw_shapes) for s in range(N_DEV)])
    sc_cw = jnp.concatenate(sc_cw_all, axis=-1)[0]
    ffn_cw = jnp.concatenate(ffn_cw_all, axis=-1)
    S = {"mix_g": mix_norm_g, "ffn_g": ffn_norm_g, "fin_g": final_norm_g, "sc_cw": sc_cw, "sc_cb": sc_conv_b[0],
         "ffn_cw": ffn_cw, "ffn_cb": ffn_conv_b}

    loss_part, grad_x, dS = _fwd_bwd(x[0], loss_target[0], S, ex)

    small_names = ["mix_g", "ffn_g", "fin_g", "sc_cb", "ffn_cb", "sc_cw", "ffn_cw"]
    small_parts = [dS[k] for k in small_names] + [loss_part.reshape(1)]
    small_mine = _pack(small_parts)
    small_all, = _run_stage("gather_small", _ag_stage([small_mine], [0]))
    small_sum = _sum_slots("sum_small", small_all.reshape((N_DEV,) + small_mine.shape))
    g_mix, g_ffn, g_fin, g_scb, g_fcb, g_scw, g_fcw, loss = _unpack(small_sum, [p.shape for p in small_parts])
    g_scw = lax.dynamic_slice_in_dim(g_scw, me * sc_conv_w.shape[-1], sc_conv_w.shape[-1], axis=-1)[None]
    g_fcw = lax.dynamic_slice_in_dim(g_fcw, me * ffn_conv_w.shape[-1], ffn_conv_w.shape[-1], axis=-1)
    g_scb = g_scb[None]
    small_g = [g_mix, g_ffn, g_fin, g_scw, g_scb, g_fcw, g_fcb]
    small_w = [mix_norm_g, ffn_norm_g, final_norm_g, sc_conv_w, sc_conv_b, ffn_conv_w, ffn_conv_b]
    small_m = [m_mix_norm_g, m_ffn_norm_g, m_final_norm_g, m_sc_conv_w, m_sc_conv_b, m_ffn_conv_w, m_ffn_conv_b]
    small_v = [v_mix_norm_g, v_ffn_norm_g, v_final_norm_g, v_sc_conv_w, v_sc_conv_b, v_ffn_conv_w, v_ffn_conv_b]
    small_out = _adamw("adamw_small", _pack(small_g)[None], _pack(small_w)[None], _pack(small_m)[None],
                       _pack(small_v)[None], 0, None)
    small_shapes = [w.shape for w in small_w]
    sg, sd, sm, sv = [_unpack(o[0], small_shapes) for o in small_out]

    moments = {"in": (m_sc_w_in, v_sc_w_in), "sco": (m_sc_w_out, v_sc_w_out), "qkv": (m_attn_w_qkv, v_attn_w_qkv),
               "ao": (m_attn_w_out, v_attn_w_out), "up": (m_ffn_w_up, v_ffn_w_up), "dn": (m_ffn_w_down, v_ffn_w_down)}
    upd = {}
    for nm, w, l in big:
        key = nm.rstrip("0123456789")
        upd[key] = _adamw(f"adamw_{nm}", ex.sums[nm], w, moments[key][0], moments[key][1], l, upd.get(key))

    def leaves(k):
        return [sg, sd, sm, sv][k][0:3] + [upd["in"][k], [sg, sd, sm, sv][k][3], [sg, sd, sm, sv][k][4], upd["sco"][k],
                                          upd["qkv"][k], upd["ao"][k], upd["up"][k], [sg, sd, sm, sv][k][5],
                                          [sg, sd, sm, sv][k][6], upd["dn"][k]]

    return (loss.reshape(()), grad_x[None], *leaves(0), *leaves(1), *leaves(2), *leaves(3))
```

```python
import math

import numpy as np
import jax
import jax.numpy as jnp
from jax import lax
from jax.experimental import pallas as pl
from jax.experimental.pallas import tpu as pltpu

F32 = jnp.float32
BF16 = jnp.bfloat16
MESH = pl.DeviceIdType.MESH

HEAD_DIM = 128
DILATED_GROUPS = ((128, 1), (512, 4), (2048, 16))
DILATIONS = tuple(d for _, d in DILATED_GROUPS)
BAND = (DILATED_GROUPS[0][0] // 2) // DILATED_GROUPS[0][1]
assert all((w // 2) // d == BAND for w, d in DILATED_GROUPS)
NORM_EPS = 1e-5
ALIBI_MAX = 8.0
NEG_INF = -1e30
ADAM_LR, ADAM_B1, ADAM_B2, ADAM_EPS, ADAM_WD, ADAM_STEP = 0.001, 0.9, 0.999, 1e-08, 0.01, 10

N_DEV = 8
LANES = 128
HALO = 16
VMEM_LIMIT = 56 * 1024 * 1024


def _cp(*sem):
    return pltpu.CompilerParams(dimension_semantics=sem, vmem_limit_bytes=VMEM_LIMIT)


def _tile(n, pref, mult):
    t = (min(n, pref) // mult) * mult
    while t >= mult:
        if n % t == 0:
            return t
        t -= mult
    return n


def _sds(shape, dtype):
    return jax.ShapeDtypeStruct(shape, dtype)


def _any_specs(n):
    return [pl.BlockSpec(memory_space=pl.ANY)] * n


class _Stage:
    def __init__(self, arrays, out_shapes, sems, start, finish):
        self.arrays, self.out_shapes, self.sems, self.start, self.finish = arrays, out_shapes, sems, start, finish


def _join(stages):
    stages = [s for s in stages if s is not None]
    if not stages:
        return None

    def split(refs, count):
        out, at = [], 0
        for s in stages:
            out.append(refs[at:at + count(s)])
            at += count(s)
        return out

    def each(which):
        def run(ins, outs, sems):
            parts = zip(split(ins, lambda s: len(s.arrays)), split(outs, lambda s: len(s.out_shapes)),
                        split(sems, lambda s: len(s.sems)))
            for s, (i, o, m) in zip(stages, parts):
                getattr(s, which)(i, o, m)
        return run

    return _Stage(sum([s.arrays for s in stages], []), sum([s.out_shapes for s in stages], []),
                  sum([s.sems for s in stages], []), each("start"), each("finish"))


def _run_stage(name, st):
    n, m = len(st.arrays), len(st.out_shapes)

    def body(*refs):
        ins, outs, sems = refs[:n], refs[n:n + m], refs[n + m:]
        st.start(ins, outs, sems)
        st.finish(ins, outs, sems)

    return pl.pallas_call(body, in_specs=_any_specs(n), out_specs=_any_specs(m), out_shape=st.out_shapes,
                          scratch_shapes=st.sems, name=name)(*st.arrays)


NN = (((1,), (0,)), ((), ()))
NT = (((1,), (1,)), ((), ()))
TN = (((0,), (0,)), ((), ()))


def _mm(name, operands, in_specs, out_sds, o_spec, grid, dims, acc_shape, has_res=False, aliases=None, comm=None):
    nk = grid[2]
    n_in = len(operands)
    n_ci, n_co = (len(comm.arrays), len(comm.out_shapes)) if comm else (0, 0)

    def body(*refs):
        a_ref, b_ref = refs[0], refs[1]
        r_ref = refs[2] if has_res else None
        o_ref = refs[n_in + n_ci]
        acc = refs[n_in + n_ci + 1 + n_co]
        c_refs = (refs[n_in:n_in + n_ci], refs[n_in + n_ci + 1:n_in + n_ci + 1 + n_co], refs[n_in + n_ci + 2 + n_co:])
        ids = [pl.program_id(q) for q in range(3)]
        if comm:
            @pl.when((ids[0] == 0) & (ids[1] == 0) & (ids[2] == 0))
            def _():
                comm.start(*c_refs)

        part = lax.dot_general(a_ref[...], b_ref[...], dims, preferred_element_type=F32)

        def finish(total):
            if has_res:
                total = total + r_ref[...]
            o_ref[...] = total.astype(o_ref.dtype)

        if nk == 1:
            finish(part)
        else:
            k = ids[2]

            @pl.when(k == 0)
            def _():
                acc[...] = part

            @pl.when((k > 0) & (k < nk - 1))
            def _():
                acc[...] += part

            @pl.when(k == nk - 1)
            def _():
                finish(acc[...] + part)

        if comm:
            @pl.when((ids[0] == grid[0] - 1) & (ids[1] == grid[1] - 1) & (ids[2] == nk - 1))
            def _():
                comm.finish(*c_refs)

    scratch = [pltpu.VMEM(acc_shape if nk > 1 else (8, LANES), F32)]
    if not comm:
        out = pl.pallas_call(
            body, grid=grid, in_specs=in_specs, out_specs=o_spec, out_shape=out_sds, scratch_shapes=scratch,
            input_output_aliases=aliases or {}, name=name,
            compiler_params=_cp("parallel", "parallel", "arbitrary"))(*operands)
        return out, []
    outs = pl.pallas_call(
        body, grid=grid, in_specs=list(in_specs) + _any_specs(n_ci), out_specs=[o_spec] + _any_specs(n_co),
        out_shape=[out_sds] + comm.out_shapes, scratch_shapes=scratch + comm.sems,
        input_output_aliases=aliases or {}, name=name,
        compiler_params=_cp("arbitrary", "arbitrary", "arbitrary"))(*operands, *comm.arrays)
    return outs[0], list(outs[1:])


def _stack(a):
    return a if a.ndim == 3 else a[None]


def _mm_nn(name, a, w, res, out_dtype, col0=0, n_cols=None, tk_pref=2048, comm=None):
    M, K = a.shape
    N = n_cols or w.shape[1]
    tm, tn, tk = _tile(M, 1024, 16), _tile(N, 1024, LANES), _tile(K, tk_pref, LANES)
    c0 = col0 // tn
    ops = [a, w]
    specs = [pl.BlockSpec((tm, tk), lambda i, j, k: (i, k)), pl.BlockSpec((tk, tn), lambda i, j, k: (k, c0 + j))]
    if res is not None:
        ops.append(res)
        specs.append(pl.BlockSpec((tm, tn), lambda i, j, k: (i, j)))
    return _mm(name, tuple(ops), specs, _sds((M, N), out_dtype), pl.BlockSpec((tm, tn), lambda i, j, k: (i, j)),
               (M // tm, N // tn, K // tk), NN, (tm, tn), has_res=res is not None, comm=comm)


def _mm_nt(name, dy, w, out_dtype, col0=0, tn_pref=1024, tk_pref=2048, comm=None):
    dy = _stack(dy)
    _, M, Np = dy.shape
    Kw = w.shape[0]
    tm, tn, tk = _tile(M, 1024, 16), _tile(Kw, tn_pref, LANES), _tile(Np, tk_pref, LANES)
    per, c0 = Np // tk, col0 // tk
    return _mm(name, (dy, w),
               [pl.BlockSpec((None, tm, tk), lambda i, j, k: (k // per, i, k % per)),
                pl.BlockSpec((tn, tk), lambda i, j, k: (j, c0 + k))],
               _sds((M, Kw), out_dtype), pl.BlockSpec((tm, tn), lambda i, j, k: (i, j)),
               (M // tm, Kw // tn, dy.shape[0] * per), NT, (tm, tn), comm=comm)


def _mm_tn(name, a, dy, n_total=None, col0=0, tkr_pref=1024, tn_pref=1024, prev=None, comm=None):
    dy = _stack(dy)
    P, M, Np = dy.shape
    Kw = a.shape[1]
    tkr, tn, tk = _tile(Kw, tkr_pref, LANES), _tile(Np, tn_pref, LANES), _tile(M, 2048, 16)
    per, c0 = Np // tn, col0 // tn
    ops = [a, dy]
    specs = [pl.BlockSpec((tk, tkr), lambda i, j, k: (k, i)),
             pl.BlockSpec((None, tk, tn), lambda i, j, k: (j // per, k, j % per))]
    aliases = None
    if prev is not None:
        ops.append(prev)
        specs.append(pl.BlockSpec(memory_space=pl.ANY))
        aliases = {2: 0}
    return _mm(name, tuple(ops), specs, _sds((Kw, n_total or P * Np), BF16),
               pl.BlockSpec((tkr, tn), lambda i, j, k: (i, c0 + j)),
               (Kw // tkr, P * per, M // tk), TN, (tkr, tn), aliases=aliases, comm=comm)


def _rmsnorm_fwd(name, x, g):
    T, D = x.shape
    tr = _tile(T, 512, 16)

    def body(x_ref, g_ref, h_ref):
        xf = x_ref[...]
        r = lax.rsqrt(jnp.mean(xf * xf, axis=-1, keepdims=True) + NORM_EPS)
        h_ref[...] = (xf * r * g_ref[...]).astype(h_ref.dtype)

    return pl.pallas_call(
        body, grid=(T // tr,),
        in_specs=[pl.BlockSpec((tr, D), lambda i: (i, 0)), pl.BlockSpec((1, D), lambda i: (0, 0))],
        out_specs=pl.BlockSpec((tr, D), lambda i: (i, 0)), out_shape=_sds((T, D), BF16),
        name=name, compiler_params=_cp("parallel"))(x, g.reshape(1, D))


def _rmsnorm_bwd(name, x, g, dhs, dres, comm=None):
    T, D = x.shape
    tr = _tile(T, 256, 16)
    n_dh = len(dhs)
    n_in = 3 + n_dh
    n_ci, n_co = (len(comm.arrays), len(comm.out_shapes)) if comm else (0, 0)

    def body(*refs):
        x_ref, g_ref = refs[0], refs[1]
        dh_refs = refs[2:2 + n_dh]
        dres_ref = refs[2 + n_dh]
        dx_ref, dxb_ref, dg_ref = refs[n_in + n_ci:n_in + n_ci + 3]
        c_refs = (refs[n_in:n_in + n_ci], refs[n_in + n_ci + 3:n_in + n_ci + 3 + n_co], refs[n_in + n_ci + 3 + n_co:])
        if comm:
            @pl.when(pl.program_id(0) == 0)
            def _():
                comm.start(*c_refs)

        xf = x_ref[...]
        r = lax.rsqrt(jnp.mean(xf * xf, axis=-1, keepdims=True) + NORM_EPS)
        xhat = xf * r
        dh = dh_refs[0][...].astype(F32)
        for q in dh_refs[1:]:
            dh = dh + q[...].astype(F32)
        dy = dh * g_ref[...]
        c = jnp.mean(dy * xhat, axis=-1, keepdims=True)
        dx = dres_ref[...] + r * (dy - xhat * c)
        dx_ref[...] = dx
        dxb_ref[...] = dx.astype(BF16)

        @pl.when(pl.program_id(0) == 0)
        def _():
            dg_ref[...] = jnp.zeros_like(dg_ref)

        dg_ref[...] += jnp.sum(dh * xhat, axis=0, keepdims=True)

        if comm:
            @pl.when(pl.program_id(0) == T // tr - 1)
            def _():
                comm.finish(*c_refs)

    row = pl.BlockSpec((tr, D), lambda i: (i, 0))
    vec = pl.BlockSpec((1, D), lambda i: (0, 0))
    outs = pl.pallas_call(
        body, grid=(T // tr,), in_specs=[row, vec] + [row] * n_dh + [row] + _any_specs(n_ci),
        out_specs=[row, row, vec] + _any_specs(n_co),
        out_shape=[_sds((T, D), F32), _sds((T, D), BF16), _sds((1, D), F32)] + (comm.out_shapes if comm else []),
        scratch_shapes=comm.sems if comm else [],
        name=name, compiler_params=_cp("arbitrary"))(x, g.reshape(1, D), *dhs, dres, *(comm.arrays if comm else []))
    return (outs[0], outs[1], outs[2][0]), list(outs[3:])


def _final_loss(name, x, g, tgt):
    T, D = x.shape
    tr = _tile(T, 256, 16)

    def body(x_ref, g_ref, t_ref, dx_ref, dxb_ref, dg_ref, loss_ref):
        xf = x_ref[...]
        r = lax.rsqrt(jnp.mean(xf * xf, axis=-1, keepdims=True) + NORM_EPS)
        xhat = xf * r
        err = xhat * g_ref[...] - t_ref[...]
        dy = err * (1.0 / D)
        dxh = dy * g_ref[...]
        c = jnp.mean(dxh * xhat, axis=-1, keepdims=True)
        dx = r * (dxh - xhat * c)
        dx_ref[...] = dx
        dxb_ref[...] = dx.astype(BF16)

        @pl.when(pl.program_id(0) == 0)
        def _():
            dg_ref[...] = jnp.zeros_like(dg_ref)
            loss_ref[...] = jnp.zeros_like(loss_ref)

        dg_ref[...] += jnp.sum(dy * xhat, axis=0, keepdims=True)
        loss_ref[...] += 0.5 * jnp.sum(jnp.mean(err * err, axis=-1, keepdims=True), axis=0, keepdims=True)

    row = pl.BlockSpec((tr, D), lambda i: (i, 0))
    vec = pl.BlockSpec((1, D), lambda i: (0, 0))
    dx, dx_b, dg, loss = pl.pallas_call(
        body, grid=(T // tr,), in_specs=[row, vec, row],
        out_specs=[row, row, vec, pl.BlockSpec((1, 1), lambda i: (0, 0))],
        out_shape=[_sds((T, D), F32), _sds((T, D), BF16), _sds((1, D), F32), _sds((1, 1), F32)],
        name=name, compiler_params=_cp("arbitrary"))(x, g.reshape(1, D), tgt)
    return loss[0, 0], dx, dx_b, dg[0]


def _halo_specs(tr, tc, n_rows, col):
    rb = tr // HALO
    last = n_rows // HALO - 1
    return [pl.BlockSpec((tr, tc), lambda *g: (g[-1], col(*g))),
            pl.BlockSpec((HALO, tc), lambda *g: (jnp.maximum(g[-1] * rb - 1, 0), col(*g))),
            pl.BlockSpec((HALO, tc), lambda *g: (jnp.minimum((g[-1] + 1) * rb, last), col(*g)))]


def _ext(cur_ref, prev_ref, next_ref, i, n_i):
    p = prev_ref[...].astype(F32) * (i > 0).astype(F32)
    n = next_ref[...].astype(F32) * (i < n_i - 1).astype(F32)
    return jnp.concatenate([p, cur_ref[...].astype(F32), n], axis=0)


def _shift_dn(x):
    return pltpu.roll(x, 1, axis=0)


def _shift_up(x):
    return pltpu.roll(x, x.shape[0] - 1, axis=0)


def _conv(x, w_ref, b_ref):
    return w_ref[0:1, :] * _shift_dn(x) + w_ref[1:2, :] * x + w_ref[2:3, :] * _shift_up(x) + b_ref[...]


def _mid(x, tr):
    return x[HALO:HALO + tr, :]


def _conv_t(g, w_ref):
    return w_ref[0:1, :] * _shift_up(g) + w_ref[1:2, :] * g + w_ref[2:3, :] * _shift_dn(g)


def _conv_wgrad(acc_ref, g, x, tr, first):
    gm = _mid(g, tr)

    @pl.when(first)
    def _():
        acc_ref[...] = jnp.zeros_like(acc_ref)

    acc_ref[0:1, :] += jnp.sum(gm * _mid(_shift_dn(x), tr), axis=0, keepdims=True)
    acc_ref[1:2, :] += jnp.sum(gm * _mid(x, tr), axis=0, keepdims=True)
    acc_ref[2:3, :] += jnp.sum(gm * _mid(_shift_up(x), tr), axis=0, keepdims=True)
    acc_ref[3:4, :] += jnp.sum(gm, axis=0, keepdims=True)


def _sigmoid(a):
    return 1.0 / (1.0 + jnp.exp(-a))


def _ffn_gate_fwd(name, up, cw, cb):
    T, F2 = up.shape
    F = F2 // 2
    tc, tr = _tile(F, 512, LANES), _tile(T, 512, HALO)
    nF, n_i = F // tc, T // tr

    def body(ac, ap, an, bc, bp, bn, wa, wb, ba, bb, o_ref):
        i = pl.program_id(1)
        ua = _mid(_conv(_ext(ac, ap, an, i, n_i), wa, ba), tr)
        ub = _mid(_conv(_ext(bc, bp, bn, i, n_i), wb, bb), tr)
        o_ref[...] = (ua * _sigmoid(ua) * ub).astype(o_ref.dtype)

    wspec = lambda o: pl.BlockSpec((3, tc), lambda j, i: (0, j + o))
    bspec = lambda o: pl.BlockSpec((1, tc), lambda j, i: (0, j + o))
    return pl.pallas_call(
        body, grid=(nF, n_i),
        in_specs=_halo_specs(tr, tc, T, lambda j, i: j) + _halo_specs(tr, tc, T, lambda j, i: j + nF)
        + [wspec(0), wspec(nF), bspec(0), bspec(nF)],
        out_specs=pl.BlockSpec((tr, tc), lambda j, i: (i, j)), out_shape=_sds((T, F), BF16),
        name=name, compiler_params=_cp("parallel", "parallel"))(up, up, up, up, up, up, cw, cw, cb, cb)


def _ffn_gate_bwd(name, up, dact, cw, cb):
    T, F2 = up.shape
    F = F2 // 2
    tc, tr = _tile(F, 512, LANES), _tile(T, 512, HALO)
    nF, n_i = F // tc, T // tr

    def body(ac, ap, an, bc, bp, bn, dc, dp, dn, wa, wb, ba, bb, o_ref, wga_ref, wgb_ref):
        i = pl.program_id(1)
        xa = _ext(ac, ap, an, i, n_i)
        xb = _ext(bc, bp, bn, i, n_i)
        da = _ext(dc, dp, dn, i, n_i)
        ua = _conv(xa, wa, ba)
        sig = _sigmoid(ua)
        ga = da * _conv(xb, wb, bb) * (sig * (1.0 + ua * (1.0 - sig)))
        o_ref[0] = _mid(_conv_t(ga, wa), tr).astype(o_ref.dtype)
        _conv_wgrad(wga_ref, ga, xa, tr, i == 0)
        gb = da * (ua * sig)
        o_ref[1] = _mid(_conv_t(gb, wb), tr).astype(o_ref.dtype)
        _conv_wgrad(wgb_ref, gb, xb, tr, i == 0)

    wspec = lambda o: pl.BlockSpec((3, tc), lambda j, i: (0, j + o))
    bspec = lambda o: pl.BlockSpec((1, tc), lambda j, i: (0, j + o))
    wg = pl.BlockSpec((8, tc), lambda j, i: (0, j))
    dup, wga, wgb = pl.pallas_call(
        body, grid=(nF, n_i),
        in_specs=_halo_specs(tr, tc, T, lambda j, i: j) + _halo_specs(tr, tc, T, lambda j, i: j + nF)
        + _halo_specs(tr, tc, T, lambda j, i: j) + [wspec(0), wspec(nF), bspec(0), bspec(nF)],
        out_specs=[pl.BlockSpec((2, tr, tc), lambda j, i: (0, i, j)), wg, wg],
        out_shape=[_sds((2, T, F), BF16), _sds((8, F), F32), _sds((8, F), F32)],
        name=name, compiler_params=_cp("parallel", "arbitrary"),
    )(up, up, up, up, up, up, dact, dact, dact, cw, cw, cb, cb)
    return dup, jnp.concatenate([wga, wgb], axis=1)


def _sc_gate_fwd(name, z, cw, cb):
    T, D3 = z.shape
    D = D3 // 3
    tc, tr = _tile(D, 512, LANES), _tile(T, 512, HALO)
    nD, n_i = D // tc, T // tr

    def body(uc, up_, un, gb, cc, cp, cn, w, b, o_ref):
        i = pl.program_id(1)
        cu = _ext(cc, cp, cn, i, n_i) * _ext(uc, up_, un, i, n_i)
        o_ref[...] = (gb[...].astype(F32) * _mid(_conv(cu, w, b), tr)).astype(o_ref.dtype)

    return pl.pallas_call(
        body, grid=(nD, n_i),
        in_specs=_halo_specs(tr, tc, T, lambda j, i: j) + [pl.BlockSpec((tr, tc), lambda j, i: (i, j + nD))]
        + _halo_specs(tr, tc, T, lambda j, i: j + 2 * nD)
        + [pl.BlockSpec((3, tc), lambda j, i: (0, j)), pl.BlockSpec((1, tc), lambda j, i: (0, j))],
        out_specs=pl.BlockSpec((tr, tc), lambda j, i: (i, j)), out_shape=_sds((T, D), BF16),
        name=name, compiler_params=_cp("parallel", "parallel"))(z, z, z, z, z, z, z, cw, cb)


def _sc_gate_bwd(name, z, dy, cw, cb):
    T, D3 = z.shape
    D = D3 // 3
    tc, tr = _tile(D, 512, LANES), _tile(T, 512, HALO)
    nD, n_i = D // tc, T // tr

    def body(uc, up_, un, bc, bp, bn, cc, cp, cn, yc, yp, yn, w, b, o_ref, wg_ref):
        i = pl.program_id(1)
        u = _ext(uc, up_, un, i, n_i)
        gc = _ext(cc, cp, cn, i, n_i)
        cu = gc * u
        g = _ext(yc, yp, yn, i, n_i) * _ext(bc, bp, bn, i, n_i)
        dcu = _mid(_conv_t(g, w), tr)
        o_ref[0] = (dcu * _mid(gc, tr)).astype(o_ref.dtype)
        o_ref[1] = (yc[...].astype(F32) * _mid(_conv(cu, w, b), tr)).astype(o_ref.dtype)
        o_ref[2] = (dcu * _mid(u, tr)).astype(o_ref.dtype)
        _conv_wgrad(wg_ref, g, cu, tr, i == 0)

    hs = lambda o: _halo_specs(tr, tc, T, lambda j, i: j + o)
    return pl.pallas_call(
        body, grid=(nD, n_i),
        in_specs=hs(0) + hs(nD) + hs(2 * nD) + hs(0)
        + [pl.BlockSpec((3, tc), lambda j, i: (0, j)), pl.BlockSpec((1, tc), lambda j, i: (0, j))],
        out_specs=[pl.BlockSpec((3, tr, tc), lambda j, i: (0, i, j)), pl.BlockSpec((8, tc), lambda j, i: (0, j))],
        out_shape=[_sds((3, T, D), BF16), _sds((8, D), F32)],
        name=name, compiler_params=_cp("parallel", "arbitrary"),
    )(z, z, z, z, z, z, z, z, z, dy, dy, dy, cw, cb)


def _slopes(n_heads):
    return jnp.asarray(2.0 ** (-ALIBI_MAX * np.arange(1, n_heads + 1) / n_heads), dtype=F32)


CHAINS = 8


def _nq(L, d):
    return max(1, min(CHAINS if d == 1 else CHAINS // 2, L // LANES // 2))


def _srows(ref, r, start, n, d):
    if d == 1:
        return ref[start:start + n, :]
    return ref[pl.ds(start * d + r, n, stride=d), :]


def _win(p_ref, c_ref, n_ref, r, b, nq):
    lo, hi, top = b * LANES - BAND, b * LANES + LANES + BAND, nq * LANES
    parts = [p_ref[r]] if lo < 0 else []
    parts.append(c_ref[r, max(lo, 0):min(hi, top), :])
    if hi > top:
        parts.append(n_ref[r])
    return parts[0] if len(parts) == 1 else jnp.concatenate(parts, axis=0)


def _nat_win(p_ref, c_ref, n_ref, r, b, nq, d):
    lo, hi, top = b * LANES - BAND, b * LANES + LANES + BAND, nq * LANES
    parts = [_srows(p_ref, r, 0, BAND, d)] if lo < 0 else []
    parts.append(_srows(c_ref, r, max(lo, 0), min(hi, top) - max(lo, 0), d))
    if hi > top:
        parts.append(_srows(n_ref, r, 0, BAND, d))
    return parts[0] if len(parts) == 1 else jnp.concatenate(parts, axis=0)


def _biases(slope, d, n, n_steps, nq, q_rows, k_rows, q0, k0):
    qi = lax.broadcasted_iota(jnp.int32, (q_rows, k_rows), 0) + q0
    kj = lax.broadcasted_iota(jnp.int32, (q_rows, k_rows), 1) + k0
    dist = jnp.abs(kj - qi)
    base = jnp.where(dist <= BAND, -slope * (dist * d).astype(F32), NEG_INF)
    out = []
    for b in range(nq):
        t = base
        if b == 0:
            t = jnp.where((n == 0) & ((kj < 0) | (qi < 0)), NEG_INF, t)
        if b == nq - 1:
            t = jnp.where((n == n_steps - 1) & ((kj >= LANES) | (qi >= LANES)), NEG_INF, t)
        out.append(t)
    return out


def _win_specs(d, H, col, nq, L):
    return [pl.BlockSpec((d, BAND, LANES), lambda h, n: (0, jnp.maximum(2 * nq * n - 1, 0), col * H + h)),
            pl.BlockSpec((d, nq * LANES, LANES), lambda h, n: (0, n, col * H + h)),
            pl.BlockSpec((d, BAND, LANES), lambda h, n: (0, jnp.minimum(2 * nq * (n + 1), L // BAND - 1), col * H + h))]


def _nat_specs(d, nq, L):
    return [pl.BlockSpec((BAND * d, LANES), lambda h, n: (jnp.maximum(2 * nq * n - 1, 0), h)),
            pl.BlockSpec((nq * LANES * d, LANES), lambda h, n: (n, h)),
            pl.BlockSpec((BAND * d, LANES), lambda h, n: (jnp.minimum(2 * nq * (n + 1), L // BAND - 1), h))]


def _over_residues(d, nq, per_r):
    if d == 1:
        per_r(0, 0)
    else:
        lax.fori_loop(0, d, per_r, 0, unroll=min(d, max(1, CHAINS // nq)))


def _attn_fwd(name, qkv, d, H):
    T = qkv.shape[0]
    D = H * HEAD_DIM
    L = T // d
    nq = _nq(L, d)
    n_steps = L // (nq * LANES)
    scale = HEAD_DIM ** -0.5
    q3 = qkv.reshape(d, L, 3 * D)

    def body(s_ref, q_ref, kp, kc, kn, vp, vc, vn, o_ref, l_ref):
        h, n = pl.program_id(0), pl.program_id(1)
        bias = _biases(s_ref[h], d, n, n_steps, nq, LANES, 2 * LANES, 0, -BAND)

        def per_r(r, carry):
            for b in range(nq):
                k, v = _win(kp, kc, kn, r, b, nq), _win(vp, vc, vn, r, b, nq)
                s = lax.dot_general(q_ref[r, b * LANES:(b + 1) * LANES, :], k, NT, preferred_element_type=F32) * scale + bias[b]
                m = jnp.max(s, axis=1, keepdims=True)
                p = jnp.exp(s - m)
                den = jnp.sum(p, axis=1, keepdims=True)
                o = lax.dot_general(p.astype(BF16), v, NN, preferred_element_type=F32) / den
                lse = jnp.broadcast_to(m + jnp.log(den), (LANES, LANES))
                if d == 1:
                    o_ref[b * LANES:(b + 1) * LANES, :] = o
                    l_ref[b * LANES:(b + 1) * LANES, :] = lse
                else:
                    o_ref[pl.ds(b * LANES * d + r, LANES, stride=d), :] = o
                    l_ref[pl.ds(b * LANES * d + r, LANES, stride=d), :] = lse
            return carry

        _over_residues(d, nq, per_r)

    out = pl.BlockSpec((nq * LANES * d, LANES), lambda h, n: (n, h))
    return pl.pallas_call(
        body, grid=(H, n_steps),
        in_specs=[pl.BlockSpec(memory_space=pltpu.SMEM), pl.BlockSpec((d, nq * LANES, LANES), lambda h, n: (0, n, h))]
        + _win_specs(d, H, 1, nq, L) + _win_specs(d, H, 2, nq, L),
        out_specs=[out, out], out_shape=[_sds((T, D), F32), _sds((T, D), F32)],
        name=name, compiler_params=_cp("parallel", "parallel"))(_slopes(H), q3, q3, q3, q3, q3, q3, q3)


def _attn_combine(name, outs, lses):
    T, D = outs[0].shape
    tr, tc = _tile(T, 512, 16), _tile(D, 512, LANES)

    def body(o0, o1, o2, l0, l1, l2, ob_ref, l_ref):
        a0, a1, a2 = l0[...], l1[...], l2[...]
        m = jnp.maximum(jnp.maximum(a0, a1), a2)
        e0, e1, e2 = jnp.exp(a0 - m), jnp.exp(a1 - m), jnp.exp(a2 - m)
        z = e0 + e1 + e2
        ob_ref[...] = ((e0 * o0[...] + e1 * o1[...] + e2 * o2[...]) / z).astype(BF16)
        l_ref[...] = m + jnp.log(z)

    blk = pl.BlockSpec((tr, tc), lambda i, j: (i, j))
    return pl.pallas_call(
        body, grid=(T // tr, D // tc), in_specs=[blk] * 6, out_specs=[blk] * 2,
        out_shape=[_sds((T, D), BF16), _sds((T, D), F32)],
        name=name, compiler_params=_cp("parallel", "parallel"))(*outs, *lses)


def _attn_stats(name, do, o, lse):
    T, D = do.shape
    tr = _tile(T, 1024, 16)

    def body(a, b, l, o_ref):
        delta = jnp.broadcast_to(jnp.sum(a[...] * b[...].astype(F32), axis=1, keepdims=True), o_ref.shape)
        lane = lax.broadcasted_iota(jnp.int32, o_ref.shape, 1)
        o_ref[...] = jnp.where(lane < BAND, l[...], delta)

    blk = pl.BlockSpec((tr, LANES), lambda i, j: (i, j))
    return pl.pallas_call(body, grid=(T // tr, D // LANES), in_specs=[blk, blk, blk], out_specs=blk,
                          out_shape=_sds((T, D), F32), name=name, compiler_params=_cp("parallel", "parallel"))(do, o, lse)


def _attn_bwd(name, qkv, do, stats, d, H):
    T = qkv.shape[0]
    D = H * HEAD_DIM
    L = T // d
    nq = _nq(L, d)
    n_steps = L // (nq * LANES)
    scale = HEAD_DIM ** -0.5
    q3 = qkv.reshape(d, L, 3 * D)
    mid = slice(BAND, BAND + LANES)

    def body(s_ref, qp, qc, qn, kp, kc, kn, vp, vc, vn, gp, gc, gn, tp, tc_, tn_, o_ref):
        h, n = pl.program_id(0), pl.program_id(1)
        bias_q = _biases(s_ref[h], d, n, n_steps, nq, LANES, 2 * LANES, 0, -BAND)
        bias_k = _biases(s_ref[h], d, n, n_steps, nq, 2 * LANES, LANES, -BAND, 0)

        def per_r(r, carry):
            for b in range(nq):
                rows = slice(b * LANES, (b + 1) * LANES)
                q_w, k_w, v_w = _win(qp, qc, qn, r, b, nq), _win(kp, kc, kn, r, b, nq), _win(vp, vc, vn, r, b, nq)
                g_w = _nat_win(gp, gc, gn, r, b, nq, d)
                t_w = _nat_win(tp, tc_, tn_, r, b, nq, d)
                g_b = g_w.astype(BF16)
                q_c, k_c, v_c, g_c, t_c = q_w[mid], k_w[mid], v_w[mid], g_b[mid], t_w[mid]
                s = lax.dot_general(q_c, k_w, NT, preferred_element_type=F32) * scale + bias_q[b]
                p = jnp.exp(s - t_c[:, 0:1])
                dp = lax.dot_general(g_c, v_w, NT, preferred_element_type=F32)
                ds = p * (dp - t_c[:, BAND:BAND + 1])
                o_ref[0, r, rows, :] = (lax.dot_general(ds.astype(BF16), k_w, NN, preferred_element_type=F32) * scale).astype(BF16)
                s2 = lax.dot_general(q_w, k_c, NT, preferred_element_type=F32) * scale + bias_k[b]
                p2 = jnp.exp(s2 - t_w[:, 0:1])
                o_ref[2, r, rows, :] = lax.dot_general(p2.astype(BF16), g_b, TN, preferred_element_type=F32).astype(BF16)
                dp2 = lax.dot_general(g_b, v_c, NT, preferred_element_type=F32)
                ds2 = p2 * (dp2 - t_w[:, BAND:BAND + 1])
                o_ref[1, r, rows, :] = (lax.dot_general(ds2.astype(BF16), q_w, TN, preferred_element_type=F32) * scale).astype(BF16)
            return carry

        _over_residues(d, nq, per_r)

    dqkv = pl.pallas_call(
        body, grid=(H, n_steps),
        in_specs=[pl.BlockSpec(memory_space=pltpu.SMEM)]
        + _win_specs(d, H, 0, nq, L) + _win_specs(d, H, 1, nq, L) + _win_specs(d, H, 2, nq, L)
        + _nat_specs(d, nq, L) + _nat_specs(d, nq, L),
        out_specs=pl.BlockSpec((3, d, nq * LANES, LANES), lambda h, n: (0, 0, n, h)),
        out_shape=_sds((3, d, L, D), BF16),
        name=name, compiler_params=_cp("parallel", "parallel"),
    )(_slopes(H), *([q3] * 9), *([do] * 3), *([stats] * 3))
    return dqkv.reshape(3, T, D)


def _to_group_order(a, d):
    if d == 1:
        return a
    T, C = a.shape
    return a.reshape(T // d, d, C).swapaxes(0, 1).reshape(T, C)


def _from_group_order(a, d):
    if d == 1:
        return a
    T, C = a.shape
    return a.reshape(d, T // d, C).swapaxes(0, 1).reshape(T, C)


def _fwd_bwd(x, tgt, S, ex):
    T, D = x.shape
    H = D // HEAD_DIM
    G3 = 3 * D

    def mm(fn, *args, rides=(), **kw):
        out, extra = fn(*args, comm=_join([getattr(ex, kind)(keys) for kind, keys in rides]), **kw)
        at = 0
        for kind, keys in rides:
            getattr(ex, kind + "_done")(keys, extra[at:at + len(keys)])
            at += len(keys)
        return out

    def ffn_fwd(l, xin, rides_up, rides_dn):
        hf = _rmsnorm_fwd(f"ffn_norm{l}", xin, S["ffn_g"][l])
        up = mm(_mm_nn, f"ffn_up{l}", hf, ex.w(f"up{l}"), None, BF16, rides=rides_up)
        act = _ffn_gate_fwd(f"ffn_gate{l}", up, S["ffn_cw"][l], S["ffn_cb"][l][None])
        return hf, up, act, mm(_mm_nn, f"ffn_down{l}", act, ex.w(f"dn{l}"), xin, F32, tk_pref=2816, rides=rides_dn)

    def ffn_bwd(l, xin, hf, up, act, dxo, dxo_b, rides):
        dact = mm(_mm_nt, f"ffn_down_dx{l}", dxo_b, ex.w(f"dn{l}"), BF16, tn_pref=1408, rides=rides[0])
        ex.grad(f"dn{l}", mm(_mm_tn, f"ffn_down_dw{l}", act, dxo_b, tkr_pref=1408))
        dup, cg = _ffn_gate_bwd(f"ffn_gate_bwd{l}", up, dact, S["ffn_cw"][l], S["ffn_cb"][l][None])
        dhf = mm(_mm_nt, f"ffn_up_dx{l}", dup, ex.w(f"up{l}"), BF16, tk_pref=2816, rides=rides[1])
        ex.grad(f"up{l}", mm(_mm_tn, f"ffn_up_dw{l}", hf, dup, tn_pref=1408, rides=rides[2]))
        dx, dx_b, dg = mm(_rmsnorm_bwd, f"ffn_norm_bwd{l}", xin, S["ffn_g"][l], [dhf], dxo)
        return dx, dx_b, dg, cg

    h0 = _rmsnorm_fwd("mix_norm0", x, S["mix_g"][0])
    z = mm(_mm_nn, "sc_in", h0, ex.w("in"), None, BF16, rides=[("ag", ["sco", "up0"])])
    y = _sc_gate_fwd("sc_gate", z, S["sc_cw"], S["sc_cb"][None])
    x1 = mm(_mm_nn, "sc_out", y, ex.w("sco"), x, F32, rides=[("ag", ["dn0"])])
    hf0, up0, act0, x2 = ffn_fwd(0, x1, [("ag", ["qkv"])], [("ag", ["ao"])])
    h1 = _rmsnorm_fwd("mix_norm1", x2, S["mix_g"][1])
    hd, qkv, outs, lses = [], [], [], []
    for g, d in enumerate(DILATIONS):
        hd.append(_to_group_order(h1, d))
        qkv.append(mm(_mm_nn, f"attn_qkv{g}", hd[g], ex.w("qkv"), None, BF16, col0=g * G3, n_cols=G3,
                      rides=[[("ag", ["dn1"])], [("ag", ["up1"])], ()][g]))
        o_g, l_g = _attn_fwd(f"attn_fwd{g}", qkv[g], d, H)
        outs.append(o_g)
        lses.append(l_g)
    o_b, lse = _attn_combine("attn_combine", outs, lses)
    x3 = mm(_mm_nn, "attn_out", o_b, ex.w("ao"), x2, F32)
    hf1, up1, act1, x4 = ffn_fwd(1, x3, (), ())
    loss, dx4, dx4_b, dg_fin = _final_loss("final_loss", x4, S["fin_g"], tgt)

    dx3, dx3_b, dg_f1, cg1 = ffn_bwd(1, x3, hf1, up1, act1, dx4, dx4_b,
                                     [(), [("pair", ["dn1"])], [("chip", ["dn1"])]])
    do = mm(_mm_nt, "attn_out_dx", dx3_b, ex.w("ao"), F32, rides=[("pair", ["up1"])])
    ex.grad("ao", mm(_mm_tn, "attn_out_dw", o_b, dx3_b))
    stats = _attn_stats("attn_stats", do, o_b, lse)
    dhs, dw_qkv = [], None
    qkv_rides = [[("chip", ["up1"]), ("pair", ["ao"])], [("chip", ["ao"])], ()]
    for g, d in enumerate(DILATIONS):
        dqkv = _attn_bwd(f"attn_bwd{g}", qkv[g], do, stats, d, H)
        dhs.append(_from_group_order(mm(_mm_nt, f"attn_qkv_dx{g}", dqkv, ex.w("qkv"), BF16, col0=g * G3,
                                        rides=qkv_rides[g]), d))
        dw_qkv = mm(_mm_tn, f"attn_qkv_dw{g}", hd[g], dqkv, n_total=len(DILATIONS) * G3, col0=g * G3, prev=dw_qkv)
    ex.grad("qkv", dw_qkv)
    dx2, dx2_b, dg_m1 = mm(_rmsnorm_bwd, "mix_norm_bwd1", x2, S["mix_g"][1], dhs, dx3)
    dx1, dx1_b, dg_f0, cg0 = ffn_bwd(0, x1, hf0, up0, act0, dx2, dx2_b,
                                     [[("pair", ["qkv"])], [("chip", ["qkv"]), ("pair", ["dn0"])], [("chip", ["dn0"])]])
    dy = mm(_mm_nt, "sc_out_dx", dx1_b, ex.w("sco"), BF16, rides=[("pair", ["up0"])])
    ex.grad("sco", mm(_mm_tn, "sc_out_dw", y, dx1_b))
    dz, cg_sc = _sc_gate_bwd("sc_gate_bwd", z, dy, S["sc_cw"], S["sc_cb"][None])
    ex.grad("in", mm(_mm_tn, "sc_in_dw", h0, dz, rides=[("chip", ["up0"]), ("pair", ["sco"])]))
    dh0 = mm(_mm_nt, "sc_in_dx", dz, ex.w("in"), BF16, rides=[("chip", ["sco"]), ("pair", ["in"])])
    dx0, _, dg_m0 = mm(_rmsnorm_bwd, "mix_norm_bwd0", x, S["mix_g"][0], [dh0], dx1, rides=[("chip", ["in"])])

    dS = {"mix_g": jnp.stack([dg_m0, dg_m1]), "ffn_g": jnp.stack([dg_f0, dg_f1]), "fin_g": dg_fin,
          "sc_cw": cg_sc[0:3], "sc_cb": cg_sc[3], "ffn_cw": jnp.stack([cg0[0:3], cg1[0:3]]),
          "ffn_cb": jnp.stack([cg0[3], cg1[3]])}
    return loss, dx0, dS


def _place():
    x, y, c = lax.axis_index("x"), lax.axis_index("y"), lax.axis_index("c")
    return x, y, c, [(1 - x, y), (x, 1 - y), (1 - x, 1 - y)]


def _block(ref, s, shape, axis):
    R, C = shape
    if axis == 0:
        return ref.at[pl.ds(pl.multiple_of(s * R, HALO), R), :]
    return ref.at[:, pl.ds(pl.multiple_of(s * C, LANES), C)]


def _whole(shape, axis):
    return (shape[0] * N_DEV, shape[1]) if axis == 0 else (shape[0], shape[1] * N_DEV)


def _ag_stage(arrs, axes):
    n = len(arrs)

    def plan(ins, outs, sems):
        send_sems, recv_sems, local_sems = sems
        x, y, c, chips = _place()
        me, sibling = 4 * x + 2 * y + c, (x, y, 1 - c)

        def copy(a, k, blk, to, src=None):
            dst = _block(outs[a], blk, arrs[a].shape, axes[a])
            return pltpu.make_async_remote_copy(src_ref=dst if src is None else src, dst_ref=dst,
                                                send_sem=send_sems.at[a, k], recv_sem=recv_sems.at[a, k],
                                                device_id=to, device_id_type=MESH)

        mine = [pltpu.make_async_copy(ins[a], _block(outs[a], me, arrs[a].shape, axes[a]), local_sems.at[a])
                for a in range(n)]
        first = []
        for a in range(n):
            first.append(copy(a, 0, me, sibling, src=ins[a]))
            first += [copy(a, 1 + j, me, (*chip, c), src=ins[a]) for j, chip in enumerate(chips)]
        return x, y, c, chips, sibling, copy, mine, first

    def start(ins, outs, sems):
        *_, mine, first = plan(ins, outs, sems)
        for cp in mine + first:
            cp.start()

    def finish(ins, outs, sems):
        x, y, c, chips, sibling, copy, mine, first = plan(ins, outs, sems)
        passed = []
        for j, (px, py) in enumerate(chips):
            for a in range(n):
                blk = 4 * px + 2 * py + c
                copy(a, 1 + j, blk, sibling).wait_recv()
                passed.append(copy(a, 4 + j, blk, sibling))
                passed[-1].start()
        for a in range(n):
            copy(a, 0, 4 * x + 2 * y + 1 - c, sibling).wait_recv()
            for j, (px, py) in enumerate(chips):
                copy(a, 4 + j, 4 * px + 2 * py + 1 - c, sibling).wait_recv()
        for cp in first + passed:
            cp.wait_send()
        for cp in mine:
            cp.wait()

    return _Stage(list(arrs), [_sds(_whole(a.shape, ax), a.dtype) for a, ax in zip(arrs, axes)],
                  [pltpu.SemaphoreType.DMA((n, 7)), pltpu.SemaphoreType.DMA((n, 7)), pltpu.SemaphoreType.DMA((n,))],
                  start, finish)


def _pair_stage(dws, shapes, axes):
    n = len(dws)

    def copies(ins, outs, sems):
        send_sems, recv_sems = sems
        x, y, c, _ = _place()
        return [pltpu.make_async_remote_copy(src_ref=_block(ins[a], 2 * k + 1 - c, shapes[a], axes[a]),
                                             dst_ref=outs[a].at[k],
                                             send_sem=send_sems.at[a, k], recv_sem=recv_sems.at[a, k],
                                             device_id=(x, y, 1 - c), device_id_type=MESH)
                for a in range(n) for k in range(4)]

    def start(ins, outs, sems):
        for cp in copies(ins, outs, sems):
            cp.start()

    def finish(ins, outs, sems):
        for cp in copies(ins, outs, sems):
            cp.wait()

    return _Stage(list(dws), [_sds((4,) + tuple(s), a.dtype) for a, s in zip(dws, shapes)],
                  [pltpu.SemaphoreType.DMA((n, 4)), pltpu.SemaphoreType.DMA((n, 4))], start, finish)


def _rs_pair_add(name, dw, got, c_arr, axis):
    _, R, C = got.shape
    tr, tc = _tile(R, 512, 16), _tile(C, 1536, LANES)
    per = R // tr if axis == 0 else C // tc

    def body(c_ref, a_ref, b_ref, o_ref):
        o_ref[...] = (a_ref[...].astype(F32) + b_ref[...].astype(F32)).astype(o_ref.dtype)

    if axis == 0:
        mine = pl.BlockSpec((tr, tc), lambda k, i, j, c_ref: ((2 * k + c_ref[0]) * per + i, j))
    else:
        mine = pl.BlockSpec((tr, tc), lambda k, i, j, c_ref: (i, (2 * k + c_ref[0]) * per + j))
    return pl.pallas_call(
        body,
        grid_spec=pltpu.PrefetchScalarGridSpec(
            num_scalar_prefetch=1, grid=(4, R // tr, C // tc),
            in_specs=[mine, pl.BlockSpec((None, tr, tc), lambda k, i, j, c_ref: (k, i, j))],
            out_specs=pl.BlockSpec((None, tr, tc), lambda k, i, j, c_ref: (k, i, j))),
        out_shape=_sds((4, R, C), BF16), name=name,
        compiler_params=_cp("parallel", "parallel", "parallel"))(c_arr, dw, got)


def _chip_stage(parts):
    n = len(parts)

    def plan(ins, outs, sems):
        send_sems, recv_sems, local_sems = sems
        x, y, c, chips = _place()
        my_chip = 2 * x + y
        mine = [pltpu.make_async_copy(ins[a].at[my_chip], outs[a].at[my_chip], local_sems.at[a]) for a in range(n)]
        sends = [pltpu.make_async_remote_copy(src_ref=ins[a].at[2 * px + py], dst_ref=outs[a].at[my_chip],
                                              send_sem=send_sems.at[a, j], recv_sem=recv_sems.at[a, j],
                                              device_id=(px, py, c), device_id_type=MESH)
                 for a in range(n) for j, (px, py) in enumerate(chips)]
        arrivals = lambda: [pltpu.make_async_remote_copy(src_ref=ins[a].at[my_chip], dst_ref=outs[a].at[2 * px + py],
                                                         send_sem=send_sems.at[a, j], recv_sem=recv_sems.at[a, j],
                                                         device_id=(px, py, c), device_id_type=MESH)
                            for a in range(n) for j, (px, py) in enumerate(chips)]
        return mine, sends, arrivals

    def start(ins, outs, sems):
        mine, sends, _ = plan(ins, outs, sems)
        for cp in mine + sends:
            cp.start()

    def finish(ins, outs, sems):
        mine, sends, arrivals = plan(ins, outs, sems)
        for cp in arrivals():
            cp.wait_recv()
        for cp in sends:
            cp.wait_send()
        for cp in mine:
            cp.wait()

    return _Stage(list(parts), [_sds(a.shape, a.dtype) for a in parts],
                  [pltpu.SemaphoreType.DMA((n, 3)), pltpu.SemaphoreType.DMA((n, 3)), pltpu.SemaphoreType.DMA((n,))],
                  start, finish)


class _Exchange:
    ROW_SHARDED = ("sco", "ao", "dn0", "dn1")

    def __init__(self, shards, c_arr):
        self.sh, self.c_arr = shards, c_arr
        self.W, self.dw, self.parts, self.sums = {}, {}, {}, {}

    def axis(self, key):
        return 0 if key in self.ROW_SHARDED else 1

    def w(self, key):
        return self.W[key]

    def ag(self, keys):
        return _ag_stage([self.sh[k] for k in keys], [self.axis(k) for k in keys])

    def ag_done(self, keys, outs):
        self.W.update(zip(keys, outs))

    def grad(self, key, dw):
        self.dw[key] = dw

    def pair(self, keys):
        return _pair_stage([self.dw[k] for k in keys], [self.sh[k].shape for k in keys], [self.axis(k) for k in keys])

    def pair_done(self, keys, outs):
        for k, got in zip(keys, outs):
            self.parts[k] = _rs_pair_add(f"rs_add_{k}", self.dw[k], got, self.c_arr, self.axis(k))

    def chip(self, keys):
        return _chip_stage([self.parts[k] for k in keys])

    def chip_done(self, keys, outs):
        self.sums.update(zip(keys, outs))


def _sum_slots(name, a):
    _, rows, _ = a.shape

    def body(a_ref, o_ref):
        s = a_ref[0]
        for k in range(1, N_DEV):
            s = s + a_ref[k]
        o_ref[...] = s

    return pl.pallas_call(body, out_shape=_sds((rows, LANES), F32), name=name)(a)


def _cast_bf16(name, w3, l):
    _, R, C = w3.shape
    tr, tc = _tile(R, 512, 16), _tile(C, 1536, LANES)

    def body(w_ref, o_ref):
        o_ref[...] = w_ref[...].astype(BF16)

    return pl.pallas_call(
        body, grid=(R // tr, C // tc), in_specs=[pl.BlockSpec((None, tr, tc), lambda i, j: (l, i, j))],
        out_specs=pl.BlockSpec((tr, tc), lambda i, j: (i, j)), out_shape=_sds((R, C), BF16),
        name=name, compiler_params=_cp("parallel", "parallel"))(w3)


def _adamw(name, g_slots, w3, m3, v3, l, prev):
    n_slots, R, C = g_slots.shape
    tr, tc = _tile(R, 256, 8), _tile(C, 1536, LANES)
    c1, c2 = 1.0 - ADAM_B1 ** ADAM_STEP, 1.0 - ADAM_B2 ** ADAM_STEP

    def body(g_ref, w_ref, m_ref, v_ref, *rest):
        og, od, om, ov = rest[-4:]
        g = g_ref[0].astype(F32)
        for k in range(1, n_slots):
            g = g + g_ref[k].astype(F32)
        m = ADAM_B1 * m_ref[...] + (1.0 - ADAM_B1) * g
        v = ADAM_B2 * v_ref[...] + (1.0 - ADAM_B2) * (g * g)
        og[...] = g
        om[...] = m
        ov[...] = v
        od[...] = -ADAM_LR * ((m / c1) / (jnp.sqrt(v / c2) + ADAM_EPS) + ADAM_WD * w_ref[...])

    lay = pl.BlockSpec((None, tr, tc), lambda i, j: (l, i, j))
    ops = [g_slots, w3, m3, v3]
    specs = [pl.BlockSpec((n_slots, tr, tc), lambda i, j: (0, i, j)), lay, lay, lay]
    aliases = {}
    if prev is not None:
        ops += list(prev)
        specs += _any_specs(4)
        aliases = {4 + k: k for k in range(4)}
    return pl.pallas_call(
        body, grid=(R // tr, C // tc), in_specs=specs, out_specs=[lay] * 4,
        out_shape=[_sds(w3.shape, F32)] * 4, input_output_aliases=aliases,
        name=name, compiler_params=_cp("parallel", "parallel"))(*ops)


def _pack(parts):
    flat = jnp.concatenate([p.reshape(-1) for p in parts])
    pad = (-flat.shape[0]) % (HALO * LANES)
    return jnp.pad(flat, (0, pad)).reshape(-1, LANES)


def _unpack(packed, shapes):
    flat = packed.reshape(-1)
    out, at = [], 0
    for s in shapes:
        n = int(np.prod(s))
        out.append(flat[at:at + n].reshape(s))
        at += n
    return out


def kernel(x, mix_norm_g, ffn_norm_g, final_norm_g, sc_w_in, sc_conv_w, sc_conv_b, sc_w_out, attn_w_qkv, attn_w_out, ffn_w_up, ffn_conv_w, ffn_conv_b, ffn_w_down, loss_target, m_mix_norm_g, m_ffn_norm_g, m_final_norm_g, m_sc_w_in, m_sc_conv_w, m_sc_conv_b, m_sc_w_out, m_attn_w_qkv, m_attn_w_out, m_ffn_w_up, m_ffn_conv_w, m_ffn_conv_b, m_ffn_w_down, v_mix_norm_g, v_ffn_norm_g, v_final_norm_g, v_sc_w_in, v_sc_conv_w, v_sc_conv_b, v_sc_w_out, v_attn_w_qkv, v_attn_w_out, v_ffn_w_up, v_ffn_conv_w, v_ffn_conv_b, v_ffn_w_down):
    n_layers = ffn_w_up.shape[0]
    me = 4 * lax.axis_index("x") + 2 * lax.axis_index("y") + lax.axis_index("c")
    c_arr = lax.axis_index("c").astype(jnp.int32).reshape(1)

    big = [("in", sc_w_in, 0), ("sco", sc_w_out, 0), ("qkv", attn_w_qkv, 0), ("ao", attn_w_out, 0)]
    big += [(f"up{l}", ffn_w_up, l) for l in range(n_layers)] + [(f"dn{l}", ffn_w_down, l) for l in range(n_layers)]
    ex = _Exchange({nm: _cast_bf16(f"cast_{nm}", w, l) for nm, w, l in big}, c_arr)

    cw_mine = [sc_conv_w.reshape(-1, sc_conv_w.shape[-1]), ffn_conv_w.reshape(-1, ffn_conv_w.shape[-1])]
    w_in_all, sc_cw, ffn_cw = _run_stage("gather_first", _join([ex.ag(["in"]), _ag_stage(cw_mine, [1, 1])]))
    ex.ag_done(["in"], [w_in_all])
    ffn_cw = ffn_cw.reshape(n_layers, -1, ffn_cw.shape[-1])
    S = {"mix_g": mix_norm_g, "ffn_g": ffn_norm_g, "fin_g": final_norm_g, "sc_cw": sc_cw, "sc_cb": sc_conv_b[0],
         "ffn_cw": ffn_cw, "ffn_cb": ffn_conv_b}

    loss_part, grad_x, dS = _fwd_bwd(x[0], loss_target[0], S, ex)

    small_names = ["mix_g", "ffn_g", "fin_g", "sc_cb", "ffn_cb", "sc_cw", "ffn_cw"]
    small_parts = [dS[k] for k in small_names] + [loss_part.reshape(1)]
    small_mine = _pack(small_parts)
    small_all, = _run_stage("gather_small", _ag_stage([small_mine], [0]))
    small_sum = _sum_slots("sum_small", small_all.reshape((N_DEV,) + small_mine.shape))
    g_mix, g_ffn, g_fin, g_scb, g_fcb, g_scw, g_fcw, loss = _unpack(small_sum, [p.shape for p in small_parts])
    g_scw = lax.dynamic_slice_in_dim(g_scw, me * sc_conv_w.shape[-1], sc_conv_w.shape[-1], axis=-1)[None]
    g_fcw = lax.dynamic_slice_in_dim(g_fcw, me * ffn_conv_w.shape[-1], ffn_conv_w.shape[-1], axis=-1)
    g_scb = g_scb[None]
    small_g = [g_mix, g_ffn, g_fin, g_scw, g_scb, g_fcw, g_fcb]
    small_w = [mix_norm_g, ffn_norm_g, final_norm_g, sc_conv_w, sc_conv_b, ffn_conv_w, ffn_conv_b]
    small_m = [m_mix_norm_g, m_ffn_norm_g, m_final_norm_g, m_sc_conv_w, m_sc_conv_b, m_ffn_conv_w, m_ffn_conv_b]
    small_v = [v_mix_norm_g, v_ffn_norm_g, v_final_norm_g, v_sc_conv_w, v_sc_conv_b, v_ffn_conv_w, v_ffn_conv_b]
    small_out = _adamw("adamw_small", _pack(small_g)[None], _pack(small_w)[None], _pack(small_m)[None],
                       _pack(small_v)[None], 0, None)
    small_shapes = [w.shape for w in small_w]
    sg, sd, sm, sv = [_unpack(o[0], small_shapes) for o in small_out]

    moments = {"in": (m_sc_w_in, v_sc_w_in), "sco": (m_sc_w_out, v_sc_w_out), "qkv": (m_attn_w_qkv, v_attn_w_qkv),
               "ao": (m_attn_w_out, v_attn_w_out), "up": (m_ffn_w_up, v_ffn_w_up), "dn": (m_ffn_w_down, v_ffn_w_down)}
    upd = {}
    for nm, w, l in big:
        key = nm.rstrip("0123456789")
        upd[key] = _adamw(f"adamw_{nm}", ex.sums[nm], w, moments[key][0], moments[key][1], l, upd.get(key))

    def leaves(k):
        return [sg, sd, sm, sv][k][0:3] + [upd["in"][k], [sg, sd, sm, sv][k][3], [sg, sd, sm, sv][k][4], upd["sco"][k],
                                          upd["qkv"][k], upd["ao"][k], upd["up"][k], [sg, sd, sm, sv][k][5],
                                          [sg, sd, sm, sv][k][6], upd["dn"][k]]

    return (loss.reshape(()), grad_x[None], *leaves(0), *leaves(1), *leaves(2), *leaves(3))
```

```python
import math

import numpy as np
import jax
import jax.numpy as jnp
from jax import lax
from jax.experimental import pallas as pl
from jax.experimental.pallas import tpu as pltpu

F32 = jnp.float32
BF16 = jnp.bfloat16
MESH = pl.DeviceIdType.MESH

HEAD_DIM = 128
DILATED_GROUPS = ((128, 1), (512, 4), (2048, 16))
DILATIONS = tuple(d for _, d in DILATED_GROUPS)
BAND = (DILATED_GROUPS[0][0] // 2) // DILATED_GROUPS[0][1]
assert all((w // 2) // d == BAND for w, d in DILATED_GROUPS)
NORM_EPS = 1e-5
ALIBI_MAX = 8.0
NEG_INF = -1e30
ADAM_LR, ADAM_B1, ADAM_B2, ADAM_EPS, ADAM_WD, ADAM_STEP = 0.001, 0.9, 0.999, 1e-08, 0.01, 10

N_DEV = 8
LANES = 128
HALO = 16
VMEM_LIMIT = 56 * 1024 * 1024


def _cp(*sem):
    return pltpu.CompilerParams(dimension_semantics=sem, vmem_limit_bytes=VMEM_LIMIT)


def _tile(n, pref, mult):
    t = (min(n, pref) // mult) * mult
    while t >= mult:
        if n % t == 0:
            return t
        t -= mult
    return n


def _sds(shape, dtype):
    return jax.ShapeDtypeStruct(shape, dtype)


def _any_specs(n):
    return [pl.BlockSpec(memory_space=pl.ANY)] * n


class _Stage:
    def __init__(self, arrays, out_shapes, sems, start, finish):
        self.arrays, self.out_shapes, self.sems, self.start, self.finish = arrays, out_shapes, sems, start, finish


def _join(stages):
    stages = [s for s in stages if s is not None]
    if not stages:
        return None

    def split(refs, count):
        out, at = [], 0
        for s in stages:
            out.append(refs[at:at + count(s)])
            at += count(s)
        return out

    def each(which):
        def run(ins, outs, sems):
            parts = zip(split(ins, lambda s: len(s.arrays)), split(outs, lambda s: len(s.out_shapes)),
                        split(sems, lambda s: len(s.sems)))
            for s, (i, o, m) in zip(stages, parts):
                getattr(s, which)(i, o, m)
        return run

    return _Stage(sum([s.arrays for s in stages], []), sum([s.out_shapes for s in stages], []),
                  sum([s.sems for s in stages], []), each("start"), each("finish"))


def _run_stage(name, st):
    n, m = len(st.arrays), len(st.out_shapes)

    def body(*refs):
        ins, outs, sems = refs[:n], refs[n:n + m], refs[n + m:]
        st.start(ins, outs, sems)
        st.finish(ins, outs, sems)

    return pl.pallas_call(body, in_specs=_any_specs(n), out_specs=_any_specs(m), out_shape=st.out_shapes,
                          scratch_shapes=st.sems, name=name)(*st.arrays)


NN = (((1,), (0,)), ((), ()))
NT = (((1,), (1,)), ((), ()))
TN = (((0,), (0,)), ((), ()))


def _mm(name, operands, in_specs, out_sds, o_spec, grid, dims, acc_shape, has_res=False, aliases=None, comm=None):
    nk = grid[2]
    n_in = len(operands)
    n_ci, n_co = (len(comm.arrays), len(comm.out_shapes)) if comm else (0, 0)

    def body(*refs):
        a_ref, b_ref = refs[0], refs[1]
        r_ref = refs[2] if has_res else None
        o_ref = refs[n_in + n_ci]
        acc = refs[n_in + n_ci + 1 + n_co]
        c_refs = (refs[n_in:n_in + n_ci], refs[n_in + n_ci + 1:n_in + n_ci + 1 + n_co], refs[n_in + n_ci + 2 + n_co:])
        ids = [pl.program_id(q) for q in range(3)]
        if comm:
            @pl.when((ids[0] == 0) & (ids[1] == 0) & (ids[2] == 0))
            def _():
                comm.start(*c_refs)

        def finish(total):
            if has_res:
                total = total + r_ref[...]
            o_ref[...] = total.astype(o_ref.dtype)

        if nk == 1:
            finish(lax.dot_general(a_ref[...], b_ref[...], dims, preferred_element_type=F32))
        else:
            k = ids[2]

            @pl.when(k == 0)
            def _():
                acc[...] = jnp.zeros_like(acc)

            acc[...] += lax.dot_general(a_ref[...], b_ref[...], dims, preferred_element_type=F32)

            @pl.when(k == nk - 1)
            def _():
                finish(acc[...])

        if comm:
            @pl.when((ids[0] == grid[0] - 1) & (ids[1] == grid[1] - 1) & (ids[2] == nk - 1))
            def _():
                comm.finish(*c_refs)

    scratch = [pltpu.VMEM(acc_shape if nk > 1 else (8, LANES), F32)]
    if not comm:
        out = pl.pallas_call(
            body, grid=grid, in_specs=in_specs, out_specs=o_spec, out_shape=out_sds, scratch_shapes=scratch,
            input_output_aliases=aliases or {}, name=name,
            compiler_params=_cp("parallel", "parallel", "arbitrary"))(*operands)
        return out, []
    outs = pl.pallas_call(
        body, grid=grid, in_specs=list(in_specs) + _any_specs(n_ci), out_specs=[o_spec] + _any_specs(n_co),
        out_shape=[out_sds] + comm.out_shapes, scratch_shapes=scratch + comm.sems,
        input_output_aliases=aliases or {}, name=name,
        compiler_params=_cp("arbitrary", "arbitrary", "arbitrary"))(*operands, *comm.arrays)
    return outs[0], list(outs[1:])


def _stack(a):
    return a if a.ndim == 3 else a[None]


def _mm_nn(name, a, w, res, out_dtype, col0=0, n_cols=None, tk_pref=2048, comm=None):
    M, K = a.shape
    N = n_cols or w.shape[1]
    tm, tn, tk = _tile(M, 1024, 16), _tile(N, 1024, LANES), _tile(K, tk_pref, LANES)
    c0 = col0 // tn
    ops = [a, w]
    specs = [pl.BlockSpec((tm, tk), lambda i, j, k: (i, k)), pl.BlockSpec((tk, tn), lambda i, j, k: (k, c0 + j))]
    if res is not None:
        ops.append(res)
        specs.append(pl.BlockSpec((tm, tn), lambda i, j, k: (i, j)))
    return _mm(name, tuple(ops), specs, _sds((M, N), out_dtype), pl.BlockSpec((tm, tn), lambda i, j, k: (i, j)),
               (M // tm, N // tn, K // tk), NN, (tm, tn), has_res=res is not None, comm=comm)


def _mm_nt(name, dy, w, out_dtype, col0=0, tn_pref=1024, tk_pref=2048, comm=None):
    dy = _stack(dy)
    _, M, Np = dy.shape
    Kw = w.shape[0]
    tm, tn, tk = _tile(M, 1024, 16), _tile(Kw, tn_pref, LANES), _tile(Np, tk_pref, LANES)
    per, c0 = Np // tk, col0 // tk
    return _mm(name, (dy, w),
               [pl.BlockSpec((None, tm, tk), lambda i, j, k: (k // per, i, k % per)),
                pl.BlockSpec((tn, tk), lambda i, j, k: (j, c0 + k))],
               _sds((M, Kw), out_dtype), pl.BlockSpec((tm, tn), lambda i, j, k: (i, j)),
               (M // tm, Kw // tn, dy.shape[0] * per), NT, (tm, tn), comm=comm)


def _mm_tn(name, a, dy, n_total=None, col0=0, tkr_pref=1024, tn_pref=1024, prev=None, comm=None):
    dy = _stack(dy)
    P, M, Np = dy.shape
    Kw = a.shape[1]
    tkr, tn, tk = _tile(Kw, tkr_pref, LANES), _tile(Np, tn_pref, LANES), _tile(M, 2048, 16)
    per, c0 = Np // tn, col0 // tn
    ops = [a, dy]
    specs = [pl.BlockSpec((tk, tkr), lambda i, j, k: (k, i)),
             pl.BlockSpec((None, tk, tn), lambda i, j, k: (j // per, k, j % per))]
    aliases = None
    if prev is not None:
        ops.append(prev)
        specs.append(pl.BlockSpec(memory_space=pl.ANY))
        aliases = {2: 0}
    return _mm(name, tuple(ops), specs, _sds((Kw, n_total or P * Np), BF16),
               pl.BlockSpec((tkr, tn), lambda i, j, k: (i, c0 + j)),
               (Kw // tkr, P * per, M // tk), TN, (tkr, tn), aliases=aliases, comm=comm)


def _rmsnorm_fwd(name, x, g):
    T, D = x.shape
    tr = _tile(T, 512, 16)

    def body(x_ref, g_ref, h_ref):
        xf = x_ref[...]
        r = lax.rsqrt(jnp.mean(xf * xf, axis=-1, keepdims=True) + NORM_EPS)
        h_ref[...] = (xf * r * g_ref[...]).astype(h_ref.dtype)

    return pl.pallas_call(
        body, grid=(T // tr,),
        in_specs=[pl.BlockSpec((tr, D), lambda i: (i, 0)), pl.BlockSpec((1, D), lambda i: (0, 0))],
        out_specs=pl.BlockSpec((tr, D), lambda i: (i, 0)), out_shape=_sds((T, D), BF16),
        name=name, compiler_params=_cp("parallel"))(x, g.reshape(1, D))


def _rmsnorm_bwd(name, x, g, dhs, dres, comm=None):
    T, D = x.shape
    tr = _tile(T, 256, 16)
    n_dh = len(dhs)
    n_in = 3 + n_dh
    n_ci, n_co = (len(comm.arrays), len(comm.out_shapes)) if comm else (0, 0)

    def body(*refs):
        x_ref, g_ref = refs[0], refs[1]
        dh_refs = refs[2:2 + n_dh]
        dres_ref = refs[2 + n_dh]
        dx_ref, dxb_ref, dg_ref = refs[n_in + n_ci:n_in + n_ci + 3]
        c_refs = (refs[n_in:n_in + n_ci], refs[n_in + n_ci + 3:n_in + n_ci + 3 + n_co], refs[n_in + n_ci + 3 + n_co:])
        if comm:
            @pl.when(pl.program_id(0) == 0)
            def _():
                comm.start(*c_refs)

        xf = x_ref[...]
        r = lax.rsqrt(jnp.mean(xf * xf, axis=-1, keepdims=True) + NORM_EPS)
        xhat = xf * r
        dh = dh_refs[0][...].astype(F32)
        for q in dh_refs[1:]:
            dh = dh + q[...].astype(F32)
        dy = dh * g_ref[...]
        c = jnp.mean(dy * xhat, axis=-1, keepdims=True)
        dx = dres_ref[...] + r * (dy - xhat * c)
        dx_ref[...] = dx
        dxb_ref[...] = dx.astype(BF16)

        @pl.when(pl.program_id(0) == 0)
        def _():
            dg_ref[...] = jnp.zeros_like(dg_ref)

        dg_ref[...] += jnp.sum(dh * xhat, axis=0, keepdims=True)

        if comm:
            @pl.when(pl.program_id(0) == T // tr - 1)
            def _():
                comm.finish(*c_refs)

    row = pl.BlockSpec((tr, D), lambda i: (i, 0))
    vec = pl.BlockSpec((1, D), lambda i: (0, 0))
    outs = pl.pallas_call(
        body, grid=(T // tr,), in_specs=[row, vec] + [row] * n_dh + [row] + _any_specs(n_ci),
        out_specs=[row, row, vec] + _any_specs(n_co),
        out_shape=[_sds((T, D), F32), _sds((T, D), BF16), _sds((1, D), F32)] + (comm.out_shapes if comm else []),
        scratch_shapes=comm.sems if comm else [],
        name=name, compiler_params=_cp("arbitrary"))(x, g.reshape(1, D), *dhs, dres, *(comm.arrays if comm else []))
    return (outs[0], outs[1], outs[2][0]), list(outs[3:])


def _final_loss(name, x, g, tgt):
    T, D = x.shape
    tr = _tile(T, 256, 16)

    def body(x_ref, g_ref, t_ref, dx_ref, dxb_ref, dg_ref, loss_ref):
        xf = x_ref[...]
        r = lax.rsqrt(jnp.mean(xf * xf, axis=-1, keepdims=True) + NORM_EPS)
        xhat = xf * r
        err = xhat * g_ref[...] - t_ref[...]
        dy = err * (1.0 / D)
        dxh = dy * g_ref[...]
        c = jnp.mean(dxh * xhat, axis=-1, keepdims=True)
        dx = r * (dxh - xhat * c)
        dx_ref[...] = dx
        dxb_ref[...] = dx.astype(BF16)

        @pl.when(pl.program_id(0) == 0)
        def _():
            dg_ref[...] = jnp.zeros_like(dg_ref)
            loss_ref[...] = jnp.zeros_like(loss_ref)

        dg_ref[...] += jnp.sum(dy * xhat, axis=0, keepdims=True)
        loss_ref[...] += 0.5 * jnp.sum(jnp.mean(err * err, axis=-1, keepdims=True), axis=0, keepdims=True)

    row = pl.BlockSpec((tr, D), lambda i: (i, 0))
    vec = pl.BlockSpec((1, D), lambda i: (0, 0))
    dx, dx_b, dg, loss = pl.pallas_call(
        body, grid=(T // tr,), in_specs=[row, vec, row],
        out_specs=[row, row, vec, pl.BlockSpec((1, 1), lambda i: (0, 0))],
        out_shape=[_sds((T, D), F32), _sds((T, D), BF16), _sds((1, D), F32), _sds((1, 1), F32)],
        name=name, compiler_params=_cp("arbitrary"))(x, g.reshape(1, D), tgt)
    return loss[0, 0], dx, dx_b, dg[0]


def _halo_specs(tr, tc, n_rows, col):
    rb = tr // HALO
    last = n_rows // HALO - 1
    return [pl.BlockSpec((tr, tc), lambda *g: (g[-1], col(*g))),
            pl.BlockSpec((HALO, tc), lambda *g: (jnp.maximum(g[-1] * rb - 1, 0), col(*g))),
            pl.BlockSpec((HALO, tc), lambda *g: (jnp.minimum((g[-1] + 1) * rb, last), col(*g)))]


def _ext(cur_ref, prev_ref, next_ref, i, n_i):
    p = prev_ref[...].astype(F32) * (i > 0).astype(F32)
    n = next_ref[...].astype(F32) * (i < n_i - 1).astype(F32)
    return jnp.concatenate([p, cur_ref[...].astype(F32), n], axis=0)


def _shift_dn(x):
    return pltpu.roll(x, 1, axis=0)


def _shift_up(x):
    return pltpu.roll(x, x.shape[0] - 1, axis=0)


def _conv(x, w_ref, b_ref):
    return w_ref[0:1, :] * _shift_dn(x) + w_ref[1:2, :] * x + w_ref[2:3, :] * _shift_up(x) + b_ref[...]


def _mid(x, tr):
    return x[HALO:HALO + tr, :]


def _conv_t(g, w_ref):
    return w_ref[0:1, :] * _shift_up(g) + w_ref[1:2, :] * g + w_ref[2:3, :] * _shift_dn(g)


def _conv_wgrad(acc_ref, g, x, tr, first):
    gm = _mid(g, tr)

    @pl.when(first)
    def _():
        acc_ref[...] = jnp.zeros_like(acc_ref)

    acc_ref[0:1, :] += jnp.sum(gm * _mid(_shift_dn(x), tr), axis=0, keepdims=True)
    acc_ref[1:2, :] += jnp.sum(gm * _mid(x, tr), axis=0, keepdims=True)
    acc_ref[2:3, :] += jnp.sum(gm * _mid(_shift_up(x), tr), axis=0, keepdims=True)
    acc_ref[3:4, :] += jnp.sum(gm, axis=0, keepdims=True)


def _sigmoid(a):
    return 1.0 / (1.0 + jnp.exp(-a))


def _ffn_gate_fwd(name, up, cw, cb):
    T, F2 = up.shape
    F = F2 // 2
    tc, tr = _tile(F, 512, LANES), _tile(T, 512, HALO)
    nF, n_i = F // tc, T // tr

    def body(ac, ap, an, bc, bp, bn, wa, wb, ba, bb, o_ref):
        i = pl.program_id(1)
        ua = _mid(_conv(_ext(ac, ap, an, i, n_i), wa, ba), tr)
        ub = _mid(_conv(_ext(bc, bp, bn, i, n_i), wb, bb), tr)
        o_ref[...] = (ua * _sigmoid(ua) * ub).astype(o_ref.dtype)

    wspec = lambda o: pl.BlockSpec((3, tc), lambda j, i: (0, j + o))
    bspec = lambda o: pl.BlockSpec((1, tc), lambda j, i: (0, j + o))
    return pl.pallas_call(
        body, grid=(nF, n_i),
        in_specs=_halo_specs(tr, tc, T, lambda j, i: j) + _halo_specs(tr, tc, T, lambda j, i: j + nF)
        + [wspec(0), wspec(nF), bspec(0), bspec(nF)],
        out_specs=pl.BlockSpec((tr, tc), lambda j, i: (i, j)), out_shape=_sds((T, F), BF16),
        name=name, compiler_params=_cp("parallel", "parallel"))(up, up, up, up, up, up, cw, cw, cb, cb)


def _ffn_gate_bwd(name, up, dact, cw, cb):
    T, F2 = up.shape
    F = F2 // 2
    tc, tr = _tile(F, 512, LANES), _tile(T, 512, HALO)
    nF, n_i = F // tc, T // tr

    def body(ac, ap, an, bc, bp, bn, dc, dp, dn, wa, wb, ba, bb, o_ref, wga_ref, wgb_ref):
        i = pl.program_id(1)
        xa = _ext(ac, ap, an, i, n_i)
        xb = _ext(bc, bp, bn, i, n_i)
        da = _ext(dc, dp, dn, i, n_i)
        ua = _conv(xa, wa, ba)
        sig = _sigmoid(ua)
        ga = da * _conv(xb, wb, bb) * (sig * (1.0 + ua * (1.0 - sig)))
        o_ref[0] = _mid(_conv_t(ga, wa), tr).astype(o_ref.dtype)
        _conv_wgrad(wga_ref, ga, xa, tr, i == 0)
        gb = da * (ua * sig)
        o_ref[1] = _mid(_conv_t(gb, wb), tr).astype(o_ref.dtype)
        _conv_wgrad(wgb_ref, gb, xb, tr, i == 0)

    wspec = lambda o: pl.BlockSpec((3, tc), lambda j, i: (0, j + o))
    bspec = lambda o: pl.BlockSpec((1, tc), lambda j, i: (0, j + o))
    wg = pl.BlockSpec((8, tc), lambda j, i: (0, j))
    dup, wga, wgb = pl.pallas_call(
        body, grid=(nF, n_i),
        in_specs=_halo_specs(tr, tc, T, lambda j, i: j) + _halo_specs(tr, tc, T, lambda j, i: j + nF)
        + _halo_specs(tr, tc, T, lambda j, i: j) + [wspec(0), wspec(nF), bspec(0), bspec(nF)],
        out_specs=[pl.BlockSpec((2, tr, tc), lambda j, i: (0, i, j)), wg, wg],
        out_shape=[_sds((2, T, F), BF16), _sds((8, F), F32), _sds((8, F), F32)],
        name=name, compiler_params=_cp("parallel", "arbitrary"),
    )(up, up, up, up, up, up, dact, dact, dact, cw, cw, cb, cb)
    return dup, jnp.concatenate([wga, wgb], axis=1)


def _sc_gate_fwd(name, z, cw, cb):
    T, D3 = z.shape
    D = D3 // 3
    tc, tr = _tile(D, 512, LANES), _tile(T, 512, HALO)
    nD, n_i = D // tc, T // tr

    def body(uc, up_, un, gb, cc, cp, cn, w, b, o_ref):
        i = pl.program_id(1)
        cu = _ext(cc, cp, cn, i, n_i) * _ext(uc, up_, un, i, n_i)
        o_ref[...] = (gb[...].astype(F32) * _mid(_conv(cu, w, b), tr)).astype(o_ref.dtype)

    return pl.pallas_call(
        body, grid=(nD, n_i),
        in_specs=_halo_specs(tr, tc, T, lambda j, i: j) + [pl.BlockSpec((tr, tc), lambda j, i: (i, j + nD))]
        + _halo_specs(tr, tc, T, lambda j, i: j + 2 * nD)
        + [pl.BlockSpec((3, tc), lambda j, i: (0, j)), pl.BlockSpec((1, tc), lambda j, i: (0, j))],
        out_specs=pl.BlockSpec((tr, tc), lambda j, i: (i, j)), out_shape=_sds((T, D), BF16),
        name=name, compiler_params=_cp("parallel", "parallel"))(z, z, z, z, z, z, z, cw, cb)


def _sc_gate_bwd(name, z, dy, cw, cb):
    T, D3 = z.shape
    D = D3 // 3
    tc, tr = _tile(D, 512, LANES), _tile(T, 512, HALO)
    nD, n_i = D // tc, T // tr

    def body(uc, up_, un, bc, bp, bn, cc, cp, cn, yc, yp, yn, w, b, o_ref, wg_ref):
        i = pl.program_id(1)
        u = _ext(uc, up_, un, i, n_i)
        gc = _ext(cc, cp, cn, i, n_i)
        cu = gc * u
        g = _ext(yc, yp, yn, i, n_i) * _ext(bc, bp, bn, i, n_i)
        dcu = _mid(_conv_t(g, w), tr)
        o_ref[0] = (dcu * _mid(gc, tr)).astype(o_ref.dtype)
        o_ref[1] = (yc[...].astype(F32) * _mid(_conv(cu, w, b), tr)).astype(o_ref.dtype)
        o_ref[2] = (dcu * _mid(u, tr)).astype(o_ref.dtype)
        _conv_wgrad(wg_ref, g, cu, tr, i == 0)

    hs = lambda o: _halo_specs(tr, tc, T, lambda j, i: j + o)
    return pl.pallas_call(
        body, grid=(nD, n_i),
        in_specs=hs(0) + hs(nD) + hs(2 * nD) + hs(0)
        + [pl.BlockSpec((3, tc), lambda j, i: (0, j)), pl.BlockSpec((1, tc), lambda j, i: (0, j))],
        out_specs=[pl.BlockSpec((3, tr, tc), lambda j, i: (0, i, j)), pl.BlockSpec((8, tc), lambda j, i: (0, j))],
        out_shape=[_sds((3, T, D), BF16), _sds((8, D), F32)],
        name=name, compiler_params=_cp("parallel", "arbitrary"),
    )(z, z, z, z, z, z, z, z, z, dy, dy, dy, cw, cb)


def _slopes(n_heads):
    return jnp.asarray(2.0 ** (-ALIBI_MAX * np.arange(1, n_heads + 1) / n_heads), dtype=F32)


CHAINS = 8


def _nq(L, d):
    return max(1, min(CHAINS if d == 1 else CHAINS // 2, L // LANES // 2))


def _srows(ref, r, start, n, d):
    if d == 1:
        return ref[start:start + n, :]
    return ref[pl.ds(start * d + r, n, stride=d), :]


def _win(p_ref, c_ref, n_ref, r, b, nq):
    lo, hi, top = b * LANES - BAND, b * LANES + LANES + BAND, nq * LANES
    parts = [p_ref[r]] if lo < 0 else []
    parts.append(c_ref[r, max(lo, 0):min(hi, top), :])
    if hi > top:
        parts.append(n_ref[r])
    return parts[0] if len(parts) == 1 else jnp.concatenate(parts, axis=0)


def _nat_win(p_ref, c_ref, n_ref, r, b, nq, d):
    lo, hi, top = b * LANES - BAND, b * LANES + LANES + BAND, nq * LANES
    parts = [_srows(p_ref, r, 0, BAND, d)] if lo < 0 else []
    parts.append(_srows(c_ref, r, max(lo, 0), min(hi, top) - max(lo, 0), d))
    if hi > top:
        parts.append(_srows(n_ref, r, 0, BAND, d))
    return parts[0] if len(parts) == 1 else jnp.concatenate(parts, axis=0)


def _biases(slope, d, n, n_steps, nq, q_rows, k_rows, q0, k0):
    qi = lax.broadcasted_iota(jnp.int32, (q_rows, k_rows), 0) + q0
    kj = lax.broadcasted_iota(jnp.int32, (q_rows, k_rows), 1) + k0
    dist = jnp.abs(kj - qi)
    base = jnp.where(dist <= BAND, -slope * (dist * d).astype(F32), NEG_INF)
    out = []
    for b in range(nq):
        t = base
        if b == 0:
            t = jnp.where((n == 0) & ((kj < 0) | (qi < 0)), NEG_INF, t)
        if b == nq - 1:
            t = jnp.where((n == n_steps - 1) & ((kj >= LANES) | (qi >= LANES)), NEG_INF, t)
        out.append(t)
    return out


def _win_specs(d, H, col, nq, L):
    return [pl.BlockSpec((d, BAND, LANES), lambda h, n: (0, jnp.maximum(2 * nq * n - 1, 0), col * H + h)),
            pl.BlockSpec((d, nq * LANES, LANES), lambda h, n: (0, n, col * H + h)),
            pl.BlockSpec((d, BAND, LANES), lambda h, n: (0, jnp.minimum(2 * nq * (n + 1), L // BAND - 1), col * H + h))]


def _nat_specs(d, nq, L):
    return [pl.BlockSpec((BAND * d, LANES), lambda h, n: (jnp.maximum(2 * nq * n - 1, 0), h)),
            pl.BlockSpec((nq * LANES * d, LANES), lambda h, n: (n, h)),
            pl.BlockSpec((BAND * d, LANES), lambda h, n: (jnp.minimum(2 * nq * (n + 1), L // BAND - 1), h))]


def _over_residues(d, nq, per_r):
    if d == 1:
        per_r(0, 0)
    else:
        lax.fori_loop(0, d, per_r, 0, unroll=min(d, max(1, CHAINS // nq)))


def _attn_fwd(name, qkv, d, H):
    T = qkv.shape[0]
    D = H * HEAD_DIM
    L = T // d
    nq = _nq(L, d)
    n_steps = L // (nq * LANES)
    scale = HEAD_DIM ** -0.5
    q3 = qkv.reshape(d, L, 3 * D)

    def body(s_ref, q_ref, kp, kc, kn, vp, vc, vn, o_ref, l_ref):
        h, n = pl.program_id(0), pl.program_id(1)
        bias = _biases(s_ref[h], d, n, n_steps, nq, LANES, 2 * LANES, 0, -BAND)

        def per_r(r, carry):
            for b in range(nq):
                k, v = _win(kp, kc, kn, r, b, nq), _win(vp, vc, vn, r, b, nq)
                s = lax.dot_general(q_ref[r, b * LANES:(b + 1) * LANES, :], k, NT, preferred_element_type=F32) * scale + bias[b]
                m = jnp.max(s, axis=1, keepdims=True)
                p = jnp.exp(s - m)
                den = jnp.sum(p, axis=1, keepdims=True)
                o = lax.dot_general(p.astype(BF16), v, NN, preferred_element_type=F32) / den
                lse = jnp.broadcast_to(m + jnp.log(den), (LANES, LANES))
                if d == 1:
                    o_ref[b * LANES:(b + 1) * LANES, :] = o
                    l_ref[b * LANES:(b + 1) * LANES, :] = lse
                else:
                    o_ref[pl.ds(b * LANES * d + r, LANES, stride=d), :] = o
                    l_ref[pl.ds(b * LANES * d + r, LANES, stride=d), :] = lse
            return carry

        _over_residues(d, nq, per_r)

    out = pl.BlockSpec((nq * LANES * d, LANES), lambda h, n: (n, h))
    return pl.pallas_call(
        body, grid=(H, n_steps),
        in_specs=[pl.BlockSpec(memory_space=pltpu.SMEM), pl.BlockSpec((d, nq * LANES, LANES), lambda h, n: (0, n, h))]
        + _win_specs(d, H, 1, nq, L) + _win_specs(d, H, 2, nq, L),
        out_specs=[out, out], out_shape=[_sds((T, D), F32), _sds((T, D), F32)],
        name=name, compiler_params=_cp("parallel", "parallel"))(_slopes(H), q3, q3, q3, q3, q3, q3, q3)


def _attn_combine(name, outs, lses):
    T, D = outs[0].shape
    tr, tc = _tile(T, 512, 16), _tile(D, 512, LANES)

    def body(o0, o1, o2, l0, l1, l2, ob_ref, l_ref):
        a0, a1, a2 = l0[...], l1[...], l2[...]
        m = jnp.maximum(jnp.maximum(a0, a1), a2)
        e0, e1, e2 = jnp.exp(a0 - m), jnp.exp(a1 - m), jnp.exp(a2 - m)
        z = e0 + e1 + e2
        ob_ref[...] = ((e0 * o0[...] + e1 * o1[...] + e2 * o2[...]) / z).astype(BF16)
        l_ref[...] = m + jnp.log(z)

    blk = pl.BlockSpec((tr, tc), lambda i, j: (i, j))
    return pl.pallas_call(
        body, grid=(T // tr, D // tc), in_specs=[blk] * 6, out_specs=[blk] * 2,
        out_shape=[_sds((T, D), BF16), _sds((T, D), F32)],
        name=name, compiler_params=_cp("parallel", "parallel"))(*outs, *lses)


def _attn_stats(name, do, o, lse):
    T, D = do.shape
    tr = _tile(T, 1024, 16)

    def body(a, b, l, o_ref):
        delta = jnp.broadcast_to(jnp.sum(a[...] * b[...].astype(F32), axis=1, keepdims=True), o_ref.shape)
        lane = lax.broadcasted_iota(jnp.int32, o_ref.shape, 1)
        o_ref[...] = jnp.where(lane < BAND, l[...], delta)

    blk = pl.BlockSpec((tr, LANES), lambda i, j: (i, j))
    return pl.pallas_call(body, grid=(T // tr, D // LANES), in_specs=[blk, blk, blk], out_specs=blk,
                          out_shape=_sds((T, D), F32), name=name, compiler_params=_cp("parallel", "parallel"))(do, o, lse)


def _attn_bwd(name, qkv, do, stats, d, H):
    T = qkv.shape[0]
    D = H * HEAD_DIM
    L = T // d
    nq = _nq(L, d)
    n_steps = L // (nq * LANES)
    scale = HEAD_DIM ** -0.5
    q3 = qkv.reshape(d, L, 3 * D)
    mid = slice(BAND, BAND + LANES)

    def body(s_ref, qp, qc, qn, kp, kc, kn, vp, vc, vn, gp, gc, gn, tp, tc_, tn_, o_ref):
        h, n = pl.program_id(0), pl.program_id(1)
        bias_q = _biases(s_ref[h], d, n, n_steps, nq, LANES, 2 * LANES, 0, -BAND)
        bias_k = _biases(s_ref[h], d, n, n_steps, nq, 2 * LANES, LANES, -BAND, 0)

        def per_r(r, carry):
            for b in range(nq):
                rows = slice(b * LANES, (b + 1) * LANES)
                q_w, k_w, v_w = _win(qp, qc, qn, r, b, nq), _win(kp, kc, kn, r, b, nq), _win(vp, vc, vn, r, b, nq)
                g_w = _nat_win(gp, gc, gn, r, b, nq, d)
                t_w = _nat_win(tp, tc_, tn_, r, b, nq, d)
                g_b = g_w.astype(BF16)
                q_c, k_c, v_c, g_c, t_c = q_w[mid], k_w[mid], v_w[mid], g_b[mid], t_w[mid]
                s = lax.dot_general(q_c, k_w, NT, preferred_element_type=F32) * scale + bias_q[b]
                p = jnp.exp(s - t_c[:, 0:1])
                dp = lax.dot_general(g_c, v_w, NT, preferred_element_type=F32)
                ds = p * (dp - t_c[:, BAND:BAND + 1])
                o_ref[0, r, rows, :] = (lax.dot_general(ds.astype(BF16), k_w, NN, preferred_element_type=F32) * scale).astype(BF16)
                s2 = lax.dot_general(q_w, k_c, NT, preferred_element_type=F32) * scale + bias_k[b]
                p2 = jnp.exp(s2 - t_w[:, 0:1])
                o_ref[2, r, rows, :] = lax.dot_general(p2.astype(BF16), g_b, TN, preferred_element_type=F32).astype(BF16)
                dp2 = lax.dot_general(g_b, v_c, NT, preferred_element_type=F32)
                ds2 = p2 * (dp2 - t_w[:, BAND:BAND + 1])
                o_ref[1, r, rows, :] = (lax.dot_general(ds2.astype(BF16), q_w, TN, preferred_element_type=F32) * scale).astype(BF16)
            return carry

        _over_residues(d, nq, per_r)

    dqkv = pl.pallas_call(
        body, grid=(H, n_steps),
        in_specs=[pl.BlockSpec(memory_space=pltpu.SMEM)]
        + _win_specs(d, H, 0, nq, L) + _win_specs(d, H, 1, nq, L) + _win_specs(d, H, 2, nq, L)
        + _nat_specs(d, nq, L) + _nat_specs(d, nq, L),
        out_specs=pl.BlockSpec((3, d, nq * LANES, LANES), lambda h, n: (0, 0, n, h)),
        out_shape=_sds((3, d, L, D), BF16),
        name=name, compiler_params=_cp("parallel", "parallel"),
    )(_slopes(H), *([q3] * 9), *([do] * 3), *([stats] * 3))
    return dqkv.reshape(3, T, D)


def _to_group_order(a, d):
    if d == 1:
        return a
    T, C = a.shape
    return a.reshape(T // d, d, C).swapaxes(0, 1).reshape(T, C)


def _from_group_order(a, d):
    if d == 1:
        return a
    T, C = a.shape
    return a.reshape(d, T // d, C).swapaxes(0, 1).reshape(T, C)


def _fwd_bwd(x, tgt, S, ex):
    T, D = x.shape
    H = D // HEAD_DIM
    G3 = 3 * D

    def mm(fn, *args, rides=(), **kw):
        out, extra = fn(*args, comm=_join([getattr(ex, kind)(keys) for kind, keys in rides]), **kw)
        at = 0
        for kind, keys in rides:
            getattr(ex, kind + "_done")(keys, extra[at:at + len(keys)])
            at += len(keys)
        return out

    def ffn_fwd(l, xin, rides_up, rides_dn):
        hf = _rmsnorm_fwd(f"ffn_norm{l}", xin, S["ffn_g"][l])
        up = mm(_mm_nn, f"ffn_up{l}", hf, ex.w(f"up{l}"), None, BF16, rides=rides_up)
        act = _ffn_gate_fwd(f"ffn_gate{l}", up, S["ffn_cw"][l], S["ffn_cb"][l][None])
        return hf, up, act, mm(_mm_nn, f"ffn_down{l}", act, ex.w(f"dn{l}"), xin, F32, tk_pref=2816, rides=rides_dn)

    def ffn_bwd(l, xin, hf, up, act, dxo, dxo_b, rides):
        dact = mm(_mm_nt, f"ffn_down_dx{l}", dxo_b, ex.w(f"dn{l}"), BF16, tn_pref=1408, rides=rides[0])
        ex.grad(f"dn{l}", mm(_mm_tn, f"ffn_down_dw{l}", act, dxo_b, tkr_pref=1408))
        dup, cg = _ffn_gate_bwd(f"ffn_gate_bwd{l}", up, dact, S["ffn_cw"][l], S["ffn_cb"][l][None])
        dhf = mm(_mm_nt, f"ffn_up_dx{l}", dup, ex.w(f"up{l}"), BF16, tk_pref=2816, rides=rides[1])
        ex.grad(f"up{l}", mm(_mm_tn, f"ffn_up_dw{l}", hf, dup, tn_pref=1408, rides=rides[2]))
        dx, dx_b, dg = mm(_rmsnorm_bwd, f"ffn_norm_bwd{l}", xin, S["ffn_g"][l], [dhf], dxo)
        return dx, dx_b, dg, cg

    h0 = _rmsnorm_fwd("mix_norm0", x, S["mix_g"][0])
    z = mm(_mm_nn, "sc_in", h0, ex.w("in"), None, BF16, rides=[("ag", ["sco", "up0"])])
    y = _sc_gate_fwd("sc_gate", z, S["sc_cw"], S["sc_cb"][None])
    x1 = mm(_mm_nn, "sc_out", y, ex.w("sco"), x, F32, rides=[("ag", ["dn0"])])
    hf0, up0, act0, x2 = ffn_fwd(0, x1, [("ag", ["qkv"])], [("ag", ["ao"])])
    h1 = _rmsnorm_fwd("mix_norm1", x2, S["mix_g"][1])
    hd, qkv, outs, lses = [], [], [], []
    for g, d in enumerate(DILATIONS):
        hd.append(_to_group_order(h1, d))
        qkv.append(mm(_mm_nn, f"attn_qkv{g}", hd[g], ex.w("qkv"), None, BF16, col0=g * G3, n_cols=G3,
                      rides=[[("ag", ["dn1"])], [("ag", ["up1"])], ()][g]))
        o_g, l_g = _attn_fwd(f"attn_fwd{g}", qkv[g], d, H)
        outs.append(o_g)
        lses.append(l_g)
    o_b, lse = _attn_combine("attn_combine", outs, lses)
    x3 = mm(_mm_nn, "attn_out", o_b, ex.w("ao"), x2, F32)
    hf1, up1, act1, x4 = ffn_fwd(1, x3, (), ())
    loss, dx4, dx4_b, dg_fin = _final_loss("final_loss", x4, S["fin_g"], tgt)

    dx3, dx3_b, dg_f1, cg1 = ffn_bwd(1, x3, hf1, up1, act1, dx4, dx4_b,
                                     [(), [("pair", ["dn1"])], [("chip", ["dn1"])]])
    do = mm(_mm_nt, "attn_out_dx", dx3_b, ex.w("ao"), F32, rides=[("pair", ["up1"])])
    ex.grad("ao", mm(_mm_tn, "attn_out_dw", o_b, dx3_b))
    stats = _attn_stats("attn_stats", do, o_b, lse)
    dhs, dw_qkv = [], None
    qkv_rides = [[("chip", ["up1"]), ("pair", ["ao"])], [("chip", ["ao"])], ()]
    for g, d in enumerate(DILATIONS):
        dqkv = _attn_bwd(f"attn_bwd{g}", qkv[g], do, stats, d, H)
        dhs.append(_from_group_order(mm(_mm_nt, f"attn_qkv_dx{g}", dqkv, ex.w("qkv"), BF16, col0=g * G3,
                                        rides=qkv_rides[g]), d))
        dw_qkv = mm(_mm_tn, f"attn_qkv_dw{g}", hd[g], dqkv, n_total=len(DILATIONS) * G3, col0=g * G3, prev=dw_qkv)
    ex.grad("qkv", dw_qkv)
    dx2, dx2_b, dg_m1 = mm(_rmsnorm_bwd, "mix_norm_bwd1", x2, S["mix_g"][1], dhs, dx3)
    dx1, dx1_b, dg_f0, cg0 = ffn_bwd(0, x1, hf0, up0, act0, dx2, dx2_b,
                                     [[("pair", ["qkv"])], [("chip", ["qkv"]), ("pair", ["dn0"])], [("chip", ["dn0"])]])
    dy = mm(_mm_nt, "sc_out_dx", dx1_b, ex.w("sco"), BF16, rides=[("pair", ["up0"])])
    ex.grad("sco", mm(_mm_tn, "sc_out_dw", y, dx1_b))
    dz, cg_sc = _sc_gate_bwd("sc_gate_bwd", z, dy, S["sc_cw"], S["sc_cb"][None])
    ex.grad("in", mm(_mm_tn, "sc_in_dw", h0, dz, rides=[("chip", ["up0"]), ("pair", ["sco"])]))
    dh0 = mm(_mm_nt, "sc_in_dx", dz, ex.w("in"), BF16, rides=[("chip", ["sco"]), ("pair", ["in"])])
    dx0, _, dg_m0 = mm(_rmsnorm_bwd, "mix_norm_bwd0", x, S["mix_g"][0], [dh0], dx1, rides=[("chip", ["in"])])

    dS = {"mix_g": jnp.stack([dg_m0, dg_m1]), "ffn_g": jnp.stack([dg_f0, dg_f1]), "fin_g": dg_fin,
          "sc_cw": cg_sc[0:3], "sc_cb": cg_sc[3], "ffn_cw": jnp.stack([cg0[0:3], cg1[0:3]]),
          "ffn_cb": jnp.stack([cg0[3], cg1[3]])}
    return loss, dx0, dS


def _place():
    x, y, c = lax.axis_index("x"), lax.axis_index("y"), lax.axis_index("c")
    return x, y, c, [(1 - x, y), (x, 1 - y), (1 - x, 1 - y)]


def _block(ref, s, shape, axis):
    R, C = shape
    if axis == 0:
        return ref.at[pl.ds(pl.multiple_of(s * R, HALO), R), :]
    return ref.at[:, pl.ds(pl.multiple_of(s * C, LANES), C)]


def _whole(shape, axis):
    return (shape[0] * N_DEV, shape[1]) if axis == 0 else (shape[0], shape[1] * N_DEV)


def _ag_stage(arrs, axes):
    n = len(arrs)

    def plan(ins, outs, sems):
        send_sems, recv_sems, local_sems = sems
        x, y, c, chips = _place()
        me, sibling = 4 * x + 2 * y + c, (x, y, 1 - c)

        def copy(a, k, blk, to, src=None):
            dst = _block(outs[a], blk, arrs[a].shape, axes[a])
            return pltpu.make_async_remote_copy(src_ref=dst if src is None else src, dst_ref=dst,
                                                send_sem=send_sems.at[a, k], recv_sem=recv_sems.at[a, k],
                                                device_id=to, device_id_type=MESH)

        mine = [pltpu.make_async_copy(ins[a], _block(outs[a], me, arrs[a].shape, axes[a]), local_sems.at[a])
                for a in range(n)]
        first = []
        for a in range(n):
            first.append(copy(a, 0, me, sibling, src=ins[a]))
            first += [copy(a, 1 + j, me, (*chip, c), src=ins[a]) for j, chip in enumerate(chips)]
        return x, y, c, chips, sibling, copy, mine, first

    def start(ins, outs, sems):
        *_, mine, first = plan(ins, outs, sems)
        for cp in mine + first:
            cp.start()

    def finish(ins, outs, sems):
        x, y, c, chips, sibling, copy, mine, first = plan(ins, outs, sems)
        passed = []
        for j, (px, py) in enumerate(chips):
            for a in range(n):
                blk = 4 * px + 2 * py + c
                copy(a, 1 + j, blk, sibling).wait_recv()
                passed.append(copy(a, 4 + j, blk, sibling))
                passed[-1].start()
        for a in range(n):
            copy(a, 0, 4 * x + 2 * y + 1 - c, sibling).wait_recv()
            for j, (px, py) in enumerate(chips):
                copy(a, 4 + j, 4 * px + 2 * py + 1 - c, sibling).wait_recv()
        for cp in first + passed:
            cp.wait_send()
        for cp in mine:
            cp.wait()

    return _Stage(list(arrs), [_sds(_whole(a.shape, ax), a.dtype) for a, ax in zip(arrs, axes)],
                  [pltpu.SemaphoreType.DMA((n, 7)), pltpu.SemaphoreType.DMA((n, 7)), pltpu.SemaphoreType.DMA((n,))],
                  start, finish)


def _pair_stage(dws, shapes, axes):
    n = len(dws)

    def copies(ins, outs, sems):
        send_sems, recv_sems = sems
        x, y, c, _ = _place()
        return [pltpu.make_async_remote_copy(src_ref=_block(ins[a], 2 * k + 1 - c, shapes[a], axes[a]),
                                             dst_ref=outs[a].at[k],
                                             send_sem=send_sems.at[a, k], recv_sem=recv_sems.at[a, k],
                                             device_id=(x, y, 1 - c), device_id_type=MESH)
                for a in range(n) for k in range(4)]

    def start(ins, outs, sems):
        for cp in copies(ins, outs, sems):
            cp.start()

    def finish(ins, outs, sems):
        for cp in copies(ins, outs, sems):
            cp.wait()

    return _Stage(list(dws), [_sds((4,) + tuple(s), a.dtype) for a, s in zip(dws, shapes)],
                  [pltpu.SemaphoreType.DMA((n, 4)), pltpu.SemaphoreType.DMA((n, 4))], start, finish)


def _rs_pair_add(name, dw, got, c_arr, axis):
    _, R, C = got.shape
    tr, tc = _tile(R, 512, 16), _tile(C, 1536, LANES)
    per = R // tr if axis == 0 else C // tc

    def body(c_ref, a_ref, b_ref, o_ref):
        o_ref[...] = (a_ref[...].astype(F32) + b_ref[...].astype(F32)).astype(o_ref.dtype)

    if axis == 0:
        mine = pl.BlockSpec((tr, tc), lambda k, i, j, c_ref: ((2 * k + c_ref[0]) * per + i, j))
    else:
        mine = pl.BlockSpec((tr, tc), lambda k, i, j, c_ref: (i, (2 * k + c_ref[0]) * per + j))
    return pl.pallas_call(
        body,
        grid_spec=pltpu.PrefetchScalarGridSpec(
            num_scalar_prefetch=1, grid=(4, R // tr, C // tc),
            in_specs=[mine, pl.BlockSpec((None, tr, tc), lambda k, i, j, c_ref: (k, i, j))],
            out_specs=pl.BlockSpec((None, tr, tc), lambda k, i, j, c_ref: (k, i, j))),
        out_shape=_sds((4, R, C), BF16), name=name,
        compiler_params=_cp("parallel", "parallel", "parallel"))(c_arr, dw, got)


def _chip_stage(parts):
    n = len(parts)

    def plan(ins, outs, sems):
        send_sems, recv_sems, local_sems = sems
        x, y, c, chips = _place()
        my_chip = 2 * x + y
        mine = [pltpu.make_async_copy(ins[a].at[my_chip], outs[a].at[my_chip], local_sems.at[a]) for a in range(n)]
        sends = [pltpu.make_async_remote_copy(src_ref=ins[a].at[2 * px + py], dst_ref=outs[a].at[my_chip],
                                              send_sem=send_sems.at[a, j], recv_sem=recv_sems.at[a, j],
                                              device_id=(px, py, c), device_id_type=MESH)
                 for a in range(n) for j, (px, py) in enumerate(chips)]
        arrivals = lambda: [pltpu.make_async_remote_copy(src_ref=ins[a].at[my_chip], dst_ref=outs[a].at[2 * px + py],
                                                         send_sem=send_sems.at[a, j], recv_sem=recv_sems.at[a, j],
                                                         device_id=(px, py, c), device_id_type=MESH)
                            for a in range(n) for j, (px, py) in enumerate(chips)]
        return mine, sends, arrivals

    def start(ins, outs, sems):
        mine, sends, _ = plan(ins, outs, sems)
        for cp in mine + sends:
            cp.start()

    def finish(ins, outs, sems):
        mine, sends, arrivals = plan(ins, outs, sems)
        for cp in arrivals():
            cp.wait_recv()
        for cp in sends:
            cp.wait_send()
        for cp in mine:
            cp.wait()

    return _Stage(list(parts), [_sds(a.shape, a.dtype) for a in parts],
                  [pltpu.SemaphoreType.DMA((n, 3)), pltpu.SemaphoreType.DMA((n, 3)), pltpu.SemaphoreType.DMA((n,))],
                  start, finish)


class _Exchange:
    ROW_SHARDED = ("sco", "ao", "dn0", "dn1")

    def __init__(self, shards, c_arr):
        self.sh, self.c_arr = shards, c_arr
        self.W, self.dw, self.parts, self.sums = {}, {}, {}, {}

    def axis(self, key):
        return 0 if key in self.ROW_SHARDED else 1

    def w(self, key):
        return self.W[key]

    def ag(self, keys):
        return _ag_stage([self.sh[k] for k in keys], [self.axis(k) for k in keys])

    def ag_done(self, keys, outs):
        self.W.update(zip(keys, outs))

    def grad(self, key, dw):
        self.dw[key] = dw

    def pair(self, keys):
        return _pair_stage([self.dw[k] for k in keys], [self.sh[k].shape for k in keys], [self.axis(k) for k in keys])

    def pair_done(self, keys, outs):
        for k, got in zip(keys, outs):
            self.parts[k] = _rs_pair_add(f"rs_add_{k}", self.dw[k], got, self.c_arr, self.axis(k))

    def chip(self, keys):
        return _chip_stage([self.parts[k] for k in keys])

    def chip_done(self, keys, outs):
        self.sums.update(zip(keys, outs))


def _sum_slots(name, a):
    _, rows, _ = a.shape

    def body(a_ref, o_ref):
        s = a_ref[0]
        for k in range(1, N_DEV):
            s = s + a_ref[k]
        o_ref[...] = s

    return pl.pallas_call(body, out_shape=_sds((rows, LANES), F32), name=name)(a)


def _cast_bf16(name, w3, l):
    _, R, C = w3.shape
    tr, tc = _tile(R, 512, 16), _tile(C, 1536, LANES)

    def body(w_ref, o_ref):
        o_ref[...] = w_ref[...].astype(BF16)

    return pl.pallas_call(
        body, grid=(R // tr, C // tc), in_specs=[pl.BlockSpec((None, tr, tc), lambda i, j: (l, i, j))],
        out_specs=pl.BlockSpec((tr, tc), lambda i, j: (i, j)), out_shape=_sds((R, C), BF16),
        name=name, compiler_params=_cp("parallel", "parallel"))(w3)


def _adamw(name, g_slots, w3, m3, v3, l, prev):
    n_slots, R, C = g_slots.shape
    tr, tc = _tile(R, 256, 8), _tile(C, 1536, LANES)
    c1, c2 = 1.0 - ADAM_B1 ** ADAM_STEP, 1.0 - ADAM_B2 ** ADAM_STEP

    def body(g_ref, w_ref, m_ref, v_ref, *rest):
        og, od, om, ov = rest[-4:]
        g = g_ref[0].astype(F32)
        for k in range(1, n_slots):
            g = g + g_ref[k].astype(F32)
        m = ADAM_B1 * m_ref[...] + (1.0 - ADAM_B1) * g
        v = ADAM_B2 * v_ref[...] + (1.0 - ADAM_B2) * (g * g)
        og[...] = g
        om[...] = m
        ov[...] = v
        od[...] = -ADAM_LR * ((m / c1) / (jnp.sqrt(v / c2) + ADAM_EPS) + ADAM_WD * w_ref[...])

    lay = pl.BlockSpec((None, tr, tc), lambda i, j: (l, i, j))
    ops = [g_slots, w3, m3, v3]
    specs = [pl.BlockSpec((n_slots, tr, tc), lambda i, j: (0, i, j)), lay, lay, lay]
    aliases = {}
    if prev is not None:
        ops += list(prev)
        specs += _any_specs(4)
        aliases = {4 + k: k for k in range(4)}
    return pl.pallas_call(
        body, grid=(R // tr, C // tc), in_specs=specs, out_specs=[lay] * 4,
        out_shape=[_sds(w3.shape, F32)] * 4, input_output_aliases=aliases,
        name=name, compiler_params=_cp("parallel", "parallel"))(*ops)


def _pack(parts):
    flat = jnp.concatenate([p.reshape(-1) for p in parts])
    pad = (-flat.shape[0]) % (HALO * LANES)
    return jnp.pad(flat, (0, pad)).reshape(-1, LANES)


def _unpack(packed, shapes):
    flat = packed.reshape(-1)
    out, at = [], 0
    for s in shapes:
        n = int(np.prod(s))
        out.append(flat[at:at + n].reshape(s))
        at += n
    return out


def kernel(x, mix_norm_g, ffn_norm_g, final_norm_g, sc_w_in, sc_conv_w, sc_conv_b, sc_w_out, attn_w_qkv, attn_w_out, ffn_w_up, ffn_conv_w, ffn_conv_b, ffn_w_down, loss_target, m_mix_norm_g, m_ffn_norm_g, m_final_norm_g, m_sc_w_in, m_sc_conv_w, m_sc_conv_b, m_sc_w_out, m_attn_w_qkv, m_attn_w_out, m_ffn_w_up, m_ffn_conv_w, m_ffn_conv_b, m_ffn_w_down, v_mix_norm_g, v_ffn_norm_g, v_final_norm_g, v_sc_w_in, v_sc_conv_w, v_sc_conv_b, v_sc_w_out, v_attn_w_qkv, v_attn_w_out, v_ffn_w_up, v_ffn_conv_w, v_ffn_conv_b, v_ffn_w_down):
    n_layers = ffn_w_up.shape[0]
    me = 4 * lax.axis_index("x") + 2 * lax.axis_index("y") + lax.axis_index("c")
    c_arr = lax.axis_index("c").astype(jnp.int32).reshape(1)

    big = [("in", sc_w_in, 0), ("sco", sc_w_out, 0), ("qkv", attn_w_qkv, 0), ("ao", attn_w_out, 0)]
    big += [(f"up{l}", ffn_w_up, l) for l in range(n_layers)] + [(f"dn{l}", ffn_w_down, l) for l in range(n_layers)]
    ex = _Exchange({nm: _cast_bf16(f"cast_{nm}", w, l) for nm, w, l in big}, c_arr)

    cw_mine = [sc_conv_w.reshape(-1, sc_conv_w.shape[-1]), ffn_conv_w.reshape(-1, ffn_conv_w.shape[-1])]
    w_in_all, sc_cw, ffn_cw = _run_stage("gather_first", _join([ex.ag(["in"]), _ag_stage(cw_mine, [1, 1])]))
    ex.ag_done(["in"], [w_in_all])
    ffn_cw = ffn_cw.reshape(n_layers, -1, ffn_cw.shape[-1])
    S = {"mix_g": mix_norm_g, "ffn_g": ffn_norm_g, "fin_g": final_norm_g, "sc_cw": sc_cw, "sc_cb": sc_conv_b[0],
         "ffn_cw": ffn_cw, "ffn_cb": ffn_conv_b}

    loss_part, grad_x, dS = _fwd_bwd(x[0], loss_target[0], S, ex)

    small_names = ["mix_g", "ffn_g", "fin_g", "sc_cb", "ffn_cb", "sc_cw", "ffn_cw"]
    small_parts = [dS[k] for k in small_names] + [loss_part.reshape(1)]
    small_mine = _pack(small_parts)
    small_all, = _run_stage("gather_small", _ag_stage([small_mine], [0]))
    small_sum = _sum_slots("sum_small", small_all.reshape((N_DEV,) + small_mine.shape))
    g_mix, g_ffn, g_fin, g_scb, g_fcb, g_scw, g_fcw, loss = _unpack(small_sum, [p.shape for p in small_parts])
    g_scw = lax.dynamic_slice_in_dim(g_scw, me * sc_conv_w.shape[-1], sc_conv_w.shape[-1], axis=-1)[None]
    g_fcw = lax.dynamic_slice_in_dim(g_fcw, me * ffn_conv_w.shape[-1], ffn_conv_w.shape[-1], axis=-1)
    g_scb = g_scb[None]
    small_g = [g_mix, g_ffn, g_fin, g_scw, g_scb, g_fcw, g_fcb]
    small_w = [mix_norm_g, ffn_norm_g, final_norm_g, sc_conv_w, sc_conv_b, ffn_conv_w, ffn_conv_b]
    small_m = [m_mix_norm_g, m_ffn_norm_g, m_final_norm_g, m_sc_conv_w, m_sc_conv_b, m_ffn_conv_w, m_ffn_conv_b]
    small_v = [v_mix_norm_g, v_ffn_norm_g, v_final_norm_g, v_sc_conv_w, v_sc_conv_b, v_ffn_conv_w, v_ffn_conv_b]
    small_out = _adamw("adamw_small", _pack(small_g)[None], _pack(small_w)[None], _pack(small_m)[None],
                       _pack(small_v)[None], 0, None)
    small_shapes = [w.shape for w in small_w]
    sg, sd, sm, sv = [_unpack(o[0], small_shapes) for o in small_out]

    moments = {"in": (m_sc_w_in, v_sc_w_in), "sco": (m_sc_w_out, v_sc_w_out), "qkv": (m_attn_w_qkv, v_attn_w_qkv),
               "ao": (m_attn_w_out, v_attn_w_out), "up": (m_ffn_w_up, v_ffn_w_up), "dn": (m_ffn_w_down, v_ffn_w_down)}
    upd = {}
    for nm, w, l in big:
        key = nm.rstrip("0123456789")
        upd[key] = _adamw(f"adamw_{nm}", ex.sums[nm], w, moments[key][0], moments[key][1], l, upd.get(key))

    def leaves(k):
        return [sg, sd, sm, sv][k][0:3] + [upd["in"][k], [sg, sd, sm, sv][k][3], [sg, sd, sm, sv][k][4], upd["sco"][k],
                                          upd["qkv"][k], upd["ao"][k], upd["up"][k], [sg, sd, sm, sv][k][5],
                                          [sg, sd, sm, sv][k][6], upd["dn"][k]]

    return (loss.reshape(()), grad_x[None], *leaves(0), *leaves(1), *leaves(2), *leaves(3))
```

```python
import math

import numpy as np
import jax
import jax.numpy as jnp
from jax import lax
from jax.experimental import pallas as pl
from jax.experimental.pallas import tpu as pltpu

F32 = jnp.float32
BF16 = jnp.bfloat16
MESH = pl.DeviceIdType.MESH

HEAD_DIM = 128
DILATED_GROUPS = ((128, 1), (512, 4), (2048, 16))
DILATIONS = tuple(d for _, d in DILATED_GROUPS)
BAND = (DILATED_GROUPS[0][0] // 2) // DILATED_GROUPS[0][1]
assert all((w // 2) // d == BAND for w, d in DILATED_GROUPS)
NORM_EPS = 1e-5
ALIBI_MAX = 8.0
NEG_INF = -1e30
ADAM_LR, ADAM_B1, ADAM_B2, ADAM_EPS, ADAM_WD, ADAM_STEP = 0.001, 0.9, 0.999, 1e-08, 0.01, 10

N_DEV = 8
LANES = 128
HALO = 16
VMEM_LIMIT = 56 * 1024 * 1024


def _cp(*sem):
    return pltpu.CompilerParams(dimension_semantics=sem, vmem_limit_bytes=VMEM_LIMIT)


def _tile(n, pref, mult):
    t = (min(n, pref) // mult) * mult
    while t >= mult:
        if n % t == 0:
            return t
        t -= mult
    return n


def _sds(shape, dtype):
    return jax.ShapeDtypeStruct(shape, dtype)


def _any_specs(n):
    return [pl.BlockSpec(memory_space=pl.ANY)] * n


class _Stage:
    def __init__(self, arrays, out_shapes, sems, start, finish):
        self.arrays, self.out_shapes, self.sems, self.start, self.finish = arrays, out_shapes, sems, start, finish


def _join(stages):
    stages = [s for s in stages if s is not None]
    if not stages:
        return None

    def split(refs, count):
        out, at = [], 0
        for s in stages:
            out.append(refs[at:at + count(s)])
            at += count(s)
        return out

    def each(which):
        def run(ins, outs, sems):
            parts = zip(split(ins, lambda s: len(s.arrays)), split(outs, lambda s: len(s.out_shapes)),
                        split(sems, lambda s: len(s.sems)))
            for s, (i, o, m) in zip(stages, parts):
                getattr(s, which)(i, o, m)
        return run

    return _Stage(sum([s.arrays for s in stages], []), sum([s.out_shapes for s in stages], []),
                  sum([s.sems for s in stages], []), each("start"), each("finish"))


def _run_stage(name, st):
    n, m = len(st.arrays), len(st.out_shapes)

    def body(*refs):
        ins, outs, sems = refs[:n], refs[n:n + m], refs[n + m:]
        st.start(ins, outs, sems)
        st.finish(ins, outs, sems)

    return pl.pallas_call(body, in_specs=_any_specs(n), out_specs=_any_specs(m), out_shape=st.out_shapes,
                          scratch_shapes=st.sems, name=name)(*st.arrays)


NN = (((1,), (0,)), ((), ()))
NT = (((1,), (1,)), ((), ()))
TN = (((0,), (0,)), ((), ()))


def _mm(name, operands, in_specs, out_sds, o_spec, grid, dims, acc_shape, has_res=False, aliases=None, comm=None):
    nk = grid[2]
    n_in = len(operands)
    n_ci, n_co = (len(comm.arrays), len(comm.out_shapes)) if comm else (0, 0)

    def body(*refs):
        a_ref, b_ref = refs[0], refs[1]
        r_ref = refs[2] if has_res else None
        o_ref = refs[n_in + n_ci]
        acc = refs[n_in + n_ci + 1 + n_co]
        c_refs = (refs[n_in:n_in + n_ci], refs[n_in + n_ci + 1:n_in + n_ci + 1 + n_co], refs[n_in + n_ci + 2 + n_co:])
        ids = [pl.program_id(q) for q in range(3)]
        if comm:
            @pl.when((ids[0] == 0) & (ids[1] == 0) & (ids[2] == 0))
            def _():
                comm.start(*c_refs)

        def finish(total):
            if has_res:
                total = total + r_ref[...]
            o_ref[...] = total.astype(o_ref.dtype)

        if nk == 1:
            finish(lax.dot_general(a_ref[...], b_ref[...], dims, preferred_element_type=F32))
        else:
            k = ids[2]

            @pl.when(k == 0)
            def _():
                acc[...] = jnp.zeros_like(acc)

            acc[...] += lax.dot_general(a_ref[...], b_ref[...], dims, preferred_element_type=F32)

            @pl.when(k == nk - 1)
            def _():
                finish(acc[...])

        if comm:
            @pl.when((ids[0] == grid[0] - 1) & (ids[1] == grid[1] - 1) & (ids[2] == nk - 1))
            def _():
                comm.finish(*c_refs)

    scratch = [pltpu.VMEM(acc_shape if nk > 1 else (8, LANES), F32)]
    if not comm:
        out = pl.pallas_call(
            body, grid=grid, in_specs=in_specs, out_specs=o_spec, out_shape=out_sds, scratch_shapes=scratch,
            input_output_aliases=aliases or {}, name=name,
            compiler_params=_cp("parallel", "parallel", "arbitrary"))(*operands)
        return out, []
    outs = pl.pallas_call(
        body, grid=grid, in_specs=list(in_specs) + _any_specs(n_ci), out_specs=[o_spec] + _any_specs(n_co),
        out_shape=[out_sds] + comm.out_shapes, scratch_shapes=scratch + comm.sems,
        input_output_aliases=aliases or {}, name=name,
        compiler_params=_cp("arbitrary", "arbitrary", "arbitrary"))(*operands, *comm.arrays)
    return outs[0], list(outs[1:])


def _stack(a):
    return a if a.ndim == 3 else a[None]


def _mm_nn(name, a, w, res, out_dtype, col0=0, n_cols=None, tk_pref=2048, comm=None):
    M, K = a.shape
    N = n_cols or w.shape[1]
    tm, tn, tk = _tile(M, 1024, 16), _tile(N, 1024, LANES), _tile(K, tk_pref, LANES)
    c0 = col0 // tn
    ops = [a, w]
    specs = [pl.BlockSpec((tm, tk), lambda i, j, k: (i, k)), pl.BlockSpec((tk, tn), lambda i, j, k: (k, c0 + j))]
    if res is not None:
        ops.append(res)
        specs.append(pl.BlockSpec((tm, tn), lambda i, j, k: (i, j)))
    return _mm(name, tuple(ops), specs, _sds((M, N), out_dtype), pl.BlockSpec((tm, tn), lambda i, j, k: (i, j)),
               (M // tm, N // tn, K // tk), NN, (tm, tn), has_res=res is not None, comm=comm)


def _mm_nt(name, dy, w, out_dtype, col0=0, tn_pref=1024, tk_pref=2048, comm=None):
    dy = _stack(dy)
    _, M, Np = dy.shape
    Kw = w.shape[0]
    tm, tn, tk = _tile(M, 1024, 16), _tile(Kw, tn_pref, LANES), _tile(Np, tk_pref, LANES)
    per, c0 = Np // tk, col0 // tk
    return _mm(name, (dy, w),
               [pl.BlockSpec((None, tm, tk), lambda i, j, k: (k // per, i, k % per)),
                pl.BlockSpec((tn, tk), lambda i, j, k: (j, c0 + k))],
               _sds((M, Kw), out_dtype), pl.BlockSpec((tm, tn), lambda i, j, k: (i, j)),
               (M // tm, Kw // tn, dy.shape[0] * per), NT, (tm, tn), comm=comm)


def _mm_tn(name, a, dy, n_total=None, col0=0, tkr_pref=1024, tn_pref=1024, tk_pref=2048, prev=None, comm=None):
    dy = _stack(dy)
    P, M, Np = dy.shape
    Kw = a.shape[1]
    tkr, tn, tk = _tile(Kw, tkr_pref, LANES), _tile(Np, tn_pref, LANES), _tile(M, tk_pref, 16)
    per, c0 = Np // tn, col0 // tn
    ops = [a, dy]
    specs = [pl.BlockSpec((tk, tkr), lambda i, j, k: (k, i)),
             pl.BlockSpec((None, tk, tn), lambda i, j, k: (j // per, k, j % per))]
    aliases = None
    if prev is not None:
        ops.append(prev)
        specs.append(pl.BlockSpec(memory_space=pl.ANY))
        aliases = {2: 0}
    return _mm(name, tuple(ops), specs, _sds((Kw, n_total or P * Np), BF16),
               pl.BlockSpec((tkr, tn), lambda i, j, k: (i, c0 + j)),
               (Kw // tkr, P * per, M // tk), TN, (tkr, tn), aliases=aliases, comm=comm)


def _rmsnorm_fwd(name, x, g, comm=None):
    T, D = x.shape
    tr = _tile(T, 512, 16)
    n_ci, n_co = (len(comm.arrays), len(comm.out_shapes)) if comm else (0, 0)

    def body(*refs):
        x_ref, g_ref, h_ref = refs[0], refs[1], refs[2 + n_ci]
        c_refs = (refs[2:2 + n_ci], refs[3 + n_ci:3 + n_ci + n_co], refs[3 + n_ci + n_co:])
        if comm:
            @pl.when(pl.program_id(0) == 0)
            def _():
                comm.start(*c_refs)

        xf = x_ref[...]
        r = lax.rsqrt(jnp.mean(xf * xf, axis=-1, keepdims=True) + NORM_EPS)
        h_ref[...] = (xf * r * g_ref[...]).astype(h_ref.dtype)

        if comm:
            @pl.when(pl.program_id(0) == T // tr - 1)
            def _():
                comm.finish(*c_refs)

    outs = pl.pallas_call(
        body, grid=(T // tr,),
        in_specs=[pl.BlockSpec((tr, D), lambda i: (i, 0)), pl.BlockSpec((1, D), lambda i: (0, 0))] + _any_specs(n_ci),
        out_specs=[pl.BlockSpec((tr, D), lambda i: (i, 0))] + _any_specs(n_co),
        out_shape=[_sds((T, D), BF16)] + (comm.out_shapes if comm else []),
        scratch_shapes=comm.sems if comm else [],
        name=name, compiler_params=_cp("arbitrary" if comm else "parallel"))(x, g.reshape(1, D), *(comm.arrays if comm else []))
    return outs[0], list(outs[1:])


def _rmsnorm_bwd(name, x, g, dhs, dres, comm=None):
    T, D = x.shape
    tr = _tile(T, 256, 16)
    n_dh = len(dhs)
    n_in = 3 + n_dh
    n_ci, n_co = (len(comm.arrays), len(comm.out_shapes)) if comm else (0, 0)

    def body(*refs):
        x_ref, g_ref = refs[0], refs[1]
        dh_refs = refs[2:2 + n_dh]
        dres_ref = refs[2 + n_dh]
        dx_ref, dxb_ref, dg_ref = refs[n_in + n_ci:n_in + n_ci + 3]
        c_refs = (refs[n_in:n_in + n_ci], refs[n_in + n_ci + 3:n_in + n_ci + 3 + n_co], refs[n_in + n_ci + 3 + n_co:])
        if comm:
            @pl.when(pl.program_id(0) == 0)
            def _():
                comm.start(*c_refs)

        xf = x_ref[...]
        r = lax.rsqrt(jnp.mean(xf * xf, axis=-1, keepdims=True) + NORM_EPS)
        xhat = xf * r
        dh = dh_refs[0][...].astype(F32)
        for q in dh_refs[1:]:
            dh = dh + q[...].astype(F32)
        dy = dh * g_ref[...]
        c = jnp.mean(dy * xhat, axis=-1, keepdims=True)
        dx = dres_ref[...] + r * (dy - xhat * c)
        dx_ref[...] = dx
        dxb_ref[...] = dx.astype(BF16)

        @pl.when(pl.program_id(0) == 0)
        def _():
            dg_ref[...] = jnp.zeros_like(dg_ref)

        dg_ref[...] += jnp.sum(dh * xhat, axis=0, keepdims=True)

        if comm:
            @pl.when(pl.program_id(0) == T // tr - 1)
            def _():
                comm.finish(*c_refs)

    row = pl.BlockSpec((tr, D), lambda i: (i, 0))
    vec = pl.BlockSpec((1, D), lambda i: (0, 0))
    outs = pl.pallas_call(
        body, grid=(T // tr,), in_specs=[row, vec] + [row] * n_dh + [row] + _any_specs(n_ci),
        out_specs=[row, row, vec] + _any_specs(n_co),
        out_shape=[_sds((T, D), F32), _sds((T, D), BF16), _sds((1, D), F32)] + (comm.out_shapes if comm else []),
        scratch_shapes=comm.sems if comm else [],
        name=name, compiler_params=_cp("arbitrary"))(x, g.reshape(1, D), *dhs, dres, *(comm.arrays if comm else []))
    return (outs[0], outs[1], outs[2][0]), list(outs[3:])


def _final_loss(name, x, g, tgt):
    T, D = x.shape
    tr = _tile(T, 256, 16)

    def body(x_ref, g_ref, t_ref, dx_ref, dxb_ref, dg_ref, loss_ref):
        xf = x_ref[...]
        r = lax.rsqrt(jnp.mean(xf * xf, axis=-1, keepdims=True) + NORM_EPS)
        xhat = xf * r
        err = xhat * g_ref[...] - t_ref[...]
        dy = err * (1.0 / D)
        dxh = dy * g_ref[...]
        c = jnp.mean(dxh * xhat, axis=-1, keepdims=True)
        dx = r * (dxh - xhat * c)
        dx_ref[...] = dx
        dxb_ref[...] = dx.astype(BF16)

        @pl.when(pl.program_id(0) == 0)
        def _():
            dg_ref[...] = jnp.zeros_like(dg_ref)
            loss_ref[...] = jnp.zeros_like(loss_ref)

        dg_ref[...] += jnp.sum(dy * xhat, axis=0, keepdims=True)
        loss_ref[...] += 0.5 * jnp.sum(jnp.mean(err * err, axis=-1, keepdims=True), axis=0, keepdims=True)

    row = pl.BlockSpec((tr, D), lambda i: (i, 0))
    vec = pl.BlockSpec((1, D), lambda i: (0, 0))
    dx, dx_b, dg, loss = pl.pallas_call(
        body, grid=(T // tr,), in_specs=[row, vec, row],
        out_specs=[row, row, vec, pl.BlockSpec((1, 1), lambda i: (0, 0))],
        out_shape=[_sds((T, D), F32), _sds((T, D), BF16), _sds((1, D), F32), _sds((1, 1), F32)],
        name=name, compiler_params=_cp("arbitrary"))(x, g.reshape(1, D), tgt)
    return loss[0, 0], dx, dx_b, dg[0]


def _halo_specs(tr, tc, n_rows, col):
    rb = tr // HALO
    last = n_rows // HALO - 1
    return [pl.BlockSpec((tr, tc), lambda *g: (g[-1], col(*g))),
            pl.BlockSpec((HALO, tc), lambda *g: (jnp.maximum(g[-1] * rb - 1, 0), col(*g))),
            pl.BlockSpec((HALO, tc), lambda *g: (jnp.minimum((g[-1] + 1) * rb, last), col(*g)))]


def _ext(cur_ref, prev_ref, next_ref, i, n_i):
    p = prev_ref[...].astype(F32) * (i > 0).astype(F32)
    n = next_ref[...].astype(F32) * (i < n_i - 1).astype(F32)
    return jnp.concatenate([p, cur_ref[...].astype(F32), n], axis=0)


def _shift_dn(x):
    return pltpu.roll(x, 1, axis=0)


def _shift_up(x):
    return pltpu.roll(x, x.shape[0] - 1, axis=0)


def _conv(x, w_ref, b_ref):
    return w_ref[0:1, :] * _shift_dn(x) + w_ref[1:2, :] * x + w_ref[2:3, :] * _shift_up(x) + b_ref[...]


def _mid(x, tr):
    return x[HALO:HALO + tr, :]


def _conv_t(g, w_ref):
    return w_ref[0:1, :] * _shift_up(g) + w_ref[1:2, :] * g + w_ref[2:3, :] * _shift_dn(g)


def _conv_wgrad(acc_ref, g, x, tr, first):
    gm = _mid(g, tr)

    @pl.when(first)
    def _():
        acc_ref[...] = jnp.zeros_like(acc_ref)

    acc_ref[0:1, :] += jnp.sum(gm * _mid(_shift_dn(x), tr), axis=0, keepdims=True)
    acc_ref[1:2, :] += jnp.sum(gm * _mid(x, tr), axis=0, keepdims=True)
    acc_ref[2:3, :] += jnp.sum(gm * _mid(_shift_up(x), tr), axis=0, keepdims=True)
    acc_ref[3:4, :] += jnp.sum(gm, axis=0, keepdims=True)


def _sigmoid(a):
    return 1.0 / (1.0 + jnp.exp(-a))


def _ffn_gate_fwd(name, up, cw, cb):
    T, F2 = up.shape
    F = F2 // 2
    tc, tr = _tile(F, 512, LANES), _tile(T, 512, HALO)
    nF, n_i = F // tc, T // tr

    def body(ac, ap, an, bc, bp, bn, wa, wb, ba, bb, o_ref):
        i = pl.program_id(1)
        ua = _mid(_conv(_ext(ac, ap, an, i, n_i), wa, ba), tr)
        ub = _mid(_conv(_ext(bc, bp, bn, i, n_i), wb, bb), tr)
        o_ref[...] = (ua * _sigmoid(ua) * ub).astype(o_ref.dtype)

    wspec = lambda o: pl.BlockSpec((3, tc), lambda j, i: (0, j + o))
    bspec = lambda o: pl.BlockSpec((1, tc), lambda j, i: (0, j + o))
    return pl.pallas_call(
        body, grid=(nF, n_i),
        in_specs=_halo_specs(tr, tc, T, lambda j, i: j) + _halo_specs(tr, tc, T, lambda j, i: j + nF)
        + [wspec(0), wspec(nF), bspec(0), bspec(nF)],
        out_specs=pl.BlockSpec((tr, tc), lambda j, i: (i, j)), out_shape=_sds((T, F), BF16),
        name=name, compiler_params=_cp("parallel", "parallel"))(up, up, up, up, up, up, cw, cw, cb, cb)


def _ffn_gate_bwd(name, up, dact, cw, cb):
    T, F2 = up.shape
    F = F2 // 2
    tc, tr = _tile(F, 512, LANES), _tile(T, 512, HALO)
    nF, n_i = F // tc, T // tr

    def body(ac, ap, an, bc, bp, bn, dc, dp, dn, wa, wb, ba, bb, o_ref, wga_ref, wgb_ref):
        i = pl.program_id(1)
        xa = _ext(ac, ap, an, i, n_i)
        xb = _ext(bc, bp, bn, i, n_i)
        da = _ext(dc, dp, dn, i, n_i)
        ua = _conv(xa, wa, ba)
        sig = _sigmoid(ua)
        ga = da * _conv(xb, wb, bb) * (sig * (1.0 + ua * (1.0 - sig)))
        o_ref[0] = _mid(_conv_t(ga, wa), tr).astype(o_ref.dtype)
        _conv_wgrad(wga_ref, ga, xa, tr, i == 0)
        gb = da * (ua * sig)
        o_ref[1] = _mid(_conv_t(gb, wb), tr).astype(o_ref.dtype)
        _conv_wgrad(wgb_ref, gb, xb, tr, i == 0)

    wspec = lambda o: pl.BlockSpec((3, tc), lambda j, i: (0, j + o))
    bspec = lambda o: pl.BlockSpec((1, tc), lambda j, i: (0, j + o))
    wg = pl.BlockSpec((8, tc), lambda j, i: (0, j))
    dup, wga, wgb = pl.pallas_call(
        body, grid=(nF, n_i),
        in_specs=_halo_specs(tr, tc, T, lambda j, i: j) + _halo_specs(tr, tc, T, lambda j, i: j + nF)
        + _halo_specs(tr, tc, T, lambda j, i: j) + [wspec(0), wspec(nF), bspec(0), bspec(nF)],
        out_specs=[pl.BlockSpec((2, tr, tc), lambda j, i: (0, i, j)), wg, wg],
        out_shape=[_sds((2, T, F), BF16), _sds((8, F), F32), _sds((8, F), F32)],
        name=name, compiler_params=_cp("parallel", "arbitrary"),
    )(up, up, up, up, up, up, dact, dact, dact, cw, cw, cb, cb)
    return dup, jnp.concatenate([wga, wgb], axis=1)


def _sc_gate_fwd(name, z, cw, cb):
    T, D3 = z.shape
    D = D3 // 3
    tc, tr = _tile(D, 512, LANES), _tile(T, 512, HALO)
    nD, n_i = D // tc, T // tr

    def body(uc, up_, un, gb, cc, cp, cn, w, b, o_ref):
        i = pl.program_id(1)
        cu = _ext(cc, cp, cn, i, n_i) * _ext(uc, up_, un, i, n_i)
        o_ref[...] = (gb[...].astype(F32) * _mid(_conv(cu, w, b), tr)).astype(o_ref.dtype)

    return pl.pallas_call(
        body, grid=(nD, n_i),
        in_specs=_halo_specs(tr, tc, T, lambda j, i: j) + [pl.BlockSpec((tr, tc), lambda j, i: (i, j + nD))]
        + _halo_specs(tr, tc, T, lambda j, i: j + 2 * nD)
        + [pl.BlockSpec((3, tc), lambda j, i: (0, j)), pl.BlockSpec((1, tc), lambda j, i: (0, j))],
        out_specs=pl.BlockSpec((tr, tc), lambda j, i: (i, j)), out_shape=_sds((T, D), BF16),
        name=name, compiler_params=_cp("parallel", "parallel"))(z, z, z, z, z, z, z, cw, cb)


def _sc_gate_bwd(name, z, dy, cw, cb):
    T, D3 = z.shape
    D = D3 // 3
    tc, tr = _tile(D, 512, LANES), _tile(T, 512, HALO)
    nD, n_i = D // tc, T // tr

    def body(uc, up_, un, bc, bp, bn, cc, cp, cn, yc, yp, yn, w, b, o_ref, wg_ref):
        i = pl.program_id(1)
        u = _ext(uc, up_, un, i, n_i)
        gc = _ext(cc, cp, cn, i, n_i)
        cu = gc * u
        g = _ext(yc, yp, yn, i, n_i) * _ext(bc, bp, bn, i, n_i)
        dcu = _mid(_conv_t(g, w), tr)
        o_ref[0] = (dcu * _mid(gc, tr)).astype(o_ref.dtype)
        o_ref[1] = (yc[...].astype(F32) * _mid(_conv(cu, w, b), tr)).astype(o_ref.dtype)
        o_ref[2] = (dcu * _mid(u, tr)).astype(o_ref.dtype)
        _conv_wgrad(wg_ref, g, cu, tr, i == 0)

    hs = lambda o: _halo_specs(tr, tc, T, lambda j, i: j + o)
    return pl.pallas_call(
        body, grid=(nD, n_i),
        in_specs=hs(0) + hs(nD) + hs(2 * nD) + hs(0)
        + [pl.BlockSpec((3, tc), lambda j, i: (0, j)), pl.BlockSpec((1, tc), lambda j, i: (0, j))],
        out_specs=[pl.BlockSpec((3, tr, tc), lambda j, i: (0, i, j)), pl.BlockSpec((8, tc), lambda j, i: (0, j))],
        out_shape=[_sds((3, T, D), BF16), _sds((8, D), F32)],
        name=name, compiler_params=_cp("parallel", "arbitrary"),
    )(z, z, z, z, z, z, z, z, z, dy, dy, dy, cw, cb)


def _slopes(n_heads):
    return jnp.asarray(2.0 ** (-ALIBI_MAX * np.arange(1, n_heads + 1) / n_heads), dtype=F32)


CHAINS = 8


def _nq(L, d):
    return max(1, min(CHAINS if d == 1 else CHAINS // 2, L // LANES // 2))


def _srows(ref, r, start, n, d):
    if d == 1:
        return ref[start:start + n, :]
    return ref[pl.ds(start * d + r, n, stride=d), :]


def _win(p_ref, c_ref, n_ref, r, b, nq):
    lo, hi, top = b * LANES - BAND, b * LANES + LANES + BAND, nq * LANES
    parts = [p_ref[r]] if lo < 0 else []
    parts.append(c_ref[r, max(lo, 0):min(hi, top), :])
    if hi > top:
        parts.append(n_ref[r])
    return parts[0] if len(parts) == 1 else jnp.concatenate(parts, axis=0)


def _nat_win(p_ref, c_ref, n_ref, r, b, nq, d):
    lo, hi, top = b * LANES - BAND, b * LANES + LANES + BAND, nq * LANES
    parts = [_srows(p_ref, r, 0, BAND, d)] if lo < 0 else []
    parts.append(_srows(c_ref, r, max(lo, 0), min(hi, top) - max(lo, 0), d))
    if hi > top:
        parts.append(_srows(n_ref, r, 0, BAND, d))
    return parts[0] if len(parts) == 1 else jnp.concatenate(parts, axis=0)


def _biases(slope, d, n, n_steps, nq, q_rows, k_rows, q0, k0):
    qi = lax.broadcasted_iota(jnp.int32, (q_rows, k_rows), 0) + q0
    kj = lax.broadcasted_iota(jnp.int32, (q_rows, k_rows), 1) + k0
    dist = jnp.abs(kj - qi)
    base = jnp.where(dist <= BAND, -slope * (dist * d).astype(F32), NEG_INF)
    out = []
    for b in range(nq):
        t = base
        if b == 0:
            t = jnp.where((n == 0) & ((kj < 0) | (qi < 0)), NEG_INF, t)
        if b == nq - 1:
            t = jnp.where((n == n_steps - 1) & ((kj >= LANES) | (qi >= LANES)), NEG_INF, t)
        out.append(t)
    return out


def _win_specs(d, H, col, nq, L):
    return [pl.BlockSpec((d, BAND, LANES), lambda h, n: (0, jnp.maximum(2 * nq * n - 1, 0), col * H + h)),
            pl.BlockSpec((d, nq * LANES, LANES), lambda h, n: (0, n, col * H + h)),
            pl.BlockSpec((d, BAND, LANES), lambda h, n: (0, jnp.minimum(2 * nq * (n + 1), L // BAND - 1), col * H + h))]


def _nat_specs(d, nq, L):
    return [pl.BlockSpec((BAND * d, LANES), lambda h, n: (jnp.maximum(2 * nq * n - 1, 0), h)),
            pl.BlockSpec((nq * LANES * d, LANES), lambda h, n: (n, h)),
            pl.BlockSpec((BAND * d, LANES), lambda h, n: (jnp.minimum(2 * nq * (n + 1), L // BAND - 1), h))]


def _over_residues(d, nq, per_r):
    if d == 1:
        per_r(0, 0)
    else:
        lax.fori_loop(0, d, per_r, 0, unroll=min(d, max(1, CHAINS // nq)))


def _attn_fwd(name, qkv, d, H):
    T = qkv.shape[0]
    D = H * HEAD_DIM
    L = T // d
    nq = _nq(L, d)
    n_steps = L // (nq * LANES)
    scale = HEAD_DIM ** -0.5
    q3 = qkv.reshape(d, L, 3 * D)

    def body(s_ref, q_ref, kp, kc, kn, vp, vc, vn, o_ref, l_ref):
        h, n = pl.program_id(0), pl.program_id(1)
        bias = _biases(s_ref[h], d, n, n_steps, nq, LANES, 2 * LANES, 0, -BAND)

        def per_r(r, carry):
            for b in range(nq):
                k, v = _win(kp, kc, kn, r, b, nq), _win(vp, vc, vn, r, b, nq)
                s = lax.dot_general(q_ref[r, b * LANES:(b + 1) * LANES, :], k, NT, preferred_element_type=F32) * scale + bias[b]
                m = jnp.max(s, axis=1, keepdims=True)
                p = jnp.exp(s - m)
                den = jnp.sum(p, axis=1, keepdims=True)
                o = lax.dot_general(p.astype(BF16), v, NN, preferred_element_type=F32) / den
                lse = jnp.broadcast_to(m + jnp.log(den), (LANES, LANES))
                if d == 1:
                    o_ref[b * LANES:(b + 1) * LANES, :] = o
                    l_ref[b * LANES:(b + 1) * LANES, :] = lse
                else:
                    o_ref[pl.ds(b * LANES * d + r, LANES, stride=d), :] = o
                    l_ref[pl.ds(b * LANES * d + r, LANES, stride=d), :] = lse
            return carry

        _over_residues(d, nq, per_r)

    out = pl.BlockSpec((nq * LANES * d, LANES), lambda h, n: (n, h))
    return pl.pallas_call(
        body, grid=(H, n_steps),
        in_specs=[pl.BlockSpec(memory_space=pltpu.SMEM), pl.BlockSpec((d, nq * LANES, LANES), lambda h, n: (0, n, h))]
        + _win_specs(d, H, 1, nq, L) + _win_specs(d, H, 2, nq, L),
        out_specs=[out, out], out_shape=[_sds((T, D), F32), _sds((T, D), F32)],
        name=name, compiler_params=_cp("parallel", "parallel"))(_slopes(H), q3, q3, q3, q3, q3, q3, q3)


def _attn_combine(name, outs, lses):
    T, D = outs[0].shape
    tr, tc = _tile(T, 512, 16), _tile(D, 512, LANES)

    def body(o0, o1, o2, l0, l1, l2, ob_ref, l_ref):
        a0, a1, a2 = l0[...], l1[...], l2[...]
        m = jnp.maximum(jnp.maximum(a0, a1), a2)
        e0, e1, e2 = jnp.exp(a0 - m), jnp.exp(a1 - m), jnp.exp(a2 - m)
        z = e0 + e1 + e2
        ob_ref[...] = ((e0 * o0[...] + e1 * o1[...] + e2 * o2[...]) / z).astype(BF16)
        l_ref[...] = m + jnp.log(z)

    blk = pl.BlockSpec((tr, tc), lambda i, j: (i, j))
    return pl.pallas_call(
        body, grid=(T // tr, D // tc), in_specs=[blk] * 6, out_specs=[blk] * 2,
        out_shape=[_sds((T, D), BF16), _sds((T, D), F32)],
        name=name, compiler_params=_cp("parallel", "parallel"))(*outs, *lses)


def _attn_stats(name, do, o, lse):
    T, D = do.shape
    tr = _tile(T, 1024, 16)

    def body(a, b, l, o_ref):
        delta = jnp.broadcast_to(jnp.sum(a[...] * b[...].astype(F32), axis=1, keepdims=True), o_ref.shape)
        lane = lax.broadcasted_iota(jnp.int32, o_ref.shape, 1)
        o_ref[...] = jnp.where(lane < BAND, l[...], delta)

    blk = pl.BlockSpec((tr, LANES), lambda i, j: (i, j))
    return pl.pallas_call(body, grid=(T // tr, D // LANES), in_specs=[blk, blk, blk], out_specs=blk,
                          out_shape=_sds((T, D), F32), name=name, compiler_params=_cp("parallel", "parallel"))(do, o, lse)


def _attn_bwd(name, qkv, do, stats, d, H):
    T = qkv.shape[0]
    D = H * HEAD_DIM
    L = T // d
    nq = _nq(L, d)
    n_steps = L // (nq * LANES)
    scale = HEAD_DIM ** -0.5
    q3 = qkv.reshape(d, L, 3 * D)
    mid = slice(BAND, BAND + LANES)

    def body(s_ref, qp, qc, qn, kp, kc, kn, vp, vc, vn, gp, gc, gn, tp, tc_, tn_, o_ref):
        h, n = pl.program_id(0), pl.program_id(1)
        bias_q = _biases(s_ref[h], d, n, n_steps, nq, LANES, 2 * LANES, 0, -BAND)
        bias_k = _biases(s_ref[h], d, n, n_steps, nq, 2 * LANES, LANES, -BAND, 0)

        def per_r(r, carry):
            for b in range(nq):
                rows = slice(b * LANES, (b + 1) * LANES)
                q_w, k_w, v_w = _win(qp, qc, qn, r, b, nq), _win(kp, kc, kn, r, b, nq), _win(vp, vc, vn, r, b, nq)
                g_w = _nat_win(gp, gc, gn, r, b, nq, d)
                t_w = _nat_win(tp, tc_, tn_, r, b, nq, d)
                g_b = g_w.astype(BF16)
                q_c, k_c, v_c, g_c, t_c = q_w[mid], k_w[mid], v_w[mid], g_b[mid], t_w[mid]
                s = lax.dot_general(q_c, k_w, NT, preferred_element_type=F32) * scale + bias_q[b]
                p = jnp.exp(s - t_c[:, 0:1])
                dp = lax.dot_general(g_c, v_w, NT, preferred_element_type=F32)
                ds = p * (dp - t_c[:, BAND:BAND + 1])
                o_ref[0, r, rows, :] = (lax.dot_general(ds.astype(BF16), k_w, NN, preferred_element_type=F32) * scale).astype(BF16)
                s2 = lax.dot_general(q_w, k_c, NT, preferred_element_type=F32) * scale + bias_k[b]
                p2 = jnp.exp(s2 - t_w[:, 0:1])
                o_ref[2, r, rows, :] = lax.dot_general(p2.astype(BF16), g_b, TN, preferred_element_type=F32).astype(BF16)
                dp2 = lax.dot_general(g_b, v_c, NT, preferred_element_type=F32)
                ds2 = p2 * (dp2 - t_w[:, BAND:BAND + 1])
                o_ref[1, r, rows, :] = (lax.dot_general(ds2.astype(BF16), q_w, TN, preferred_element_type=F32) * scale).astype(BF16)
            return carry

        _over_residues(d, nq, per_r)

    dqkv = pl.pallas_call(
        body, grid=(H, n_steps),
        in_specs=[pl.BlockSpec(memory_space=pltpu.SMEM)]
        + _win_specs(d, H, 0, nq, L) + _win_specs(d, H, 1, nq, L) + _win_specs(d, H, 2, nq, L)
        + _nat_specs(d, nq, L) + _nat_specs(d, nq, L),
        out_specs=pl.BlockSpec((3, d, nq * LANES, LANES), lambda h, n: (0, 0, n, h)),
        out_shape=_sds((3, d, L, D), BF16),
        name=name, compiler_params=_cp("parallel", "parallel"),
    )(_slopes(H), *([q3] * 9), *([do] * 3), *([stats] * 3))
    return dqkv.reshape(3, T, D)


def _to_group_order(a, d):
    if d == 1:
        return a
    T, C = a.shape
    return a.reshape(T // d, d, C).swapaxes(0, 1).reshape(T, C)


def _from_group_order(a, d):
    if d == 1:
        return a
    T, C = a.shape
    return a.reshape(d, T // d, C).swapaxes(0, 1).reshape(T, C)


def _fwd_bwd(x, tgt, S, ex):
    T, D = x.shape
    H = D // HEAD_DIM
    G3 = 3 * D

    def mm(fn, *args, rides=(), **kw):
        out, extra = fn(*args, comm=_join([getattr(ex, kind)(keys) for kind, keys in rides]), **kw)
        at = 0
        for kind, keys in rides:
            getattr(ex, kind + "_done")(keys, extra[at:at + len(keys)])
            at += len(keys)
        return out

    def ffn_fwd(l, xin, rides_up, rides_dn):
        hf = mm(_rmsnorm_fwd, f"ffn_norm{l}", xin, S["ffn_g"][l])
        up = mm(_mm_nn, f"ffn_up{l}", hf, ex.w(f"up{l}"), None, BF16, rides=rides_up)
        act = _ffn_gate_fwd(f"ffn_gate{l}", up, ffn_cw[l], S["ffn_cb"][l][None])
        return hf, up, act, mm(_mm_nn, f"ffn_down{l}", act, ex.w(f"dn{l}"), xin, F32, tk_pref=2816, rides=rides_dn)

    def ffn_bwd(l, xin, hf, up, act, dxo, dxo_b, rides):
        dact = mm(_mm_nt, f"ffn_down_dx{l}", dxo_b, ex.w(f"dn{l}"), BF16, tn_pref=512, rides=rides[0])
        ex.grad(f"dn{l}", mm(_mm_tn, f"ffn_down_dw{l}", act, dxo_b, tkr_pref=1408))
        dup, cg = _ffn_gate_bwd(f"ffn_gate_bwd{l}", up, dact, ffn_cw[l], S["ffn_cb"][l][None])
        dhf = mm(_mm_nt, f"ffn_up_dx{l}", dup, ex.w(f"up{l}"), BF16, tk_pref=2816, rides=rides[1])
        ex.grad(f"up{l}", mm(_mm_tn, f"ffn_up_dw{l}", hf, dup, tn_pref=2816, tk_pref=1024, rides=rides[2]))
        dx, dx_b, dg = mm(_rmsnorm_bwd, f"ffn_norm_bwd{l}", xin, S["ffn_g"][l], [dhf], dxo)
        return dx, dx_b, dg, cg

    h0 = mm(_rmsnorm_fwd, "mix_norm0", x, S["mix_g"][0], rides=[("ag", ["in", "scw", "fcw"])])
    sc_cw = ex.w("scw")
    ffn_cw = ex.w("fcw").reshape(S["ffn_cb"].shape[0], 3, -1)
    z = mm(_mm_nn, "sc_in", h0, ex.w("in"), None, BF16, rides=[("ag", ["sco", "up0"])])
    y = _sc_gate_fwd("sc_gate", z, sc_cw, S["sc_cb"][None])
    x1 = mm(_mm_nn, "sc_out", y, ex.w("sco"), x, F32, rides=[("ag", ["dn0"])])
    hf0, up0, act0, x2 = ffn_fwd(0, x1, [("ag", ["qkv"])], [("ag", ["ao"])])
    h1 = mm(_rmsnorm_fwd, "mix_norm1", x2, S["mix_g"][1])
    hd, qkv, outs, lses = [], [], [], []
    for g, d in enumerate(DILATIONS):
        hd.append(_to_group_order(h1, d))
        qkv.append(mm(_mm_nn, f"attn_qkv{g}", hd[g], ex.w("qkv"), None, BF16, col0=g * G3, n_cols=G3,
                      rides=[[("ag", ["dn1"])], [("ag", ["up1"])], ()][g]))
        o_g, l_g = _attn_fwd(f"attn_fwd{g}", qkv[g], d, H)
        outs.append(o_g)
        lses.append(l_g)
    o_b, lse = _attn_combine("attn_combine", outs, lses)
    x3 = mm(_mm_nn, "attn_out", o_b, ex.w("ao"), x2, F32)
    hf1, up1, act1, x4 = ffn_fwd(1, x3, (), ())
    loss, dx4, dx4_b, dg_fin = _final_loss("final_loss", x4, S["fin_g"], tgt)

    dx3, dx3_b, dg_f1, cg1 = ffn_bwd(1, x3, hf1, up1, act1, dx4, dx4_b,
                                     [(), [("pair", ["dn1"])], [("chip", ["dn1"])]])
    do = mm(_mm_nt, "attn_out_dx", dx3_b, ex.w("ao"), F32, rides=[("pair", ["up1"])])
    ex.grad("ao", mm(_mm_tn, "attn_out_dw", o_b, dx3_b))
    stats = _attn_stats("attn_stats", do, o_b, lse)
    dhs, dw_qkv = [], None
    qkv_rides = [[("chip", ["up1"]), ("pair", ["ao"])], [("chip", ["ao"])], ()]
    for g, d in enumerate(DILATIONS):
        dqkv = _attn_bwd(f"attn_bwd{g}", qkv[g], do, stats, d, H)
        dhs.append(_from_group_order(mm(_mm_nt, f"attn_qkv_dx{g}", dqkv, ex.w("qkv"), BF16, col0=g * G3,
                                        rides=qkv_rides[g]), d))
        dw_qkv = mm(_mm_tn, f"attn_qkv_dw{g}", hd[g], dqkv, n_total=len(DILATIONS) * G3, col0=g * G3, prev=dw_qkv)
    ex.grad("qkv", dw_qkv)
    dx2, dx2_b, dg_m1 = mm(_rmsnorm_bwd, "mix_norm_bwd1", x2, S["mix_g"][1], dhs, dx3)
    dx1, dx1_b, dg_f0, cg0 = ffn_bwd(0, x1, hf0, up0, act0, dx2, dx2_b,
                                     [[("pair", ["qkv"])], [("chip", ["qkv"]), ("pair", ["dn0"])], [("chip", ["dn0"])]])
    dy = mm(_mm_nt, "sc_out_dx", dx1_b, ex.w("sco"), BF16, rides=[("pair", ["up0"])])
    ex.grad("sco", mm(_mm_tn, "sc_out_dw", y, dx1_b))
    dz, cg_sc = _sc_gate_bwd("sc_gate_bwd", z, dy, sc_cw, S["sc_cb"][None])
    ex.grad("in", mm(_mm_tn, "sc_in_dw", h0, dz, rides=[("chip", ["up0"]), ("pair", ["sco"])]))
    dh0 = mm(_mm_nt, "sc_in_dx", dz, ex.w("in"), BF16, rides=[("chip", ["sco"]), ("pair", ["in"])])
    dx0, _, dg_m0 = mm(_rmsnorm_bwd, "mix_norm_bwd0", x, S["mix_g"][0], [dh0], dx1, rides=[("chip", ["in"])])

    dS = {"mix_g": jnp.stack([dg_m0, dg_m1]), "ffn_g": jnp.stack([dg_f0, dg_f1]), "fin_g": dg_fin,
          "sc_cw": cg_sc[0:3], "sc_cb": cg_sc[3], "ffn_cw": jnp.stack([cg0[0:3], cg1[0:3]]),
          "ffn_cb": jnp.stack([cg0[3], cg1[3]])}
    return loss, dx0, dS


def _place():
    x, y, c = lax.axis_index("x"), lax.axis_index("y"), lax.axis_index("c")
    return x, y, c, [(1 - x, y), (x, 1 - y), (1 - x, 1 - y)]


def _block(ref, s, shape, axis):
    R, C = shape
    if axis == 0:
        return ref.at[pl.ds(pl.multiple_of(s * R, HALO), R), :]
    return ref.at[:, pl.ds(pl.multiple_of(s * C, LANES), C)]


def _whole(shape, axis):
    return (shape[0] * N_DEV, shape[1]) if axis == 0 else (shape[0], shape[1] * N_DEV)


def _ag_stage(arrs, axes):
    n = len(arrs)

    def plan(ins, outs, sems):
        send_sems, recv_sems, local_sems = sems
        x, y, c, chips = _place()
        me, sibling = 4 * x + 2 * y + c, (x, y, 1 - c)

        def copy(a, k, blk, to, src=None):
            dst = _block(outs[a], blk, arrs[a].shape, axes[a])
            return pltpu.make_async_remote_copy(src_ref=dst if src is None else src, dst_ref=dst,
                                                send_sem=send_sems.at[a, k], recv_sem=recv_sems.at[a, k],
                                                device_id=to, device_id_type=MESH)

        mine = [pltpu.make_async_copy(ins[a], _block(outs[a], me, arrs[a].shape, axes[a]), local_sems.at[a])
                for a in range(n)]
        first = []
        for a in range(n):
            first.append(copy(a, 0, me, sibling, src=ins[a]))
            first += [copy(a, 1 + j, me, (*chip, c), src=ins[a]) for j, chip in enumerate(chips)]
        return x, y, c, chips, sibling, copy, mine, first

    def start(ins, outs, sems):
        *_, mine, first = plan(ins, outs, sems)
        for cp in mine + first:
            cp.start()

    def finish(ins, outs, sems):
        x, y, c, chips, sibling, copy, mine, first = plan(ins, outs, sems)
        passed = []
        for j, (px, py) in enumerate(chips):
            for a in range(n):
                blk = 4 * px + 2 * py + c
                copy(a, 1 + j, blk, sibling).wait_recv()
                passed.append(copy(a, 4 + j, blk, sibling))
                passed[-1].start()
        for a in range(n):
            copy(a, 0, 4 * x + 2 * y + 1 - c, sibling).wait_recv()
            for j, (px, py) in enumerate(chips):
                copy(a, 4 + j, 4 * px + 2 * py + 1 - c, sibling).wait_recv()
        for cp in first + passed:
            cp.wait_send()
        for cp in mine:
            cp.wait()

    return _Stage(list(arrs), [_sds(_whole(a.shape, ax), a.dtype) for a, ax in zip(arrs, axes)],
                  [pltpu.SemaphoreType.DMA((n, 7)), pltpu.SemaphoreType.DMA((n, 7)), pltpu.SemaphoreType.DMA((n,))],
                  start, finish)


def _pair_stage(dws, shapes, axes):
    n = len(dws)

    def copies(ins, outs, sems):
        send_sems, recv_sems = sems
        x, y, c, _ = _place()
        return [pltpu.make_async_remote_copy(src_ref=_block(ins[a], 2 * k + 1 - c, shapes[a], axes[a]),
                                             dst_ref=outs[a].at[k],
                                             send_sem=send_sems.at[a, k], recv_sem=recv_sems.at[a, k],
                                             device_id=(x, y, 1 - c), device_id_type=MESH)
                for a in range(n) for k in range(4)]

    def start(ins, outs, sems):
        for cp in copies(ins, outs, sems):
            cp.start()

    def finish(ins, outs, sems):
        for cp in copies(ins, outs, sems):
            cp.wait()

    return _Stage(list(dws), [_sds((4,) + tuple(s), a.dtype) for a, s in zip(dws, shapes)],
                  [pltpu.SemaphoreType.DMA((n, 4)), pltpu.SemaphoreType.DMA((n, 4))], start, finish)


def _rs_pair_add(name, dw, got, c_arr, axis):
    _, R, C = got.shape
    tr, tc = _tile(R, 512, 16), _tile(C, 1536, LANES)
    per = R // tr if axis == 0 else C // tc

    def body(c_ref, a_ref, b_ref, o_ref):
        o_ref[...] = (a_ref[...].astype(F32) + b_ref[...].astype(F32)).astype(o_ref.dtype)

    if axis == 0:
        mine = pl.BlockSpec((tr, tc), lambda k, i, j, c_ref: ((2 * k + c_ref[0]) * per + i, j))
    else:
        mine = pl.BlockSpec((tr, tc), lambda k, i, j, c_ref: (i, (2 * k + c_ref[0]) * per + j))
    return pl.pallas_call(
        body,
        grid_spec=pltpu.PrefetchScalarGridSpec(
            num_scalar_prefetch=1, grid=(4, R // tr, C // tc),
            in_specs=[mine, pl.BlockSpec((None, tr, tc), lambda k, i, j, c_ref: (k, i, j))],
            out_specs=pl.BlockSpec((None, tr, tc), lambda k, i, j, c_ref: (k, i, j))),
        out_shape=_sds((4, R, C), BF16), name=name,
        compiler_params=_cp("parallel", "parallel", "parallel"))(c_arr, dw, got)


def _chip_stage(parts):
    n = len(parts)

    def plan(ins, outs, sems):
        send_sems, recv_sems, local_sems = sems
        x, y, c, chips = _place()
        my_chip = 2 * x + y
        mine = [pltpu.make_async_copy(ins[a].at[my_chip], outs[a].at[my_chip], local_sems.at[a]) for a in range(n)]
        sends = [pltpu.make_async_remote_copy(src_ref=ins[a].at[2 * px + py], dst_ref=outs[a].at[my_chip],
                                              send_sem=send_sems.at[a, j], recv_sem=recv_sems.at[a, j],
                                              device_id=(px, py, c), device_id_type=MESH)
                 for a in range(n) for j, (px, py) in enumerate(chips)]
        arrivals = lambda: [pltpu.make_async_remote_copy(src_ref=ins[a].at[my_chip], dst_ref=outs[a].at[2 * px + py],
                                                         send_sem=send_sems.at[a, j], recv_sem=recv_sems.at[a, j],
                                                         device_id=(px, py, c), device_id_type=MESH)
                            for a in range(n) for j, (px, py) in enumerate(chips)]
        return mine, sends, arrivals

    def start(ins, outs, sems):
        mine, sends, _ = plan(ins, outs, sems)
        for cp in mine + sends:
            cp.start()

    def finish(ins, outs, sems):
        mine, sends, arrivals = plan(ins, outs, sems)
        for cp in arrivals():
            cp.wait_recv()
        for cp in sends:
            cp.wait_send()
        for cp in mine:
            cp.wait()

    return _Stage(list(parts), [_sds(a.shape, a.dtype) for a in parts],
                  [pltpu.SemaphoreType.DMA((n, 3)), pltpu.SemaphoreType.DMA((n, 3)), pltpu.SemaphoreType.DMA((n,))],
                  start, finish)


class _Exchange:
    ROW_SHARDED = ("sco", "ao", "dn0", "dn1")

    def __init__(self, shards, c_arr):
        self.sh, self.c_arr = shards, c_arr
        self.W, self.dw, self.parts, self.sums = {}, {}, {}, {}

    def axis(self, key):
        return 0 if key in self.ROW_SHARDED else 1

    def w(self, key):
        return self.W[key]

    def ag(self, keys):
        return _ag_stage([self.sh[k] for k in keys], [self.axis(k) for k in keys])

    def ag_done(self, keys, outs):
        self.W.update(zip(keys, outs))

    def grad(self, key, dw):
        self.dw[key] = dw

    def pair(self, keys):
        return _pair_stage([self.dw[k] for k in keys], [self.sh[k].shape for k in keys], [self.axis(k) for k in keys])

    def pair_done(self, keys, outs):
        for k, got in zip(keys, outs):
            self.parts[k] = _rs_pair_add(f"rs_add_{k}", self.dw[k], got, self.c_arr, self.axis(k))

    def chip(self, keys):
        return _chip_stage([self.parts[k] for k in keys])

    def chip_done(self, keys, outs):
        self.sums.update(zip(keys, outs))


def _sum_slots(name, a):
    _, rows, _ = a.shape

    def body(a_ref, o_ref):
        s = a_ref[0]
        for k in range(1, N_DEV):
            s = s + a_ref[k]
        o_ref[...] = s

    return pl.pallas_call(body, out_shape=_sds((rows, LANES), F32), name=name)(a)


def _cast_bf16(name, w3, l):
    _, R, C = w3.shape
    tr, tc = _tile(R, 512, 16), _tile(C, 1536, LANES)

    def body(w_ref, o_ref):
        o_ref[...] = w_ref[...].astype(BF16)

    return pl.pallas_call(
        body, grid=(R // tr, C // tc), in_specs=[pl.BlockSpec((None, tr, tc), lambda i, j: (l, i, j))],
        out_specs=pl.BlockSpec((tr, tc), lambda i, j: (i, j)), out_shape=_sds((R, C), BF16),
        name=name, compiler_params=_cp("parallel", "parallel"))(w3)


def _adamw(name, g_slots, w3, m3, v3, l, prev):
    n_slots, R, C = g_slots.shape
    tr, tc = _tile(R, 256, 8), _tile(C, 1536, LANES)
    c1, c2 = 1.0 - ADAM_B1 ** ADAM_STEP, 1.0 - ADAM_B2 ** ADAM_STEP

    def body(g_ref, w_ref, m_ref, v_ref, *rest):
        og, od, om, ov = rest[-4:]
        g = g_ref[0].astype(F32)
        for k in range(1, n_slots):
            g = g + g_ref[k].astype(F32)
        m = ADAM_B1 * m_ref[...] + (1.0 - ADAM_B1) * g
        v = ADAM_B2 * v_ref[...] + (1.0 - ADAM_B2) * (g * g)
        og[...] = g
        om[...] = m
        ov[...] = v
        od[...] = -ADAM_LR * ((m / c1) / (jnp.sqrt(v / c2) + ADAM_EPS) + ADAM_WD * w_ref[...])

    lay = pl.BlockSpec((None, tr, tc), lambda i, j: (l, i, j))
    ops = [g_slots, w3, m3, v3]
    specs = [pl.BlockSpec((n_slots, tr, tc), lambda i, j: (0, i, j)), lay, lay, lay]
    aliases = {}
    if prev is not None:
        ops += list(prev)
        specs += _any_specs(4)
        aliases = {4 + k: k for k in range(4)}
    return pl.pallas_call(
        body, grid=(R // tr, C // tc), in_specs=specs, out_specs=[lay] * 4,
        out_shape=[_sds(w3.shape, F32)] * 4, input_output_aliases=aliases,
        name=name, compiler_params=_cp("parallel", "parallel"))(*ops)


def _pack(parts):
    flat = jnp.concatenate([p.reshape(-1) for p in parts])
    pad = (-flat.shape[0]) % (HALO * LANES)
    return jnp.pad(flat, (0, pad)).reshape(-1, LANES)


def _unpack(packed, shapes):
    flat = packed.reshape(-1)
    out, at = [], 0
    for s in shapes:
        n = int(np.prod(s))
        out.append(flat[at:at + n].reshape(s))
        at += n
    return out


def kernel(x, mix_norm_g, ffn_norm_g, final_norm_g, sc_w_in, sc_conv_w, sc_conv_b, sc_w_out, attn_w_qkv, attn_w_out, ffn_w_up, ffn_conv_w, ffn_conv_b, ffn_w_down, loss_target, m_mix_norm_g, m_ffn_norm_g, m_final_norm_g, m_sc_w_in, m_sc_conv_w, m_sc_conv_b, m_sc_w_out, m_attn_w_qkv, m_attn_w_out, m_ffn_w_up, m_ffn_conv_w, m_ffn_conv_b, m_ffn_w_down, v_mix_norm_g, v_ffn_norm_g, v_final_norm_g, v_sc_w_in, v_sc_conv_w, v_sc_conv_b, v_sc_w_out, v_attn_w_qkv, v_attn_w_out, v_ffn_w_up, v_ffn_conv_w, v_ffn_conv_b, v_ffn_w_down):
    n_layers = ffn_w_up.shape[0]
    me = 4 * lax.axis_index("x") + 2 * lax.axis_index("y") + lax.axis_index("c")
    c_arr = lax.axis_index("c").astype(jnp.int32).reshape(1)

    big = [("in", sc_w_in, 0), ("sco", sc_w_out, 0), ("qkv", attn_w_qkv, 0), ("ao", attn_w_out, 0)]
    big += [(f"up{l}", ffn_w_up, l) for l in range(n_layers)] + [(f"dn{l}", ffn_w_down, l) for l in range(n_layers)]
    shards = {nm: _cast_bf16(f"cast_{nm}", w, l) for nm, w, l in big}
    shards["scw"] = sc_conv_w.reshape(-1, sc_conv_w.shape[-1])
    shards["fcw"] = ffn_conv_w.reshape(-1, ffn_conv_w.shape[-1])
    ex = _Exchange(shards, c_arr)
    S = {"mix_g": mix_norm_g, "ffn_g": ffn_norm_g, "fin_g": final_norm_g, "sc_cb": sc_conv_b[0], "ffn_cb": ffn_conv_b}

    loss_part, grad_x, dS = _fwd_bwd(x[0], loss_target[0], S, ex)

    small_names = ["mix_g", "ffn_g", "fin_g", "sc_cb", "ffn_cb", "sc_cw", "ffn_cw"]
    small_parts = [dS[k] for k in small_names] + [loss_part.reshape(1)]
    small_mine = _pack(small_parts)
    small_all, = _run_stage("gather_small", _ag_stage([small_mine], [0]))
    small_sum = _sum_slots("sum_small", small_all.reshape((N_DEV,) + small_mine.shape))
    g_mix, g_ffn, g_fin, g_scb, g_fcb, g_scw, g_fcw, loss = _unpack(small_sum, [p.shape for p in small_parts])
    g_scw = lax.dynamic_slice_in_dim(g_scw, me * sc_conv_w.shape[-1], sc_conv_w.shape[-1], axis=-1)[None]
    g_fcw = lax.dynamic_slice_in_dim(g_fcw, me * ffn_conv_w.shape[-1], ffn_conv_w.shape[-1], axis=-1)
    g_scb = g_scb[None]
    small_g = [g_mix, g_ffn, g_fin, g_scw, g_scb, g_fcw, g_fcb]
    small_w = [mix_norm_g, ffn_norm_g, final_norm_g, sc_conv_w, sc_conv_b, ffn_conv_w, ffn_conv_b]
    small_m = [m_mix_norm_g, m_ffn_norm_g, m_final_norm_g, m_sc_conv_w, m_sc_conv_b, m_ffn_conv_w, m_ffn_conv_b]
    small_v = [v_mix_norm_g, v_ffn_norm_g, v_final_norm_g, v_sc_conv_w, v_sc_conv_b, v_ffn_conv_w, v_ffn_conv_b]
    small_out = _adamw("adamw_small", _pack(small_g)[None], _pack(small_w)[None], _pack(small_m)[None],
                       _pack(small_v)[None], 0, None)
    small_shapes = [w.shape for w in small_w]
    sg, sd, sm, sv = [_unpack(o[0], small_shapes) for o in small_out]

    moments = {"in": (m_sc_w_in, v_sc_w_in), "sco": (m_sc_w_out, v_sc_w_out), "qkv": (m_attn_w_qkv, v_attn_w_qkv),
               "ao": (m_attn_w_out, v_attn_w_out), "up": (m_ffn_w_up, v_ffn_w_up), "dn": (m_ffn_w_down, v_ffn_w_down)}
    upd = {}
    for nm, w, l in big:
        key = nm.rstrip("0123456789")
        upd[key] = _adamw(f"adamw_{nm}", ex.sums[nm], w, moments[key][0], moments[key][1], l, upd.get(key))

    def leaves(k):
        return [sg, sd, sm, sv][k][0:3] + [upd["in"][k], [sg, sd, sm, sv][k][3], [sg, sd, sm, sv][k][4], upd["sco"][k],
                                          upd["qkv"][k], upd["ao"][k], upd["up"][k], [sg, sd, sm, sv][k][5],
                                          [sg, sd, sm, sv][k][6], upd["dn"][k]]

    return (loss.reshape(()), grad_x[None], *leaves(0), *leaves(1), *leaves(2), *leaves(3))
```

```python
import math

import numpy as np
import jax
import jax.numpy as jnp
from jax import lax
from jax.experimental import pallas as pl
from jax.experimental.pallas import tpu as pltpu

F32 = jnp.float32
BF16 = jnp.bfloat16
MESH = pl.DeviceIdType.MESH

HEAD_DIM = 128
DILATED_GROUPS = ((128, 1), (512, 4), (2048, 16))
DILATIONS = tuple(d for _, d in DILATED_GROUPS)
BAND = (DILATED_GROUPS[0][0] // 2) // DILATED_GROUPS[0][1]
assert all((w // 2) // d == BAND for w, d in DILATED_GROUPS)
NORM_EPS = 1e-5
ALIBI_MAX = 8.0
NEG_INF = -1e30
ADAM_LR, ADAM_B1, ADAM_B2, ADAM_EPS, ADAM_WD, ADAM_STEP = 0.001, 0.9, 0.999, 1e-08, 0.01, 10

N_DEV = 8
LANES = 128
HALO = 16
VMEM_LIMIT = 56 * 1024 * 1024


def _cp(*sem):
    return pltpu.CompilerParams(dimension_semantics=sem, vmem_limit_bytes=VMEM_LIMIT)


def _tile(n, pref, mult):
    t = (min(n, pref) // mult) * mult
    while t >= mult:
        if n % t == 0:
            return t
        t -= mult
    return n


def _sds(shape, dtype):
    return jax.ShapeDtypeStruct(shape, dtype)


def _any_specs(n):
    return [pl.BlockSpec(memory_space=pl.ANY)] * n


class _Stage:
    def __init__(self, arrays, out_shapes, sems, start, finish):
        self.arrays, self.out_shapes, self.sems, self.start, self.finish = arrays, out_shapes, sems, start, finish


def _join(stages):
    stages = [s for s in stages if s is not None]
    if not stages:
        return None

    def split(refs, count):
        out, at = [], 0
        for s in stages:
            out.append(refs[at:at + count(s)])
            at += count(s)
        return out

    def each(which):
        def run(ins, outs, sems):
            parts = zip(split(ins, lambda s: len(s.arrays)), split(outs, lambda s: len(s.out_shapes)),
                        split(sems, lambda s: len(s.sems)))
            for s, (i, o, m) in zip(stages, parts):
                getattr(s, which)(i, o, m)
        return run

    return _Stage(sum([s.arrays for s in stages], []), sum([s.out_shapes for s in stages], []),
                  sum([s.sems for s in stages], []), each("start"), each("finish"))


def _run_stage(name, st):
    n, m = len(st.arrays), len(st.out_shapes)

    def body(*refs):
        ins, outs, sems = refs[:n], refs[n:n + m], refs[n + m:]
        st.start(ins, outs, sems)
        st.finish(ins, outs, sems)

    return pl.pallas_call(body, in_specs=_any_specs(n), out_specs=_any_specs(m), out_shape=st.out_shapes,
                          scratch_shapes=st.sems, name=name)(*st.arrays)


NN = (((1,), (0,)), ((), ()))
NT = (((1,), (1,)), ((), ()))
TN = (((0,), (0,)), ((), ()))


def _mm(name, operands, in_specs, out_sds, o_spec, grid, dims, acc_shape, has_res=False, aliases=None, comm=None):
    nk = grid[2]
    n_in = len(operands)
    n_ci, n_co = (len(comm.arrays), len(comm.out_shapes)) if comm else (0, 0)

    def body(*refs):
        a_ref, b_ref = refs[0], refs[1]
        r_ref = refs[2] if has_res else None
        o_ref = refs[n_in + n_ci]
        acc = refs[n_in + n_ci + 1 + n_co]
        c_refs = (refs[n_in:n_in + n_ci], refs[n_in + n_ci + 1:n_in + n_ci + 1 + n_co], refs[n_in + n_ci + 2 + n_co:])
        ids = [pl.program_id(q) for q in range(3)]
        if comm:
            @pl.when((ids[0] == 0) & (ids[1] == 0) & (ids[2] == 0))
            def _():
                comm.start(*c_refs)

        def finish(total):
            if has_res:
                total = total + r_ref[...]
            o_ref[...] = total.astype(o_ref.dtype)

        if nk == 1:
            finish(lax.dot_general(a_ref[...], b_ref[...], dims, preferred_element_type=F32))
        else:
            k = ids[2]

            @pl.when(k == 0)
            def _():
                acc[...] = jnp.zeros_like(acc)

            acc[...] += lax.dot_general(a_ref[...], b_ref[...], dims, preferred_element_type=F32)

            @pl.when(k == nk - 1)
            def _():
                finish(acc[...])

        if comm:
            @pl.when((ids[0] == grid[0] - 1) & (ids[1] == grid[1] - 1) & (ids[2] == nk - 1))
            def _():
                comm.finish(*c_refs)

    scratch = [pltpu.VMEM(acc_shape if nk > 1 else (8, LANES), F32)]
    if not comm:
        out = pl.pallas_call(
            body, grid=grid, in_specs=in_specs, out_specs=o_spec, out_shape=out_sds, scratch_shapes=scratch,
            input_output_aliases=aliases or {}, name=name,
            compiler_params=_cp("parallel", "parallel", "arbitrary"))(*operands)
        return out, []
    outs = pl.pallas_call(
        body, grid=grid, in_specs=list(in_specs) + _any_specs(n_ci), out_specs=[o_spec] + _any_specs(n_co),
        out_shape=[out_sds] + comm.out_shapes, scratch_shapes=scratch + comm.sems,
        input_output_aliases=aliases or {}, name=name,
        compiler_params=_cp("arbitrary", "arbitrary", "arbitrary"))(*operands, *comm.arrays)
    return outs[0], list(outs[1:])


def _stack(a):
    return a if a.ndim == 3 else a[None]


def _mm_nn(name, a, w, res, out_dtype, col0=0, n_cols=None, tk_pref=2048, comm=None):
    M, K = a.shape
    N = n_cols or w.shape[1]
    tm, tn, tk = _tile(M, 1024, 16), _tile(N, 1024, LANES), _tile(K, tk_pref, LANES)
    c0 = col0 // tn
    ops = [a, w]
    specs = [pl.BlockSpec((tm, tk), lambda i, j, k: (i, k)), pl.BlockSpec((tk, tn), lambda i, j, k: (k, c0 + j))]
    if res is not None:
        ops.append(res)
        specs.append(pl.BlockSpec((tm, tn), lambda i, j, k: (i, j)))
    return _mm(name, tuple(ops), specs, _sds((M, N), out_dtype), pl.BlockSpec((tm, tn), lambda i, j, k: (i, j)),
               (M // tm, N // tn, K // tk), NN, (tm, tn), has_res=res is not None, comm=comm)


def _mm_nt(name, dy, w, out_dtype, col0=0, tn_pref=1024, tk_pref=2048, comm=None):
    dy = _stack(dy)
    _, M, Np = dy.shape
    Kw = w.shape[0]
    tm, tn, tk = _tile(M, 1024, 16), _tile(Kw, tn_pref, LANES), _tile(Np, tk_pref, LANES)
    per, c0 = Np // tk, col0 // tk
    return _mm(name, (dy, w),
               [pl.BlockSpec((None, tm, tk), lambda i, j, k: (k // per, i, k % per)),
                pl.BlockSpec((tn, tk), lambda i, j, k: (j, c0 + k))],
               _sds((M, Kw), out_dtype), pl.BlockSpec((tm, tn), lambda i, j, k: (i, j)),
               (M // tm, Kw // tn, dy.shape[0] * per), NT, (tm, tn), comm=comm)


def _mm_tn(name, a, dy, n_total=None, col0=0, tkr_pref=1024, tn_pref=1024, tk_pref=2048, prev=None, comm=None):
    dy = _stack(dy)
    P, M, Np = dy.shape
    Kw = a.shape[1]
    tkr, tn, tk = _tile(Kw, tkr_pref, LANES), _tile(Np, tn_pref, LANES), _tile(M, tk_pref, 16)
    per, c0 = Np // tn, col0 // tn
    ops = [a, dy]
    specs = [pl.BlockSpec((tk, tkr), lambda i, j, k: (k, i)),
             pl.BlockSpec((None, tk, tn), lambda i, j, k: (j // per, k, j % per))]
    aliases = None
    if prev is not None:
        ops.append(prev)
        specs.append(pl.BlockSpec(memory_space=pl.ANY))
        aliases = {2: 0}
    return _mm(name, tuple(ops), specs, _sds((Kw, n_total or P * Np), BF16),
               pl.BlockSpec((tkr, tn), lambda i, j, k: (i, c0 + j)),
               (Kw // tkr, P * per, M // tk), TN, (tkr, tn), aliases=aliases, comm=comm)


def _rmsnorm_fwd(name, x, g, comm=None):
    T, D = x.shape
    tr = _tile(T, 512, 16)
    n_ci, n_co = (len(comm.arrays), len(comm.out_shapes)) if comm else (0, 0)

    def body(*refs):
        x_ref, g_ref, h_ref = refs[0], refs[1], refs[2 + n_ci]
        c_refs = (refs[2:2 + n_ci], refs[3 + n_ci:3 + n_ci + n_co], refs[3 + n_ci + n_co:])
        if comm:
            @pl.when(pl.program_id(0) == 0)
            def _():
                comm.start(*c_refs)

        xf = x_ref[...]
        r = lax.rsqrt(jnp.mean(xf * xf, axis=-1, keepdims=True) + NORM_EPS)
        h_ref[...] = (xf * r * g_ref[...]).astype(h_ref.dtype)

        if comm:
            @pl.when(pl.program_id(0) == T // tr - 1)
            def _():
                comm.finish(*c_refs)

    outs = pl.pallas_call(
        body, grid=(T // tr,),
        in_specs=[pl.BlockSpec((tr, D), lambda i: (i, 0)), pl.BlockSpec((1, D), lambda i: (0, 0))] + _any_specs(n_ci),
        out_specs=[pl.BlockSpec((tr, D), lambda i: (i, 0))] + _any_specs(n_co),
        out_shape=[_sds((T, D), BF16)] + (comm.out_shapes if comm else []),
        scratch_shapes=comm.sems if comm else [],
        name=name, compiler_params=_cp("arbitrary" if comm else "parallel"))(x, g.reshape(1, D), *(comm.arrays if comm else []))
    return outs[0], list(outs[1:])


def _rmsnorm_bwd(name, x, g, dhs, dres, comm=None):
    T, D = x.shape
    tr = _tile(T, 256, 16)
    n_dh = len(dhs)
    n_in = 3 + n_dh
    n_ci, n_co = (len(comm.arrays), len(comm.out_shapes)) if comm else (0, 0)

    def body(*refs):
        x_ref, g_ref = refs[0], refs[1]
        dh_refs = refs[2:2 + n_dh]
        dres_ref = refs[2 + n_dh]
        dx_ref, dxb_ref, dg_ref = refs[n_in + n_ci:n_in + n_ci + 3]
        c_refs = (refs[n_in:n_in + n_ci], refs[n_in + n_ci + 3:n_in + n_ci + 3 + n_co], refs[n_in + n_ci + 3 + n_co:])
        if comm:
            @pl.when(pl.program_id(0) == 0)
            def _():
                comm.start(*c_refs)

        xf = x_ref[...]
        r = lax.rsqrt(jnp.mean(xf * xf, axis=-1, keepdims=True) + NORM_EPS)
        xhat = xf * r
        dh = dh_refs[0][...].astype(F32)
        for q in dh_refs[1:]:
            dh = dh + q[...].astype(F32)
        dy = dh * g_ref[...]
        c = jnp.mean(dy * xhat, axis=-1, keepdims=True)
        dx = dres_ref[...] + r * (dy - xhat * c)
        dx_ref[...] = dx
        dxb_ref[...] = dx.astype(BF16)

        @pl.when(pl.program_id(0) == 0)
        def _():
            dg_ref[...] = jnp.zeros_like(dg_ref)

        dg_ref[...] += jnp.sum(dh * xhat, axis=0, keepdims=True)

        if comm:
            @pl.when(pl.program_id(0) == T // tr - 1)
            def _():
                comm.finish(*c_refs)

    row = pl.BlockSpec((tr, D), lambda i: (i, 0))
    vec = pl.BlockSpec((1, D), lambda i: (0, 0))
    outs = pl.pallas_call(
        body, grid=(T // tr,), in_specs=[row, vec] + [row] * n_dh + [row] + _any_specs(n_ci),
        out_specs=[row, row, vec] + _any_specs(n_co),
        out_shape=[_sds((T, D), F32), _sds((T, D), BF16), _sds((1, D), F32)] + (comm.out_shapes if comm else []),
        scratch_shapes=comm.sems if comm else [],
        name=name, compiler_params=_cp("arbitrary"))(x, g.reshape(1, D), *dhs, dres, *(comm.arrays if comm else []))
    return (outs[0], outs[1], outs[2][0]), list(outs[3:])


def _final_loss(name, x, g, tgt):
    T, D = x.shape
    tr = _tile(T, 256, 16)

    def body(x_ref, g_ref, t_ref, dx_ref, dxb_ref, dg_ref, loss_ref):
        xf = x_ref[...]
        r = lax.rsqrt(jnp.mean(xf * xf, axis=-1, keepdims=True) + NORM_EPS)
        xhat = xf * r
        err = xhat * g_ref[...] - t_ref[...]
        dy = err * (1.0 / D)
        dxh = dy * g_ref[...]
        c = jnp.mean(dxh * xhat, axis=-1, keepdims=True)
        dx = r * (dxh - xhat * c)
        dx_ref[...] = dx
        dxb_ref[...] = dx.astype(BF16)

        @pl.when(pl.program_id(0) == 0)
        def _():
            dg_ref[...] = jnp.zeros_like(dg_ref)
            loss_ref[...] = jnp.zeros_like(loss_ref)

        dg_ref[...] += jnp.sum(dy * xhat, axis=0, keepdims=True)
        loss_ref[...] += 0.5 * jnp.sum(jnp.mean(err * err, axis=-1, keepdims=True), axis=0, keepdims=True)

    row = pl.BlockSpec((tr, D), lambda i: (i, 0))
    vec = pl.BlockSpec((1, D), lambda i: (0, 0))
    dx, dx_b, dg, loss = pl.pallas_call(
        body, grid=(T // tr,), in_specs=[row, vec, row],
        out_specs=[row, row, vec, pl.BlockSpec((1, 1), lambda i: (0, 0))],
        out_shape=[_sds((T, D), F32), _sds((T, D), BF16), _sds((1, D), F32), _sds((1, 1), F32)],
        name=name, compiler_params=_cp("arbitrary"))(x, g.reshape(1, D), tgt)
    return loss[0, 0], dx, dx_b, dg[0]


def _halo_specs(tr, tc, n_rows, col):
    rb = tr // HALO
    last = n_rows // HALO - 1
    return [pl.BlockSpec((tr, tc), lambda *g: (g[-1], col(*g))),
            pl.BlockSpec((HALO, tc), lambda *g: (jnp.maximum(g[-1] * rb - 1, 0), col(*g))),
            pl.BlockSpec((HALO, tc), lambda *g: (jnp.minimum((g[-1] + 1) * rb, last), col(*g)))]


def _ext(cur_ref, prev_ref, next_ref, i, n_i):
    p = prev_ref[...].astype(F32) * (i > 0).astype(F32)
    n = next_ref[...].astype(F32) * (i < n_i - 1).astype(F32)
    return jnp.concatenate([p, cur_ref[...].astype(F32), n], axis=0)


def _shift_dn(x):
    return pltpu.roll(x, 1, axis=0)


def _shift_up(x):
    return pltpu.roll(x, x.shape[0] - 1, axis=0)


def _conv(x, w_ref, b_ref):
    return w_ref[0:1, :] * _shift_dn(x) + w_ref[1:2, :] * x + w_ref[2:3, :] * _shift_up(x) + b_ref[...]


def _mid(x, tr):
    return x[HALO:HALO + tr, :]


def _conv_t(g, w_ref):
    return w_ref[0:1, :] * _shift_up(g) + w_ref[1:2, :] * g + w_ref[2:3, :] * _shift_dn(g)


def _conv_wgrad(acc_ref, g, x, tr, first):
    gm = _mid(g, tr)

    @pl.when(first)
    def _():
        acc_ref[...] = jnp.zeros_like(acc_ref)

    acc_ref[0:1, :] += jnp.sum(gm * _mid(_shift_dn(x), tr), axis=0, keepdims=True)
    acc_ref[1:2, :] += jnp.sum(gm * _mid(x, tr), axis=0, keepdims=True)
    acc_ref[2:3, :] += jnp.sum(gm * _mid(_shift_up(x), tr), axis=0, keepdims=True)
    acc_ref[3:4, :] += jnp.sum(gm, axis=0, keepdims=True)


def _sigmoid(a):
    return 1.0 / (1.0 + jnp.exp(-a))


def _ffn_gate_fwd(name, up, cw, cb, comm=None):
    T, F2 = up.shape
    F = F2 // 2
    tc, tr = _tile(F, 512, LANES), _tile(T, 512, HALO)
    nF, n_i = F // tc, T // tr

    n_ci, n_co = (len(comm.arrays), len(comm.out_shapes)) if comm else (0, 0)

    def body(*refs):
        ac, ap, an, bc, bp, bn, wa, wb, ba, bb = refs[:10]
        o_ref = refs[10 + n_ci]
        c_refs = (refs[10:10 + n_ci], refs[11 + n_ci:11 + n_ci + n_co], refs[11 + n_ci + n_co:])
        j, i = pl.program_id(0), pl.program_id(1)
        if comm:
            @pl.when((j == 0) & (i == 0))
            def _():
                comm.start(*c_refs)

        ua = _mid(_conv(_ext(ac, ap, an, i, n_i), wa, ba), tr)
        ub = _mid(_conv(_ext(bc, bp, bn, i, n_i), wb, bb), tr)
        o_ref[...] = (ua * _sigmoid(ua) * ub).astype(o_ref.dtype)

        if comm:
            @pl.when((j == nF - 1) & (i == n_i - 1))
            def _():
                comm.finish(*c_refs)

    wspec = lambda o: pl.BlockSpec((3, tc), lambda j, i: (0, j + o))
    bspec = lambda o: pl.BlockSpec((1, tc), lambda j, i: (0, j + o))
    sem = ("arbitrary", "arbitrary") if comm else ("parallel", "parallel")
    outs = pl.pallas_call(
        body, grid=(nF, n_i),
        in_specs=_halo_specs(tr, tc, T, lambda j, i: j) + _halo_specs(tr, tc, T, lambda j, i: j + nF)
        + [wspec(0), wspec(nF), bspec(0), bspec(nF)] + _any_specs(n_ci),
        out_specs=[pl.BlockSpec((tr, tc), lambda j, i: (i, j))] + _any_specs(n_co),
        out_shape=[_sds((T, F), BF16)] + (comm.out_shapes if comm else []),
        scratch_shapes=comm.sems if comm else [],
        name=name, compiler_params=_cp(*sem))(up, up, up, up, up, up, cw, cw, cb, cb, *(comm.arrays if comm else []))
    return outs[0], list(outs[1:])


def _ffn_gate_bwd(name, up, dact, cw, cb):
    T, F2 = up.shape
    F = F2 // 2
    tc, tr = _tile(F, 512, LANES), _tile(T, 512, HALO)
    nF, n_i = F // tc, T // tr

    def body(ac, ap, an, bc, bp, bn, dc, dp, dn, wa, wb, ba, bb, o_ref, wga_ref, wgb_ref):
        i = pl.program_id(1)
        xa = _ext(ac, ap, an, i, n_i)
        xb = _ext(bc, bp, bn, i, n_i)
        da = _ext(dc, dp, dn, i, n_i)
        ua = _conv(xa, wa, ba)
        sig = _sigmoid(ua)
        ga = da * _conv(xb, wb, bb) * (sig * (1.0 + ua * (1.0 - sig)))
        o_ref[0] = _mid(_conv_t(ga, wa), tr).astype(o_ref.dtype)
        _conv_wgrad(wga_ref, ga, xa, tr, i == 0)
        gb = da * (ua * sig)
        o_ref[1] = _mid(_conv_t(gb, wb), tr).astype(o_ref.dtype)
        _conv_wgrad(wgb_ref, gb, xb, tr, i == 0)

    wspec = lambda o: pl.BlockSpec((3, tc), lambda j, i: (0, j + o))
    bspec = lambda o: pl.BlockSpec((1, tc), lambda j, i: (0, j + o))
    wg = pl.BlockSpec((8, tc), lambda j, i: (0, j))
    dup, wga, wgb = pl.pallas_call(
        body, grid=(nF, n_i),
        in_specs=_halo_specs(tr, tc, T, lambda j, i: j) + _halo_specs(tr, tc, T, lambda j, i: j + nF)
        + _halo_specs(tr, tc, T, lambda j, i: j) + [wspec(0), wspec(nF), bspec(0), bspec(nF)],
        out_specs=[pl.BlockSpec((2, tr, tc), lambda j, i: (0, i, j)), wg, wg],
        out_shape=[_sds((2, T, F), BF16), _sds((8, F), F32), _sds((8, F), F32)],
        name=name, compiler_params=_cp("parallel", "arbitrary"),
    )(up, up, up, up, up, up, dact, dact, dact, cw, cw, cb, cb)
    return dup, jnp.concatenate([wga, wgb], axis=1)


def _sc_gate_fwd(name, z, cw, cb):
    T, D3 = z.shape
    D = D3 // 3
    tc, tr = _tile(D, 512, LANES), _tile(T, 512, HALO)
    nD, n_i = D // tc, T // tr

    def body(uc, up_, un, gb, cc, cp, cn, w, b, o_ref):
        i = pl.program_id(1)
        cu = _ext(cc, cp, cn, i, n_i) * _ext(uc, up_, un, i, n_i)
        o_ref[...] = (gb[...].astype(F32) * _mid(_conv(cu, w, b), tr)).astype(o_ref.dtype)

    return pl.pallas_call(
        body, grid=(nD, n_i),
        in_specs=_halo_specs(tr, tc, T, lambda j, i: j) + [pl.BlockSpec((tr, tc), lambda j, i: (i, j + nD))]
        + _halo_specs(tr, tc, T, lambda j, i: j + 2 * nD)
        + [pl.BlockSpec((3, tc), lambda j, i: (0, j)), pl.BlockSpec((1, tc), lambda j, i: (0, j))],
        out_specs=pl.BlockSpec((tr, tc), lambda j, i: (i, j)), out_shape=_sds((T, D), BF16),
        name=name, compiler_params=_cp("parallel", "parallel"))(z, z, z, z, z, z, z, cw, cb)


def _sc_gate_bwd(name, z, dy, cw, cb):
    T, D3 = z.shape
    D = D3 // 3
    tc, tr = _tile(D, 512, LANES), _tile(T, 512, HALO)
    nD, n_i = D // tc, T // tr

    def body(uc, up_, un, bc, bp, bn, cc, cp, cn, yc, yp, yn, w, b, o_ref, wg_ref):
        i = pl.program_id(1)
        u = _ext(uc, up_, un, i, n_i)
        gc = _ext(cc, cp, cn, i, n_i)
        cu = gc * u
        g = _ext(yc, yp, yn, i, n_i) * _ext(bc, bp, bn, i, n_i)
        dcu = _mid(_conv_t(g, w), tr)
        o_ref[0] = (dcu * _mid(gc, tr)).astype(o_ref.dtype)
        o_ref[1] = (yc[...].astype(F32) * _mid(_conv(cu, w, b), tr)).astype(o_ref.dtype)
        o_ref[2] = (dcu * _mid(u, tr)).astype(o_ref.dtype)
        _conv_wgrad(wg_ref, g, cu, tr, i == 0)

    hs = lambda o: _halo_specs(tr, tc, T, lambda j, i: j + o)
    return pl.pallas_call(
        body, grid=(nD, n_i),
        in_specs=hs(0) + hs(nD) + hs(2 * nD) + hs(0)
        + [pl.BlockSpec((3, tc), lambda j, i: (0, j)), pl.BlockSpec((1, tc), lambda j, i: (0, j))],
        out_specs=[pl.BlockSpec((3, tr, tc), lambda j, i: (0, i, j)), pl.BlockSpec((8, tc), lambda j, i: (0, j))],
        out_shape=[_sds((3, T, D), BF16), _sds((8, D), F32)],
        name=name, compiler_params=_cp("parallel", "arbitrary"),
    )(z, z, z, z, z, z, z, z, z, dy, dy, dy, cw, cb)


def _slopes(n_heads):
    return jnp.asarray(2.0 ** (-ALIBI_MAX * np.arange(1, n_heads + 1) / n_heads), dtype=F32)


CHAINS = 8


def _nq(L, d):
    return max(1, min(CHAINS if d == 1 else CHAINS // 2, L // LANES // 2))


def _srows(ref, r, start, n, d):
    if d == 1:
        return ref[start:start + n, :]
    return ref[pl.ds(start * d + r, n, stride=d), :]


def _win(p_ref, c_ref, n_ref, r, b, nq):
    lo, hi, top = b * LANES - BAND, b * LANES + LANES + BAND, nq * LANES
    parts = [p_ref[r]] if lo < 0 else []
    parts.append(c_ref[r, max(lo, 0):min(hi, top), :])
    if hi > top:
        parts.append(n_ref[r])
    return parts[0] if len(parts) == 1 else jnp.concatenate(parts, axis=0)


def _nat_win(p_ref, c_ref, n_ref, r, b, nq, d):
    lo, hi, top = b * LANES - BAND, b * LANES + LANES + BAND, nq * LANES
    parts = [_srows(p_ref, r, 0, BAND, d)] if lo < 0 else []
    parts.append(_srows(c_ref, r, max(lo, 0), min(hi, top) - max(lo, 0), d))
    if hi > top:
        parts.append(_srows(n_ref, r, 0, BAND, d))
    return parts[0] if len(parts) == 1 else jnp.concatenate(parts, axis=0)


def _biases(slope, d, n, n_steps, nq, q_rows, k_rows, q0, k0):
    qi = lax.broadcasted_iota(jnp.int32, (q_rows, k_rows), 0) + q0
    kj = lax.broadcasted_iota(jnp.int32, (q_rows, k_rows), 1) + k0
    dist = jnp.abs(kj - qi)
    base = jnp.where(dist <= BAND, -slope * (dist * d).astype(F32), NEG_INF)
    out = []
    for b in range(nq):
        t = base
        if b == 0:
            t = jnp.where((n == 0) & ((kj < 0) | (qi < 0)), NEG_INF, t)
        if b == nq - 1:
            t = jnp.where((n == n_steps - 1) & ((kj >= LANES) | (qi >= LANES)), NEG_INF, t)
        out.append(t)
    return out


def _win_specs(d, H, col, nq, L):
    return [pl.BlockSpec((d, BAND, LANES), lambda h, n: (0, jnp.maximum(2 * nq * n - 1, 0), col * H + h)),
            pl.BlockSpec((d, nq * LANES, LANES), lambda h, n: (0, n, col * H + h)),
            pl.BlockSpec((d, BAND, LANES), lambda h, n: (0, jnp.minimum(2 * nq * (n + 1), L // BAND - 1), col * H + h))]


def _nat_specs(d, nq, L):
    return [pl.BlockSpec((BAND * d, LANES), lambda h, n: (jnp.maximum(2 * nq * n - 1, 0), h)),
            pl.BlockSpec((nq * LANES * d, LANES), lambda h, n: (n, h)),
            pl.BlockSpec((BAND * d, LANES), lambda h, n: (jnp.minimum(2 * nq * (n + 1), L // BAND - 1), h))]


def _over_residues(d, nq, per_r):
    if d == 1:
        per_r(0, 0)
    else:
        lax.fori_loop(0, d, per_r, 0, unroll=min(d, max(1, CHAINS // nq)))


def _attn_fwd(name, qkv, d, H):
    T = qkv.shape[0]
    D = H * HEAD_DIM
    L = T // d
    nq = _nq(L, d)
    n_steps = L // (nq * LANES)
    scale = HEAD_DIM ** -0.5
    q3 = qkv.reshape(d, L, 3 * D)

    def body(s_ref, q_ref, kp, kc, kn, vp, vc, vn, o_ref, l_ref):
        h, n = pl.program_id(0), pl.program_id(1)
        bias = _biases(s_ref[h], d, n, n_steps, nq, LANES, 2 * LANES, 0, -BAND)

        def per_r(r, carry):
            for b in range(nq):
                k, v = _win(kp, kc, kn, r, b, nq), _win(vp, vc, vn, r, b, nq)
                s = lax.dot_general(q_ref[r, b * LANES:(b + 1) * LANES, :], k, NT, preferred_element_type=F32) * scale + bias[b]
                m = jnp.max(s, axis=1, keepdims=True)
                p = jnp.exp(s - m)
                den = jnp.sum(p, axis=1, keepdims=True)
                o = lax.dot_general(p.astype(BF16), v, NN, preferred_element_type=F32) / den
                lse = jnp.broadcast_to(m + jnp.log(den), (LANES, LANES))
                if d == 1:
                    o_ref[b * LANES:(b + 1) * LANES, :] = o
                    l_ref[b * LANES:(b + 1) * LANES, :] = lse
                else:
                    o_ref[pl.ds(b * LANES * d + r, LANES, stride=d), :] = o
                    l_ref[pl.ds(b * LANES * d + r, LANES, stride=d), :] = lse
            return carry

        _over_residues(d, nq, per_r)

    out = pl.BlockSpec((nq * LANES * d, LANES), lambda h, n: (n, h))
    return pl.pallas_call(
        body, grid=(H, n_steps),
        in_specs=[pl.BlockSpec(memory_space=pltpu.SMEM), pl.BlockSpec((d, nq * LANES, LANES), lambda h, n: (0, n, h))]
        + _win_specs(d, H, 1, nq, L) + _win_specs(d, H, 2, nq, L),
        out_specs=[out, out], out_shape=[_sds((T, D), F32), _sds((T, D), F32)],
        name=name, compiler_params=_cp("parallel", "parallel"))(_slopes(H), q3, q3, q3, q3, q3, q3, q3)


def _attn_combine(name, outs, lses):
    T, D = outs[0].shape
    tr, tc = _tile(T, 512, 16), _tile(D, 512, LANES)

    def body(o0, o1, o2, l0, l1, l2, ob_ref, l_ref):
        a0, a1, a2 = l0[...], l1[...], l2[...]
        m = jnp.maximum(jnp.maximum(a0, a1), a2)
        e0, e1, e2 = jnp.exp(a0 - m), jnp.exp(a1 - m), jnp.exp(a2 - m)
        z = e0 + e1 + e2
        ob_ref[...] = ((e0 * o0[...] + e1 * o1[...] + e2 * o2[...]) / z).astype(BF16)
        l_ref[...] = m + jnp.log(z)

    blk = pl.BlockSpec((tr, tc), lambda i, j: (i, j))
    return pl.pallas_call(
        body, grid=(T // tr, D // tc), in_specs=[blk] * 6, out_specs=[blk] * 2,
        out_shape=[_sds((T, D), BF16), _sds((T, D), F32)],
        name=name, compiler_params=_cp("parallel", "parallel"))(*outs, *lses)


def _attn_stats(name, do, o, lse):
    T, D = do.shape
    tr = _tile(T, 1024, 16)

    def body(a, b, l, o_ref):
        delta = jnp.broadcast_to(jnp.sum(a[...] * b[...].astype(F32), axis=1, keepdims=True), o_ref.shape)
        lane = lax.broadcasted_iota(jnp.int32, o_ref.shape, 1)
        o_ref[...] = jnp.where(lane < BAND, l[...], delta)

    blk = pl.BlockSpec((tr, LANES), lambda i, j: (i, j))
    return pl.pallas_call(body, grid=(T // tr, D // LANES), in_specs=[blk, blk, blk], out_specs=blk,
                          out_shape=_sds((T, D), F32), name=name, compiler_params=_cp("parallel", "parallel"))(do, o, lse)


def _attn_bwd(name, qkv, do, stats, d, H):
    T = qkv.shape[0]
    D = H * HEAD_DIM
    L = T // d
    nq = _nq(L, d)
    n_steps = L // (nq * LANES)
    scale = HEAD_DIM ** -0.5
    q3 = qkv.reshape(d, L, 3 * D)
    mid = slice(BAND, BAND + LANES)

    def body(s_ref, qp, qc, qn, kp, kc, kn, vp, vc, vn, gp, gc, gn, tp, tc_, tn_, o_ref):
        h, n = pl.program_id(0), pl.program_id(1)
        bias_q = _biases(s_ref[h], d, n, n_steps, nq, LANES, 2 * LANES, 0, -BAND)
        bias_k = _biases(s_ref[h], d, n, n_steps, nq, 2 * LANES, LANES, -BAND, 0)

        def per_r(r, carry):
            for b in range(nq):
                rows = slice(b * LANES, (b + 1) * LANES)
                q_w, k_w, v_w = _win(qp, qc, qn, r, b, nq), _win(kp, kc, kn, r, b, nq), _win(vp, vc, vn, r, b, nq)
                g_w = _nat_win(gp, gc, gn, r, b, nq, d)
                t_w = _nat_win(tp, tc_, tn_, r, b, nq, d)
                g_b = g_w.astype(BF16)
                q_c, k_c, v_c, g_c, t_c = q_w[mid], k_w[mid], v_w[mid], g_b[mid], t_w[mid]
                s = lax.dot_general(q_c, k_w, NT, preferred_element_type=F32) * scale + bias_q[b]
                p = jnp.exp(s - t_c[:, 0:1])
                dp = lax.dot_general(g_c, v_w, NT, preferred_element_type=F32)
                ds = p * (dp - t_c[:, BAND:BAND + 1])
                o_ref[0, r, rows, :] = (lax.dot_general(ds.astype(BF16), k_w, NN, preferred_element_type=F32) * scale).astype(BF16)
                s2 = lax.dot_general(q_w, k_c, NT, preferred_element_type=F32) * scale + bias_k[b]
                p2 = jnp.exp(s2 - t_w[:, 0:1])
                o_ref[2, r, rows, :] = lax.dot_general(p2.astype(BF16), g_b, TN, preferred_element_type=F32).astype(BF16)
                dp2 = lax.dot_general(g_b, v_c, NT, preferred_element_type=F32)
                ds2 = p2 * (dp2 - t_w[:, BAND:BAND + 1])
                o_ref[1, r, rows, :] = (lax.dot_general(ds2.astype(BF16), q_w, TN, preferred_element_type=F32) * scale).astype(BF16)
            return carry

        _over_residues(d, nq, per_r)

    dqkv = pl.pallas_call(
        body, grid=(H, n_steps),
        in_specs=[pl.BlockSpec(memory_space=pltpu.SMEM)]
        + _win_specs(d, H, 0, nq, L) + _win_specs(d, H, 1, nq, L) + _win_specs(d, H, 2, nq, L)
        + _nat_specs(d, nq, L) + _nat_specs(d, nq, L),
        out_specs=pl.BlockSpec((3, d, nq * LANES, LANES), lambda h, n: (0, 0, n, h)),
        out_shape=_sds((3, d, L, D), BF16),
        name=name, compiler_params=_cp("parallel", "parallel"),
    )(_slopes(H), *([q3] * 9), *([do] * 3), *([stats] * 3))
    return dqkv.reshape(3, T, D)


def _to_group_order(a, d):
    if d == 1:
        return a
    T, C = a.shape
    return a.reshape(T // d, d, C).swapaxes(0, 1).reshape(T, C)


def _from_group_order(a, d):
    if d == 1:
        return a
    T, C = a.shape
    return a.reshape(d, T // d, C).swapaxes(0, 1).reshape(T, C)


def _fwd_bwd(x, tgt, S, ex):
    T, D = x.shape
    H = D // HEAD_DIM
    G3 = 3 * D

    def mm(fn, *args, rides=(), **kw):
        out, extra = fn(*args, comm=_join([getattr(ex, kind)(keys) for kind, keys in rides]), **kw)
        at = 0
        for kind, keys in rides:
            getattr(ex, kind + "_done")(keys, extra[at:at + len(keys)])
            at += len(keys)
        return out

    def ffn_fwd(l, xin, rides):
        hf = mm(_rmsnorm_fwd, f"ffn_norm{l}", xin, S["ffn_g"][l])
        up = mm(_mm_nn, f"ffn_up{l}", hf, ex.w(f"up{l}"), None, BF16, rides=rides[0])
        act = mm(_ffn_gate_fwd, f"ffn_gate{l}", up, ffn_cw[l], S["ffn_cb"][l][None], rides=rides[1])
        return hf, up, act, mm(_mm_nn, f"ffn_down{l}", act, ex.w(f"dn{l}"), xin, F32, tk_pref=2816, rides=rides[2])

    def ffn_bwd(l, xin, hf, up, act, dxo, dxo_b, rides):
        dact = mm(_mm_nt, f"ffn_down_dx{l}", dxo_b, ex.w(f"dn{l}"), BF16, tn_pref=1408, rides=rides[0])
        ex.grad(f"dn{l}", mm(_mm_tn, f"ffn_down_dw{l}", act, dxo_b, tkr_pref=1408))
        dup, cg = _ffn_gate_bwd(f"ffn_gate_bwd{l}", up, dact, ffn_cw[l], S["ffn_cb"][l][None])
        dhf = mm(_mm_nt, f"ffn_up_dx{l}", dup, ex.w(f"up{l}"), BF16, tk_pref=2816, rides=rides[1])
        ex.grad(f"up{l}", mm(_mm_tn, f"ffn_up_dw{l}", hf, dup, tn_pref=2816, tk_pref=1024, rides=rides[2]))
        dx, dx_b, dg = mm(_rmsnorm_bwd, f"ffn_norm_bwd{l}", xin, S["ffn_g"][l], [dhf], dxo)
        return dx, dx_b, dg, cg

    h0 = mm(_rmsnorm_fwd, "mix_norm0", x, S["mix_g"][0], rides=[("ag", ["in", "scw", "fcw"])])
    sc_cw = ex.w("scw")
    ffn_cw = ex.w("fcw").reshape(S["ffn_cb"].shape[0], 3, -1)
    z = mm(_mm_nn, "sc_in", h0, ex.w("in"), None, BF16, rides=[("ag", ["sco", "up0"])])
    y = _sc_gate_fwd("sc_gate", z, sc_cw, S["sc_cb"][None])
    x1 = mm(_mm_nn, "sc_out", y, ex.w("sco"), x, F32, rides=[("ag", ["dn0"])])
    hf0, up0, act0, x2 = ffn_fwd(0, x1, [[("ag", ["qkv"])], [("ag", ["up1"])], [("ag", ["ao", "dn1"])]])
    h1 = mm(_rmsnorm_fwd, "mix_norm1", x2, S["mix_g"][1])
    hd, qkv, outs, lses = [], [], [], []
    for g, d in enumerate(DILATIONS):
        hd.append(_to_group_order(h1, d))
        qkv.append(mm(_mm_nn, f"attn_qkv{g}", hd[g], ex.w("qkv"), None, BF16, col0=g * G3, n_cols=G3))
        o_g, l_g = _attn_fwd(f"attn_fwd{g}", qkv[g], d, H)
        outs.append(o_g)
        lses.append(l_g)
    o_b, lse = _attn_combine("attn_combine", outs, lses)
    x3 = mm(_mm_nn, "attn_out", o_b, ex.w("ao"), x2, F32)
    hf1, up1, act1, x4 = ffn_fwd(1, x3, [(), (), ()])
    loss, dx4, dx4_b, dg_fin = _final_loss("final_loss", x4, S["fin_g"], tgt)

    dx3, dx3_b, dg_f1, cg1 = ffn_bwd(1, x3, hf1, up1, act1, dx4, dx4_b,
                                     [(), [("pair", ["dn1"])], [("chip", ["dn1"])]])
    do = mm(_mm_nt, "attn_out_dx", dx3_b, ex.w("ao"), F32, rides=[("pair", ["up1"])])
    ex.grad("ao", mm(_mm_tn, "attn_out_dw", o_b, dx3_b))
    stats = _attn_stats("attn_stats", do, o_b, lse)
    dhs, dw_qkv = [], None
    qkv_rides = [[("chip", ["up1"]), ("pair", ["ao"])], [("chip", ["ao"])], ()]
    for g, d in enumerate(DILATIONS):
        dqkv = _attn_bwd(f"attn_bwd{g}", qkv[g], do, stats, d, H)
        dhs.append(_from_group_order(mm(_mm_nt, f"attn_qkv_dx{g}", dqkv, ex.w("qkv"), BF16, col0=g * G3,
                                        rides=qkv_rides[g]), d))
        dw_qkv = mm(_mm_tn, f"attn_qkv_dw{g}", hd[g], dqkv, n_total=len(DILATIONS) * G3, col0=g * G3, prev=dw_qkv)
    ex.grad("qkv", dw_qkv)
    dx2, dx2_b, dg_m1 = mm(_rmsnorm_bwd, "mix_norm_bwd1", x2, S["mix_g"][1], dhs, dx3)
    dx1, dx1_b, dg_f0, cg0 = ffn_bwd(0, x1, hf0, up0, act0, dx2, dx2_b,
                                     [[("pair", ["qkv"])], [("chip", ["qkv"]), ("pair", ["dn0"])], [("chip", ["dn0"])]])
    dy = mm(_mm_nt, "sc_out_dx", dx1_b, ex.w("sco"), BF16, rides=[("pair", ["up0"])])
    ex.grad("sco", mm(_mm_tn, "sc_out_dw", y, dx1_b))
    dz, cg_sc = _sc_gate_bwd("sc_gate_bwd", z, dy, sc_cw, S["sc_cb"][None])
    ex.grad("in", mm(_mm_tn, "sc_in_dw", h0, dz, rides=[("chip", ["up0"]), ("pair", ["sco"])]))
    dh0 = mm(_mm_nt, "sc_in_dx", dz, ex.w("in"), BF16, rides=[("chip", ["sco"]), ("pair", ["in"])])
    dx0, _, dg_m0 = mm(_rmsnorm_bwd, "mix_norm_bwd0", x, S["mix_g"][0], [dh0], dx1, rides=[("chip", ["in"])])

    dS = {"mix_g": jnp.stack([dg_m0, dg_m1]), "ffn_g": jnp.stack([dg_f0, dg_f1]), "fin_g": dg_fin,
          "sc_cw": cg_sc[0:3], "sc_cb": cg_sc[3], "ffn_cw": jnp.stack([cg0[0:3], cg1[0:3]]),
          "ffn_cb": jnp.stack([cg0[3], cg1[3]])}
    return loss, dx0, dS


def _place():
    x, y, c = lax.axis_index("x"), lax.axis_index("y"), lax.axis_index("c")
    return x, y, c, [(1 - x, y), (x, 1 - y), (1 - x, 1 - y)]


def _block(ref, s, shape, axis):
    R, C = shape
    if axis == 0:
        return ref.at[pl.ds(pl.multiple_of(s * R, HALO), R), :]
    return ref.at[:, pl.ds(pl.multiple_of(s * C, LANES), C)]


def _whole(shape, axis):
    return (shape[0] * N_DEV, shape[1]) if axis == 0 else (shape[0], shape[1] * N_DEV)


def _ag_stage(arrs, axes):
    n = len(arrs)

    def plan(ins, outs, sems):
        send_sems, recv_sems, local_sems = sems
        x, y, c, chips = _place()
        me, sibling = 4 * x + 2 * y + c, (x, y, 1 - c)

        def copy(a, k, blk, to, src=None):
            dst = _block(outs[a], blk, arrs[a].shape, axes[a])
            return pltpu.make_async_remote_copy(src_ref=dst if src is None else src, dst_ref=dst,
                                                send_sem=send_sems.at[a, k], recv_sem=recv_sems.at[a, k],
                                                device_id=to, device_id_type=MESH)

        mine = [pltpu.make_async_copy(ins[a], _block(outs[a], me, arrs[a].shape, axes[a]), local_sems.at[a])
                for a in range(n)]
        first = []
        for a in range(n):
            first.append(copy(a, 0, me, sibling, src=ins[a]))
            first += [copy(a, 1 + j, me, (*chip, c), src=ins[a]) for j, chip in enumerate(chips)]
        return x, y, c, chips, sibling, copy, mine, first

    def start(ins, outs, sems):
        *_, mine, first = plan(ins, outs, sems)
        for cp in mine + first:
            cp.start()

    def finish(ins, outs, sems):
        x, y, c, chips, sibling, copy, mine, first = plan(ins, outs, sems)
        passed = []
        for j, (px, py) in enumerate(chips):
            for a in range(n):
                blk = 4 * px + 2 * py + c
                copy(a, 1 + j, blk, sibling).wait_recv()
                passed.append(copy(a, 4 + j, blk, sibling))
                passed[-1].start()
        for a in range(n):
            copy(a, 0, 4 * x + 2 * y + 1 - c, sibling).wait_recv()
            for j, (px, py) in enumerate(chips):
                copy(a, 4 + j, 4 * px + 2 * py + 1 - c, sibling).wait_recv()
        for cp in first + passed:
            cp.wait_send()
        for cp in mine:
            cp.wait()

    return _Stage(list(arrs), [_sds(_whole(a.shape, ax), a.dtype) for a, ax in zip(arrs, axes)],
                  [pltpu.SemaphoreType.DMA((n, 7)), pltpu.SemaphoreType.DMA((n, 7)), pltpu.SemaphoreType.DMA((n,))],
                  start, finish)


def _pair_stage(dws, shapes, axes):
    n = len(dws)

    def copies(ins, outs, sems):
        send_sems, recv_sems = sems
        x, y, c, _ = _place()
        return [pltpu.make_async_remote_copy(src_ref=_block(ins[a], 2 * k + 1 - c, shapes[a], axes[a]),
                                             dst_ref=outs[a].at[k],
                                             send_sem=send_sems.at[a, k], recv_sem=recv_sems.at[a, k],
                                             device_id=(x, y, 1 - c), device_id_type=MESH)
                for a in range(n) for k in range(4)]

    def start(ins, outs, sems):
        for cp in copies(ins, outs, sems):
            cp.start()

    def finish(ins, outs, sems):
        for cp in copies(ins, outs, sems):
            cp.wait()

    return _Stage(list(dws), [_sds((4,) + tuple(s), a.dtype) for a, s in zip(dws, shapes)],
                  [pltpu.SemaphoreType.DMA((n, 4)), pltpu.SemaphoreType.DMA((n, 4))], start, finish)


def _rs_pair_add(name, dw, got, c_arr, axis):
    _, R, C = got.shape
    tr, tc = _tile(R, 512, 16), _tile(C, 1536, LANES)
    per = R // tr if axis == 0 else C // tc

    def body(c_ref, a_ref, b_ref, o_ref):
        o_ref[...] = (a_ref[...].astype(F32) + b_ref[...].astype(F32)).astype(o_ref.dtype)

    if axis == 0:
        mine = pl.BlockSpec((tr, tc), lambda k, i, j, c_ref: ((2 * k + c_ref[0]) * per + i, j))
    else:
        mine = pl.BlockSpec((tr, tc), lambda k, i, j, c_ref: (i, (2 * k + c_ref[0]) * per + j))
    return pl.pallas_call(
        body,
        grid_spec=pltpu.PrefetchScalarGridSpec(
            num_scalar_prefetch=1, grid=(4, R // tr, C // tc),
            in_specs=[mine, pl.BlockSpec((None, tr, tc), lambda k, i, j, c_ref: (k, i, j))],
            out_specs=pl.BlockSpec((None, tr, tc), lambda k, i, j, c_ref: (k, i, j))),
        out_shape=_sds((4, R, C), BF16), name=name,
        compiler_params=_cp("parallel", "parallel", "parallel"))(c_arr, dw, got)


def _chip_stage(parts):
    n = len(parts)

    def plan(ins, outs, sems):
        send_sems, recv_sems, local_sems = sems
        x, y, c, chips = _place()
        my_chip = 2 * x + y
        mine = [pltpu.make_async_copy(ins[a].at[my_chip], outs[a].at[my_chip], local_sems.at[a]) for a in range(n)]
        sends = [pltpu.make_async_remote_copy(src_ref=ins[a].at[2 * px + py], dst_ref=outs[a].at[my_chip],
                                              send_sem=send_sems.at[a, j], recv_sem=recv_sems.at[a, j],
                                              device_id=(px, py, c), device_id_type=MESH)
                 for a in range(n) for j, (px, py) in enumerate(chips)]
        arrivals = lambda: [pltpu.make_async_remote_copy(src_ref=ins[a].at[my_chip], dst_ref=outs[a].at[2 * px + py],
                                                         send_sem=send_sems.at[a, j], recv_sem=recv_sems.at[a, j],
                                                         device_id=(px, py, c), device_id_type=MESH)
                            for a in range(n) for j, (px, py) in enumerate(chips)]
        return mine, sends, arrivals

    def start(ins, outs, sems):
        mine, sends, _ = plan(ins, outs, sems)
        for cp in mine + sends:
            cp.start()

    def finish(ins, outs, sems):
        mine, sends, arrivals = plan(ins, outs, sems)
        for cp in arrivals():
            cp.wait_recv()
        for cp in sends:
            cp.wait_send()
        for cp in mine:
            cp.wait()

    return _Stage(list(parts), [_sds(a.shape, a.dtype) for a in parts],
                  [pltpu.SemaphoreType.DMA((n, 3)), pltpu.SemaphoreType.DMA((n, 3)), pltpu.SemaphoreType.DMA((n,))],
                  start, finish)


class _Exchange:
    ROW_SHARDED = ("sco", "ao", "dn0", "dn1")

    def __init__(self, shards, c_arr):
        self.sh, self.c_arr = shards, c_arr
        self.W, self.dw, self.parts, self.sums = {}, {}, {}, {}

    def axis(self, key):
        return 0 if key in self.ROW_SHARDED else 1

    def w(self, key):
        return self.W[key]

    def ag(self, keys):
        return _ag_stage([self.sh[k] for k in keys], [self.axis(k) for k in keys])

    def ag_done(self, keys, outs):
        self.W.update(zip(keys, outs))

    def grad(self, key, dw):
        self.dw[key] = dw

    def pair(self, keys):
        return _pair_stage([self.dw[k] for k in keys], [self.sh[k].shape for k in keys], [self.axis(k) for k in keys])

    def pair_done(self, keys, outs):
        for k, got in zip(keys, outs):
            self.parts[k] = _rs_pair_add(f"rs_add_{k}", self.dw[k], got, self.c_arr, self.axis(k))

    def chip(self, keys):
        return _chip_stage([self.parts[k] for k in keys])

    def chip_done(self, keys, outs):
        self.sums.update(zip(keys, outs))


def _sum_slots(name, a):
    _, rows, _ = a.shape

    def body(a_ref, o_ref):
        s = a_ref[0]
        for k in range(1, N_DEV):
            s = s + a_ref[k]
        o_ref[...] = s

    return pl.pallas_call(body, out_shape=_sds((rows, LANES), F32), name=name)(a)


def _cast_bf16(name, w3, l):
    _, R, C = w3.shape
    tr, tc = _tile(R, 512, 16), _tile(C, 1536, LANES)

    def body(w_ref, o_ref):
        o_ref[...] = w_ref[...].astype(BF16)

    return pl.pallas_call(
        body, grid=(R // tr, C // tc), in_specs=[pl.BlockSpec((None, tr, tc), lambda i, j: (l, i, j))],
        out_specs=pl.BlockSpec((tr, tc), lambda i, j: (i, j)), out_shape=_sds((R, C), BF16),
        name=name, compiler_params=_cp("parallel", "parallel"))(w3)


def _adamw(name, g_slots, w3, m3, v3, l, prev):
    n_slots, R, C = g_slots.shape
    tr, tc = _tile(R, 256, 8), _tile(C, 1536, LANES)
    c1, c2 = 1.0 - ADAM_B1 ** ADAM_STEP, 1.0 - ADAM_B2 ** ADAM_STEP

    def body(g_ref, w_ref, m_ref, v_ref, *rest):
        og, od, om, ov = rest[-4:]
        g = g_ref[0].astype(F32)
        for k in range(1, n_slots):
            g = g + g_ref[k].astype(F32)
        m = ADAM_B1 * m_ref[...] + (1.0 - ADAM_B1) * g
        v = ADAM_B2 * v_ref[...] + (1.0 - ADAM_B2) * (g * g)
        og[...] = g
        om[...] = m
        ov[...] = v
        od[...] = -ADAM_LR * ((m / c1) / (jnp.sqrt(v / c2) + ADAM_EPS) + ADAM_WD * w_ref[...])

    lay = pl.BlockSpec((None, tr, tc), lambda i, j: (l, i, j))
    ops = [g_slots, w3, m3, v3]
    specs = [pl.BlockSpec((n_slots, tr, tc), lambda i, j: (0, i, j)), lay, lay, lay]
    aliases = {}
    if prev is not None:
        ops += list(prev)
        specs += _any_specs(4)
        aliases = {4 + k: k for k in range(4)}
    return pl.pallas_call(
        body, grid=(R // tr, C // tc), in_specs=specs, out_specs=[lay] * 4,
        out_shape=[_sds(w3.shape, F32)] * 4, input_output_aliases=aliases,
        name=name, compiler_params=_cp("parallel", "parallel"))(*ops)


def _pack(parts):
    flat = jnp.concatenate([p.reshape(-1) for p in parts])
    pad = (-flat.shape[0]) % (HALO * LANES)
    return jnp.pad(flat, (0, pad)).reshape(-1, LANES)


def _unpack(packed, shapes):
    flat = packed.reshape(-1)
    out, at = [], 0
    for s in shapes:
        n = int(np.prod(s))
        out.append(flat[at:at + n].reshape(s))
        at += n
    return out


def kernel(x, mix_norm_g, ffn_norm_g, final_norm_g, sc_w_in, sc_conv_w, sc_conv_b, sc_w_out, attn_w_qkv, attn_w_out, ffn_w_up, ffn_conv_w, ffn_conv_b, ffn_w_down, loss_target, m_mix_norm_g, m_ffn_norm_g, m_final_norm_g, m_sc_w_in, m_sc_conv_w, m_sc_conv_b, m_sc_w_out, m_attn_w_qkv, m_attn_w_out, m_ffn_w_up, m_ffn_conv_w, m_ffn_conv_b, m_ffn_w_down, v_mix_norm_g, v_ffn_norm_g, v_final_norm_g, v_sc_w_in, v_sc_conv_w, v_sc_conv_b, v_sc_w_out, v_attn_w_qkv, v_attn_w_out, v_ffn_w_up, v_ffn_conv_w, v_ffn_conv_b, v_ffn_w_down):
    n_layers = ffn_w_up.shape[0]
    me = 4 * lax.axis_index("x") + 2 * lax.axis_index("y") + lax.axis_index("c")
    c_arr = lax.axis_index("c").astype(jnp.int32).reshape(1)

    big = [("in", sc_w_in, 0), ("sco", sc_w_out, 0), ("qkv", attn_w_qkv, 0), ("ao", attn_w_out, 0)]
    big += [(f"up{l}", ffn_w_up, l) for l in range(n_layers)] + [(f"dn{l}", ffn_w_down, l) for l in range(n_layers)]
    shards = {nm: _cast_bf16(f"cast_{nm}", w, l) for nm, w, l in big}
    shards["scw"] = sc_conv_w.reshape(-1, sc_conv_w.shape[-1])
    shards["fcw"] = ffn_conv_w.reshape(-1, ffn_conv_w.shape[-1])
    ex = _Exchange(shards, c_arr)
    S = {"mix_g": mix_norm_g, "ffn_g": ffn_norm_g, "fin_g": final_norm_g, "sc_cb": sc_conv_b[0], "ffn_cb": ffn_conv_b}

    loss_part, grad_x, dS = _fwd_bwd(x[0], loss_target[0], S, ex)

    small_names = ["mix_g", "ffn_g", "fin_g", "sc_cb", "ffn_cb", "sc_cw", "ffn_cw"]
    small_parts = [dS[k] for k in small_names] + [loss_part.reshape(1)]
    small_mine = _pack(small_parts)
    small_all, = _run_stage("gather_small", _ag_stage([small_mine], [0]))
    small_sum = _sum_slots("sum_small", small_all.reshape((N_DEV,) + small_mine.shape))
    g_mix, g_ffn, g_fin, g_scb, g_fcb, g_scw, g_fcw, loss = _unpack(small_sum, [p.shape for p in small_parts])
    g_scw = lax.dynamic_slice_in_dim(g_scw, me * sc_conv_w.shape[-1], sc_conv_w.shape[-1], axis=-1)[None]
    g_fcw = lax.dynamic_slice_in_dim(g_fcw, me * ffn_conv_w.shape[-1], ffn_conv_w.shape[-1], axis=-1)
    g_scb = g_scb[None]
    small_g = [g_mix, g_ffn, g_fin, g_scw, g_scb, g_fcw, g_fcb]
    small_w = [mix_norm_g, ffn_norm_g, final_norm_g, sc_conv_w, sc_conv_b, ffn_conv_w, ffn_conv_b]
    small_m = [m_mix_norm_g, m_ffn_norm_g, m_final_norm_g, m_sc_conv_w, m_sc_conv_b, m_ffn_conv_w, m_ffn_conv_b]
    small_v = [v_mix_norm_g, v_ffn_norm_g, v_final_norm_g, v_sc_conv_w, v_sc_conv_b, v_ffn_conv_w, v_ffn_conv_b]
    small_out = _adamw("adamw_small", _pack(small_g)[None], _pack(small_w)[None], _pack(small_m)[None],
                       _pack(small_v)[None], 0, None)
    small_shapes = [w.shape for w in small_w]
    sg, sd, sm, sv = [_unpack(o[0], small_shapes) for o in small_out]

    moments = {"in": (m_sc_w_in, v_sc_w_in), "sco": (m_sc_w_out, v_sc_w_out), "qkv": (m_attn_w_qkv, v_attn_w_qkv),
               "ao": (m_attn_w_out, v_attn_w_out), "up": (m_ffn_w_up, v_ffn_w_up), "dn": (m_ffn_w_down, v_ffn_w_down)}
    upd = {}
    for nm, w, l in big:
        key = nm.rstrip("0123456789")
        upd[key] = _adamw(f"adamw_{nm}", ex.sums[nm], w, moments[key][0], moments[key][1], l, upd.get(key))

    def leaves(k):
        return [sg, sd, sm, sv][k][0:3] + [upd["in"][k], [sg, sd, sm, sv][k][3], [sg, sd, sm, sv][k][4], upd["sco"][k],
                                          upd["qkv"][k], upd["ao"][k], upd["up"][k], [sg, sd, sm, sv][k][5],
                                          [sg, sd, sm, sv][k][6], upd["dn"][k]]

    return (loss.reshape(()), grad_x[None], *leaves(0), *leaves(1), *leaves(2), *leaves(3))
```

```python
import math

import numpy as np
import jax
import jax.numpy as jnp
from jax import lax
from jax.experimental import pallas as pl
from jax.experimental.pallas import tpu as pltpu

F32 = jnp.float32
BF16 = jnp.bfloat16
MESH = pl.DeviceIdType.MESH

HEAD_DIM = 128
DILATED_GROUPS = ((128, 1), (512, 4), (2048, 16))
DILATIONS = tuple(d for _, d in DILATED_GROUPS)
BAND = (DILATED_GROUPS[0][0] // 2) // DILATED_GROUPS[0][1]
assert all((w // 2) // d == BAND for w, d in DILATED_GROUPS)
NORM_EPS = 1e-5
ALIBI_MAX = 8.0
NEG_INF = -1e30
ADAM_LR, ADAM_B1, ADAM_B2, ADAM_EPS, ADAM_WD, ADAM_STEP = 0.001, 0.9, 0.999, 1e-08, 0.01, 10

N_DEV = 8
LANES = 128
HALO = 16
VMEM_LIMIT = 56 * 1024 * 1024


def _cp(*sem):
    return pltpu.CompilerParams(dimension_semantics=sem, vmem_limit_bytes=VMEM_LIMIT)


def _tile(n, pref, mult):
    t = (min(n, pref) // mult) * mult
    while t >= mult:
        if n % t == 0:
            return t
        t -= mult
    return n


def _sds(shape, dtype):
    return jax.ShapeDtypeStruct(shape, dtype)


def _any_specs(n):
    return [pl.BlockSpec(memory_space=pl.ANY)] * n


class _Stage:
    def __init__(self, arrays, out_shapes, sems, start, finish):
        self.arrays, self.out_shapes, self.sems, self.start, self.finish = arrays, out_shapes, sems, start, finish


def _join(stages):
    stages = [s for s in stages if s is not None]
    if not stages:
        return None

    def split(refs, count):
        out, at = [], 0
        for s in stages:
            out.append(refs[at:at + count(s)])
            at += count(s)
        return out

    def each(which):
        def run(ins, outs, sems):
            parts = zip(split(ins, lambda s: len(s.arrays)), split(outs, lambda s: len(s.out_shapes)),
                        split(sems, lambda s: len(s.sems)))
            for s, (i, o, m) in zip(stages, parts):
                getattr(s, which)(i, o, m)
        return run

    return _Stage(sum([s.arrays for s in stages], []), sum([s.out_shapes for s in stages], []),
                  sum([s.sems for s in stages], []), each("start"), each("finish"))


def _run_stage(name, st):
    n, m = len(st.arrays), len(st.out_shapes)

    def body(*refs):
        ins, outs, sems = refs[:n], refs[n:n + m], refs[n + m:]
        st.start(ins, outs, sems)
        st.finish(ins, outs, sems)

    return pl.pallas_call(body, in_specs=_any_specs(n), out_specs=_any_specs(m), out_shape=st.out_shapes,
                          scratch_shapes=st.sems, name=name)(*st.arrays)


NN = (((1,), (0,)), ((), ()))
NT = (((1,), (1,)), ((), ()))
TN = (((0,), (0,)), ((), ()))


def _mm(name, operands, in_specs, out_sds, o_spec, grid, dims, acc_shape, has_res=False, aliases=None, comm=None):
    nk = grid[2]
    n_in = len(operands)
    n_ci, n_co = (len(comm.arrays), len(comm.out_shapes)) if comm else (0, 0)

    def body(*refs):
        a_ref, b_ref = refs[0], refs[1]
        r_ref = refs[2] if has_res else None
        o_ref = refs[n_in + n_ci]
        acc = refs[n_in + n_ci + 1 + n_co]
        c_refs = (refs[n_in:n_in + n_ci], refs[n_in + n_ci + 1:n_in + n_ci + 1 + n_co], refs[n_in + n_ci + 2 + n_co:])
        ids = [pl.program_id(q) for q in range(3)]
        if comm:
            @pl.when((ids[0] == 0) & (ids[1] == 0) & (ids[2] == 0))
            def _():
                comm.start(*c_refs)

        def finish(total):
            if has_res:
                total = total + r_ref[...]
            o_ref[...] = total.astype(o_ref.dtype)

        if nk == 1:
            finish(lax.dot_general(a_ref[...], b_ref[...], dims, preferred_element_type=F32))
        else:
            k = ids[2]

            @pl.when(k == 0)
            def _():
                acc[...] = jnp.zeros_like(acc)

            acc[...] += lax.dot_general(a_ref[...], b_ref[...], dims, preferred_element_type=F32)

            @pl.when(k == nk - 1)
            def _():
                finish(acc[...])

        if comm:
            @pl.when((ids[0] == grid[0] - 1) & (ids[1] == grid[1] - 1) & (ids[2] == nk - 1))
            def _():
                comm.finish(*c_refs)

    scratch = [pltpu.VMEM(acc_shape if nk > 1 else (8, LANES), F32)]
    if not comm:
        out = pl.pallas_call(
            body, grid=grid, in_specs=in_specs, out_specs=o_spec, out_shape=out_sds, scratch_shapes=scratch,
            input_output_aliases=aliases or {}, name=name,
            compiler_params=_cp("parallel", "parallel", "arbitrary"))(*operands)
        return out, []
    outs = pl.pallas_call(
        body, grid=grid, in_specs=list(in_specs) + _any_specs(n_ci), out_specs=[o_spec] + _any_specs(n_co),
        out_shape=[out_sds] + comm.out_shapes, scratch_shapes=scratch + comm.sems,
        input_output_aliases=aliases or {}, name=name,
        compiler_params=_cp("arbitrary", "arbitrary", "arbitrary"))(*operands, *comm.arrays)
    return outs[0], list(outs[1:])


def _stack(a):
    return a if a.ndim == 3 else a[None]


def _mm_nn(name, a, w, res, out_dtype, col0=0, n_cols=None, tm_pref=1024, tn_pref=1024, tk_pref=2048, comm=None):
    M, K = a.shape
    N = n_cols or w.shape[1]
    tm, tn, tk = _tile(M, tm_pref, 16), _tile(N, tn_pref, LANES), _tile(K, tk_pref, LANES)
    c0 = col0 // tn
    ops = [a, w]
    specs = [pl.BlockSpec((tm, tk), lambda i, j, k: (i, k)), pl.BlockSpec((tk, tn), lambda i, j, k: (k, c0 + j))]
    if res is not None:
        ops.append(res)
        specs.append(pl.BlockSpec((tm, tn), lambda i, j, k: (i, j)))
    return _mm(name, tuple(ops), specs, _sds((M, N), out_dtype), pl.BlockSpec((tm, tn), lambda i, j, k: (i, j)),
               (M // tm, N // tn, K // tk), NN, (tm, tn), has_res=res is not None, comm=comm)


def _mm_nt(name, dy, w, out_dtype, col0=0, tn_pref=1024, tk_pref=2048, comm=None):
    dy = _stack(dy)
    _, M, Np = dy.shape
    Kw = w.shape[0]
    tm, tn, tk = _tile(M, 1024, 16), _tile(Kw, tn_pref, LANES), _tile(Np, tk_pref, LANES)
    per, c0 = Np // tk, col0 // tk
    return _mm(name, (dy, w),
               [pl.BlockSpec((None, tm, tk), lambda i, j, k: (k // per, i, k % per)),
                pl.BlockSpec((tn, tk), lambda i, j, k: (j, c0 + k))],
               _sds((M, Kw), out_dtype), pl.BlockSpec((tm, tn), lambda i, j, k: (i, j)),
               (M // tm, Kw // tn, dy.shape[0] * per), NT, (tm, tn), comm=comm)


def _mm_tn(name, a, dy, n_total=None, col0=0, tkr_pref=1024, tn_pref=1024, tk_pref=2048, prev=None, comm=None):
    dy = _stack(dy)
    P, M, Np = dy.shape
    Kw = a.shape[1]
    tkr, tn, tk = _tile(Kw, tkr_pref, LANES), _tile(Np, tn_pref, LANES), _tile(M, tk_pref, 16)
    per, c0 = Np // tn, col0 // tn
    ops = [a, dy]
    specs = [pl.BlockSpec((tk, tkr), lambda i, j, k: (k, i)),
             pl.BlockSpec((None, tk, tn), lambda i, j, k: (j // per, k, j % per))]
    aliases = None
    if prev is not None:
        ops.append(prev)
        specs.append(pl.BlockSpec(memory_space=pl.ANY))
        aliases = {2: 0}
    return _mm(name, tuple(ops), specs, _sds((Kw, n_total or P * Np), BF16),
               pl.BlockSpec((tkr, tn), lambda i, j, k: (i, c0 + j)),
               (Kw // tkr, P * per, M // tk), TN, (tkr, tn), aliases=aliases, comm=comm)


def _rmsnorm_fwd(name, x, g, comm=None):
    T, D = x.shape
    tr = _tile(T, 512, 16)
    n_ci, n_co = (len(comm.arrays), len(comm.out_shapes)) if comm else (0, 0)

    def body(*refs):
        x_ref, g_ref, h_ref = refs[0], refs[1], refs[2 + n_ci]
        c_refs = (refs[2:2 + n_ci], refs[3 + n_ci:3 + n_ci + n_co], refs[3 + n_ci + n_co:])
        if comm:
            @pl.when(pl.program_id(0) == 0)
            def _():
                comm.start(*c_refs)

        xf = x_ref[...]
        r = lax.rsqrt(jnp.mean(xf * xf, axis=-1, keepdims=True) + NORM_EPS)
        h_ref[...] = (xf * r * g_ref[...]).astype(h_ref.dtype)

        if comm:
            @pl.when(pl.program_id(0) == T // tr - 1)
            def _():
                comm.finish(*c_refs)

    outs = pl.pallas_call(
        body, grid=(T // tr,),
        in_specs=[pl.BlockSpec((tr, D), lambda i: (i, 0)), pl.BlockSpec((1, D), lambda i: (0, 0))] + _any_specs(n_ci),
        out_specs=[pl.BlockSpec((tr, D), lambda i: (i, 0))] + _any_specs(n_co),
        out_shape=[_sds((T, D), BF16)] + (comm.out_shapes if comm else []),
        scratch_shapes=comm.sems if comm else [],
        name=name, compiler_params=_cp("arbitrary" if comm else "parallel"))(x, g.reshape(1, D), *(comm.arrays if comm else []))
    return outs[0], list(outs[1:])


def _rmsnorm_bwd(name, x, g, dhs, dres, comm=None):
    T, D = x.shape
    tr = _tile(T, 256, 16)
    n_dh = len(dhs)
    n_in = 3 + n_dh
    n_ci, n_co = (len(comm.arrays), len(comm.out_shapes)) if comm else (0, 0)

    def body(*refs):
        x_ref, g_ref = refs[0], refs[1]
        dh_refs = refs[2:2 + n_dh]
        dres_ref = refs[2 + n_dh]
        dx_ref, dxb_ref, dg_ref = refs[n_in + n_ci:n_in + n_ci + 3]
        c_refs = (refs[n_in:n_in + n_ci], refs[n_in + n_ci + 3:n_in + n_ci + 3 + n_co], refs[n_in + n_ci + 3 + n_co:])
        if comm:
            @pl.when(pl.program_id(0) == 0)
            def _():
                comm.start(*c_refs)

        xf = x_ref[...]
        r = lax.rsqrt(jnp.mean(xf * xf, axis=-1, keepdims=True) + NORM_EPS)
        xhat = xf * r
        dh = dh_refs[0][...].astype(F32)
        for q in dh_refs[1:]:
            dh = dh + q[...].astype(F32)
        dy = dh * g_ref[...]
        c = jnp.mean(dy * xhat, axis=-1, keepdims=True)
        dx = dres_ref[...] + r * (dy - xhat * c)
        dx_ref[...] = dx
        dxb_ref[...] = dx.astype(BF16)

        @pl.when(pl.program_id(0) == 0)
        def _():
            dg_ref[...] = jnp.zeros_like(dg_ref)

        dg_ref[...] += jnp.sum(dh * xhat, axis=0, keepdims=True)

        if comm:
            @pl.when(pl.program_id(0) == T // tr - 1)
            def _():
                comm.finish(*c_refs)

    row = pl.BlockSpec((tr, D), lambda i: (i, 0))
    vec = pl.BlockSpec((1, D), lambda i: (0, 0))
    outs = pl.pallas_call(
        body, grid=(T // tr,), in_specs=[row, vec] + [row] * n_dh + [row] + _any_specs(n_ci),
        out_specs=[row, row, vec] + _any_specs(n_co),
        out_shape=[_sds((T, D), F32), _sds((T, D), BF16), _sds((1, D), F32)] + (comm.out_shapes if comm else []),
        scratch_shapes=comm.sems if comm else [],
        name=name, compiler_params=_cp("arbitrary"))(x, g.reshape(1, D), *dhs, dres, *(comm.arrays if comm else []))
    return (outs[0], outs[1], outs[2][0]), list(outs[3:])


def _final_loss(name, x, g, tgt):
    T, D = x.shape
    tr = _tile(T, 256, 16)

    def body(x_ref, g_ref, t_ref, dx_ref, dxb_ref, dg_ref, loss_ref):
        xf = x_ref[...]
        r = lax.rsqrt(jnp.mean(xf * xf, axis=-1, keepdims=True) + NORM_EPS)
        xhat = xf * r
        err = xhat * g_ref[...] - t_ref[...]
        dy = err * (1.0 / D)
        dxh = dy * g_ref[...]
        c = jnp.mean(dxh * xhat, axis=-1, keepdims=True)
        dx = r * (dxh - xhat * c)
        dx_ref[...] = dx
        dxb_ref[...] = dx.astype(BF16)

        @pl.when(pl.program_id(0) == 0)
        def _():
            dg_ref[...] = jnp.zeros_like(dg_ref)
            loss_ref[...] = jnp.zeros_like(loss_ref)

        dg_ref[...] += jnp.sum(dy * xhat, axis=0, keepdims=True)
        loss_ref[...] += 0.5 * jnp.sum(jnp.mean(err * err, axis=-1, keepdims=True), axis=0, keepdims=True)

    row = pl.BlockSpec((tr, D), lambda i: (i, 0))
    vec = pl.BlockSpec((1, D), lambda i: (0, 0))
    dx, dx_b, dg, loss = pl.pallas_call(
        body, grid=(T // tr,), in_specs=[row, vec, row],
        out_specs=[row, row, vec, pl.BlockSpec((1, 1), lambda i: (0, 0))],
        out_shape=[_sds((T, D), F32), _sds((T, D), BF16), _sds((1, D), F32), _sds((1, 1), F32)],
        name=name, compiler_params=_cp("arbitrary"))(x, g.reshape(1, D), tgt)
    return loss[0, 0], dx, dx_b, dg[0]


def _halo_specs(tr, tc, n_rows, col):
    rb = tr // HALO
    last = n_rows // HALO - 1
    return [pl.BlockSpec((tr, tc), lambda *g: (g[-1], col(*g))),
            pl.BlockSpec((HALO, tc), lambda *g: (jnp.maximum(g[-1] * rb - 1, 0), col(*g))),
            pl.BlockSpec((HALO, tc), lambda *g: (jnp.minimum((g[-1] + 1) * rb, last), col(*g)))]


def _ext(cur_ref, prev_ref, next_ref, i, n_i):
    p = prev_ref[...].astype(F32) * (i > 0).astype(F32)
    n = next_ref[...].astype(F32) * (i < n_i - 1).astype(F32)
    return jnp.concatenate([p, cur_ref[...].astype(F32), n], axis=0)


def _shift_dn(x):
    return pltpu.roll(x, 1, axis=0)


def _shift_up(x):
    return pltpu.roll(x, x.shape[0] - 1, axis=0)


def _conv(x, w_ref, b_ref):
    return w_ref[0:1, :] * _shift_dn(x) + w_ref[1:2, :] * x + w_ref[2:3, :] * _shift_up(x) + b_ref[...]


def _mid(x, tr):
    return x[HALO:HALO + tr, :]


def _conv_t(g, w_ref):
    return w_ref[0:1, :] * _shift_up(g) + w_ref[1:2, :] * g + w_ref[2:3, :] * _shift_dn(g)


def _conv_wgrad(acc_ref, g, x, tr, first):
    gm = _mid(g, tr)

    @pl.when(first)
    def _():
        acc_ref[...] = jnp.zeros_like(acc_ref)

    acc_ref[0:1, :] += jnp.sum(gm * _mid(_shift_dn(x), tr), axis=0, keepdims=True)
    acc_ref[1:2, :] += jnp.sum(gm * _mid(x, tr), axis=0, keepdims=True)
    acc_ref[2:3, :] += jnp.sum(gm * _mid(_shift_up(x), tr), axis=0, keepdims=True)
    acc_ref[3:4, :] += jnp.sum(gm, axis=0, keepdims=True)


def _sigmoid(a):
    return 1.0 / (1.0 + jnp.exp(-a))


def _ffn_gate_fwd(name, up, cw, cb, comm=None):
    T, F2 = up.shape
    F = F2 // 2
    tc, tr = _tile(F, 512, LANES), _tile(T, 512, HALO)
    nF, n_i = F // tc, T // tr

    n_ci, n_co = (len(comm.arrays), len(comm.out_shapes)) if comm else (0, 0)

    def body(*refs):
        ac, ap, an, bc, bp, bn, wa, wb, ba, bb = refs[:10]
        o_ref = refs[10 + n_ci]
        c_refs = (refs[10:10 + n_ci], refs[11 + n_ci:11 + n_ci + n_co], refs[11 + n_ci + n_co:])
        j, i = pl.program_id(0), pl.program_id(1)
        if comm:
            @pl.when((j == 0) & (i == 0))
            def _():
                comm.start(*c_refs)

        ua = _mid(_conv(_ext(ac, ap, an, i, n_i), wa, ba), tr)
        ub = _mid(_conv(_ext(bc, bp, bn, i, n_i), wb, bb), tr)
        o_ref[...] = (ua * _sigmoid(ua) * ub).astype(o_ref.dtype)

        if comm:
            @pl.when((j == nF - 1) & (i == n_i - 1))
            def _():
                comm.finish(*c_refs)

    wspec = lambda o: pl.BlockSpec((3, tc), lambda j, i: (0, j + o))
    bspec = lambda o: pl.BlockSpec((1, tc), lambda j, i: (0, j + o))
    sem = ("arbitrary", "arbitrary") if comm else ("parallel", "parallel")
    outs = pl.pallas_call(
        body, grid=(nF, n_i),
        in_specs=_halo_specs(tr, tc, T, lambda j, i: j) + _halo_specs(tr, tc, T, lambda j, i: j + nF)
        + [wspec(0), wspec(nF), bspec(0), bspec(nF)] + _any_specs(n_ci),
        out_specs=[pl.BlockSpec((tr, tc), lambda j, i: (i, j))] + _any_specs(n_co),
        out_shape=[_sds((T, F), BF16)] + (comm.out_shapes if comm else []),
        scratch_shapes=comm.sems if comm else [],
        name=name, compiler_params=_cp(*sem))(up, up, up, up, up, up, cw, cw, cb, cb, *(comm.arrays if comm else []))
    return outs[0], list(outs[1:])


def _ffn_gate_bwd(name, up, dact, cw, cb):
    T, F2 = up.shape
    F = F2 // 2
    tc, tr = _tile(F, 512, LANES), _tile(T, 512, HALO)
    nF, n_i = F // tc, T // tr

    def body(ac, ap, an, bc, bp, bn, dc, dp, dn, wa, wb, ba, bb, o_ref, wga_ref, wgb_ref):
        i = pl.program_id(1)
        xa = _ext(ac, ap, an, i, n_i)
        xb = _ext(bc, bp, bn, i, n_i)
        da = _ext(dc, dp, dn, i, n_i)
        ua = _conv(xa, wa, ba)
        sig = _sigmoid(ua)
        ga = da * _conv(xb, wb, bb) * (sig * (1.0 + ua * (1.0 - sig)))
        o_ref[0] = _mid(_conv_t(ga, wa), tr).astype(o_ref.dtype)
        _conv_wgrad(wga_ref, ga, xa, tr, i == 0)
        gb = da * (ua * sig)
        o_ref[1] = _mid(_conv_t(gb, wb), tr).astype(o_ref.dtype)
        _conv_wgrad(wgb_ref, gb, xb, tr, i == 0)

    wspec = lambda o: pl.BlockSpec((3, tc), lambda j, i: (0, j + o))
    bspec = lambda o: pl.BlockSpec((1, tc), lambda j, i: (0, j + o))
    wg = pl.BlockSpec((8, tc), lambda j, i: (0, j))
    dup, wga, wgb = pl.pallas_call(
        body, grid=(nF, n_i),
        in_specs=_halo_specs(tr, tc, T, lambda j, i: j) + _halo_specs(tr, tc, T, lambda j, i: j + nF)
        + _halo_specs(tr, tc, T, lambda j, i: j) + [wspec(0), wspec(nF), bspec(0), bspec(nF)],
        out_specs=[pl.BlockSpec((2, tr, tc), lambda j, i: (0, i, j)), wg, wg],
        out_shape=[_sds((2, T, F), BF16), _sds((8, F), F32), _sds((8, F), F32)],
        name=name, compiler_params=_cp("parallel", "arbitrary"),
    )(up, up, up, up, up, up, dact, dact, dact, cw, cw, cb, cb)
    return dup, jnp.concatenate([wga, wgb], axis=1)


def _sc_gate_fwd(name, z, cw, cb):
    T, D3 = z.shape
    D = D3 // 3
    tc, tr = _tile(D, 512, LANES), _tile(T, 512, HALO)
    nD, n_i = D // tc, T // tr

    def body(uc, up_, un, gb, cc, cp, cn, w, b, o_ref):
        i = pl.program_id(1)
        cu = _ext(cc, cp, cn, i, n_i) * _ext(uc, up_, un, i, n_i)
        o_ref[...] = (gb[...].astype(F32) * _mid(_conv(cu, w, b), tr)).astype(o_ref.dtype)

    return pl.pallas_call(
        body, grid=(nD, n_i),
        in_specs=_halo_specs(tr, tc, T, lambda j, i: j) + [pl.BlockSpec((tr, tc), lambda j, i: (i, j + nD))]
        + _halo_specs(tr, tc, T, lambda j, i: j + 2 * nD)
        + [pl.BlockSpec((3, tc), lambda j, i: (0, j)), pl.BlockSpec((1, tc), lambda j, i: (0, j))],
        out_specs=pl.BlockSpec((tr, tc), lambda j, i: (i, j)), out_shape=_sds((T, D), BF16),
        name=name, compiler_params=_cp("parallel", "parallel"))(z, z, z, z, z, z, z, cw, cb)


def _sc_gate_bwd(name, z, dy, cw, cb):
    T, D3 = z.shape
    D = D3 // 3
    tc, tr = _tile(D, 512, LANES), _tile(T, 512, HALO)
    nD, n_i = D // tc, T // tr

    def body(uc, up_, un, bc, bp, bn, cc, cp, cn, yc, yp, yn, w, b, o_ref, wg_ref):
        i = pl.program_id(1)
        u = _ext(uc, up_, un, i, n_i)
        gc = _ext(cc, cp, cn, i, n_i)
        cu = gc * u
        g = _ext(yc, yp, yn, i, n_i) * _ext(bc, bp, bn, i, n_i)
        dcu = _mid(_conv_t(g, w), tr)
        o_ref[0] = (dcu * _mid(gc, tr)).astype(o_ref.dtype)
        o_ref[1] = (yc[...].astype(F32) * _mid(_conv(cu, w, b), tr)).astype(o_ref.dtype)
        o_ref[2] = (dcu * _mid(u, tr)).astype(o_ref.dtype)
        _conv_wgrad(wg_ref, g, cu, tr, i == 0)

    hs = lambda o: _halo_specs(tr, tc, T, lambda j, i: j + o)
    return pl.pallas_call(
        body, grid=(nD, n_i),
        in_specs=hs(0) + hs(nD) + hs(2 * nD) + hs(0)
        + [pl.BlockSpec((3, tc), lambda j, i: (0, j)), pl.BlockSpec((1, tc), lambda j, i: (0, j))],
        out_specs=[pl.BlockSpec((3, tr, tc), lambda j, i: (0, i, j)), pl.BlockSpec((8, tc), lambda j, i: (0, j))],
        out_shape=[_sds((3, T, D), BF16), _sds((8, D), F32)],
        name=name, compiler_params=_cp("parallel", "arbitrary"),
    )(z, z, z, z, z, z, z, z, z, dy, dy, dy, cw, cb)


def _slopes(n_heads):
    return jnp.asarray(2.0 ** (-ALIBI_MAX * np.arange(1, n_heads + 1) / n_heads), dtype=F32)


CHAINS = 32


def _nq(L, d):
    return max(1, min(CHAINS if d == 1 else CHAINS // 2, L // LANES // 2))


def _srows(ref, r, start, n, d):
    if d == 1:
        return ref[start:start + n, :]
    return ref[pl.ds(start * d + r, n, stride=d), :]


def _win(p_ref, c_ref, n_ref, r, b, nq):
    lo, hi, top = b * LANES - BAND, b * LANES + LANES + BAND, nq * LANES
    parts = [p_ref[r]] if lo < 0 else []
    parts.append(c_ref[r, max(lo, 0):min(hi, top), :])
    if hi > top:
        parts.append(n_ref[r])
    return parts[0] if len(parts) == 1 else jnp.concatenate(parts, axis=0)


def _nat_win(p_ref, c_ref, n_ref, r, b, nq, d):
    lo, hi, top = b * LANES - BAND, b * LANES + LANES + BAND, nq * LANES
    parts = [_srows(p_ref, r, 0, BAND, d)] if lo < 0 else []
    parts.append(_srows(c_ref, r, max(lo, 0), min(hi, top) - max(lo, 0), d))
    if hi > top:
        parts.append(_srows(n_ref, r, 0, BAND, d))
    return parts[0] if len(parts) == 1 else jnp.concatenate(parts, axis=0)


def _biases(slope, d, n, n_steps, nq, q_rows, k_rows, q0, k0):
    qi = lax.broadcasted_iota(jnp.int32, (q_rows, k_rows), 0) + q0
    kj = lax.broadcasted_iota(jnp.int32, (q_rows, k_rows), 1) + k0
    dist = jnp.abs(kj - qi)
    base = jnp.where(dist <= BAND, -slope * (dist * d).astype(F32), NEG_INF)
    out = []
    for b in range(nq):
        t = base
        if b == 0:
            t = jnp.where((n == 0) & ((kj < 0) | (qi < 0)), NEG_INF, t)
        if b == nq - 1:
            t = jnp.where((n == n_steps - 1) & ((kj >= LANES) | (qi >= LANES)), NEG_INF, t)
        out.append(t)
    return out


def _win_specs(d, H, col, nq, L):
    return [pl.BlockSpec((d, BAND, LANES), lambda h, n: (0, jnp.maximum(2 * nq * n - 1, 0), col * H + h)),
            pl.BlockSpec((d, nq * LANES, LANES), lambda h, n: (0, n, col * H + h)),
            pl.BlockSpec((d, BAND, LANES), lambda h, n: (0, jnp.minimum(2 * nq * (n + 1), L // BAND - 1), col * H + h))]


def _nat_specs(d, nq, L):
    return [pl.BlockSpec((BAND * d, LANES), lambda h, n: (jnp.maximum(2 * nq * n - 1, 0), h)),
            pl.BlockSpec((nq * LANES * d, LANES), lambda h, n: (n, h)),
            pl.BlockSpec((BAND * d, LANES), lambda h, n: (jnp.minimum(2 * nq * (n + 1), L // BAND - 1), h))]


def _over_residues(d, nq, per_r):
    if d == 1:
        per_r(0, 0)
    else:
        lax.fori_loop(0, d, per_r, 0, unroll=min(d, max(1, CHAINS // nq)))


def _attn_fwd(name, qkv, d, H):
    T = qkv.shape[0]
    D = H * HEAD_DIM
    L = T // d
    nq = _nq(L, d)
    n_steps = L // (nq * LANES)
    scale = HEAD_DIM ** -0.5
    q3 = qkv.reshape(d, L, 3 * D)

    def body(s_ref, q_ref, kp, kc, kn, vp, vc, vn, o_ref, l_ref):
        h, n = pl.program_id(0), pl.program_id(1)
        bias = _biases(s_ref[h], d, n, n_steps, nq, LANES, 2 * LANES, 0, -BAND)

        def per_r(r, carry):
            for b in range(nq):
                k, v = _win(kp, kc, kn, r, b, nq), _win(vp, vc, vn, r, b, nq)
                s = lax.dot_general(q_ref[r, b * LANES:(b + 1) * LANES, :], k, NT, preferred_element_type=F32) * scale + bias[b]
                m = jnp.max(s, axis=1, keepdims=True)
                p = jnp.exp(s - m)
                den = jnp.sum(p, axis=1, keepdims=True)
                o = lax.dot_general(p.astype(BF16), v, NN, preferred_element_type=F32) / den
                lse = jnp.broadcast_to(m + jnp.log(den), (LANES, LANES))
                if d == 1:
                    o_ref[b * LANES:(b + 1) * LANES, :] = o
                    l_ref[b * LANES:(b + 1) * LANES, :] = lse
                else:
                    o_ref[pl.ds(b * LANES * d + r, LANES, stride=d), :] = o
                    l_ref[pl.ds(b * LANES * d + r, LANES, stride=d), :] = lse
            return carry

        _over_residues(d, nq, per_r)

    out = pl.BlockSpec((nq * LANES * d, LANES), lambda h, n: (n, h))
    return pl.pallas_call(
        body, grid=(H, n_steps),
        in_specs=[pl.BlockSpec(memory_space=pltpu.SMEM), pl.BlockSpec((d, nq * LANES, LANES), lambda h, n: (0, n, h))]
        + _win_specs(d, H, 1, nq, L) + _win_specs(d, H, 2, nq, L),
        out_specs=[out, out], out_shape=[_sds((T, D), F32), _sds((T, D), F32)],
        name=name, compiler_params=_cp("parallel", "parallel"))(_slopes(H), q3, q3, q3, q3, q3, q3, q3)


def _attn_combine(name, outs, lses):
    T, D = outs[0].shape
    tr, tc = _tile(T, 512, 16), _tile(D, 512, LANES)

    def body(o0, o1, o2, l0, l1, l2, ob_ref, l_ref):
        a0, a1, a2 = l0[...], l1[...], l2[...]
        m = jnp.maximum(jnp.maximum(a0, a1), a2)
        e0, e1, e2 = jnp.exp(a0 - m), jnp.exp(a1 - m), jnp.exp(a2 - m)
        z = e0 + e1 + e2
        ob_ref[...] = ((e0 * o0[...] + e1 * o1[...] + e2 * o2[...]) / z).astype(BF16)
        l_ref[...] = m + jnp.log(z)

    blk = pl.BlockSpec((tr, tc), lambda i, j: (i, j))
    return pl.pallas_call(
        body, grid=(T // tr, D // tc), in_specs=[blk] * 6, out_specs=[blk] * 2,
        out_shape=[_sds((T, D), BF16), _sds((T, D), F32)],
        name=name, compiler_params=_cp("parallel", "parallel"))(*outs, *lses)


def _attn_stats(name, do, o, lse):
    T, D = do.shape
    tr = _tile(T, 1024, 16)

    def body(a, b, l, o_ref):
        delta = jnp.broadcast_to(jnp.sum(a[...] * b[...].astype(F32), axis=1, keepdims=True), o_ref.shape)
        lane = lax.broadcasted_iota(jnp.int32, o_ref.shape, 1)
        o_ref[...] = jnp.where(lane < BAND, l[...], delta)

    blk = pl.BlockSpec((tr, LANES), lambda i, j: (i, j))
    return pl.pallas_call(body, grid=(T // tr, D // LANES), in_specs=[blk, blk, blk], out_specs=blk,
                          out_shape=_sds((T, D), F32), name=name, compiler_params=_cp("parallel", "parallel"))(do, o, lse)


def _attn_bwd(name, qkv, do, stats, d, H):
    T = qkv.shape[0]
    D = H * HEAD_DIM
    L = T // d
    nq = _nq(L, d)
    n_steps = L // (nq * LANES)
    scale = HEAD_DIM ** -0.5
    q3 = qkv.reshape(d, L, 3 * D)
    mid = slice(BAND, BAND + LANES)

    def body(s_ref, qp, qc, qn, kp, kc, kn, vp, vc, vn, gp, gc, gn, tp, tc_, tn_, o_ref):
        h, n = pl.program_id(0), pl.program_id(1)
        bias_q = _biases(s_ref[h], d, n, n_steps, nq, LANES, 2 * LANES, 0, -BAND)
        bias_k = _biases(s_ref[h], d, n, n_steps, nq, 2 * LANES, LANES, -BAND, 0)

        def per_r(r, carry):
            for b in range(nq):
                rows = slice(b * LANES, (b + 1) * LANES)
                q_w, k_w, v_w = _win(qp, qc, qn, r, b, nq), _win(kp, kc, kn, r, b, nq), _win(vp, vc, vn, r, b, nq)
                g_w = _nat_win(gp, gc, gn, r, b, nq, d)
                t_w = _nat_win(tp, tc_, tn_, r, b, nq, d)
                g_b = g_w.astype(BF16)
                q_c, k_c, v_c, g_c, t_c = q_w[mid], k_w[mid], v_w[mid], g_b[mid], t_w[mid]
                s = lax.dot_general(q_c, k_w, NT, preferred_element_type=F32) * scale + bias_q[b]
                p = jnp.exp(s - t_c[:, 0:1])
                dp = lax.dot_general(g_c, v_w, NT, preferred_element_type=F32)
                ds = p * (dp - t_c[:, BAND:BAND + 1])
                o_ref[0, r, rows, :] = (lax.dot_general(ds.astype(BF16), k_w, NN, preferred_element_type=F32) * scale).astype(BF16)
                s2 = lax.dot_general(q_w, k_c, NT, preferred_element_type=F32) * scale + bias_k[b]
                p2 = jnp.exp(s2 - t_w[:, 0:1])
                o_ref[2, r, rows, :] = lax.dot_general(p2.astype(BF16), g_b, TN, preferred_element_type=F32).astype(BF16)
                dp2 = lax.dot_general(g_b, v_c, NT, preferred_element_type=F32)
                ds2 = p2 * (dp2 - t_w[:, BAND:BAND + 1])
                o_ref[1, r, rows, :] = (lax.dot_general(ds2.astype(BF16), q_w, TN, preferred_element_type=F32) * scale).astype(BF16)
            return carry

        _over_residues(d, nq, per_r)

    dqkv = pl.pallas_call(
        body, grid=(H, n_steps),
        in_specs=[pl.BlockSpec(memory_space=pltpu.SMEM)]
        + _win_specs(d, H, 0, nq, L) + _win_specs(d, H, 1, nq, L) + _win_specs(d, H, 2, nq, L)
        + _nat_specs(d, nq, L) + _nat_specs(d, nq, L),
        out_specs=pl.BlockSpec((3, d, nq * LANES, LANES), lambda h, n: (0, 0, n, h)),
        out_shape=_sds((3, d, L, D), BF16),
        name=name, compiler_params=_cp("parallel", "parallel"),
    )(_slopes(H), *([q3] * 9), *([do] * 3), *([stats] * 3))
    return dqkv.reshape(3, T, D)


def _to_group_order(a, d):
    if d == 1:
        return a
    T, C = a.shape
    return a.reshape(T // d, d, C).swapaxes(0, 1).reshape(T, C)


def _from_group_order(a, d):
    if d == 1:
        return a
    T, C = a.shape
    return a.reshape(d, T // d, C).swapaxes(0, 1).reshape(T, C)


def _fwd_bwd(x, tgt, S, ex):
    T, D = x.shape
    H = D // HEAD_DIM
    G3 = 3 * D

    def mm(fn, *args, rides=(), **kw):
        out, extra = fn(*args, comm=_join([getattr(ex, kind)(keys) for kind, keys in rides]), **kw)
        at = 0
        for kind, keys in rides:
            getattr(ex, kind + "_done")(keys, extra[at:at + len(keys)])
            at += len(keys)
        return out

    def ffn_fwd(l, xin, rides):
        hf = mm(_rmsnorm_fwd, f"ffn_norm{l}", xin, S["ffn_g"][l])
        up = mm(_mm_nn, f"ffn_up{l}", hf, ex.w(f"up{l}"), None, BF16, rides=rides[0])
        act = mm(_ffn_gate_fwd, f"ffn_gate{l}", up, ffn_cw[l], S["ffn_cb"][l][None], rides=rides[1])
        return hf, up, act, mm(_mm_nn, f"ffn_down{l}", act, ex.w(f"dn{l}"), xin, F32, tk_pref=2816, rides=rides[2])

    def ffn_bwd(l, xin, hf, up, act, dxo, dxo_b, rides):
        dact = mm(_mm_nt, f"ffn_down_dx{l}", dxo_b, ex.w(f"dn{l}"), BF16, tn_pref=1408, rides=rides[0])
        ex.grad(f"dn{l}", mm(_mm_tn, f"ffn_down_dw{l}", act, dxo_b, tkr_pref=1408))
        dup, cg = _ffn_gate_bwd(f"ffn_gate_bwd{l}", up, dact, ffn_cw[l], S["ffn_cb"][l][None])
        dhf = mm(_mm_nt, f"ffn_up_dx{l}", dup, ex.w(f"up{l}"), BF16, tk_pref=2816, rides=rides[1])
        ex.grad(f"up{l}", mm(_mm_tn, f"ffn_up_dw{l}", hf, dup, tn_pref=2816, tk_pref=1024, rides=rides[2]))
        dx, dx_b, dg = mm(_rmsnorm_bwd, f"ffn_norm_bwd{l}", xin, S["ffn_g"][l], [dhf], dxo)
        return dx, dx_b, dg, cg

    h0 = mm(_rmsnorm_fwd, "mix_norm0", x, S["mix_g"][0], rides=[("ag", ["in", "scw", "fcw"])])
    sc_cw = ex.w("scw")
    ffn_cw = ex.w("fcw").reshape(S["ffn_cb"].shape[0], 3, -1)
    z = mm(_mm_nn, "sc_in", h0, ex.w("in"), None, BF16, rides=[("ag", ["sco", "up0"])])
    y = _sc_gate_fwd("sc_gate", z, sc_cw, S["sc_cb"][None])
    x1 = mm(_mm_nn, "sc_out", y, ex.w("sco"), x, F32, tm_pref=512, tn_pref=2048, rides=[("ag", ["dn0"])])
    hf0, up0, act0, x2 = ffn_fwd(0, x1, [[("ag", ["qkv"])], [("ag", ["up1"])], [("ag", ["ao", "dn1"])]])
    h1 = mm(_rmsnorm_fwd, "mix_norm1", x2, S["mix_g"][1])
    hd, qkv, outs, lses = [], [], [], []
    for g, d in enumerate(DILATIONS):
        hd.append(_to_group_order(h1, d))
        qkv.append(mm(_mm_nn, f"attn_qkv{g}", hd[g], ex.w("qkv"), None, BF16, col0=g * G3, n_cols=G3))
        o_g, l_g = _attn_fwd(f"attn_fwd{g}", qkv[g], d, H)
        outs.append(o_g)
        lses.append(l_g)
    o_b, lse = _attn_combine("attn_combine", outs, lses)
    x3 = mm(_mm_nn, "attn_out", o_b, ex.w("ao"), x2, F32, tm_pref=512, tn_pref=2048)
    hf1, up1, act1, x4 = ffn_fwd(1, x3, [(), (), ()])
    loss, dx4, dx4_b, dg_fin = _final_loss("final_loss", x4, S["fin_g"], tgt)

    dx3, dx3_b, dg_f1, cg1 = ffn_bwd(1, x3, hf1, up1, act1, dx4, dx4_b,
                                     [(), [("pair", ["dn1"])], [("chip", ["dn1"])]])
    do = mm(_mm_nt, "attn_out_dx", dx3_b, ex.w("ao"), F32, rides=[("pair", ["up1"])])
    ex.grad("ao", mm(_mm_tn, "attn_out_dw", o_b, dx3_b))
    stats = _attn_stats("attn_stats", do, o_b, lse)
    dhs, dw_qkv = [], None
    qkv_rides = [[("chip", ["up1"]), ("pair", ["ao"])], [("chip", ["ao"])], ()]
    for g, d in enumerate(DILATIONS):
        dqkv = _attn_bwd(f"attn_bwd{g}", qkv[g], do, stats, d, H)
        dhs.append(_from_group_order(mm(_mm_nt, f"attn_qkv_dx{g}", dqkv, ex.w("qkv"), BF16, col0=g * G3,
                                        rides=qkv_rides[g]), d))
        dw_qkv = mm(_mm_tn, f"attn_qkv_dw{g}", hd[g], dqkv, n_total=len(DILATIONS) * G3, col0=g * G3, prev=dw_qkv)
    ex.grad("qkv", dw_qkv)
    dx2, dx2_b, dg_m1 = mm(_rmsnorm_bwd, "mix_norm_bwd1", x2, S["mix_g"][1], dhs, dx3)
    dx1, dx1_b, dg_f0, cg0 = ffn_bwd(0, x1, hf0, up0, act0, dx2, dx2_b,
                                     [[("pair", ["qkv"])], [("chip", ["qkv"]), ("pair", ["dn0"])], [("chip", ["dn0"])]])
    dy = mm(_mm_nt, "sc_out_dx", dx1_b, ex.w("sco"), BF16, rides=[("pair", ["up0"])])
    ex.grad("sco", mm(_mm_tn, "sc_out_dw", y, dx1_b))
    dz, cg_sc = _sc_gate_bwd("sc_gate_bwd", z, dy, sc_cw, S["sc_cb"][None])
    ex.grad("in", mm(_mm_tn, "sc_in_dw", h0, dz, rides=[("chip", ["up0"]), ("pair", ["sco"])]))
    dh0 = mm(_mm_nt, "sc_in_dx", dz, ex.w("in"), BF16, rides=[("chip", ["sco"]), ("pair", ["in"])])
    dx0, _, dg_m0 = mm(_rmsnorm_bwd, "mix_norm_bwd0", x, S["mix_g"][0], [dh0], dx1, rides=[("chip", ["in"])])

    dS = {"mix_g": jnp.stack([dg_m0, dg_m1]), "ffn_g": jnp.stack([dg_f0, dg_f1]), "fin_g": dg_fin,
          "sc_cw": cg_sc[0:3], "sc_cb": cg_sc[3], "ffn_cw": jnp.stack([cg0[0:3], cg1[0:3]]),
          "ffn_cb": jnp.stack([cg0[3], cg1[3]])}
    return loss, dx0, dS


def _place():
    x, y, c = lax.axis_index("x"), lax.axis_index("y"), lax.axis_index("c")
    return x, y, c, [(1 - x, y), (x, 1 - y), (1 - x, 1 - y)]


def _block(ref, s, shape, axis):
    R, C = shape
    if axis == 0:
        return ref.at[pl.ds(pl.multiple_of(s * R, HALO), R), :]
    return ref.at[:, pl.ds(pl.multiple_of(s * C, LANES), C)]


def _whole(shape, axis):
    return (shape[0] * N_DEV, shape[1]) if axis == 0 else (shape[0], shape[1] * N_DEV)


def _ag_stage(arrs, axes):
    n = len(arrs)

    def plan(ins, outs, sems):
        send_sems, recv_sems, local_sems = sems
        x, y, c, chips = _place()
        me, sibling = 4 * x + 2 * y + c, (x, y, 1 - c)

        def copy(a, k, blk, to, src=None):
            dst = _block(outs[a], blk, arrs[a].shape, axes[a])
            return pltpu.make_async_remote_copy(src_ref=dst if src is None else src, dst_ref=dst,
                                                send_sem=send_sems.at[a, k], recv_sem=recv_sems.at[a, k],
                                                device_id=to, device_id_type=MESH)

        mine = [pltpu.make_async_copy(ins[a], _block(outs[a], me, arrs[a].shape, axes[a]), local_sems.at[a])
                for a in range(n)]
        first = []
        for a in range(n):
            first.append(copy(a, 0, me, sibling, src=ins[a]))
            first += [copy(a, 1 + j, me, (*chip, c), src=ins[a]) for j, chip in enumerate(chips)]
        return x, y, c, chips, sibling, copy, mine, first

    def start(ins, outs, sems):
        *_, mine, first = plan(ins, outs, sems)
        for cp in mine + first:
            cp.start()

    def finish(ins, outs, sems):
        x, y, c, chips, sibling, copy, mine, first = plan(ins, outs, sems)
        passed = []
        for j, (px, py) in enumerate(chips):
            for a in range(n):
                blk = 4 * px + 2 * py + c
                copy(a, 1 + j, blk, sibling).wait_recv()
                passed.append(copy(a, 4 + j, blk, sibling))
                passed[-1].start()
        for a in range(n):
            copy(a, 0, 4 * x + 2 * y + 1 - c, sibling).wait_recv()
            for j, (px, py) in enumerate(chips):
                copy(a, 4 + j, 4 * px + 2 * py + 1 - c, sibling).wait_recv()
        for cp in first + passed:
            cp.wait_send()
        for cp in mine:
            cp.wait()

    return _Stage(list(arrs), [_sds(_whole(a.shape, ax), a.dtype) for a, ax in zip(arrs, axes)],
                  [pltpu.SemaphoreType.DMA((n, 7)), pltpu.SemaphoreType.DMA((n, 7)), pltpu.SemaphoreType.DMA((n,))],
                  start, finish)


def _pair_stage(dws, shapes, axes):
    n = len(dws)

    def copies(ins, outs, sems):
        send_sems, recv_sems = sems
        x, y, c, _ = _place()
        return [pltpu.make_async_remote_copy(src_ref=_block(ins[a], 2 * k + 1 - c, shapes[a], axes[a]),
                                             dst_ref=outs[a].at[k],
                                             send_sem=send_sems.at[a, k], recv_sem=recv_sems.at[a, k],
                                             device_id=(x, y, 1 - c), device_id_type=MESH)
                for a in range(n) for k in range(4)]

    def start(ins, outs, sems):
        for cp in copies(ins, outs, sems):
            cp.start()

    def finish(ins, outs, sems):
        for cp in copies(ins, outs, sems):
            cp.wait()

    return _Stage(list(dws), [_sds((4,) + tuple(s), a.dtype) for a, s in zip(dws, shapes)],
                  [pltpu.SemaphoreType.DMA((n, 4)), pltpu.SemaphoreType.DMA((n, 4))], start, finish)


def _rs_pair_add(name, dw, got, c_arr, axis):
    _, R, C = got.shape
    tr, tc = _tile(R, 512, 16), _tile(C, 1536, LANES)
    per = R // tr if axis == 0 else C // tc

    def body(c_ref, a_ref, b_ref, o_ref):
        o_ref[...] = (a_ref[...].astype(F32) + b_ref[...].astype(F32)).astype(o_ref.dtype)

    if axis == 0:
        mine = pl.BlockSpec((tr, tc), lambda k, i, j, c_ref: ((2 * k + c_ref[0]) * per + i, j))
    else:
        mine = pl.BlockSpec((tr, tc), lambda k, i, j, c_ref: (i, (2 * k + c_ref[0]) * per + j))
    return pl.pallas_call(
        body,
        grid_spec=pltpu.PrefetchScalarGridSpec(
            num_scalar_prefetch=1, grid=(4, R // tr, C // tc),
            in_specs=[mine, pl.BlockSpec((None, tr, tc), lambda k, i, j, c_ref: (k, i, j))],
            out_specs=pl.BlockSpec((None, tr, tc), lambda k, i, j, c_ref: (k, i, j))),
        out_shape=_sds((4, R, C), BF16), name=name,
        compiler_params=_cp("parallel", "parallel", "parallel"))(c_arr, dw, got)


def _chip_stage(parts):
    n = len(parts)

    def plan(ins, outs, sems):
        send_sems, recv_sems, local_sems = sems
        x, y, c, chips = _place()
        my_chip = 2 * x + y
        mine = [pltpu.make_async_copy(ins[a].at[my_chip], outs[a].at[my_chip], local_sems.at[a]) for a in range(n)]
        sends = [pltpu.make_async_remote_copy(src_ref=ins[a].at[2 * px + py], dst_ref=outs[a].at[my_chip],
                                              send_sem=send_sems.at[a, j], recv_sem=recv_sems.at[a, j],
                                              device_id=(px, py, c), device_id_type=MESH)
                 for a in range(n) for j, (px, py) in enumerate(chips)]
        arrivals = lambda: [pltpu.make_async_remote_copy(src_ref=ins[a].at[my_chip], dst_ref=outs[a].at[2 * px + py],
                                                         send_sem=send_sems.at[a, j], recv_sem=recv_sems.at[a, j],
                                                         device_id=(px, py, c), device_id_type=MESH)
                            for a in range(n) for j, (px, py) in enumerate(chips)]
        return mine, sends, arrivals

    def start(ins, outs, sems):
        mine, sends, _ = plan(ins, outs, sems)
        for cp in mine + sends:
            cp.start()

    def finish(ins, outs, sems):
        mine, sends, arrivals = plan(ins, outs, sems)
        for cp in arrivals():
            cp.wait_recv()
        for cp in sends:
            cp.wait_send()
        for cp in mine:
            cp.wait()

    return _Stage(list(parts), [_sds(a.shape, a.dtype) for a in parts],
                  [pltpu.SemaphoreType.DMA((n, 3)), pltpu.SemaphoreType.DMA((n, 3)), pltpu.SemaphoreType.DMA((n,))],
                  start, finish)


class _Exchange:
    ROW_SHARDED = ("sco", "ao", "dn0", "dn1")

    def __init__(self, shards, c_arr):
        self.sh, self.c_arr = shards, c_arr
        self.W, self.dw, self.parts, self.sums = {}, {}, {}, {}

    def axis(self, key):
        return 0 if key in self.ROW_SHARDED else 1

    def w(self, key):
        return self.W[key]

    def ag(self, keys):
        return _ag_stage([self.sh[k] for k in keys], [self.axis(k) for k in keys])

    def ag_done(self, keys, outs):
        self.W.update(zip(keys, outs))

    def grad(self, key, dw):
        self.dw[key] = dw

    def pair(self, keys):
        return _pair_stage([self.dw[k] for k in keys], [self.sh[k].shape for k in keys], [self.axis(k) for k in keys])

    def pair_done(self, keys, outs):
        for k, got in zip(keys, outs):
            self.parts[k] = _rs_pair_add(f"rs_add_{k}", self.dw[k], got, self.c_arr, self.axis(k))

    def chip(self, keys):
        return _chip_stage([self.parts[k] for k in keys])

    def chip_done(self, keys, outs):
        self.sums.update(zip(keys, outs))


def _sum_slots(name, a):
    _, rows, _ = a.shape

    def body(a_ref, o_ref):
        s = a_ref[0]
        for k in range(1, N_DEV):
            s = s + a_ref[k]
        o_ref[...] = s

    return pl.pallas_call(body, out_shape=_sds((rows, LANES), F32), name=name)(a)


def _cast_bf16(name, w3, l):
    _, R, C = w3.shape
    tr, tc = _tile(R, 512, 16), _tile(C, 1536, LANES)

    def body(w_ref, o_ref):
        o_ref[...] = w_ref[...].astype(BF16)

    return pl.pallas_call(
        body, grid=(R // tr, C // tc), in_specs=[pl.BlockSpec((None, tr, tc), lambda i, j: (l, i, j))],
        out_specs=pl.BlockSpec((tr, tc), lambda i, j: (i, j)), out_shape=_sds((R, C), BF16),
        name=name, compiler_params=_cp("parallel", "parallel"))(w3)


def _adamw(name, g_slots, w3, m3, v3, l, prev):
    n_slots, R, C = g_slots.shape
    tr, tc = _tile(R, 256, 8), _tile(C, 1536, LANES)
    c1, c2 = 1.0 - ADAM_B1 ** ADAM_STEP, 1.0 - ADAM_B2 ** ADAM_STEP

    def body(g_ref, w_ref, m_ref, v_ref, *rest):
        og, od, om, ov = rest[-4:]
        g = g_ref[0].astype(F32)
        for k in range(1, n_slots):
            g = g + g_ref[k].astype(F32)
        m = ADAM_B1 * m_ref[...] + (1.0 - ADAM_B1) * g
        v = ADAM_B2 * v_ref[...] + (1.0 - ADAM_B2) * (g * g)
        og[...] = g
        om[...] = m
        ov[...] = v
        od[...] = -ADAM_LR * ((m / c1) / (jnp.sqrt(v / c2) + ADAM_EPS) + ADAM_WD * w_ref[...])

    lay = pl.BlockSpec((None, tr, tc), lambda i, j: (l, i, j))
    ops = [g_slots, w3, m3, v3]
    specs = [pl.BlockSpec((n_slots, tr, tc), lambda i, j: (0, i, j)), lay, lay, lay]
    aliases = {}
    if prev is not None:
        ops += list(prev)
        specs += _any_specs(4)
        aliases = {4 + k: k for k in range(4)}
    return pl.pallas_call(
        body, grid=(R // tr, C // tc), in_specs=specs, out_specs=[lay] * 4,
        out_shape=[_sds(w3.shape, F32)] * 4, input_output_aliases=aliases,
        name=name, compiler_params=_cp("parallel", "parallel"))(*ops)


def _pack(parts):
    flat = jnp.concatenate([p.reshape(-1) for p in parts])
    pad = (-flat.shape[0]) % (HALO * LANES)
    return jnp.pad(flat, (0, pad)).reshape(-1, LANES)


def _unpack(packed, shapes):
    flat = packed.reshape(-1)
    out, at = [], 0
    for s in shapes:
        n = int(np.prod(s))
        out.append(flat[at:at + n].reshape(s))
        at += n
    return out


def kernel(x, mix_norm_g, ffn_norm_g, final_norm_g, sc_w_in, sc_conv_w, sc_conv_b, sc_w_out, attn_w_qkv, attn_w_out, ffn_w_up, ffn_conv_w, ffn_conv_b, ffn_w_down, loss_target, m_mix_norm_g, m_ffn_norm_g, m_final_norm_g, m_sc_w_in, m_sc_conv_w, m_sc_conv_b, m_sc_w_out, m_attn_w_qkv, m_attn_w_out, m_ffn_w_up, m_ffn_conv_w, m_ffn_conv_b, m_ffn_w_down, v_mix_norm_g, v_ffn_norm_g, v_final_norm_g, v_sc_w_in, v_sc_conv_w, v_sc_conv_b, v_sc_w_out, v_attn_w_qkv, v_attn_w_out, v_ffn_w_up, v_ffn_conv_w, v_ffn_conv_b, v_ffn_w_down):
    n_layers = ffn_w_up.shape[0]
    me = 4 * lax.axis_index("x") + 2 * lax.axis_index("y") + lax.axis_index("c")
    c_arr = lax.axis_index("c").astype(jnp.int32).reshape(1)

    big = [("in", sc_w_in, 0), ("sco", sc_w_out, 0), ("qkv", attn_w_qkv, 0), ("ao", attn_w_out, 0)]
    big += [(f"up{l}", ffn_w_up, l) for l in range(n_layers)] + [(f"dn{l}", ffn_w_down, l) for l in range(n_layers)]
    shards = {nm: _cast_bf16(f"cast_{nm}", w, l) for nm, w, l in big}
    shards["scw"] = sc_conv_w.reshape(-1, sc_conv_w.shape[-1])
    shards["fcw"] = ffn_conv_w.reshape(-1, ffn_conv_w.shape[-1])
    ex = _Exchange(shards, c_arr)
    S = {"mix_g": mix_norm_g, "ffn_g": ffn_norm_g, "fin_g": final_norm_g, "sc_cb": sc_conv_b[0], "ffn_cb": ffn_conv_b}

    loss_part, grad_x, dS = _fwd_bwd(x[0], loss_target[0], S, ex)

    small_names = ["mix_g", "ffn_g", "fin_g", "sc_cb", "ffn_cb", "sc_cw", "ffn_cw"]
    small_parts = [dS[k] for k in small_names] + [loss_part.reshape(1)]
    small_mine = _pack(small_parts)
    small_all, = _run_stage("gather_small", _ag_stage([small_mine], [0]))
    small_sum = _sum_slots("sum_small", small_all.reshape((N_DEV,) + small_mine.shape))
    g_mix, g_ffn, g_fin, g_scb, g_fcb, g_scw, g_fcw, loss = _unpack(small_sum, [p.shape for p in small_parts])
    g_scw = lax.dynamic_slice_in_dim(g_scw, me * sc_conv_w.shape[-1], sc_conv_w.shape[-1], axis=-1)[None]
    g_fcw = lax.dynamic_slice_in_dim(g_fcw, me * ffn_conv_w.shape[-1], ffn_conv_w.shape[-1], axis=-1)
    g_scb = g_scb[None]
    small_g = [g_mix, g_ffn, g_fin, g_scw, g_scb, g_fcw, g_fcb]
    small_w = [mix_norm_g, ffn_norm_g, final_norm_g, sc_conv_w, sc_conv_b, ffn_conv_w, ffn_conv_b]
    small_m = [m_mix_norm_g, m_ffn_norm_g, m_final_norm_g, m_sc_conv_w, m_sc_conv_b, m_ffn_conv_w, m_ffn_conv_b]
    small_v = [v_mix_norm_g, v_ffn_norm_g, v_final_norm_g, v_sc_conv_w, v_sc_conv_b, v_ffn_conv_w, v_ffn_conv_b]
    small_out = _adamw("adamw_small", _pack(small_g)[None], _pack(small_w)[None], _pack(small_m)[None],
                       _pack(small_v)[None], 0, None)
    small_shapes = [w.shape for w in small_w]
    sg, sd, sm, sv = [_unpack(o[0], small_shapes) for o in small_out]

    moments = {"in": (m_sc_w_in, v_sc_w_in), "sco": (m_sc_w_out, v_sc_w_out), "qkv": (m_attn_w_qkv, v_attn_w_qkv),
               "ao": (m_attn_w_out, v_attn_w_out), "up": (m_ffn_w_up, v_ffn_w_up), "dn": (m_ffn_w_down, v_ffn_w_down)}
    upd = {}
    for nm, w, l in big:
        key = nm.rstrip("0123456789")
        upd[key] = _adamw(f"adamw_{nm}", ex.sums[nm], w, moments[key][0], moments[key][1], l, upd.get(key))

    def leaves(k):
        return [sg, sd, sm, sv][k][0:3] + [upd["in"][k], [sg, sd, sm, sv][k][3], [sg, sd, sm, sv][k][4], upd["sco"][k],
                                          upd["qkv"][k], upd["ao"][k], upd["up"][k], [sg, sd, sm, sv][k][5],
                                          [sg, sd, sm, sv][k][6], upd["dn"][k]]

    return (loss.reshape(()), grad_x[None], *leaves(0), *leaves(1), *leaves(2), *leaves(3))
```

```python
import math

import numpy as np
import jax
import jax.numpy as jnp
from jax import lax
from jax.experimental import pallas as pl
from jax.experimental.pallas import tpu as pltpu

F32 = jnp.float32
BF16 = jnp.bfloat16
MESH = pl.DeviceIdType.MESH

HEAD_DIM = 128
DILATED_GROUPS = ((128, 1), (512, 4), (2048, 16))
DILATIONS = tuple(d for _, d in DILATED_GROUPS)
BAND = (DILATED_GROUPS[0][0] // 2) // DILATED_GROUPS[0][1]
assert all((w // 2) // d == BAND for w, d in DILATED_GROUPS)
NORM_EPS = 1e-5
ALIBI_MAX = 8.0
NEG_INF = -1e30
ADAM_LR, ADAM_B1, ADAM_B2, ADAM_EPS, ADAM_WD, ADAM_STEP = 0.001, 0.9, 0.999, 1e-08, 0.01, 10

N_DEV = 8
LANES = 128
HALO = 16
VMEM_LIMIT = 56 * 1024 * 1024


def _cp(*sem):
    return pltpu.CompilerParams(dimension_semantics=sem, vmem_limit_bytes=VMEM_LIMIT)


def _tile(n, pref, mult):
    t = (min(n, pref) // mult) * mult
    while t >= mult:
        if n % t == 0:
            return t
        t -= mult
    return n


def _sds(shape, dtype):
    return jax.ShapeDtypeStruct(shape, dtype)


def _any_specs(n):
    return [pl.BlockSpec(memory_space=pl.ANY)] * n


class _Stage:
    def __init__(self, arrays, out_shapes, sems, start, finish):
        self.arrays, self.out_shapes, self.sems, self.start, self.finish = arrays, out_shapes, sems, start, finish


def _join(stages):
    stages = [s for s in stages if s is not None]
    if not stages:
        return None

    def split(refs, count):
        out, at = [], 0
        for s in stages:
            out.append(refs[at:at + count(s)])
            at += count(s)
        return out

    def each(which):
        def run(ins, outs, sems):
            parts = zip(split(ins, lambda s: len(s.arrays)), split(outs, lambda s: len(s.out_shapes)),
                        split(sems, lambda s: len(s.sems)))
            for s, (i, o, m) in zip(stages, parts):
                getattr(s, which)(i, o, m)
        return run

    return _Stage(sum([s.arrays for s in stages], []), sum([s.out_shapes for s in stages], []),
                  sum([s.sems for s in stages], []), each("start"), each("finish"))


def _run_stage(name, st):
    n, m = len(st.arrays), len(st.out_shapes)

    def body(*refs):
        ins, outs, sems = refs[:n], refs[n:n + m], refs[n + m:]
        st.start(ins, outs, sems)
        st.finish(ins, outs, sems)

    return pl.pallas_call(body, in_specs=_any_specs(n), out_specs=_any_specs(m), out_shape=st.out_shapes,
                          scratch_shapes=st.sems, name=name)(*st.arrays)


NN = (((1,), (0,)), ((), ()))
NT = (((1,), (1,)), ((), ()))
TN = (((0,), (0,)), ((), ()))


def _mm(name, operands, in_specs, out_sds, o_spec, grid, dims, acc_shape, has_res=False, aliases=None, comm=None):
    nk = grid[2]
    n_in = len(operands)
    n_ci, n_co = (len(comm.arrays), len(comm.out_shapes)) if comm else (0, 0)

    def body(*refs):
        a_ref, b_ref = refs[0], refs[1]
        r_ref = refs[2] if has_res else None
        o_ref = refs[n_in + n_ci]
        acc = refs[n_in + n_ci + 1 + n_co]
        c_refs = (refs[n_in:n_in + n_ci], refs[n_in + n_ci + 1:n_in + n_ci + 1 + n_co], refs[n_in + n_ci + 2 + n_co:])
        ids = [pl.program_id(q) for q in range(3)]
        if comm:
            @pl.when((ids[0] == 0) & (ids[1] == 0) & (ids[2] == 0))
            def _():
                comm.start(*c_refs)

        def finish(total):
            if has_res:
                total = total + r_ref[...]
            o_ref[...] = total.astype(o_ref.dtype)

        if nk == 1:
            finish(lax.dot_general(a_ref[...], b_ref[...], dims, preferred_element_type=F32))
        else:
            k = ids[2]

            @pl.when(k == 0)
            def _():
                acc[...] = jnp.zeros_like(acc)

            acc[...] += lax.dot_general(a_ref[...], b_ref[...], dims, preferred_element_type=F32)

            @pl.when(k == nk - 1)
            def _():
                finish(acc[...])

        if comm:
            @pl.when((ids[0] == grid[0] - 1) & (ids[1] == grid[1] - 1) & (ids[2] == nk - 1))
            def _():
                comm.finish(*c_refs)

    scratch = [pltpu.VMEM(acc_shape if nk > 1 else (8, LANES), F32)]
    if not comm:
        out = pl.pallas_call(
            body, grid=grid, in_specs=in_specs, out_specs=o_spec, out_shape=out_sds, scratch_shapes=scratch,
            input_output_aliases=aliases or {}, name=name,
            compiler_params=_cp("parallel", "parallel", "arbitrary"))(*operands)
        return out, []
    outs = pl.pallas_call(
        body, grid=grid, in_specs=list(in_specs) + _any_specs(n_ci), out_specs=[o_spec] + _any_specs(n_co),
        out_shape=[out_sds] + comm.out_shapes, scratch_shapes=scratch + comm.sems,
        input_output_aliases=aliases or {}, name=name,
        compiler_params=_cp("arbitrary", "arbitrary", "arbitrary"))(*operands, *comm.arrays)
    return outs[0], list(outs[1:])


def _stack(a):
    return a if a.ndim == 3 else a[None]


def _mm_nn(name, a, w, res, out_dtype, col0=0, n_cols=None, tm_pref=1024, tn_pref=1024, tk_pref=2048, comm=None):
    M, K = a.shape
    N = n_cols or w.shape[1]
    tm, tn, tk = _tile(M, tm_pref, 16), _tile(N, tn_pref, LANES), _tile(K, tk_pref, LANES)
    c0 = col0 // tn
    ops = [a, w]
    specs = [pl.BlockSpec((tm, tk), lambda i, j, k: (i, k)), pl.BlockSpec((tk, tn), lambda i, j, k: (k, c0 + j))]
    if res is not None:
        ops.append(res)
        specs.append(pl.BlockSpec((tm, tn), lambda i, j, k: (i, j)))
    return _mm(name, tuple(ops), specs, _sds((M, N), out_dtype), pl.BlockSpec((tm, tn), lambda i, j, k: (i, j)),
               (M // tm, N // tn, K // tk), NN, (tm, tn), has_res=res is not None, comm=comm)


def _mm_nt(name, dy, w, out_dtype, col0=0, tn_pref=1024, tk_pref=2048, comm=None):
    dy = _stack(dy)
    _, M, Np = dy.shape
    Kw = w.shape[0]
    tm, tn, tk = _tile(M, 1024, 16), _tile(Kw, tn_pref, LANES), _tile(Np, tk_pref, LANES)
    per, c0 = Np // tk, col0 // tk
    return _mm(name, (dy, w),
               [pl.BlockSpec((None, tm, tk), lambda i, j, k: (k // per, i, k % per)),
                pl.BlockSpec((tn, tk), lambda i, j, k: (j, c0 + k))],
               _sds((M, Kw), out_dtype), pl.BlockSpec((tm, tn), lambda i, j, k: (i, j)),
               (M // tm, Kw // tn, dy.shape[0] * per), NT, (tm, tn), comm=comm)


def _mm_tn(name, a, dy, n_total=None, col0=0, tkr_pref=1024, tn_pref=1024, tk_pref=2048, prev=None, comm=None):
    dy = _stack(dy)
    P, M, Np = dy.shape
    Kw = a.shape[1]
    tkr, tn, tk = _tile(Kw, tkr_pref, LANES), _tile(Np, tn_pref, LANES), _tile(M, tk_pref, 16)
    per, c0 = Np // tn, col0 // tn
    ops = [a, dy]
    specs = [pl.BlockSpec((tk, tkr), lambda i, j, k: (k, i)),
             pl.BlockSpec((None, tk, tn), lambda i, j, k: (j // per, k, j % per))]
    aliases = None
    if prev is not None:
        ops.append(prev)
        specs.append(pl.BlockSpec(memory_space=pl.ANY))
        aliases = {2: 0}
    return _mm(name, tuple(ops), specs, _sds((Kw, n_total or P * Np), BF16),
               pl.BlockSpec((tkr, tn), lambda i, j, k: (i, c0 + j)),
               (Kw // tkr, P * per, M // tk), TN, (tkr, tn), aliases=aliases, comm=comm)


def _rmsnorm_fwd(name, x, g, comm=None):
    T, D = x.shape
    tr = _tile(T, 512, 16)
    n_ci, n_co = (len(comm.arrays), len(comm.out_shapes)) if comm else (0, 0)

    def body(*refs):
        x_ref, g_ref, h_ref = refs[0], refs[1], refs[2 + n_ci]
        c_refs = (refs[2:2 + n_ci], refs[3 + n_ci:3 + n_ci + n_co], refs[3 + n_ci + n_co:])
        if comm:
            @pl.when(pl.program_id(0) == 0)
            def _():
                comm.start(*c_refs)

        xf = x_ref[...]
        r = lax.rsqrt(jnp.mean(xf * xf, axis=-1, keepdims=True) + NORM_EPS)
        h_ref[...] = (xf * r * g_ref[...]).astype(h_ref.dtype)

        if comm:
            @pl.when(pl.program_id(0) == T // tr - 1)
            def _():
                comm.finish(*c_refs)

    outs = pl.pallas_call(
        body, grid=(T // tr,),
        in_specs=[pl.BlockSpec((tr, D), lambda i: (i, 0)), pl.BlockSpec((1, D), lambda i: (0, 0))] + _any_specs(n_ci),
        out_specs=[pl.BlockSpec((tr, D), lambda i: (i, 0))] + _any_specs(n_co),
        out_shape=[_sds((T, D), BF16)] + (comm.out_shapes if comm else []),
        scratch_shapes=comm.sems if comm else [],
        name=name, compiler_params=_cp("arbitrary" if comm else "parallel"))(x, g.reshape(1, D), *(comm.arrays if comm else []))
    return outs[0], list(outs[1:])


def _rmsnorm_bwd(name, x, g, dhs, dres, comm=None):
    T, D = x.shape
    tr = _tile(T, 256, 16)
    n_dh = len(dhs)
    n_in = 3 + n_dh
    n_ci, n_co = (len(comm.arrays), len(comm.out_shapes)) if comm else (0, 0)

    def body(*refs):
        x_ref, g_ref = refs[0], refs[1]
        dh_refs = refs[2:2 + n_dh]
        dres_ref = refs[2 + n_dh]
        dx_ref, dxb_ref, dg_ref = refs[n_in + n_ci:n_in + n_ci + 3]
        c_refs = (refs[n_in:n_in + n_ci], refs[n_in + n_ci + 3:n_in + n_ci + 3 + n_co], refs[n_in + n_ci + 3 + n_co:])
        if comm:
            @pl.when(pl.program_id(0) == 0)
            def _():
                comm.start(*c_refs)

        xf = x_ref[...]
        r = lax.rsqrt(jnp.mean(xf * xf, axis=-1, keepdims=True) + NORM_EPS)
        xhat = xf * r
        dh = dh_refs[0][...].astype(F32)
        for q in dh_refs[1:]:
            dh = dh + q[...].astype(F32)
        dy = dh * g_ref[...]
        c = jnp.mean(dy * xhat, axis=-1, keepdims=True)
        dx = dres_ref[...] + r * (dy - xhat * c)
        dx_ref[...] = dx
        dxb_ref[...] = dx.astype(BF16)

        @pl.when(pl.program_id(0) == 0)
        def _():
            dg_ref[...] = jnp.zeros_like(dg_ref)

        dg_ref[...] += jnp.sum(dh * xhat, axis=0, keepdims=True)

        if comm:
            @pl.when(pl.program_id(0) == T // tr - 1)
            def _():
                comm.finish(*c_refs)

    row = pl.BlockSpec((tr, D), lambda i: (i, 0))
    vec = pl.BlockSpec((1, D), lambda i: (0, 0))
    outs = pl.pallas_call(
        body, grid=(T // tr,), in_specs=[row, vec] + [row] * n_dh + [row] + _any_specs(n_ci),
        out_specs=[row, row, vec] + _any_specs(n_co),
        out_shape=[_sds((T, D), F32), _sds((T, D), BF16), _sds((1, D), F32)] + (comm.out_shapes if comm else []),
        scratch_shapes=comm.sems if comm else [],
        name=name, compiler_params=_cp("arbitrary"))(x, g.reshape(1, D), *dhs, dres, *(comm.arrays if comm else []))
    return (outs[0], outs[1], outs[2][0]), list(outs[3:])


def _final_loss(name, x, g, tgt):
    T, D = x.shape
    tr = _tile(T, 256, 16)

    def body(x_ref, g_ref, t_ref, dx_ref, dxb_ref, dg_ref, loss_ref):
        xf = x_ref[...]
        r = lax.rsqrt(jnp.mean(xf * xf, axis=-1, keepdims=True) + NORM_EPS)
        xhat = xf * r
        err = xhat * g_ref[...] - t_ref[...]
        dy = err * (1.0 / D)
        dxh = dy * g_ref[...]
        c = jnp.mean(dxh * xhat, axis=-1, keepdims=True)
        dx = r * (dxh - xhat * c)
        dx_ref[...] = dx
        dxb_ref[...] = dx.astype(BF16)

        @pl.when(pl.program_id(0) == 0)
        def _():
            dg_ref[...] = jnp.zeros_like(dg_ref)
            loss_ref[...] = jnp.zeros_like(loss_ref)

        dg_ref[...] += jnp.sum(dy * xhat, axis=0, keepdims=True)
        loss_ref[...] += 0.5 * jnp.sum(jnp.mean(err * err, axis=-1, keepdims=True), axis=0, keepdims=True)

    row = pl.BlockSpec((tr, D), lambda i: (i, 0))
    vec = pl.BlockSpec((1, D), lambda i: (0, 0))
    dx, dx_b, dg, loss = pl.pallas_call(
        body, grid=(T // tr,), in_specs=[row, vec, row],
        out_specs=[row, row, vec, pl.BlockSpec((1, 1), lambda i: (0, 0))],
        out_shape=[_sds((T, D), F32), _sds((T, D), BF16), _sds((1, D), F32), _sds((1, 1), F32)],
        name=name, compiler_params=_cp("arbitrary"))(x, g.reshape(1, D), tgt)
    return loss[0, 0], dx, dx_b, dg[0]


def _halo_specs(tr, tc, n_rows, col):
    rb = tr // HALO
    last = n_rows // HALO - 1
    return [pl.BlockSpec((tr, tc), lambda *g: (g[-1], col(*g))),
            pl.BlockSpec((HALO, tc), lambda *g: (jnp.maximum(g[-1] * rb - 1, 0), col(*g))),
            pl.BlockSpec((HALO, tc), lambda *g: (jnp.minimum((g[-1] + 1) * rb, last), col(*g)))]


def _ext(cur_ref, prev_ref, next_ref, i, n_i):
    p = prev_ref[...].astype(F32) * (i > 0).astype(F32)
    n = next_ref[...].astype(F32) * (i < n_i - 1).astype(F32)
    return jnp.concatenate([p, cur_ref[...].astype(F32), n], axis=0)


def _shift_dn(x):
    return pltpu.roll(x, 1, axis=0)


def _shift_up(x):
    return pltpu.roll(x, x.shape[0] - 1, axis=0)


def _conv(x, w_ref, b_ref):
    return w_ref[0:1, :] * _shift_dn(x) + w_ref[1:2, :] * x + w_ref[2:3, :] * _shift_up(x) + b_ref[...]


def _mid(x, tr):
    return x[HALO:HALO + tr, :]


def _conv_t(g, w_ref):
    return w_ref[0:1, :] * _shift_up(g) + w_ref[1:2, :] * g + w_ref[2:3, :] * _shift_dn(g)


def _conv_wgrad(acc_ref, g, x, tr, first):
    gm = _mid(g, tr)

    @pl.when(first)
    def _():
        acc_ref[...] = jnp.zeros_like(acc_ref)

    acc_ref[0:1, :] += jnp.sum(gm * _mid(_shift_dn(x), tr), axis=0, keepdims=True)
    acc_ref[1:2, :] += jnp.sum(gm * _mid(x, tr), axis=0, keepdims=True)
    acc_ref[2:3, :] += jnp.sum(gm * _mid(_shift_up(x), tr), axis=0, keepdims=True)
    acc_ref[3:4, :] += jnp.sum(gm, axis=0, keepdims=True)


def _sigmoid(a):
    return 1.0 / (1.0 + jnp.exp(-a))


def _ffn_gate_fwd(name, up, cw, cb, comm=None):
    T, F2 = up.shape
    F = F2 // 2
    tc, tr = _tile(F, 128, LANES), _tile(T, 2048, HALO)
    nF, n_i = F // tc, T // tr

    n_ci, n_co = (len(comm.arrays), len(comm.out_shapes)) if comm else (0, 0)

    def body(*refs):
        ac, ap, an, bc, bp, bn, wa, wb, ba, bb = refs[:10]
        o_ref = refs[10 + n_ci]
        c_refs = (refs[10:10 + n_ci], refs[11 + n_ci:11 + n_ci + n_co], refs[11 + n_ci + n_co:])
        j, i = pl.program_id(0), pl.program_id(1)
        if comm:
            @pl.when((j == 0) & (i == 0))
            def _():
                comm.start(*c_refs)

        ua = _mid(_conv(_ext(ac, ap, an, i, n_i), wa, ba), tr)
        ub = _mid(_conv(_ext(bc, bp, bn, i, n_i), wb, bb), tr)
        o_ref[...] = (ua * _sigmoid(ua) * ub).astype(o_ref.dtype)

        if comm:
            @pl.when((j == nF - 1) & (i == n_i - 1))
            def _():
                comm.finish(*c_refs)

    wspec = lambda o: pl.BlockSpec((3, tc), lambda j, i: (0, j + o))
    bspec = lambda o: pl.BlockSpec((1, tc), lambda j, i: (0, j + o))
    sem = ("arbitrary", "arbitrary") if comm else ("parallel", "parallel")
    outs = pl.pallas_call(
        body, grid=(nF, n_i),
        in_specs=_halo_specs(tr, tc, T, lambda j, i: j) + _halo_specs(tr, tc, T, lambda j, i: j + nF)
        + [wspec(0), wspec(nF), bspec(0), bspec(nF)] + _any_specs(n_ci),
        out_specs=[pl.BlockSpec((tr, tc), lambda j, i: (i, j))] + _any_specs(n_co),
        out_shape=[_sds((T, F), BF16)] + (comm.out_shapes if comm else []),
        scratch_shapes=comm.sems if comm else [],
        name=name, compiler_params=_cp(*sem))(up, up, up, up, up, up, cw, cw, cb, cb, *(comm.arrays if comm else []))
    return outs[0], list(outs[1:])


def _ffn_gate_bwd(name, up, dact, cw, cb):
    T, F2 = up.shape
    F = F2 // 2
    tc, tr = _tile(F, 128, LANES), _tile(T, 2048, HALO)
    nF, n_i = F // tc, T // tr

    def body(ac, ap, an, bc, bp, bn, dc, dp, dn, wa, wb, ba, bb, o_ref, wga_ref, wgb_ref):
        i = pl.program_id(1)
        xa = _ext(ac, ap, an, i, n_i)
        xb = _ext(bc, bp, bn, i, n_i)
        da = _ext(dc, dp, dn, i, n_i)
        ua = _conv(xa, wa, ba)
        sig = _sigmoid(ua)
        ga = da * _conv(xb, wb, bb) * (sig * (1.0 + ua * (1.0 - sig)))
        o_ref[0] = _mid(_conv_t(ga, wa), tr).astype(o_ref.dtype)
        _conv_wgrad(wga_ref, ga, xa, tr, i == 0)
        gb = da * (ua * sig)
        o_ref[1] = _mid(_conv_t(gb, wb), tr).astype(o_ref.dtype)
        _conv_wgrad(wgb_ref, gb, xb, tr, i == 0)

    wspec = lambda o: pl.BlockSpec((3, tc), lambda j, i: (0, j + o))
    bspec = lambda o: pl.BlockSpec((1, tc), lambda j, i: (0, j + o))
    wg = pl.BlockSpec((8, tc), lambda j, i: (0, j))
    dup, wga, wgb = pl.pallas_call(
        body, grid=(nF, n_i),
        in_specs=_halo_specs(tr, tc, T, lambda j, i: j) + _halo_specs(tr, tc, T, lambda j, i: j + nF)
        + _halo_specs(tr, tc, T, lambda j, i: j) + [wspec(0), wspec(nF), bspec(0), bspec(nF)],
        out_specs=[pl.BlockSpec((2, tr, tc), lambda j, i: (0, i, j)), wg, wg],
        out_shape=[_sds((2, T, F), BF16), _sds((8, F), F32), _sds((8, F), F32)],
        name=name, compiler_params=_cp("parallel", "arbitrary"),
    )(up, up, up, up, up, up, dact, dact, dact, cw, cw, cb, cb)
    return dup, jnp.concatenate([wga, wgb], axis=1)


def _sc_gate_fwd(name, z, cw, cb):
    T, D3 = z.shape
    D = D3 // 3
    tc, tr = _tile(D, 128, LANES), _tile(T, 2048, HALO)
    nD, n_i = D // tc, T // tr

    def body(uc, up_, un, gb, cc, cp, cn, w, b, o_ref):
        i = pl.program_id(1)
        cu = _ext(cc, cp, cn, i, n_i) * _ext(uc, up_, un, i, n_i)
        o_ref[...] = (gb[...].astype(F32) * _mid(_conv(cu, w, b), tr)).astype(o_ref.dtype)

    return pl.pallas_call(
        body, grid=(nD, n_i),
        in_specs=_halo_specs(tr, tc, T, lambda j, i: j) + [pl.BlockSpec((tr, tc), lambda j, i: (i, j + nD))]
        + _halo_specs(tr, tc, T, lambda j, i: j + 2 * nD)
        + [pl.BlockSpec((3, tc), lambda j, i: (0, j)), pl.BlockSpec((1, tc), lambda j, i: (0, j))],
        out_specs=pl.BlockSpec((tr, tc), lambda j, i: (i, j)), out_shape=_sds((T, D), BF16),
        name=name, compiler_params=_cp("parallel", "parallel"))(z, z, z, z, z, z, z, cw, cb)


def _sc_gate_bwd(name, z, dy, cw, cb):
    T, D3 = z.shape
    D = D3 // 3
    tc, tr = _tile(D, 128, LANES), _tile(T, 2048, HALO)
    nD, n_i = D // tc, T // tr

    def body(uc, up_, un, bc, bp, bn, cc, cp, cn, yc, yp, yn, w, b, o_ref, wg_ref):
        i = pl.program_id(1)
        u = _ext(uc, up_, un, i, n_i)
        gc = _ext(cc, cp, cn, i, n_i)
        cu = gc * u
        g = _ext(yc, yp, yn, i, n_i) * _ext(bc, bp, bn, i, n_i)
        dcu = _mid(_conv_t(g, w), tr)
        o_ref[0] = (dcu * _mid(gc, tr)).astype(o_ref.dtype)
        o_ref[1] = (yc[...].astype(F32) * _mid(_conv(cu, w, b), tr)).astype(o_ref.dtype)
        o_ref[2] = (dcu * _mid(u, tr)).astype(o_ref.dtype)
        _conv_wgrad(wg_ref, g, cu, tr, i == 0)

    hs = lambda o: _halo_specs(tr, tc, T, lambda j, i: j + o)
    return pl.pallas_call(
        body, grid=(nD, n_i),
        in_specs=hs(0) + hs(nD) + hs(2 * nD) + hs(0)
        + [pl.BlockSpec((3, tc), lambda j, i: (0, j)), pl.BlockSpec((1, tc), lambda j, i: (0, j))],
        out_specs=[pl.BlockSpec((3, tr, tc), lambda j, i: (0, i, j)), pl.BlockSpec((8, tc), lambda j, i: (0, j))],
        out_shape=[_sds((3, T, D), BF16), _sds((8, D), F32)],
        name=name, compiler_params=_cp("parallel", "arbitrary"),
    )(z, z, z, z, z, z, z, z, z, dy, dy, dy, cw, cb)


def _slopes(n_heads):
    return jnp.asarray(2.0 ** (-ALIBI_MAX * np.arange(1, n_heads + 1) / n_heads), dtype=F32)


CHAINS = 32


def _nq(L, d):
    return max(1, min(CHAINS if d == 1 else CHAINS // 2, L // LANES // 2))


def _srows(ref, r, start, n, d):
    if d == 1:
        return ref[start:start + n, :]
    return ref[pl.ds(start * d + r, n, stride=d), :]


def _win(p_ref, c_ref, n_ref, r, b, nq):
    lo, hi, top = b * LANES - BAND, b * LANES + LANES + BAND, nq * LANES
    parts = [p_ref[r]] if lo < 0 else []
    parts.append(c_ref[r, max(lo, 0):min(hi, top), :])
    if hi > top:
        parts.append(n_ref[r])
    return parts[0] if len(parts) == 1 else jnp.concatenate(parts, axis=0)


def _nat_win(p_ref, c_ref, n_ref, r, b, nq, d):
    lo, hi, top = b * LANES - BAND, b * LANES + LANES + BAND, nq * LANES
    parts = [_srows(p_ref, r, 0, BAND, d)] if lo < 0 else []
    parts.append(_srows(c_ref, r, max(lo, 0), min(hi, top) - max(lo, 0), d))
    if hi > top:
        parts.append(_srows(n_ref, r, 0, BAND, d))
    return parts[0] if len(parts) == 1 else jnp.concatenate(parts, axis=0)


def _biases(slope, d, n, n_steps, nq, q_rows, k_rows, q0, k0):
    qi = lax.broadcasted_iota(jnp.int32, (q_rows, k_rows), 0) + q0
    kj = lax.broadcasted_iota(jnp.int32, (q_rows, k_rows), 1) + k0
    dist = jnp.abs(kj - qi)
    base = jnp.where(dist <= BAND, -slope * (dist * d).astype(F32), NEG_INF)
    out = []
    for b in range(nq):
        t = base
        if b == 0:
            t = jnp.where((n == 0) & ((kj < 0) | (qi < 0)), NEG_INF, t)
        if b == nq - 1:
            t = jnp.where((n == n_steps - 1) & ((kj >= LANES) | (qi >= LANES)), NEG_INF, t)
        out.append(t)
    return out


def _win_specs(d, H, col, nq, L):
    return [pl.BlockSpec((d, BAND, LANES), lambda h, n: (0, jnp.maximum(2 * nq * n - 1, 0), col * H + h)),
            pl.BlockSpec((d, nq * LANES, LANES), lambda h, n: (0, n, col * H + h)),
            pl.BlockSpec((d, BAND, LANES), lambda h, n: (0, jnp.minimum(2 * nq * (n + 1), L // BAND - 1), col * H + h))]


def _nat_specs(d, nq, L):
    return [pl.BlockSpec((BAND * d, LANES), lambda h, n: (jnp.maximum(2 * nq * n - 1, 0), h)),
            pl.BlockSpec((nq * LANES * d, LANES), lambda h, n: (n, h)),
            pl.BlockSpec((BAND * d, LANES), lambda h, n: (jnp.minimum(2 * nq * (n + 1), L // BAND - 1), h))]


def _over_residues(d, nq, per_r):
    if d == 1:
        per_r(0, 0)
    else:
        lax.fori_loop(0, d, per_r, 0, unroll=min(d, max(1, CHAINS // nq)))


def _attn_fwd(name, qkv, d, H):
    T = qkv.shape[0]
    D = H * HEAD_DIM
    L = T // d
    nq = _nq(L, d)
    n_steps = L // (nq * LANES)
    scale = HEAD_DIM ** -0.5
    q3 = qkv.reshape(d, L, 3 * D)

    def body(s_ref, q_ref, kp, kc, kn, vp, vc, vn, o_ref, l_ref):
        h, n = pl.program_id(0), pl.program_id(1)
        bias = _biases(s_ref[h], d, n, n_steps, nq, LANES, 2 * LANES, 0, -BAND)

        def per_r(r, carry):
            for b in range(nq):
                k, v = _win(kp, kc, kn, r, b, nq), _win(vp, vc, vn, r, b, nq)
                s = lax.dot_general(q_ref[r, b * LANES:(b + 1) * LANES, :], k, NT, preferred_element_type=F32) * scale + bias[b]
                m = jnp.max(s, axis=1, keepdims=True)
                p = jnp.exp(s - m)
                den = jnp.sum(p, axis=1, keepdims=True)
                o = lax.dot_general(p.astype(BF16), v, NN, preferred_element_type=F32) / den
                lse = jnp.broadcast_to(m + jnp.log(den), (LANES, LANES))
                if d == 1:
                    o_ref[b * LANES:(b + 1) * LANES, :] = o
                    l_ref[b * LANES:(b + 1) * LANES, :] = lse
                else:
                    o_ref[pl.ds(b * LANES * d + r, LANES, stride=d), :] = o
                    l_ref[pl.ds(b * LANES * d + r, LANES, stride=d), :] = lse
            return carry

        _over_residues(d, nq, per_r)

    out = pl.BlockSpec((nq * LANES * d, LANES), lambda h, n: (n, h))
    return pl.pallas_call(
        body, grid=(H, n_steps),
        in_specs=[pl.BlockSpec(memory_space=pltpu.SMEM), pl.BlockSpec((d, nq * LANES, LANES), lambda h, n: (0, n, h))]
        + _win_specs(d, H, 1, nq, L) + _win_specs(d, H, 2, nq, L),
        out_specs=[out, out], out_shape=[_sds((T, D), F32), _sds((T, D), F32)],
        name=name, compiler_params=_cp("parallel", "parallel"))(_slopes(H), q3, q3, q3, q3, q3, q3, q3)


def _attn_combine(name, outs, lses):
    T, D = outs[0].shape
    tr, tc = _tile(T, 512, 16), _tile(D, 512, LANES)

    def body(o0, o1, o2, l0, l1, l2, ob_ref, l_ref):
        a0, a1, a2 = l0[...], l1[...], l2[...]
        m = jnp.maximum(jnp.maximum(a0, a1), a2)
        e0, e1, e2 = jnp.exp(a0 - m), jnp.exp(a1 - m), jnp.exp(a2 - m)
        z = e0 + e1 + e2
        ob_ref[...] = ((e0 * o0[...] + e1 * o1[...] + e2 * o2[...]) / z).astype(BF16)
        l_ref[...] = m + jnp.log(z)

    blk = pl.BlockSpec((tr, tc), lambda i, j: (i, j))
    return pl.pallas_call(
        body, grid=(T // tr, D // tc), in_specs=[blk] * 6, out_specs=[blk] * 2,
        out_shape=[_sds((T, D), BF16), _sds((T, D), F32)],
        name=name, compiler_params=_cp("parallel", "parallel"))(*outs, *lses)


def _attn_stats(name, do, o, lse):
    T, D = do.shape
    tr = _tile(T, 1024, 16)

    def body(a, b, l, o_ref):
        delta = jnp.broadcast_to(jnp.sum(a[...] * b[...].astype(F32), axis=1, keepdims=True), o_ref.shape)
        lane = lax.broadcasted_iota(jnp.int32, o_ref.shape, 1)
        o_ref[...] = jnp.where(lane < BAND, l[...], delta)

    blk = pl.BlockSpec((tr, LANES), lambda i, j: (i, j))
    return pl.pallas_call(body, grid=(T // tr, D // LANES), in_specs=[blk, blk, blk], out_specs=blk,
                          out_shape=_sds((T, D), F32), name=name, compiler_params=_cp("parallel", "parallel"))(do, o, lse)


def _attn_bwd(name, qkv, do, stats, d, H):
    T = qkv.shape[0]
    D = H * HEAD_DIM
    L = T // d
    nq = _nq(L, d)
    n_steps = L // (nq * LANES)
    scale = HEAD_DIM ** -0.5
    q3 = qkv.reshape(d, L, 3 * D)
    mid = slice(BAND, BAND + LANES)

    def body(s_ref, qp, qc, qn, kp, kc, kn, vp, vc, vn, gp, gc, gn, tp, tc_, tn_, o_ref):
        h, n = pl.program_id(0), pl.program_id(1)
        bias_q = _biases(s_ref[h], d, n, n_steps, nq, LANES, 2 * LANES, 0, -BAND)
        bias_k = _biases(s_ref[h], d, n, n_steps, nq, 2 * LANES, LANES, -BAND, 0)

        def per_r(r, carry):
            for b in range(nq):
                rows = slice(b * LANES, (b + 1) * LANES)
                q_w, k_w, v_w = _win(qp, qc, qn, r, b, nq), _win(kp, kc, kn, r, b, nq), _win(vp, vc, vn, r, b, nq)
                g_w = _nat_win(gp, gc, gn, r, b, nq, d)
                t_w = _nat_win(tp, tc_, tn_, r, b, nq, d)
                g_b = g_w.astype(BF16)
                q_c, k_c, v_c, g_c, t_c = q_w[mid], k_w[mid], v_w[mid], g_b[mid], t_w[mid]
                s = lax.dot_general(q_c, k_w, NT, preferred_element_type=F32) * scale + bias_q[b]
                p = jnp.exp(s - t_c[:, 0:1])
                dp = lax.dot_general(g_c, v_w, NT, preferred_element_type=F32)
                ds = p * (dp - t_c[:, BAND:BAND + 1])
                o_ref[0, r, rows, :] = (lax.dot_general(ds.astype(BF16), k_w, NN, preferred_element_type=F32) * scale).astype(BF16)
                s2 = lax.dot_general(q_w, k_c, NT, preferred_element_type=F32) * scale + bias_k[b]
                p2 = jnp.exp(s2 - t_w[:, 0:1])
                o_ref[2, r, rows, :] = lax.dot_general(p2.astype(BF16), g_b, TN, preferred_element_type=F32).astype(BF16)
                dp2 = lax.dot_general(g_b, v_c, NT, preferred_element_type=F32)
                ds2 = p2 * (dp2 - t_w[:, BAND:BAND + 1])
                o_ref[1, r, rows, :] = (lax.dot_general(ds2.astype(BF16), q_w, TN, preferred_element_type=F32) * scale).astype(BF16)
            return carry

        _over_residues(d, nq, per_r)

    dqkv = pl.pallas_call(
        body, grid=(H, n_steps),
        in_specs=[pl.BlockSpec(memory_space=pltpu.SMEM)]
        + _win_specs(d, H, 0, nq, L) + _win_specs(d, H, 1, nq, L) + _win_specs(d, H, 2, nq, L)
        + _nat_specs(d, nq, L) + _nat_specs(d, nq, L),
        out_specs=pl.BlockSpec((3, d, nq * LANES, LANES), lambda h, n: (0, 0, n, h)),
        out_shape=_sds((3, d, L, D), BF16),
        name=name, compiler_params=_cp("parallel", "parallel"),
    )(_slopes(H), *([q3] * 9), *([do] * 3), *([stats] * 3))
    return dqkv.reshape(3, T, D)


def _to_group_order(a, d):
    if d == 1:
        return a
    T, C = a.shape
    return a.reshape(T // d, d, C).swapaxes(0, 1).reshape(T, C)


def _from_group_order(a, d):
    if d == 1:
        return a
    T, C = a.shape
    return a.reshape(d, T // d, C).swapaxes(0, 1).reshape(T, C)


def _fwd_bwd(x, tgt, S, ex):
    T, D = x.shape
    H = D // HEAD_DIM
    G3 = 3 * D

    def mm(fn, *args, rides=(), **kw):
        out, extra = fn(*args, comm=_join([getattr(ex, kind)(keys) for kind, keys in rides]), **kw)
        at = 0
        for kind, keys in rides:
            getattr(ex, kind + "_done")(keys, extra[at:at + len(keys)])
            at += len(keys)
        return out

    def ffn_fwd(l, xin, rides):
        hf = mm(_rmsnorm_fwd, f"ffn_norm{l}", xin, S["ffn_g"][l])
        up = mm(_mm_nn, f"ffn_up{l}", hf, ex.w(f"up{l}"), None, BF16, rides=rides[0])
        act = mm(_ffn_gate_fwd, f"ffn_gate{l}", up, ffn_cw[l], S["ffn_cb"][l][None], rides=rides[1])
        return hf, up, act, mm(_mm_nn, f"ffn_down{l}", act, ex.w(f"dn{l}"), xin, F32, tk_pref=2816, rides=rides[2])

    def ffn_bwd(l, xin, hf, up, act, dxo, dxo_b, rides):
        dact = mm(_mm_nt, f"ffn_down_dx{l}", dxo_b, ex.w(f"dn{l}"), BF16, tn_pref=1408, rides=rides[0])
        ex.grad(f"dn{l}", mm(_mm_tn, f"ffn_down_dw{l}", act, dxo_b, tkr_pref=1408))
        dup, cg = _ffn_gate_bwd(f"ffn_gate_bwd{l}", up, dact, ffn_cw[l], S["ffn_cb"][l][None])
        dhf = mm(_mm_nt, f"ffn_up_dx{l}", dup, ex.w(f"up{l}"), BF16, tk_pref=2816, rides=rides[1])
        ex.grad(f"up{l}", mm(_mm_tn, f"ffn_up_dw{l}", hf, dup, tn_pref=2816, tk_pref=1024, rides=rides[2]))
        dx, dx_b, dg = mm(_rmsnorm_bwd, f"ffn_norm_bwd{l}", xin, S["ffn_g"][l], [dhf], dxo)
        return dx, dx_b, dg, cg

    h0 = mm(_rmsnorm_fwd, "mix_norm0", x, S["mix_g"][0], rides=[("ag", ["in", "scw", "fcw"])])
    sc_cw = ex.w("scw")
    ffn_cw = ex.w("fcw").reshape(S["ffn_cb"].shape[0], 3, -1)
    z = mm(_mm_nn, "sc_in", h0, ex.w("in"), None, BF16, rides=[("ag", ["sco", "up0"])])
    y = _sc_gate_fwd("sc_gate", z, sc_cw, S["sc_cb"][None])
    x1 = mm(_mm_nn, "sc_out", y, ex.w("sco"), x, F32, tm_pref=512, tn_pref=2048, rides=[("ag", ["dn0"])])
    hf0, up0, act0, x2 = ffn_fwd(0, x1, [[("ag", ["qkv"])], [("ag", ["up1"])], [("ag", ["ao", "dn1"])]])
    h1 = mm(_rmsnorm_fwd, "mix_norm1", x2, S["mix_g"][1])
    hd, qkv, outs, lses = [], [], [], []
    for g, d in enumerate(DILATIONS):
        hd.append(_to_group_order(h1, d))
        qkv.append(mm(_mm_nn, f"attn_qkv{g}", hd[g], ex.w("qkv"), None, BF16, col0=g * G3, n_cols=G3))
        o_g, l_g = _attn_fwd(f"attn_fwd{g}", qkv[g], d, H)
        outs.append(o_g)
        lses.append(l_g)
    o_b, lse = _attn_combine("attn_combine", outs, lses)
    x3 = mm(_mm_nn, "attn_out", o_b, ex.w("ao"), x2, F32, tm_pref=512, tn_pref=2048)
    hf1, up1, act1, x4 = ffn_fwd(1, x3, [(), (), ()])
    loss, dx4, dx4_b, dg_fin = _final_loss("final_loss", x4, S["fin_g"], tgt)

    dx3, dx3_b, dg_f1, cg1 = ffn_bwd(1, x3, hf1, up1, act1, dx4, dx4_b,
                                     [(), [("pair", ["dn1"])], [("chip", ["dn1"])]])
    do = mm(_mm_nt, "attn_out_dx", dx3_b, ex.w("ao"), F32, rides=[("pair", ["up1"])])
    ex.grad("ao", mm(_mm_tn, "attn_out_dw", o_b, dx3_b))
    stats = _attn_stats("attn_stats", do, o_b, lse)
    dhs, dw_qkv = [], None
    qkv_rides = [[("chip", ["up1"]), ("pair", ["ao"])], [("chip", ["ao"])], ()]
    for g, d in enumerate(DILATIONS):
        dqkv = _attn_bwd(f"attn_bwd{g}", qkv[g], do, stats, d, H)
        dhs.append(_from_group_order(mm(_mm_nt, f"attn_qkv_dx{g}", dqkv, ex.w("qkv"), BF16, col0=g * G3,
                                        rides=qkv_rides[g]), d))
        dw_qkv = mm(_mm_tn, f"attn_qkv_dw{g}", hd[g], dqkv, n_total=len(DILATIONS) * G3, col0=g * G3, prev=dw_qkv)
    ex.grad("qkv", dw_qkv)
    dx2, dx2_b, dg_m1 = mm(_rmsnorm_bwd, "mix_norm_bwd1", x2, S["mix_g"][1], dhs, dx3)
    dx1, dx1_b, dg_f0, cg0 = ffn_bwd(0, x1, hf0, up0, act0, dx2, dx2_b,
                                     [[("pair", ["qkv"])], [("chip", ["qkv"]), ("pair", ["dn0"])], [("chip", ["dn0"])]])
    dy = mm(_mm_nt, "sc_out_dx", dx1_b, ex.w("sco"), BF16, rides=[("pair", ["up0"])])
    ex.grad("sco", mm(_mm_tn, "sc_out_dw", y, dx1_b))
    dz, cg_sc = _sc_gate_bwd("sc_gate_bwd", z, dy, sc_cw, S["sc_cb"][None])
    ex.grad("in", mm(_mm_tn, "sc_in_dw", h0, dz, rides=[("chip", ["up0"]), ("pair", ["sco"])]))
    dh0 = mm(_mm_nt, "sc_in_dx", dz, ex.w("in"), BF16, rides=[("chip", ["sco"]), ("pair", ["in"])])
    dx0, _, dg_m0 = mm(_rmsnorm_bwd, "mix_norm_bwd0", x, S["mix_g"][0], [dh0], dx1, rides=[("chip", ["in"])])

    dS = {"mix_g": jnp.stack([dg_m0, dg_m1]), "ffn_g": jnp.stack([dg_f0, dg_f1]), "fin_g": dg_fin,
          "sc_cw": cg_sc[0:3], "sc_cb": cg_sc[3], "ffn_cw": jnp.stack([cg0[0:3], cg1[0:3]]),
          "ffn_cb": jnp.stack([cg0[3], cg1[3]])}
    return loss, dx0, dS


def _place():
    x, y, c = lax.axis_index("x"), lax.axis_index("y"), lax.axis_index("c")
    return x, y, c, [(1 - x, y), (x, 1 - y), (1 - x, 1 - y)]


def _block(ref, s, shape, axis):
    R, C = shape
    if axis == 0:
        return ref.at[pl.ds(pl.multiple_of(s * R, HALO), R), :]
    return ref.at[:, pl.ds(pl.multiple_of(s * C, LANES), C)]


def _whole(shape, axis):
    return (shape[0] * N_DEV, shape[1]) if axis == 0 else (shape[0], shape[1] * N_DEV)


def _ag_stage(arrs, axes):
    n = len(arrs)

    def plan(ins, outs, sems):
        send_sems, recv_sems, local_sems = sems
        x, y, c, chips = _place()
        me, sibling = 4 * x + 2 * y + c, (x, y, 1 - c)

        def copy(a, k, blk, to, src=None):
            dst = _block(outs[a], blk, arrs[a].shape, axes[a])
            return pltpu.make_async_remote_copy(src_ref=dst if src is None else src, dst_ref=dst,
                                                send_sem=send_sems.at[a, k], recv_sem=recv_sems.at[a, k],
                                                device_id=to, device_id_type=MESH)

        mine = [pltpu.make_async_copy(ins[a], _block(outs[a], me, arrs[a].shape, axes[a]), local_sems.at[a])
                for a in range(n)]
        first = []
        for a in range(n):
            first.append(copy(a, 0, me, sibling, src=ins[a]))
            first += [copy(a, 1 + j, me, (*chip, c), src=ins[a]) for j, chip in enumerate(chips)]
        return x, y, c, chips, sibling, copy, mine, first

    def start(ins, outs, sems):
        *_, mine, first = plan(ins, outs, sems)
        for cp in mine + first:
            cp.start()

    def finish(ins, outs, sems):
        x, y, c, chips, sibling, copy, mine, first = plan(ins, outs, sems)
        passed = []
        for j, (px, py) in enumerate(chips):
            for a in range(n):
                blk = 4 * px + 2 * py + c
                copy(a, 1 + j, blk, sibling).wait_recv()
                passed.append(copy(a, 4 + j, blk, sibling))
                passed[-1].start()
        for a in range(n):
            copy(a, 0, 4 * x + 2 * y + 1 - c, sibling).wait_recv()
            for j, (px, py) in enumerate(chips):
                copy(a, 4 + j, 4 * px + 2 * py + 1 - c, sibling).wait_recv()
        for cp in first + passed:
            cp.wait_send()
        for cp in mine:
            cp.wait()

    return _Stage(list(arrs), [_sds(_whole(a.shape, ax), a.dtype) for a, ax in zip(arrs, axes)],
                  [pltpu.SemaphoreType.DMA((n, 7)), pltpu.SemaphoreType.DMA((n, 7)), pltpu.SemaphoreType.DMA((n,))],
                  start, finish)


def _pair_stage(dws, shapes, axes):
    n = len(dws)

    def copies(ins, outs, sems):
        send_sems, recv_sems = sems
        x, y, c, _ = _place()
        return [pltpu.make_async_remote_copy(src_ref=_block(ins[a], 2 * k + 1 - c, shapes[a], axes[a]),
                                             dst_ref=outs[a].at[k],
                                             send_sem=send_sems.at[a, k], recv_sem=recv_sems.at[a, k],
                                             device_id=(x, y, 1 - c), device_id_type=MESH)
                for a in range(n) for k in range(4)]

    def start(ins, outs, sems):
        for cp in copies(ins, outs, sems):
            cp.start()

    def finish(ins, outs, sems):
        for cp in copies(ins, outs, sems):
            cp.wait()

    return _Stage(list(dws), [_sds((4,) + tuple(s), a.dtype) for a, s in zip(dws, shapes)],
                  [pltpu.SemaphoreType.DMA((n, 4)), pltpu.SemaphoreType.DMA((n, 4))], start, finish)


def _rs_pair_add(name, dw, got, c_arr, axis):
    _, R, C = got.shape
    tr, tc = _tile(R, 512, 16), _tile(C, 1536, LANES)
    per = R // tr if axis == 0 else C // tc

    def body(c_ref, a_ref, b_ref, o_ref):
        o_ref[...] = (a_ref[...].astype(F32) + b_ref[...].astype(F32)).astype(o_ref.dtype)

    if axis == 0:
        mine = pl.BlockSpec((tr, tc), lambda k, i, j, c_ref: ((2 * k + c_ref[0]) * per + i, j))
    else:
        mine = pl.BlockSpec((tr, tc), lambda k, i, j, c_ref: (i, (2 * k + c_ref[0]) * per + j))
    return pl.pallas_call(
        body,
        grid_spec=pltpu.PrefetchScalarGridSpec(
            num_scalar_prefetch=1, grid=(4, R // tr, C // tc),
            in_specs=[mine, pl.BlockSpec((None, tr, tc), lambda k, i, j, c_ref: (k, i, j))],
            out_specs=pl.BlockSpec((None, tr, tc), lambda k, i, j, c_ref: (k, i, j))),
        out_shape=_sds((4, R, C), BF16), name=name,
        compiler_params=_cp("parallel", "parallel", "parallel"))(c_arr, dw, got)


def _chip_stage(parts):
    n = len(parts)

    def plan(ins, outs, sems):
        send_sems, recv_sems, local_sems = sems
        x, y, c, chips = _place()
        my_chip = 2 * x + y
        mine = [pltpu.make_async_copy(ins[a].at[my_chip], outs[a].at[my_chip], local_sems.at[a]) for a in range(n)]
        sends = [pltpu.make_async_remote_copy(src_ref=ins[a].at[2 * px + py], dst_ref=outs[a].at[my_chip],
                                              send_sem=send_sems.at[a, j], recv_sem=recv_sems.at[a, j],
                                              device_id=(px, py, c), device_id_type=MESH)
                 for a in range(n) for j, (px, py) in enumerate(chips)]
        arrivals = lambda: [pltpu.make_async_remote_copy(src_ref=ins[a].at[my_chip], dst_ref=outs[a].at[2 * px + py],
                                                         send_sem=send_sems.at[a, j], recv_sem=recv_sems.at[a, j],
                                                         device_id=(px, py, c), device_id_type=MESH)
                            for a in range(n) for j, (px, py) in enumerate(chips)]
        return mine, sends, arrivals

    def start(ins, outs, sems):
        mine, sends, _ = plan(ins, outs, sems)
        for cp in mine + sends:
            cp.start()

    def finish(ins, outs, sems):
        mine, sends, arrivals = plan(ins, outs, sems)
        for cp in arrivals():
            cp.wait_recv()
        for cp in sends:
            cp.wait_send()
        for cp in mine:
            cp.wait()

    return _Stage(list(parts), [_sds(a.shape, a.dtype) for a in parts],
                  [pltpu.SemaphoreType.DMA((n, 3)), pltpu.SemaphoreType.DMA((n, 3)), pltpu.SemaphoreType.DMA((n,))],
                  start, finish)


class _Exchange:
    ROW_SHARDED = ("sco", "ao", "dn0", "dn1")

    def __init__(self, shards, c_arr):
        self.sh, self.c_arr = shards, c_arr
        self.W, self.dw, self.parts, self.sums = {}, {}, {}, {}

    def axis(self, key):
        return 0 if key in self.ROW_SHARDED else 1

    def w(self, key):
        return self.W[key]

    def ag(self, keys):
        return _ag_stage([self.sh[k] for k in keys], [self.axis(k) for k in keys])

    def ag_done(self, keys, outs):
        self.W.update(zip(keys, outs))

    def grad(self, key, dw):
        self.dw[key] = dw

    def pair(self, keys):
        return _pair_stage([self.dw[k] for k in keys], [self.sh[k].shape for k in keys], [self.axis(k) for k in keys])

    def pair_done(self, keys, outs):
        for k, got in zip(keys, outs):
            self.parts[k] = _rs_pair_add(f"rs_add_{k}", self.dw[k], got, self.c_arr, self.axis(k))

    def chip(self, keys):
        return _chip_stage([self.parts[k] for k in keys])

    def chip_done(self, keys, outs):
        self.sums.update(zip(keys, outs))


def _sum_slots(name, a):
    _, rows, _ = a.shape

    def body(a_ref, o_ref):
        s = a_ref[0]
        for k in range(1, N_DEV):
            s = s + a_ref[k]
        o_ref[...] = s

    return pl.pallas_call(body, out_shape=_sds((rows, LANES), F32), name=name)(a)


def _cast_bf16(name, w3, l):
    _, R, C = w3.shape
    tr, tc = _tile(R, 512, 16), _tile(C, 1536, LANES)

    def body(w_ref, o_ref):
        o_ref[...] = w_ref[...].astype(BF16)

    return pl.pallas_call(
        body, grid=(R // tr, C // tc), in_specs=[pl.BlockSpec((None, tr, tc), lambda i, j: (l, i, j))],
        out_specs=pl.BlockSpec((tr, tc), lambda i, j: (i, j)), out_shape=_sds((R, C), BF16),
        name=name, compiler_params=_cp("parallel", "parallel"))(w3)


def _adamw(name, g_slots, w3, m3, v3, l, prev):
    n_slots, R, C = g_slots.shape
    tr, tc = _tile(R, 256, 8), _tile(C, 1536, LANES)
    c1, c2 = 1.0 - ADAM_B1 ** ADAM_STEP, 1.0 - ADAM_B2 ** ADAM_STEP

    def body(g_ref, w_ref, m_ref, v_ref, *rest):
        og, od, om, ov = rest[-4:]
        g = g_ref[0].astype(F32)
        for k in range(1, n_slots):
            g = g + g_ref[k].astype(F32)
        m = ADAM_B1 * m_ref[...] + (1.0 - ADAM_B1) * g
        v = ADAM_B2 * v_ref[...] + (1.0 - ADAM_B2) * (g * g)
        og[...] = g
        om[...] = m
        ov[...] = v
        od[...] = -ADAM_LR * ((m / c1) / (jnp.sqrt(v / c2) + ADAM_EPS) + ADAM_WD * w_ref[...])

    lay = pl.BlockSpec((None, tr, tc), lambda i, j: (l, i, j))
    ops = [g_slots, w3, m3, v3]
    specs = [pl.BlockSpec((n_slots, tr, tc), lambda i, j: (0, i, j)), lay, lay, lay]
    aliases = {}
    if prev is not None:
        ops += list(prev)
        specs += _any_specs(4)
        aliases = {4 + k: k for k in range(4)}
    return pl.pallas_call(
        body, grid=(R // tr, C // tc), in_specs=specs, out_specs=[lay] * 4,
        out_shape=[_sds(w3.shape, F32)] * 4, input_output_aliases=aliases,
        name=name, compiler_params=_cp("parallel", "parallel"))(*ops)


def _pack(parts):
    flat = jnp.concatenate([p.reshape(-1) for p in parts])
    pad = (-flat.shape[0]) % (HALO * LANES)
    return jnp.pad(flat, (0, pad)).reshape(-1, LANES)


def _unpack(packed, shapes):
    flat = packed.reshape(-1)
    out, at = [], 0
    for s in shapes:
        n = int(np.prod(s))
        out.append(flat[at:at + n].reshape(s))
        at += n
    return out


def kernel(x, mix_norm_g, ffn_norm_g, final_norm_g, sc_w_in, sc_conv_w, sc_conv_b, sc_w_out, attn_w_qkv, attn_w_out, ffn_w_up, ffn_conv_w, ffn_conv_b, ffn_w_down, loss_target, m_mix_norm_g, m_ffn_norm_g, m_final_norm_g, m_sc_w_in, m_sc_conv_w, m_sc_conv_b, m_sc_w_out, m_attn_w_qkv, m_attn_w_out, m_ffn_w_up, m_ffn_conv_w, m_ffn_conv_b, m_ffn_w_down, v_mix_norm_g, v_ffn_norm_g, v_final_norm_g, v_sc_w_in, v_sc_conv_w, v_sc_conv_b, v_sc_w_out, v_attn_w_qkv, v_attn_w_out, v_ffn_w_up, v_ffn_conv_w, v_ffn_conv_b, v_ffn_w_down):
    n_layers = ffn_w_up.shape[0]
    me = 4 * lax.axis_index("x") + 2 * lax.axis_index("y") + lax.axis_index("c")
    c_arr = lax.axis_index("c").astype(jnp.int32).reshape(1)

    big = [("in", sc_w_in, 0), ("sco", sc_w_out, 0), ("qkv", attn_w_qkv, 0), ("ao", attn_w_out, 0)]
    big += [(f"up{l}", ffn_w_up, l) for l in range(n_layers)] + [(f"dn{l}", ffn_w_down, l) for l in range(n_layers)]
    shards = {nm: _cast_bf16(f"cast_{nm}", w, l) for nm, w, l in big}
    shards["scw"] = sc_conv_w.reshape(-1, sc_conv_w.shape[-1])
    shards["fcw"] = ffn_conv_w.reshape(-1, ffn_conv_w.shape[-1])
    ex = _Exchange(shards, c_arr)
    S = {"mix_g": mix_norm_g, "ffn_g": ffn_norm_g, "fin_g": final_norm_g, "sc_cb": sc_conv_b[0], "ffn_cb": ffn_conv_b}

    loss_part, grad_x, dS = _fwd_bwd(x[0], loss_target[0], S, ex)

    small_names = ["mix_g", "ffn_g", "fin_g", "sc_cb", "ffn_cb", "sc_cw", "ffn_cw"]
    small_parts = [dS[k] for k in small_names] + [loss_part.reshape(1)]
    small_mine = _pack(small_parts)
    small_all, = _run_stage("gather_small", _ag_stage([small_mine], [0]))
    small_sum = _sum_slots("sum_small", small_all.reshape((N_DEV,) + small_mine.shape))
    g_mix, g_ffn, g_fin, g_scb, g_fcb, g_scw, g_fcw, loss = _unpack(small_sum, [p.shape for p in small_parts])
    g_scw = lax.dynamic_slice_in_dim(g_scw, me * sc_conv_w.shape[-1], sc_conv_w.shape[-1], axis=-1)[None]
    g_fcw = lax.dynamic_slice_in_dim(g_fcw, me * ffn_conv_w.shape[-1], ffn_conv_w.shape[-1], axis=-1)
    g_scb = g_scb[None]
    small_g = [g_mix, g_ffn, g_fin, g_scw, g_scb, g_fcw, g_fcb]
    small_w = [mix_norm_g, ffn_norm_g, final_norm_g, sc_conv_w, sc_conv_b, ffn_conv_w, ffn_conv_b]
    small_m = [m_mix_norm_g, m_ffn_norm_g, m_final_norm_g, m_sc_conv_w, m_sc_conv_b, m_ffn_conv_w, m_ffn_conv_b]
    small_v = [v_mix_norm_g, v_ffn_norm_g, v_final_norm_g, v_sc_conv_w, v_sc_conv_b, v_ffn_conv_w, v_ffn_conv_b]
    small_out = _adamw("adamw_small", _pack(small_g)[None], _pack(small_w)[None], _pack(small_m)[None],
                       _pack(small_v)[None], 0, None)
    small_shapes = [w.shape for w in small_w]
    sg, sd, sm, sv = [_unpack(o[0], small_shapes) for o in small_out]

    moments = {"in": (m_sc_w_in, v_sc_w_in), "sco": (m_sc_w_out, v_sc_w_out), "qkv": (m_attn_w_qkv, v_attn_w_qkv),
               "ao": (m_attn_w_out, v_attn_w_out), "up": (m_ffn_w_up, v_ffn_w_up), "dn": (m_ffn_w_down, v_ffn_w_down)}
    upd = {}
    for nm, w, l in big:
        key = nm.rstrip("0123456789")
        upd[key] = _adamw(f"adamw_{nm}", ex.sums[nm], w, moments[key][0], moments[key][1], l, upd.get(key))

    def leaves(k):
        return [sg, sd, sm, sv][k][0:3] + [upd["in"][k], [sg, sd, sm, sv][k][3], [sg, sd, sm, sv][k][4], upd["sco"][k],
                                          upd["qkv"][k], upd["ao"][k], upd["up"][k], [sg, sd, sm, sv][k][5],
                                          [sg, sd, sm, sv][k][6], upd["dn"][k]]

    return (loss.reshape(()), grad_x[None], *leaves(0), *leaves(1), *leaves(2), *leaves(3))
```

```python
import math

import numpy as np
import jax
import jax.numpy as jnp
from jax import lax
from jax.experimental import pallas as pl
from jax.experimental.pallas import tpu as pltpu

F32 = jnp.float32
BF16 = jnp.bfloat16
MESH = pl.DeviceIdType.MESH

HEAD_DIM = 128
DILATED_GROUPS = ((128, 1), (512, 4), (2048, 16))
DILATIONS = tuple(d for _, d in DILATED_GROUPS)
BAND = (DILATED_GROUPS[0][0] // 2) // DILATED_GROUPS[0][1]
assert all((w // 2) // d == BAND for w, d in DILATED_GROUPS)
NORM_EPS = 1e-5
ALIBI_MAX = 8.0
NEG_INF = -1e30
ADAM_LR, ADAM_B1, ADAM_B2, ADAM_EPS, ADAM_WD, ADAM_STEP = 0.001, 0.9, 0.999, 1e-08, 0.01, 10

N_DEV = 8
LANES = 128
HALO = 16
VMEM_LIMIT = 56 * 1024 * 1024


def _cp(*sem):
    return pltpu.CompilerParams(dimension_semantics=sem, vmem_limit_bytes=VMEM_LIMIT)


def _tile(n, pref, mult):
    t = (min(n, pref) // mult) * mult
    while t >= mult:
        if n % t == 0:
            return t
        t -= mult
    return n


def _sds(shape, dtype):
    return jax.ShapeDtypeStruct(shape, dtype)


def _any_specs(n):
    return [pl.BlockSpec(memory_space=pl.ANY)] * n


class _Stage:
    def __init__(self, arrays, out_shapes, sems, start, finish):
        self.arrays, self.out_shapes, self.sems, self.start, self.finish = arrays, out_shapes, sems, start, finish


def _join(stages):
    stages = [s for s in stages if s is not None]
    if not stages:
        return None

    def split(refs, count):
        out, at = [], 0
        for s in stages:
            out.append(refs[at:at + count(s)])
            at += count(s)
        return out

    def each(which):
        def run(ins, outs, sems):
            parts = zip(split(ins, lambda s: len(s.arrays)), split(outs, lambda s: len(s.out_shapes)),
                        split(sems, lambda s: len(s.sems)))
            for s, (i, o, m) in zip(stages, parts):
                getattr(s, which)(i, o, m)
        return run

    return _Stage(sum([s.arrays for s in stages], []), sum([s.out_shapes for s in stages], []),
                  sum([s.sems for s in stages], []), each("start"), each("finish"))


def _run_stage(name, st):
    n, m = len(st.arrays), len(st.out_shapes)

    def body(*refs):
        ins, outs, sems = refs[:n], refs[n:n + m], refs[n + m:]
        st.start(ins, outs, sems)
        st.finish(ins, outs, sems)

    return pl.pallas_call(body, in_specs=_any_specs(n), out_specs=_any_specs(m), out_shape=st.out_shapes,
                          scratch_shapes=st.sems, name=name)(*st.arrays)


NN = (((1,), (0,)), ((), ()))
NT = (((1,), (1,)), ((), ()))
TN = (((0,), (0,)), ((), ()))


def _mm(name, operands, in_specs, out_sds, o_spec, grid, dims, acc_shape, has_res=False, aliases=None, comm=None):
    nk = grid[2]
    n_in = len(operands)
    n_ci, n_co = (len(comm.arrays), len(comm.out_shapes)) if comm else (0, 0)

    def body(*refs):
        a_ref, b_ref = refs[0], refs[1]
        r_ref = refs[2] if has_res else None
        o_ref = refs[n_in + n_ci]
        acc = refs[n_in + n_ci + 1 + n_co]
        c_refs = (refs[n_in:n_in + n_ci], refs[n_in + n_ci + 1:n_in + n_ci + 1 + n_co], refs[n_in + n_ci + 2 + n_co:])
        ids = [pl.program_id(q) for q in range(3)]
        if comm:
            @pl.when((ids[0] == 0) & (ids[1] == 0) & (ids[2] == 0))
            def _():
                comm.start(*c_refs)

        def finish(total):
            if has_res:
                total = total + r_ref[...]
            o_ref[...] = total.astype(o_ref.dtype)

        if nk == 1:
            finish(lax.dot_general(a_ref[...], b_ref[...], dims, preferred_element_type=F32))
        else:
            k = ids[2]

            @pl.when(k == 0)
            def _():
                acc[...] = jnp.zeros_like(acc)

            acc[...] += lax.dot_general(a_ref[...], b_ref[...], dims, preferred_element_type=F32)

            @pl.when(k == nk - 1)
            def _():
                finish(acc[...])

        if comm:
            @pl.when((ids[0] == grid[0] - 1) & (ids[1] == grid[1] - 1) & (ids[2] == nk - 1))
            def _():
                comm.finish(*c_refs)

    scratch = [pltpu.VMEM(acc_shape if nk > 1 else (8, LANES), F32)]
    if not comm:
        out = pl.pallas_call(
            body, grid=grid, in_specs=in_specs, out_specs=o_spec, out_shape=out_sds, scratch_shapes=scratch,
            input_output_aliases=aliases or {}, name=name,
            compiler_params=_cp("parallel", "parallel", "arbitrary"))(*operands)
        return out, []
    outs = pl.pallas_call(
        body, grid=grid, in_specs=list(in_specs) + _any_specs(n_ci), out_specs=[o_spec] + _any_specs(n_co),
        out_shape=[out_sds] + comm.out_shapes, scratch_shapes=scratch + comm.sems,
        input_output_aliases=aliases or {}, name=name,
        compiler_params=_cp("arbitrary", "arbitrary", "arbitrary"))(*operands, *comm.arrays)
    return outs[0], list(outs[1:])


def _stack(a):
    return a if a.ndim == 3 else a[None]


def _mm_nn(name, a, w, res, out_dtype, col0=0, n_cols=None, tm_pref=1024, tn_pref=1024, tk_pref=2048, comm=None):
    M, K = a.shape
    N = n_cols or w.shape[1]
    tm, tn, tk = _tile(M, tm_pref, 16), _tile(N, tn_pref, LANES), _tile(K, tk_pref, LANES)
    c0 = col0 // tn
    ops = [a, w]
    specs = [pl.BlockSpec((tm, tk), lambda i, j, k: (i, k)), pl.BlockSpec((tk, tn), lambda i, j, k: (k, c0 + j))]
    if res is not None:
        ops.append(res)
        specs.append(pl.BlockSpec((tm, tn), lambda i, j, k: (i, j)))
    return _mm(name, tuple(ops), specs, _sds((M, N), out_dtype), pl.BlockSpec((tm, tn), lambda i, j, k: (i, j)),
               (M // tm, N // tn, K // tk), NN, (tm, tn), has_res=res is not None, comm=comm)


def _mm_nt(name, dy, w, out_dtype, col0=0, tn_pref=1024, tk_pref=2048, comm=None):
    dy = _stack(dy)
    _, M, Np = dy.shape
    Kw = w.shape[0]
    tm, tn, tk = _tile(M, 1024, 16), _tile(Kw, tn_pref, LANES), _tile(Np, tk_pref, LANES)
    per, c0 = Np // tk, col0 // tk
    return _mm(name, (dy, w),
               [pl.BlockSpec((None, tm, tk), lambda i, j, k: (k // per, i, k % per)),
                pl.BlockSpec((tn, tk), lambda i, j, k: (j, c0 + k))],
               _sds((M, Kw), out_dtype), pl.BlockSpec((tm, tn), lambda i, j, k: (i, j)),
               (M // tm, Kw // tn, dy.shape[0] * per), NT, (tm, tn), comm=comm)


def _mm_tn(name, a, dy, n_total=None, col0=0, tkr_pref=1024, tn_pref=1024, tk_pref=2048, prev=None, comm=None):
    dy = _stack(dy)
    P, M, Np = dy.shape
    Kw = a.shape[1]
    tkr, tn, tk = _tile(Kw, tkr_pref, LANES), _tile(Np, tn_pref, LANES), _tile(M, tk_pref, 16)
    per, c0 = Np // tn, col0 // tn
    ops = [a, dy]
    specs = [pl.BlockSpec((tk, tkr), lambda i, j, k: (k, i)),
             pl.BlockSpec((None, tk, tn), lambda i, j, k: (j // per, k, j % per))]
    aliases = None
    if prev is not None:
        ops.append(prev)
        specs.append(pl.BlockSpec(memory_space=pl.ANY))
        aliases = {2: 0}
    return _mm(name, tuple(ops), specs, _sds((Kw, n_total or P * Np), BF16),
               pl.BlockSpec((tkr, tn), lambda i, j, k: (i, c0 + j)),
               (Kw // tkr, P * per, M // tk), TN, (tkr, tn), aliases=aliases, comm=comm)


def _rmsnorm_fwd(name, x, g, comm=None):
    T, D = x.shape
    tr = _tile(T, 512, 16)
    n_ci, n_co = (len(comm.arrays), len(comm.out_shapes)) if comm else (0, 0)

    def body(*refs):
        x_ref, g_ref, h_ref = refs[0], refs[1], refs[2 + n_ci]
        c_refs = (refs[2:2 + n_ci], refs[3 + n_ci:3 + n_ci + n_co], refs[3 + n_ci + n_co:])
        if comm:
            @pl.when(pl.program_id(0) == 0)
            def _():
                comm.start(*c_refs)

        xf = x_ref[...]
        r = lax.rsqrt(jnp.mean(xf * xf, axis=-1, keepdims=True) + NORM_EPS)
        h_ref[...] = (xf * r * g_ref[...]).astype(h_ref.dtype)

        if comm:
            @pl.when(pl.program_id(0) == T // tr - 1)
            def _():
                comm.finish(*c_refs)

    outs = pl.pallas_call(
        body, grid=(T // tr,),
        in_specs=[pl.BlockSpec((tr, D), lambda i: (i, 0)), pl.BlockSpec((1, D), lambda i: (0, 0))] + _any_specs(n_ci),
        out_specs=[pl.BlockSpec((tr, D), lambda i: (i, 0))] + _any_specs(n_co),
        out_shape=[_sds((T, D), BF16)] + (comm.out_shapes if comm else []),
        scratch_shapes=comm.sems if comm else [],
        name=name, compiler_params=_cp("arbitrary" if comm else "parallel"))(x, g.reshape(1, D), *(comm.arrays if comm else []))
    return outs[0], list(outs[1:])


def _rmsnorm_bwd(name, x, g, dhs, dres, comm=None):
    T, D = x.shape
    tr = _tile(T, 256, 16)
    n_dh = len(dhs)
    n_in = 3 + n_dh
    n_ci, n_co = (len(comm.arrays), len(comm.out_shapes)) if comm else (0, 0)

    def body(*refs):
        x_ref, g_ref = refs[0], refs[1]
        dh_refs = refs[2:2 + n_dh]
        dres_ref = refs[2 + n_dh]
        dx_ref, dxb_ref, dg_ref = refs[n_in + n_ci:n_in + n_ci + 3]
        c_refs = (refs[n_in:n_in + n_ci], refs[n_in + n_ci + 3:n_in + n_ci + 3 + n_co], refs[n_in + n_ci + 3 + n_co:])
        if comm:
            @pl.when(pl.program_id(0) == 0)
            def _():
                comm.start(*c_refs)

        xf = x_ref[...]
        r = lax.rsqrt(jnp.mean(xf * xf, axis=-1, keepdims=True) + NORM_EPS)
        xhat = xf * r
        dh = dh_refs[0][...].astype(F32)
        for q in dh_refs[1:]:
            dh = dh + q[...].astype(F32)
        dy = dh * g_ref[...]
        c = jnp.mean(dy * xhat, axis=-1, keepdims=True)
        dx = dres_ref[...] + r * (dy - xhat * c)
        dx_ref[...] = dx
        dxb_ref[...] = dx.astype(BF16)

        @pl.when(pl.program_id(0) == 0)
        def _():
            dg_ref[...] = jnp.zeros_like(dg_ref)

        dg_ref[...] += jnp.sum(dh * xhat, axis=0, keepdims=True)

        if comm:
            @pl.when(pl.program_id(0) == T // tr - 1)
            def _():
                comm.finish(*c_refs)

    row = pl.BlockSpec((tr, D), lambda i: (i, 0))
    vec = pl.BlockSpec((1, D), lambda i: (0, 0))
    outs = pl.pallas_call(
        body, grid=(T // tr,), in_specs=[row, vec] + [row] * n_dh + [row] + _any_specs(n_ci),
        out_specs=[row, row, vec] + _any_specs(n_co),
        out_shape=[_sds((T, D), F32), _sds((T, D), BF16), _sds((1, D), F32)] + (comm.out_shapes if comm else []),
        scratch_shapes=comm.sems if comm else [],
        name=name, compiler_params=_cp("arbitrary"))(x, g.reshape(1, D), *dhs, dres, *(comm.arrays if comm else []))
    return (outs[0], outs[1], outs[2][0]), list(outs[3:])


def _final_loss(name, x, g, tgt):
    T, D = x.shape
    tr = _tile(T, 256, 16)

    def body(x_ref, g_ref, t_ref, dx_ref, dxb_ref, dg_ref, loss_ref):
        xf = x_ref[...]
        r = lax.rsqrt(jnp.mean(xf * xf, axis=-1, keepdims=True) + NORM_EPS)
        xhat = xf * r
        err = xhat * g_ref[...] - t_ref[...]
        dy = err * (1.0 / D)
        dxh = dy * g_ref[...]
        c = jnp.mean(dxh * xhat, axis=-1, keepdims=True)
        dx = r * (dxh - xhat * c)
        dx_ref[...] = dx
        dxb_ref[...] = dx.astype(BF16)

        @pl.when(pl.program_id(0) == 0)
        def _():
            dg_ref[...] = jnp.zeros_like(dg_ref)
            loss_ref[...] = jnp.zeros_like(loss_ref)

        dg_ref[...] += jnp.sum(dy * xhat, axis=0, keepdims=True)
        loss_ref[...] += 0.5 * jnp.sum(jnp.mean(err * err, axis=-1, keepdims=True), axis=0, keepdims=True)

    row = pl.BlockSpec((tr, D), lambda i: (i, 0))
    vec = pl.BlockSpec((1, D), lambda i: (0, 0))
    dx, dx_b, dg, loss = pl.pallas_call(
        body, grid=(T // tr,), in_specs=[row, vec, row],
        out_specs=[row, row, vec, pl.BlockSpec((1, 1), lambda i: (0, 0))],
        out_shape=[_sds((T, D), F32), _sds((T, D), BF16), _sds((1, D), F32), _sds((1, 1), F32)],
        name=name, compiler_params=_cp("arbitrary"))(x, g.reshape(1, D), tgt)
    return loss[0, 0], dx, dx_b, dg[0]


def _halo_specs(tr, tc, n_rows, col):
    rb = tr // HALO
    last = n_rows // HALO - 1
    return [pl.BlockSpec((tr, tc), lambda *g: (g[-1], col(*g))),
            pl.BlockSpec((HALO, tc), lambda *g: (jnp.maximum(g[-1] * rb - 1, 0), col(*g))),
            pl.BlockSpec((HALO, tc), lambda *g: (jnp.minimum((g[-1] + 1) * rb, last), col(*g)))]


def _ext(cur_ref, prev_ref, next_ref, i, n_i):
    p = prev_ref[...].astype(F32) * (i > 0).astype(F32)
    n = next_ref[...].astype(F32) * (i < n_i - 1).astype(F32)
    return jnp.concatenate([p, cur_ref[...].astype(F32), n], axis=0)


def _shift_dn(x):
    return pltpu.roll(x, 1, axis=0)


def _shift_up(x):
    return pltpu.roll(x, x.shape[0] - 1, axis=0)


def _conv(x, w_ref, b_ref):
    return w_ref[0:1, :] * _shift_dn(x) + w_ref[1:2, :] * x + w_ref[2:3, :] * _shift_up(x) + b_ref[...]


def _mid(x, tr):
    return x[HALO:HALO + tr, :]


def _conv_t(g, w_ref):
    return w_ref[0:1, :] * _shift_up(g) + w_ref[1:2, :] * g + w_ref[2:3, :] * _shift_dn(g)


def _conv_wgrad(acc_ref, g, x, tr, first):
    gm = _mid(g, tr)

    @pl.when(first)
    def _():
        acc_ref[...] = jnp.zeros_like(acc_ref)

    acc_ref[0:1, :] += jnp.sum(gm * _mid(_shift_dn(x), tr), axis=0, keepdims=True)
    acc_ref[1:2, :] += jnp.sum(gm * _mid(x, tr), axis=0, keepdims=True)
    acc_ref[2:3, :] += jnp.sum(gm * _mid(_shift_up(x), tr), axis=0, keepdims=True)
    acc_ref[3:4, :] += jnp.sum(gm, axis=0, keepdims=True)


def _sigmoid(a):
    return 1.0 / (1.0 + jnp.exp(-a))


def _ffn_gate_fwd(name, up, cw, cb, comm=None):
    T, F2 = up.shape
    F = F2 // 2
    tc, tr = _tile(F, 128, LANES), _tile(T, 4096, HALO)
    nF, n_i = F // tc, T // tr

    n_ci, n_co = (len(comm.arrays), len(comm.out_shapes)) if comm else (0, 0)

    def body(*refs):
        ac, ap, an, bc, bp, bn, wa, wb, ba, bb = refs[:10]
        o_ref = refs[10 + n_ci]
        c_refs = (refs[10:10 + n_ci], refs[11 + n_ci:11 + n_ci + n_co], refs[11 + n_ci + n_co:])
        j, i = pl.program_id(0), pl.program_id(1)
        if comm:
            @pl.when((j == 0) & (i == 0))
            def _():
                comm.start(*c_refs)

        ua = _mid(_conv(_ext(ac, ap, an, i, n_i), wa, ba), tr)
        ub = _mid(_conv(_ext(bc, bp, bn, i, n_i), wb, bb), tr)
        o_ref[...] = (ua * _sigmoid(ua) * ub).astype(o_ref.dtype)

        if comm:
            @pl.when((j == nF - 1) & (i == n_i - 1))
            def _():
                comm.finish(*c_refs)

    wspec = lambda o: pl.BlockSpec((3, tc), lambda j, i: (0, j + o))
    bspec = lambda o: pl.BlockSpec((1, tc), lambda j, i: (0, j + o))
    sem = ("arbitrary", "arbitrary") if comm else ("parallel", "parallel")
    outs = pl.pallas_call(
        body, grid=(nF, n_i),
        in_specs=_halo_specs(tr, tc, T, lambda j, i: j) + _halo_specs(tr, tc, T, lambda j, i: j + nF)
        + [wspec(0), wspec(nF), bspec(0), bspec(nF)] + _any_specs(n_ci),
        out_specs=[pl.BlockSpec((tr, tc), lambda j, i: (i, j))] + _any_specs(n_co),
        out_shape=[_sds((T, F), BF16)] + (comm.out_shapes if comm else []),
        scratch_shapes=comm.sems if comm else [],
        name=name, compiler_params=_cp(*sem))(up, up, up, up, up, up, cw, cw, cb, cb, *(comm.arrays if comm else []))
    return outs[0], list(outs[1:])


def _ffn_gate_bwd(name, up, dact, cw, cb):
    T, F2 = up.shape
    F = F2 // 2
    tc, tr = _tile(F, 128, LANES), _tile(T, 4096, HALO)
    nF, n_i = F // tc, T // tr

    def body(ac, ap, an, bc, bp, bn, dc, dp, dn, wa, wb, ba, bb, o_ref, wga_ref, wgb_ref):
        i = pl.program_id(1)
        xa = _ext(ac, ap, an, i, n_i)
        xb = _ext(bc, bp, bn, i, n_i)
        da = _ext(dc, dp, dn, i, n_i)
        ua = _conv(xa, wa, ba)
        sig = _sigmoid(ua)
        ga = da * _conv(xb, wb, bb) * (sig * (1.0 + ua * (1.0 - sig)))
        o_ref[0] = _mid(_conv_t(ga, wa), tr).astype(o_ref.dtype)
        _conv_wgrad(wga_ref, ga, xa, tr, i == 0)
        gb = da * (ua * sig)
        o_ref[1] = _mid(_conv_t(gb, wb), tr).astype(o_ref.dtype)
        _conv_wgrad(wgb_ref, gb, xb, tr, i == 0)

    wspec = lambda o: pl.BlockSpec((3, tc), lambda j, i: (0, j + o))
    bspec = lambda o: pl.BlockSpec((1, tc), lambda j, i: (0, j + o))
    wg = pl.BlockSpec((8, tc), lambda j, i: (0, j))
    dup, wga, wgb = pl.pallas_call(
        body, grid=(nF, n_i),
        in_specs=_halo_specs(tr, tc, T, lambda j, i: j) + _halo_specs(tr, tc, T, lambda j, i: j + nF)
        + _halo_specs(tr, tc, T, lambda j, i: j) + [wspec(0), wspec(nF), bspec(0), bspec(nF)],
        out_specs=[pl.BlockSpec((2, tr, tc), lambda j, i: (0, i, j)), wg, wg],
        out_shape=[_sds((2, T, F), BF16), _sds((8, F), F32), _sds((8, F), F32)],
        name=name, compiler_params=_cp("parallel", "arbitrary"),
    )(up, up, up, up, up, up, dact, dact, dact, cw, cw, cb, cb)
    return dup, jnp.concatenate([wga, wgb], axis=1)


def _sc_gate_fwd(name, z, cw, cb):
    T, D3 = z.shape
    D = D3 // 3
    tc, tr = _tile(D, 128, LANES), _tile(T, 4096, HALO)
    nD, n_i = D // tc, T // tr

    def body(uc, up_, un, gb, cc, cp, cn, w, b, o_ref):
        i = pl.program_id(1)
        cu = _ext(cc, cp, cn, i, n_i) * _ext(uc, up_, un, i, n_i)
        o_ref[...] = (gb[...].astype(F32) * _mid(_conv(cu, w, b), tr)).astype(o_ref.dtype)

    return pl.pallas_call(
        body, grid=(nD, n_i),
        in_specs=_halo_specs(tr, tc, T, lambda j, i: j) + [pl.BlockSpec((tr, tc), lambda j, i: (i, j + nD))]
        + _halo_specs(tr, tc, T, lambda j, i: j + 2 * nD)
        + [pl.BlockSpec((3, tc), lambda j, i: (0, j)), pl.BlockSpec((1, tc), lambda j, i: (0, j))],
        out_specs=pl.BlockSpec((tr, tc), lambda j, i: (i, j)), out_shape=_sds((T, D), BF16),
        name=name, compiler_params=_cp("parallel", "parallel"))(z, z, z, z, z, z, z, cw, cb)


def _sc_gate_bwd(name, z, dy, cw, cb):
    T, D3 = z.shape
    D = D3 // 3
    tc, tr = _tile(D, 128, LANES), _tile(T, 4096, HALO)
    nD, n_i = D // tc, T // tr

    def body(uc, up_, un, bc, bp, bn, cc, cp, cn, yc, yp, yn, w, b, o_ref, wg_ref):
        i = pl.program_id(1)
        u = _ext(uc, up_, un, i, n_i)
        gc = _ext(cc, cp, cn, i, n_i)
        cu = gc * u
        g = _ext(yc, yp, yn, i, n_i) * _ext(bc, bp, bn, i, n_i)
        dcu = _mid(_conv_t(g, w), tr)
        o_ref[0] = (dcu * _mid(gc, tr)).astype(o_ref.dtype)
        o_ref[1] = (yc[...].astype(F32) * _mid(_conv(cu, w, b), tr)).astype(o_ref.dtype)
        o_ref[2] = (dcu * _mid(u, tr)).astype(o_ref.dtype)
        _conv_wgrad(wg_ref, g, cu, tr, i == 0)

    hs = lambda o: _halo_specs(tr, tc, T, lambda j, i: j + o)
    return pl.pallas_call(
        body, grid=(nD, n_i),
        in_specs=hs(0) + hs(nD) + hs(2 * nD) + hs(0)
        + [pl.BlockSpec((3, tc), lambda j, i: (0, j)), pl.BlockSpec((1, tc), lambda j, i: (0, j))],
        out_specs=[pl.BlockSpec((3, tr, tc), lambda j, i: (0, i, j)), pl.BlockSpec((8, tc), lambda j, i: (0, j))],
        out_shape=[_sds((3, T, D), BF16), _sds((8, D), F32)],
        name=name, compiler_params=_cp("parallel", "arbitrary"),
    )(z, z, z, z, z, z, z, z, z, dy, dy, dy, cw, cb)


def _slopes(n_heads):
    return jnp.asarray(2.0 ** (-ALIBI_MAX * np.arange(1, n_heads + 1) / n_heads), dtype=F32)


CHAINS = 32


def _nq(L, d):
    return max(1, min(CHAINS if d == 1 else CHAINS // 2, L // LANES // 2))


def _srows(ref, r, start, n, d):
    if d == 1:
        return ref[start:start + n, :]
    return ref[pl.ds(start * d + r, n, stride=d), :]


def _win(p_ref, c_ref, n_ref, r, b, nq):
    lo, hi, top = b * LANES - BAND, b * LANES + LANES + BAND, nq * LANES
    parts = [p_ref[r]] if lo < 0 else []
    parts.append(c_ref[r, max(lo, 0):min(hi, top), :])
    if hi > top:
        parts.append(n_ref[r])
    return parts[0] if len(parts) == 1 else jnp.concatenate(parts, axis=0)


def _nat_win(p_ref, c_ref, n_ref, r, b, nq, d):
    lo, hi, top = b * LANES - BAND, b * LANES + LANES + BAND, nq * LANES
    parts = [_srows(p_ref, r, 0, BAND, d)] if lo < 0 else []
    parts.append(_srows(c_ref, r, max(lo, 0), min(hi, top) - max(lo, 0), d))
    if hi > top:
        parts.append(_srows(n_ref, r, 0, BAND, d))
    return parts[0] if len(parts) == 1 else jnp.concatenate(parts, axis=0)


def _biases(slope, d, n, n_steps, nq, q_rows, k_rows, q0, k0):
    qi = lax.broadcasted_iota(jnp.int32, (q_rows, k_rows), 0) + q0
    kj = lax.broadcasted_iota(jnp.int32, (q_rows, k_rows), 1) + k0
    dist = jnp.abs(kj - qi)
    base = jnp.where(dist <= BAND, -slope * (dist * d).astype(F32), NEG_INF)
    out = []
    for b in range(nq):
        t = base
        if b == 0:
            t = jnp.where((n == 0) & ((kj < 0) | (qi < 0)), NEG_INF, t)
        if b == nq - 1:
            t = jnp.where((n == n_steps - 1) & ((kj >= LANES) | (qi >= LANES)), NEG_INF, t)
        out.append(t)
    return out


def _win_specs(d, H, col, nq, L):
    return [pl.BlockSpec((d, BAND, LANES), lambda h, n: (0, jnp.maximum(2 * nq * n - 1, 0), col * H + h)),
            pl.BlockSpec((d, nq * LANES, LANES), lambda h, n: (0, n, col * H + h)),
            pl.BlockSpec((d, BAND, LANES), lambda h, n: (0, jnp.minimum(2 * nq * (n + 1), L // BAND - 1), col * H + h))]


def _nat_specs(d, nq, L):
    return [pl.BlockSpec((BAND * d, LANES), lambda h, n: (jnp.maximum(2 * nq * n - 1, 0), h)),
            pl.BlockSpec((nq * LANES * d, LANES), lambda h, n: (n, h)),
            pl.BlockSpec((BAND * d, LANES), lambda h, n: (jnp.minimum(2 * nq * (n + 1), L // BAND - 1), h))]


def _over_residues(d, nq, per_r):
    if d == 1:
        per_r(0, 0)
    else:
        lax.fori_loop(0, d, per_r, 0, unroll=min(d, max(1, CHAINS // nq)))


def _attn_fwd(name, qkv, d, H):
    T = qkv.shape[0]
    D = H * HEAD_DIM
    L = T // d
    nq = _nq(L, d)
    n_steps = L // (nq * LANES)
    scale = HEAD_DIM ** -0.5
    q3 = qkv.reshape(d, L, 3 * D)

    def body(s_ref, q_ref, kp, kc, kn, vp, vc, vn, o_ref, l_ref):
        h, n = pl.program_id(0), pl.program_id(1)
        bias = _biases(s_ref[h], d, n, n_steps, nq, LANES, 2 * LANES, 0, -BAND)

        def per_r(r, carry):
            for b in range(nq):
                k, v = _win(kp, kc, kn, r, b, nq), _win(vp, vc, vn, r, b, nq)
                s = lax.dot_general(q_ref[r, b * LANES:(b + 1) * LANES, :], k, NT, preferred_element_type=F32) * scale + bias[b]
                m = jnp.max(s, axis=1, keepdims=True)
                p = jnp.exp(s - m)
                den = jnp.sum(p, axis=1, keepdims=True)
                o = lax.dot_general(p.astype(BF16), v, NN, preferred_element_type=F32) / den
                lse = jnp.broadcast_to(m + jnp.log(den), (LANES, LANES))
                if d == 1:
                    o_ref[b * LANES:(b + 1) * LANES, :] = o
                    l_ref[b * LANES:(b + 1) * LANES, :] = lse
                else:
                    o_ref[pl.ds(b * LANES * d + r, LANES, stride=d), :] = o
                    l_ref[pl.ds(b * LANES * d + r, LANES, stride=d), :] = lse
            return carry

        _over_residues(d, nq, per_r)

    out = pl.BlockSpec((nq * LANES * d, LANES), lambda h, n: (n, h))
    return pl.pallas_call(
        body, grid=(H, n_steps),
        in_specs=[pl.BlockSpec(memory_space=pltpu.SMEM), pl.BlockSpec((d, nq * LANES, LANES), lambda h, n: (0, n, h))]
        + _win_specs(d, H, 1, nq, L) + _win_specs(d, H, 2, nq, L),
        out_specs=[out, out], out_shape=[_sds((T, D), F32), _sds((T, D), F32)],
        name=name, compiler_params=_cp("parallel", "parallel"))(_slopes(H), q3, q3, q3, q3, q3, q3, q3)


def _attn_combine(name, outs, lses):
    T, D = outs[0].shape
    tr, tc = _tile(T, 512, 16), _tile(D, 512, LANES)

    def body(o0, o1, o2, l0, l1, l2, ob_ref, l_ref):
        a0, a1, a2 = l0[...], l1[...], l2[...]
        m = jnp.maximum(jnp.maximum(a0, a1), a2)
        e0, e1, e2 = jnp.exp(a0 - m), jnp.exp(a1 - m), jnp.exp(a2 - m)
        z = e0 + e1 + e2
        ob_ref[...] = ((e0 * o0[...] + e1 * o1[...] + e2 * o2[...]) / z).astype(BF16)
        l_ref[...] = m + jnp.log(z)

    blk = pl.BlockSpec((tr, tc), lambda i, j: (i, j))
    return pl.pallas_call(
        body, grid=(T // tr, D // tc), in_specs=[blk] * 6, out_specs=[blk] * 2,
        out_shape=[_sds((T, D), BF16), _sds((T, D), F32)],
        name=name, compiler_params=_cp("parallel", "parallel"))(*outs, *lses)


def _attn_stats(name, do, o, lse):
    T, D = do.shape
    tr = _tile(T, 1024, 16)

    def body(a, b, l, o_ref):
        delta = jnp.broadcast_to(jnp.sum(a[...] * b[...].astype(F32), axis=1, keepdims=True), o_ref.shape)
        lane = lax.broadcasted_iota(jnp.int32, o_ref.shape, 1)
        o_ref[...] = jnp.where(lane < BAND, l[...], delta)

    blk = pl.BlockSpec((tr, LANES), lambda i, j: (i, j))
    return pl.pallas_call(body, grid=(T // tr, D // LANES), in_specs=[blk, blk, blk], out_specs=blk,
                          out_shape=_sds((T, D), F32), name=name, compiler_params=_cp("parallel", "parallel"))(do, o, lse)


def _attn_bwd(name, qkv, do, stats, d, H):
    T = qkv.shape[0]
    D = H * HEAD_DIM
    L = T // d
    nq = _nq(L, d)
    n_steps = L // (nq * LANES)
    scale = HEAD_DIM ** -0.5
    q3 = qkv.reshape(d, L, 3 * D)
    mid = slice(BAND, BAND + LANES)

    def body(s_ref, qp, qc, qn, kp, kc, kn, vp, vc, vn, gp, gc, gn, tp, tc_, tn_, o_ref):
        h, n = pl.program_id(0), pl.program_id(1)
        bias_q = _biases(s_ref[h], d, n, n_steps, nq, LANES, 2 * LANES, 0, -BAND)
        bias_k = _biases(s_ref[h], d, n, n_steps, nq, 2 * LANES, LANES, -BAND, 0)

        def per_r(r, carry):
            for b in range(nq):
                rows = slice(b * LANES, (b + 1) * LANES)
                q_w, k_w, v_w = _win(qp, qc, qn, r, b, nq), _win(kp, kc, kn, r, b, nq), _win(vp, vc, vn, r, b, nq)
                g_w = _nat_win(gp, gc, gn, r, b, nq, d)
                t_w = _nat_win(tp, tc_, tn_, r, b, nq, d)
                g_b = g_w.astype(BF16)
                q_c, k_c, v_c, g_c, t_c = q_w[mid], k_w[mid], v_w[mid], g_b[mid], t_w[mid]
                s = lax.dot_general(q_c, k_w, NT, preferred_element_type=F32) * scale + bias_q[b]
                p = jnp.exp(s - t_c[:, 0:1])
                dp = lax.dot_general(g_c, v_w, NT, preferred_element_type=F32)
                ds = p * (dp - t_c[:, BAND:BAND + 1])
                o_ref[0, r, rows, :] = (lax.dot_general(ds.astype(BF16), k_w, NN, preferred_element_type=F32) * scale).astype(BF16)
                s2 = lax.dot_general(q_w, k_c, NT, preferred_element_type=F32) * scale + bias_k[b]
                p2 = jnp.exp(s2 - t_w[:, 0:1])
                o_ref[2, r, rows, :] = lax.dot_general(p2.astype(BF16), g_b, TN, preferred_element_type=F32).astype(BF16)
                dp2 = lax.dot_general(g_b, v_c, NT, preferred_element_type=F32)
                ds2 = p2 * (dp2 - t_w[:, BAND:BAND + 1])
                o_ref[1, r, rows, :] = (lax.dot_general(ds2.astype(BF16), q_w, TN, preferred_element_type=F32) * scale).astype(BF16)
            return carry

        _over_residues(d, nq, per_r)

    dqkv = pl.pallas_call(
        body, grid=(H, n_steps),
        in_specs=[pl.BlockSpec(memory_space=pltpu.SMEM)]
        + _win_specs(d, H, 0, nq, L) + _win_specs(d, H, 1, nq, L) + _win_specs(d, H, 2, nq, L)
        + _nat_specs(d, nq, L) + _nat_specs(d, nq, L),
        out_specs=pl.BlockSpec((3, d, nq * LANES, LANES), lambda h, n: (0, 0, n, h)),
        out_shape=_sds((3, d, L, D), BF16),
        name=name, compiler_params=_cp("parallel", "parallel"),
    )(_slopes(H), *([q3] * 9), *([do] * 3), *([stats] * 3))
    return dqkv.reshape(3, T, D)


def _to_group_order(a, d):
    if d == 1:
        return a
    T, C = a.shape
    return a.reshape(T // d, d, C).swapaxes(0, 1).reshape(T, C)


def _from_group_order(a, d):
    if d == 1:
        return a
    T, C = a.shape
    return a.reshape(d, T // d, C).swapaxes(0, 1).reshape(T, C)


def _fwd_bwd(x, tgt, S, ex):
    T, D = x.shape
    H = D // HEAD_DIM
    G3 = 3 * D

    def mm(fn, *args, rides=(), **kw):
        out, extra = fn(*args, comm=_join([getattr(ex, kind)(keys) for kind, keys in rides]), **kw)
        at = 0
        for kind, keys in rides:
            getattr(ex, kind + "_done")(keys, extra[at:at + len(keys)])
            at += len(keys)
        return out

    def ffn_fwd(l, xin, rides):
        hf = mm(_rmsnorm_fwd, f"ffn_norm{l}", xin, S["ffn_g"][l])
        up = mm(_mm_nn, f"ffn_up{l}", hf, ex.w(f"up{l}"), None, BF16, rides=rides[0])
        act = mm(_ffn_gate_fwd, f"ffn_gate{l}", up, ffn_cw[l], S["ffn_cb"][l][None], rides=rides[1])
        return hf, up, act, mm(_mm_nn, f"ffn_down{l}", act, ex.w(f"dn{l}"), xin, F32, tk_pref=2816, rides=rides[2])

    def ffn_bwd(l, xin, hf, up, act, dxo, dxo_b, rides):
        dact = mm(_mm_nt, f"ffn_down_dx{l}", dxo_b, ex.w(f"dn{l}"), BF16, tn_pref=1408, rides=rides[0])
        ex.grad(f"dn{l}", mm(_mm_tn, f"ffn_down_dw{l}", act, dxo_b, tkr_pref=1408))
        dup, cg = _ffn_gate_bwd(f"ffn_gate_bwd{l}", up, dact, ffn_cw[l], S["ffn_cb"][l][None])
        dhf = mm(_mm_nt, f"ffn_up_dx{l}", dup, ex.w(f"up{l}"), BF16, tk_pref=2816, rides=rides[1])
        ex.grad(f"up{l}", mm(_mm_tn, f"ffn_up_dw{l}", hf, dup, tn_pref=2816, tk_pref=1024, rides=rides[2]))
        dx, dx_b, dg = mm(_rmsnorm_bwd, f"ffn_norm_bwd{l}", xin, S["ffn_g"][l], [dhf], dxo)
        return dx, dx_b, dg, cg

    h0 = mm(_rmsnorm_fwd, "mix_norm0", x, S["mix_g"][0], rides=[("ag", ["in", "scw", "fcw"])])
    sc_cw = ex.w("scw")
    ffn_cw = ex.w("fcw").reshape(S["ffn_cb"].shape[0], 3, -1)
    z = mm(_mm_nn, "sc_in", h0, ex.w("in"), None, BF16, rides=[("ag", ["sco", "up0"])])
    y = _sc_gate_fwd("sc_gate", z, sc_cw, S["sc_cb"][None])
    x1 = mm(_mm_nn, "sc_out", y, ex.w("sco"), x, F32, tm_pref=512, tn_pref=2048, rides=[("ag", ["dn0"])])
    hf0, up0, act0, x2 = ffn_fwd(0, x1, [[("ag", ["qkv"])], [("ag", ["up1"])], [("ag", ["ao", "dn1"])]])
    h1 = mm(_rmsnorm_fwd, "mix_norm1", x2, S["mix_g"][1])
    hd, qkv, outs, lses = [], [], [], []
    for g, d in enumerate(DILATIONS):
        hd.append(_to_group_order(h1, d))
        qkv.append(mm(_mm_nn, f"attn_qkv{g}", hd[g], ex.w("qkv"), None, BF16, col0=g * G3, n_cols=G3))
        o_g, l_g = _attn_fwd(f"attn_fwd{g}", qkv[g], d, H)
        outs.append(o_g)
        lses.append(l_g)
    o_b, lse = _attn_combine("attn_combine", outs, lses)
    x3 = mm(_mm_nn, "attn_out", o_b, ex.w("ao"), x2, F32, tm_pref=512, tn_pref=2048)
    hf1, up1, act1, x4 = ffn_fwd(1, x3, [(), (), ()])
    loss, dx4, dx4_b, dg_fin = _final_loss("final_loss", x4, S["fin_g"], tgt)

    dx3, dx3_b, dg_f1, cg1 = ffn_bwd(1, x3, hf1, up1, act1, dx4, dx4_b,
                                     [(), [("pair", ["dn1"])], [("chip", ["dn1"])]])
    do = mm(_mm_nt, "attn_out_dx", dx3_b, ex.w("ao"), F32, rides=[("pair", ["up1"])])
    ex.grad("ao", mm(_mm_tn, "attn_out_dw", o_b, dx3_b))
    stats = _attn_stats("attn_stats", do, o_b, lse)
    dhs, dw_qkv = [], None
    qkv_rides = [[("chip", ["up1"]), ("pair", ["ao"])], [("chip", ["ao"])], ()]
    for g, d in enumerate(DILATIONS):
        dqkv = _attn_bwd(f"attn_bwd{g}", qkv[g], do, stats, d, H)
        dhs.append(_from_group_order(mm(_mm_nt, f"attn_qkv_dx{g}", dqkv, ex.w("qkv"), BF16, col0=g * G3,
                                        rides=qkv_rides[g]), d))
        dw_qkv = mm(_mm_tn, f"attn_qkv_dw{g}", hd[g], dqkv, n_total=len(DILATIONS) * G3, col0=g * G3, prev=dw_qkv)
    ex.grad("qkv", dw_qkv)
    dx2, dx2_b, dg_m1 = mm(_rmsnorm_bwd, "mix_norm_bwd1", x2, S["mix_g"][1], dhs, dx3)
    dx1, dx1_b, dg_f0, cg0 = ffn_bwd(0, x1, hf0, up0, act0, dx2, dx2_b,
                                     [[("pair", ["qkv"])], [("chip", ["qkv"]), ("pair", ["dn0"])], [("chip", ["dn0"])]])
    dy = mm(_mm_nt, "sc_out_dx", dx1_b, ex.w("sco"), BF16, rides=[("pair", ["up0"])])
    ex.grad("sco", mm(_mm_tn, "sc_out_dw", y, dx1_b))
    dz, cg_sc = _sc_gate_bwd("sc_gate_bwd", z, dy, sc_cw, S["sc_cb"][None])
    ex.grad("in", mm(_mm_tn, "sc_in_dw", h0, dz, rides=[("chip", ["up0"]), ("pair", ["sco"])]))
    dh0 = mm(_mm_nt, "sc_in_dx", dz, ex.w("in"), BF16, rides=[("chip", ["sco"]), ("pair", ["in"])])
    dx0, _, dg_m0 = mm(_rmsnorm_bwd, "mix_norm_bwd0", x, S["mix_g"][0], [dh0], dx1, rides=[("chip", ["in"])])

    dS = {"mix_g": jnp.stack([dg_m0, dg_m1]), "ffn_g": jnp.stack([dg_f0, dg_f1]), "fin_g": dg_fin,
          "sc_cw": cg_sc[0:3], "sc_cb": cg_sc[3], "ffn_cw": jnp.stack([cg0[0:3], cg1[0:3]]),
          "ffn_cb": jnp.stack([cg0[3], cg1[3]])}
    return loss, dx0, dS


def _place():
    x, y, c = lax.axis_index("x"), lax.axis_index("y"), lax.axis_index("c")
    return x, y, c, [(1 - x, y), (x, 1 - y), (1 - x, 1 - y)]


def _block(ref, s, shape, axis):
    R, C = shape
    if axis == 0:
        return ref.at[pl.ds(pl.multiple_of(s * R, HALO), R), :]
    return ref.at[:, pl.ds(pl.multiple_of(s * C, LANES), C)]


def _whole(shape, axis):
    return (shape[0] * N_DEV, shape[1]) if axis == 0 else (shape[0], shape[1] * N_DEV)


def _ag_stage(arrs, axes):
    n = len(arrs)

    def plan(ins, outs, sems):
        send_sems, recv_sems, local_sems = sems
        x, y, c, chips = _place()
        me, sibling = 4 * x + 2 * y + c, (x, y, 1 - c)

        def copy(a, k, blk, to, src=None):
            dst = _block(outs[a], blk, arrs[a].shape, axes[a])
            return pltpu.make_async_remote_copy(src_ref=dst if src is None else src, dst_ref=dst,
                                                send_sem=send_sems.at[a, k], recv_sem=recv_sems.at[a, k],
                                                device_id=to, device_id_type=MESH)

        mine = [pltpu.make_async_copy(ins[a], _block(outs[a], me, arrs[a].shape, axes[a]), local_sems.at[a])
                for a in range(n)]
        first = []
        for a in range(n):
            first.append(copy(a, 0, me, sibling, src=ins[a]))
            first += [copy(a, 1 + j, me, (*chip, c), src=ins[a]) for j, chip in enumerate(chips)]
        return x, y, c, chips, sibling, copy, mine, first

    def start(ins, outs, sems):
        *_, mine, first = plan(ins, outs, sems)
        for cp in mine + first:
            cp.start()

    def finish(ins, outs, sems):
        x, y, c, chips, sibling, copy, mine, first = plan(ins, outs, sems)
        passed = []
        for j, (px, py) in enumerate(chips):
            for a in range(n):
                blk = 4 * px + 2 * py + c
                copy(a, 1 + j, blk, sibling).wait_recv()
                passed.append(copy(a, 4 + j, blk, sibling))
                passed[-1].start()
        for a in range(n):
            copy(a, 0, 4 * x + 2 * y + 1 - c, sibling).wait_recv()
            for j, (px, py) in enumerate(chips):
                copy(a, 4 + j, 4 * px + 2 * py + 1 - c, sibling).wait_recv()
        for cp in first + passed:
            cp.wait_send()
        for cp in mine:
            cp.wait()

    return _Stage(list(arrs), [_sds(_whole(a.shape, ax), a.dtype) for a, ax in zip(arrs, axes)],
                  [pltpu.SemaphoreType.DMA((n, 7)), pltpu.SemaphoreType.DMA((n, 7)), pltpu.SemaphoreType.DMA((n,))],
                  start, finish)


def _pair_stage(dws, shapes, axes):
    n = len(dws)

    def copies(ins, outs, sems):
        send_sems, recv_sems = sems
        x, y, c, _ = _place()
        return [pltpu.make_async_remote_copy(src_ref=_block(ins[a], 2 * k + 1 - c, shapes[a], axes[a]),
                                             dst_ref=outs[a].at[k],
                                             send_sem=send_sems.at[a, k], recv_sem=recv_sems.at[a, k],
                                             device_id=(x, y, 1 - c), device_id_type=MESH)
                for a in range(n) for k in range(4)]

    def start(ins, outs, sems):
        for cp in copies(ins, outs, sems):
            cp.start()

    def finish(ins, outs, sems):
        for cp in copies(ins, outs, sems):
            cp.wait()

    return _Stage(list(dws), [_sds((4,) + tuple(s), a.dtype) for a, s in zip(dws, shapes)],
                  [pltpu.SemaphoreType.DMA((n, 4)), pltpu.SemaphoreType.DMA((n, 4))], start, finish)


def _rs_pair_add(name, dw, got, c_arr, axis):
    _, R, C = got.shape
    tr, tc = _tile(R, 512, 16), _tile(C, 1536, LANES)
    per = R // tr if axis == 0 else C // tc

    def body(c_ref, a_ref, b_ref, o_ref):
        o_ref[...] = (a_ref[...].astype(F32) + b_ref[...].astype(F32)).astype(o_ref.dtype)

    if axis == 0:
        mine = pl.BlockSpec((tr, tc), lambda k, i, j, c_ref: ((2 * k + c_ref[0]) * per + i, j))
    else:
        mine = pl.BlockSpec((tr, tc), lambda k, i, j, c_ref: (i, (2 * k + c_ref[0]) * per + j))
    return pl.pallas_call(
        body,
        grid_spec=pltpu.PrefetchScalarGridSpec(
            num_scalar_prefetch=1, grid=(4, R // tr, C // tc),
            in_specs=[mine, pl.BlockSpec((None, tr, tc), lambda k, i, j, c_ref: (k, i, j))],
            out_specs=pl.BlockSpec((None, tr, tc), lambda k, i, j, c_ref: (k, i, j))),
        out_shape=_sds((4, R, C), BF16), name=name,
        compiler_params=_cp("parallel", "parallel", "parallel"))(c_arr, dw, got)


def _chip_stage(parts):
    n = len(parts)

    def plan(ins, outs, sems):
        send_sems, recv_sems, local_sems = sems
        x, y, c, chips = _place()
        my_chip = 2 * x + y
        mine = [pltpu.make_async_copy(ins[a].at[my_chip], outs[a].at[my_chip], local_sems.at[a]) for a in range(n)]
        sends = [pltpu.make_async_remote_copy(src_ref=ins[a].at[2 * px + py], dst_ref=outs[a].at[my_chip],
                                              send_sem=send_sems.at[a, j], recv_sem=recv_sems.at[a, j],
                                              device_id=(px, py, c), device_id_type=MESH)
                 for a in range(n) for j, (px, py) in enumerate(chips)]
        arrivals = lambda: [pltpu.make_async_remote_copy(src_ref=ins[a].at[my_chip], dst_ref=outs[a].at[2 * px + py],
                                                         send_sem=send_sems.at[a, j], recv_sem=recv_sems.at[a, j],
                                                         device_id=(px, py, c), device_id_type=MESH)
                            for a in range(n) for j, (px, py) in enumerate(chips)]
        return mine, sends, arrivals

    def start(ins, outs, sems):
        mine, sends, _ = plan(ins, outs, sems)
        for cp in mine + sends:
            cp.start()

    def finish(ins, outs, sems):
        mine, sends, arrivals = plan(ins, outs, sems)
        for cp in arrivals():
            cp.wait_recv()
        for cp in sends:
            cp.wait_send()
        for cp in mine:
            cp.wait()

    return _Stage(list(parts), [_sds(a.shape, a.dtype) for a in parts],
                  [pltpu.SemaphoreType.DMA((n, 3)), pltpu.SemaphoreType.DMA((n, 3)), pltpu.SemaphoreType.DMA((n,))],
                  start, finish)


class _Exchange:
    ROW_SHARDED = ("sco", "ao", "dn0", "dn1")

    def __init__(self, shards, c_arr):
        self.sh, self.c_arr = shards, c_arr
        self.W, self.dw, self.parts, self.sums = {}, {}, {}, {}

    def axis(self, key):
        return 0 if key in self.ROW_SHARDED else 1

    def w(self, key):
        return self.W[key]

    def ag(self, keys):
        return _ag_stage([self.sh[k] for k in keys], [self.axis(k) for k in keys])

    def ag_done(self, keys, outs):
        self.W.update(zip(keys, outs))

    def grad(self, key, dw):
        self.dw[key] = dw

    def pair(self, keys):
        return _pair_stage([self.dw[k] for k in keys], [self.sh[k].shape for k in keys], [self.axis(k) for k in keys])

    def pair_done(self, keys, outs):
        for k, got in zip(keys, outs):
            self.parts[k] = _rs_pair_add(f"rs_add_{k}", self.dw[k], got, self.c_arr, self.axis(k))

    def chip(self, keys):
        return _chip_stage([self.parts[k] for k in keys])

    def chip_done(self, keys, outs):
        self.sums.update(zip(keys, outs))


def _sum_slots(name, a):
    _, rows, _ = a.shape

    def body(a_ref, o_ref):
        s = a_ref[0]
        for k in range(1, N_DEV):
            s = s + a_ref[k]
        o_ref[...] = s

    return pl.pallas_call(body, out_shape=_sds((rows, LANES), F32), name=name)(a)


def _cast_bf16(name, w3, l):
    _, R, C = w3.shape
    tr, tc = _tile(R, 512, 16), _tile(C, 1536, LANES)

    def body(w_ref, o_ref):
        o_ref[...] = w_ref[...].astype(BF16)

    return pl.pallas_call(
        body, grid=(R // tr, C // tc), in_specs=[pl.BlockSpec((None, tr, tc), lambda i, j: (l, i, j))],
        out_specs=pl.BlockSpec((tr, tc), lambda i, j: (i, j)), out_shape=_sds((R, C), BF16),
        name=name, compiler_params=_cp("parallel", "parallel"))(w3)


def _adamw(name, g_slots, w3, m3, v3, l, prev):
    n_slots, R, C = g_slots.shape
    tr, tc = _tile(R, 256, 8), _tile(C, 1536, LANES)
    c1, c2 = 1.0 - ADAM_B1 ** ADAM_STEP, 1.0 - ADAM_B2 ** ADAM_STEP

    def body(g_ref, w_ref, m_ref, v_ref, *rest):
        og, od, om, ov = rest[-4:]
        g = g_ref[0].astype(F32)
        for k in range(1, n_slots):
            g = g + g_ref[k].astype(F32)
        m = ADAM_B1 * m_ref[...] + (1.0 - ADAM_B1) * g
        v = ADAM_B2 * v_ref[...] + (1.0 - ADAM_B2) * (g * g)
        og[...] = g
        om[...] = m
        ov[...] = v
        od[...] = -ADAM_LR * ((m / c1) / (jnp.sqrt(v / c2) + ADAM_EPS) + ADAM_WD * w_ref[...])

    lay = pl.BlockSpec((None, tr, tc), lambda i, j: (l, i, j))
    ops = [g_slots, w3, m3, v3]
    specs = [pl.BlockSpec((n_slots, tr, tc), lambda i, j: (0, i, j)), lay, lay, lay]
    aliases = {}
    if prev is not None:
        ops += list(prev)
        specs += _any_specs(4)
        aliases = {4 + k: k for k in range(4)}
    return pl.pallas_call(
        body, grid=(R // tr, C // tc), in_specs=specs, out_specs=[lay] * 4,
        out_shape=[_sds(w3.shape, F32)] * 4, input_output_aliases=aliases,
        name=name, compiler_params=_cp("parallel", "parallel"))(*ops)


def _pack(parts):
    flat = jnp.concatenate([p.reshape(-1) for p in parts])
    pad = (-flat.shape[0]) % (HALO * LANES)
    return jnp.pad(flat, (0, pad)).reshape(-1, LANES)


def _unpack(packed, shapes):
    flat = packed.reshape(-1)
    out, at = [], 0
    for s in shapes:
        n = int(np.prod(s))
        out.append(flat[at:at + n].reshape(s))
        at += n
    return out


def kernel(x, mix_norm_g, ffn_norm_g, final_norm_g, sc_w_in, sc_conv_w, sc_conv_b, sc_w_out, attn_w_qkv, attn_w_out, ffn_w_up, ffn_conv_w, ffn_conv_b, ffn_w_down, loss_target, m_mix_norm_g, m_ffn_norm_g, m_final_norm_g, m_sc_w_in, m_sc_conv_w, m_sc_conv_b, m_sc_w_out, m_attn_w_qkv, m_attn_w_out, m_ffn_w_up, m_ffn_conv_w, m_ffn_conv_b, m_ffn_w_down, v_mix_norm_g, v_ffn_norm_g, v_final_norm_g, v_sc_w_in, v_sc_conv_w, v_sc_conv_b, v_sc_w_out, v_attn_w_qkv, v_attn_w_out, v_ffn_w_up, v_ffn_conv_w, v_ffn_conv_b, v_ffn_w_down):
    n_layers = ffn_w_up.shape[0]
    me = 4 * lax.axis_index("x") + 2 * lax.axis_index("y") + lax.axis_index("c")
    c_arr = lax.axis_index("c").astype(jnp.int32).reshape(1)

    big = [("in", sc_w_in, 0), ("sco", sc_w_out, 0), ("qkv", attn_w_qkv, 0), ("ao", attn_w_out, 0)]
    big += [(f"up{l}", ffn_w_up, l) for l in range(n_layers)] + [(f"dn{l}", ffn_w_down, l) for l in range(n_layers)]
    shards = {nm: _cast_bf16(f"cast_{nm}", w, l) for nm, w, l in big}
    shards["scw"] = sc_conv_w.reshape(-1, sc_conv_w.shape[-1])
    shards["fcw"] = ffn_conv_w.reshape(-1, ffn_conv_w.shape[-1])
    ex = _Exchange(shards, c_arr)
    S = {"mix_g": mix_norm_g, "ffn_g": ffn_norm_g, "fin_g": final_norm_g, "sc_cb": sc_conv_b[0], "ffn_cb": ffn_conv_b}

    loss_part, grad_x, dS = _fwd_bwd(x[0], loss_target[0], S, ex)

    small_names = ["mix_g", "ffn_g", "fin_g", "sc_cb", "ffn_cb", "sc_cw", "ffn_cw"]
    small_parts = [dS[k] for k in small_names] + [loss_part.reshape(1)]
    small_mine = _pack(small_parts)
    small_all, = _run_stage("gather_small", _ag_stage([small_mine], [0]))
    small_sum = _sum_slots("sum_small", small_all.reshape((N_DEV,) + small_mine.shape))
    g_mix, g_ffn, g_fin, g_scb, g_fcb, g_scw, g_fcw, loss = _unpack(small_sum, [p.shape for p in small_parts])
    g_scw = lax.dynamic_slice_in_dim(g_scw, me * sc_conv_w.shape[-1], sc_conv_w.shape[-1], axis=-1)[None]
    g_fcw = lax.dynamic_slice_in_dim(g_fcw, me * ffn_conv_w.shape[-1], ffn_conv_w.shape[-1], axis=-1)
    g_scb = g_scb[None]
    small_g = [g_mix, g_ffn, g_fin, g_scw, g_scb, g_fcw, g_fcb]
    small_w = [mix_norm_g, ffn_norm_g, final_norm_g, sc_conv_w, sc_conv_b, ffn_conv_w, ffn_conv_b]
    small_m = [m_mix_norm_g, m_ffn_norm_g, m_final_norm_g, m_sc_conv_w, m_sc_conv_b, m_ffn_conv_w, m_ffn_conv_b]
    small_v = [v_mix_norm_g, v_ffn_norm_g, v_final_norm_g, v_sc_conv_w, v_sc_conv_b, v_ffn_conv_w, v_ffn_conv_b]
    small_out = _adamw("adamw_small", _pack(small_g)[None], _pack(small_w)[None], _pack(small_m)[None],
                       _pack(small_v)[None], 0, None)
    small_shapes = [w.shape for w in small_w]
    sg, sd, sm, sv = [_unpack(o[0], small_shapes) for o in small_out]

    moments = {"in": (m_sc_w_in, v_sc_w_in), "sco": (m_sc_w_out, v_sc_w_out), "qkv": (m_attn_w_qkv, v_attn_w_qkv),
               "ao": (m_attn_w_out, v_attn_w_out), "up": (m_ffn_w_up, v_ffn_w_up), "dn": (m_ffn_w_down, v_ffn_w_down)}
    upd = {}
    for nm, w, l in big:
        key = nm.rstrip("0123456789")
        upd[key] = _adamw(f"adamw_{nm}", ex.sums[nm], w, moments[key][0], moments[key][1], l, upd.get(key))

    def leaves(k):
        return [sg, sd, sm, sv][k][0:3] + [upd["in"][k], [sg, sd, sm, sv][k][3], [sg, sd, sm, sv][k][4], upd["sco"][k],
                                          upd["qkv"][k], upd["ao"][k], upd["up"][k], [sg, sd, sm, sv][k][5],
                                          [sg, sd, sm, sv][k][6], upd["dn"][k]]

    return (loss.reshape(()), grad_x[None], *leaves(0), *leaves(1), *leaves(2), *leaves(3))
```

```python
import math

import numpy as np
import jax
import jax.numpy as jnp
from jax import lax
from jax.experimental import pallas as pl
from jax.experimental.pallas import tpu as pltpu

F32 = jnp.float32
BF16 = jnp.bfloat16
MESH = pl.DeviceIdType.MESH

HEAD_DIM = 128
DILATED_GROUPS = ((128, 1), (512, 4), (2048, 16))
DILATIONS = tuple(d for _, d in DILATED_GROUPS)
BAND = (DILATED_GROUPS[0][0] // 2) // DILATED_GROUPS[0][1]
assert all((w // 2) // d == BAND for w, d in DILATED_GROUPS)
NORM_EPS = 1e-5
ALIBI_MAX = 8.0
NEG_INF = -1e30
ADAM_LR, ADAM_B1, ADAM_B2, ADAM_EPS, ADAM_WD, ADAM_STEP = 0.001, 0.9, 0.999, 1e-08, 0.01, 10

N_DEV = 8
LANES = 128
HALO = 16
VMEM_LIMIT = 56 * 1024 * 1024


def _cp(*sem):
    return pltpu.CompilerParams(dimension_semantics=sem, vmem_limit_bytes=VMEM_LIMIT)


def _tile(n, pref, mult):
    t = (min(n, pref) // mult) * mult
    while t >= mult:
        if n % t == 0:
            return t
        t -= mult
    return n


def _sds(shape, dtype):
    return jax.ShapeDtypeStruct(shape, dtype)


def _any_specs(n):
    return [pl.BlockSpec(memory_space=pl.ANY)] * n


class _Stage:
    def __init__(self, arrays, out_shapes, sems, start, finish):
        self.arrays, self.out_shapes, self.sems, self.start, self.finish = arrays, out_shapes, sems, start, finish


def _join(stages):
    stages = [s for s in stages if s is not None]
    if not stages:
        return None

    def split(refs, count):
        out, at = [], 0
        for s in stages:
            out.append(refs[at:at + count(s)])
            at += count(s)
        return out

    def each(which):
        def run(ins, outs, sems):
            parts = zip(split(ins, lambda s: len(s.arrays)), split(outs, lambda s: len(s.out_shapes)),
                        split(sems, lambda s: len(s.sems)))
            for s, (i, o, m) in zip(stages, parts):
                getattr(s, which)(i, o, m)
        return run

    return _Stage(sum([s.arrays for s in stages], []), sum([s.out_shapes for s in stages], []),
                  sum([s.sems for s in stages], []), each("start"), each("finish"))


def _run_stage(name, st):
    n, m = len(st.arrays), len(st.out_shapes)

    def body(*refs):
        ins, outs, sems = refs[:n], refs[n:n + m], refs[n + m:]
        st.start(ins, outs, sems)
        st.finish(ins, outs, sems)

    return pl.pallas_call(body, in_specs=_any_specs(n), out_specs=_any_specs(m), out_shape=st.out_shapes,
                          scratch_shapes=st.sems, name=name)(*st.arrays)


NN = (((1,), (0,)), ((), ()))
NT = (((1,), (1,)), ((), ()))
TN = (((0,), (0,)), ((), ()))


def _mm(name, operands, in_specs, out_sds, o_spec, grid, dims, acc_shape, has_res=False, aliases=None, comm=None):
    nk = grid[2]
    n_in = len(operands)
    n_ci, n_co = (len(comm.arrays), len(comm.out_shapes)) if comm else (0, 0)

    def body(*refs):
        a_ref, b_ref = refs[0], refs[1]
        r_ref = refs[2] if has_res else None
        o_ref = refs[n_in + n_ci]
        acc = refs[n_in + n_ci + 1 + n_co]
        c_refs = (refs[n_in:n_in + n_ci], refs[n_in + n_ci + 1:n_in + n_ci + 1 + n_co], refs[n_in + n_ci + 2 + n_co:])
        ids = [pl.program_id(q) for q in range(3)]
        if comm:
            @pl.when((ids[0] == 0) & (ids[1] == 0) & (ids[2] == 0))
            def _():
                comm.start(*c_refs)

        def finish(total):
            if has_res:
                total = total + r_ref[...]
            o_ref[...] = total.astype(o_ref.dtype)

        if nk == 1:
            finish(lax.dot_general(a_ref[...], b_ref[...], dims, preferred_element_type=F32))
        else:
            k = ids[2]

            @pl.when(k == 0)
            def _():
                acc[...] = jnp.zeros_like(acc)

            acc[...] += lax.dot_general(a_ref[...], b_ref[...], dims, preferred_element_type=F32)

            @pl.when(k == nk - 1)
            def _():
                finish(acc[...])

        if comm:
            @pl.when((ids[0] == grid[0] - 1) & (ids[1] == grid[1] - 1) & (ids[2] == nk - 1))
            def _():
                comm.finish(*c_refs)

    scratch = [pltpu.VMEM(acc_shape if nk > 1 else (8, LANES), F32)]
    if not comm:
        out = pl.pallas_call(
            body, grid=grid, in_specs=in_specs, out_specs=o_spec, out_shape=out_sds, scratch_shapes=scratch,
            input_output_aliases=aliases or {}, name=name,
            compiler_params=_cp("parallel", "parallel", "arbitrary"))(*operands)
        return out, []
    outs = pl.pallas_call(
        body, grid=grid, in_specs=list(in_specs) + _any_specs(n_ci), out_specs=[o_spec] + _any_specs(n_co),
        out_shape=[out_sds] + comm.out_shapes, scratch_shapes=scratch + comm.sems,
        input_output_aliases=aliases or {}, name=name,
        compiler_params=_cp("arbitrary", "arbitrary", "arbitrary"))(*operands, *comm.arrays)
    return outs[0], list(outs[1:])


def _stack(a):
    return a if a.ndim == 3 else a[None]


def _mm_nn(name, a, w, res, out_dtype, col0=0, n_cols=None, tm_pref=1024, tn_pref=1024, tk_pref=2048, comm=None):
    M, K = a.shape
    N = n_cols or w.shape[1]
    tm, tn, tk = _tile(M, tm_pref, 16), _tile(N, tn_pref, LANES), _tile(K, tk_pref, LANES)
    c0 = col0 // tn
    ops = [a, w]
    specs = [pl.BlockSpec((tm, tk), lambda i, j, k: (i, k)), pl.BlockSpec((tk, tn), lambda i, j, k: (k, c0 + j))]
    if res is not None:
        ops.append(res)
        specs.append(pl.BlockSpec((tm, tn), lambda i, j, k: (i, j)))
    return _mm(name, tuple(ops), specs, _sds((M, N), out_dtype), pl.BlockSpec((tm, tn), lambda i, j, k: (i, j)),
               (M // tm, N // tn, K // tk), NN, (tm, tn), has_res=res is not None, comm=comm)


def _mm_nt(name, dy, w, out_dtype, col0=0, tn_pref=1024, tk_pref=2048, comm=None):
    dy = _stack(dy)
    _, M, Np = dy.shape
    Kw = w.shape[0]
    tm, tn, tk = _tile(M, 1024, 16), _tile(Kw, tn_pref, LANES), _tile(Np, tk_pref, LANES)
    per, c0 = Np // tk, col0 // tk
    return _mm(name, (dy, w),
               [pl.BlockSpec((None, tm, tk), lambda i, j, k: (k // per, i, k % per)),
                pl.BlockSpec((tn, tk), lambda i, j, k: (j, c0 + k))],
               _sds((M, Kw), out_dtype), pl.BlockSpec((tm, tn), lambda i, j, k: (i, j)),
               (M // tm, Kw // tn, dy.shape[0] * per), NT, (tm, tn), comm=comm)


def _mm_tn(name, a, dy, n_total=None, col0=0, tkr_pref=1024, tn_pref=1024, tk_pref=2048, prev=None, comm=None):
    dy = _stack(dy)
    P, M, Np = dy.shape
    Kw = a.shape[1]
    tkr, tn, tk = _tile(Kw, tkr_pref, LANES), _tile(Np, tn_pref, LANES), _tile(M, tk_pref, 16)
    per, c0 = Np // tn, col0 // tn
    ops = [a, dy]
    specs = [pl.BlockSpec((tk, tkr), lambda i, j, k: (k, i)),
             pl.BlockSpec((None, tk, tn), lambda i, j, k: (j // per, k, j % per))]
    aliases = None
    if prev is not None:
        ops.append(prev)
        specs.append(pl.BlockSpec(memory_space=pl.ANY))
        aliases = {2: 0}
    return _mm(name, tuple(ops), specs, _sds((Kw, n_total or P * Np), BF16),
               pl.BlockSpec((tkr, tn), lambda i, j, k: (i, c0 + j)),
               (Kw // tkr, P * per, M // tk), TN, (tkr, tn), aliases=aliases, comm=comm)


def _rmsnorm_fwd(name, x, g, comm=None):
    T, D = x.shape
    tr = _tile(T, 512, 16)
    n_ci, n_co = (len(comm.arrays), len(comm.out_shapes)) if comm else (0, 0)

    def body(*refs):
        x_ref, g_ref, h_ref = refs[0], refs[1], refs[2 + n_ci]
        c_refs = (refs[2:2 + n_ci], refs[3 + n_ci:3 + n_ci + n_co], refs[3 + n_ci + n_co:])
        if comm:
            @pl.when(pl.program_id(0) == 0)
            def _():
                comm.start(*c_refs)

        xf = x_ref[...]
        r = lax.rsqrt(jnp.mean(xf * xf, axis=-1, keepdims=True) + NORM_EPS)
        h_ref[...] = (xf * r * g_ref[...]).astype(h_ref.dtype)

        if comm:
            @pl.when(pl.program_id(0) == T // tr - 1)
            def _():
                comm.finish(*c_refs)

    outs = pl.pallas_call(
        body, grid=(T // tr,),
        in_specs=[pl.BlockSpec((tr, D), lambda i: (i, 0)), pl.BlockSpec((1, D), lambda i: (0, 0))] + _any_specs(n_ci),
        out_specs=[pl.BlockSpec((tr, D), lambda i: (i, 0))] + _any_specs(n_co),
        out_shape=[_sds((T, D), BF16)] + (comm.out_shapes if comm else []),
        scratch_shapes=comm.sems if comm else [],
        name=name, compiler_params=_cp("arbitrary" if comm else "parallel"))(x, g.reshape(1, D), *(comm.arrays if comm else []))
    return outs[0], list(outs[1:])


def _rmsnorm_bwd(name, x, g, dhs, dres, comm=None):
    T, D = x.shape
    tr = _tile(T, 256, 16)
    n_dh = len(dhs)
    n_in = 3 + n_dh
    n_ci, n_co = (len(comm.arrays), len(comm.out_shapes)) if comm else (0, 0)

    def body(*refs):
        x_ref, g_ref = refs[0], refs[1]
        dh_refs = refs[2:2 + n_dh]
        dres_ref = refs[2 + n_dh]
        dx_ref, dxb_ref, dg_ref = refs[n_in + n_ci:n_in + n_ci + 3]
        c_refs = (refs[n_in:n_in + n_ci], refs[n_in + n_ci + 3:n_in + n_ci + 3 + n_co], refs[n_in + n_ci + 3 + n_co:])
        if comm:
            @pl.when(pl.program_id(0) == 0)
            def _():
                comm.start(*c_refs)

        xf = x_ref[...]
        r = lax.rsqrt(jnp.mean(xf * xf, axis=-1, keepdims=True) + NORM_EPS)
        xhat = xf * r
        dh = dh_refs[0][...].astype(F32)
        for q in dh_refs[1:]:
            dh = dh + q[...].astype(F32)
        dy = dh * g_ref[...]
        c = jnp.mean(dy * xhat, axis=-1, keepdims=True)
        dx = dres_ref[...] + r * (dy - xhat * c)
        dx_ref[...] = dx
        dxb_ref[...] = dx.astype(BF16)

        @pl.when(pl.program_id(0) == 0)
        def _():
            dg_ref[...] = jnp.zeros_like(dg_ref)

        dg_ref[...] += jnp.sum(dh * xhat, axis=0, keepdims=True)

        if comm:
            @pl.when(pl.program_id(0) == T // tr - 1)
            def _():
                comm.finish(*c_refs)

    row = pl.BlockSpec((tr, D), lambda i: (i, 0))
    vec = pl.BlockSpec((1, D), lambda i: (0, 0))
    outs = pl.pallas_call(
        body, grid=(T // tr,), in_specs=[row, vec] + [row] * n_dh + [row] + _any_specs(n_ci),
        out_specs=[row, row, vec] + _any_specs(n_co),
        out_shape=[_sds((T, D), F32), _sds((T, D), BF16), _sds((1, D), F32)] + (comm.out_shapes if comm else []),
        scratch_shapes=comm.sems if comm else [],
        name=name, compiler_params=_cp("arbitrary"))(x, g.reshape(1, D), *dhs, dres, *(comm.arrays if comm else []))
    return (outs[0], outs[1], outs[2][0]), list(outs[3:])


def _final_loss(name, x, g, tgt):
    T, D = x.shape
    tr = _tile(T, 256, 16)

    def body(x_ref, g_ref, t_ref, dx_ref, dxb_ref, dg_ref, loss_ref):
        xf = x_ref[...]
        r = lax.rsqrt(jnp.mean(xf * xf, axis=-1, keepdims=True) + NORM_EPS)
        xhat = xf * r
        err = xhat * g_ref[...] - t_ref[...]
        dy = err * (1.0 / D)
        dxh = dy * g_ref[...]
        c = jnp.mean(dxh * xhat, axis=-1, keepdims=True)
        dx = r * (dxh - xhat * c)
        dx_ref[...] = dx
        dxb_ref[...] = dx.astype(BF16)

        @pl.when(pl.program_id(0) == 0)
        def _():
            dg_ref[...] = jnp.zeros_like(dg_ref)
            loss_ref[...] = jnp.zeros_like(loss_ref)

        dg_ref[...] += jnp.sum(dy * xhat, axis=0, keepdims=True)
        loss_ref[...] += 0.5 * jnp.sum(jnp.mean(err * err, axis=-1, keepdims=True), axis=0, keepdims=True)

    row = pl.BlockSpec((tr, D), lambda i: (i, 0))
    vec = pl.BlockSpec((1, D), lambda i: (0, 0))
    dx, dx_b, dg, loss = pl.pallas_call(
        body, grid=(T // tr,), in_specs=[row, vec, row],
        out_specs=[row, row, vec, pl.BlockSpec((1, 1), lambda i: (0, 0))],
        out_shape=[_sds((T, D), F32), _sds((T, D), BF16), _sds((1, D), F32), _sds((1, 1), F32)],
        name=name, compiler_params=_cp("arbitrary"))(x, g.reshape(1, D), tgt)
    return loss[0, 0], dx, dx_b, dg[0]


def _halo_specs(tr, tc, n_rows, col):
    rb = tr // HALO
    last = n_rows // HALO - 1
    return [pl.BlockSpec((tr, tc), lambda *g: (g[-1], col(*g))),
            pl.BlockSpec((HALO, tc), lambda *g: (jnp.maximum(g[-1] * rb - 1, 0), col(*g))),
            pl.BlockSpec((HALO, tc), lambda *g: (jnp.minimum((g[-1] + 1) * rb, last), col(*g)))]


def _ext(cur_ref, prev_ref, next_ref, i, n_i):
    p = prev_ref[...].astype(F32) * (i > 0).astype(F32)
    n = next_ref[...].astype(F32) * (i < n_i - 1).astype(F32)
    return jnp.concatenate([p, cur_ref[...].astype(F32), n], axis=0)


def _shift_dn(x):
    return pltpu.roll(x, 1, axis=0)


def _shift_up(x):
    return pltpu.roll(x, x.shape[0] - 1, axis=0)


def _conv(x, w_ref, b_ref):
    return w_ref[0:1, :] * _shift_dn(x) + w_ref[1:2, :] * x + w_ref[2:3, :] * _shift_up(x) + b_ref[...]


def _mid(x, tr):
    return x[HALO:HALO + tr, :]


def _conv_t(g, w_ref):
    return w_ref[0:1, :] * _shift_up(g) + w_ref[1:2, :] * g + w_ref[2:3, :] * _shift_dn(g)


def _conv_wgrad(acc_ref, g, x, tr, first):
    gm = _mid(g, tr)

    @pl.when(first)
    def _():
        acc_ref[...] = jnp.zeros_like(acc_ref)

    acc_ref[0:1, :] += jnp.sum(gm * _mid(_shift_dn(x), tr), axis=0, keepdims=True)
    acc_ref[1:2, :] += jnp.sum(gm * _mid(x, tr), axis=0, keepdims=True)
    acc_ref[2:3, :] += jnp.sum(gm * _mid(_shift_up(x), tr), axis=0, keepdims=True)
    acc_ref[3:4, :] += jnp.sum(gm, axis=0, keepdims=True)


def _sigmoid(a):
    return 1.0 / (1.0 + jnp.exp(-a))


def _ffn_gate_fwd(name, up, cw, cb, comm=None):
    T, F2 = up.shape
    F = F2 // 2
    tc, tr = _tile(F, 128, LANES), _tile(T, 4096, HALO)
    nF, n_i = F // tc, T // tr

    n_ci, n_co = (len(comm.arrays), len(comm.out_shapes)) if comm else (0, 0)

    def body(*refs):
        ac, ap, an, bc, bp, bn, wa, wb, ba, bb = refs[:10]
        o_ref = refs[10 + n_ci]
        c_refs = (refs[10:10 + n_ci], refs[11 + n_ci:11 + n_ci + n_co], refs[11 + n_ci + n_co:])
        j, i = pl.program_id(0), pl.program_id(1)
        if comm:
            @pl.when((j == 0) & (i == 0))
            def _():
                comm.start(*c_refs)

        ua = _mid(_conv(_ext(ac, ap, an, i, n_i), wa, ba), tr)
        ub = _mid(_conv(_ext(bc, bp, bn, i, n_i), wb, bb), tr)
        o_ref[...] = (ua * _sigmoid(ua) * ub).astype(o_ref.dtype)

        if comm:
            @pl.when((j == nF - 1) & (i == n_i - 1))
            def _():
                comm.finish(*c_refs)

    wspec = lambda o: pl.BlockSpec((3, tc), lambda j, i: (0, j + o))
    bspec = lambda o: pl.BlockSpec((1, tc), lambda j, i: (0, j + o))
    sem = ("arbitrary", "arbitrary") if comm else ("parallel", "parallel")
    outs = pl.pallas_call(
        body, grid=(nF, n_i),
        in_specs=_halo_specs(tr, tc, T, lambda j, i: j) + _halo_specs(tr, tc, T, lambda j, i: j + nF)
        + [wspec(0), wspec(nF), bspec(0), bspec(nF)] + _any_specs(n_ci),
        out_specs=[pl.BlockSpec((tr, tc), lambda j, i: (i, j))] + _any_specs(n_co),
        out_shape=[_sds((T, F), BF16)] + (comm.out_shapes if comm else []),
        scratch_shapes=comm.sems if comm else [],
        name=name, compiler_params=_cp(*sem))(up, up, up, up, up, up, cw, cw, cb, cb, *(comm.arrays if comm else []))
    return outs[0], list(outs[1:])


def _ffn_gate_bwd(name, up, dact, cw, cb):
    T, F2 = up.shape
    F = F2 // 2
    tc, tr = _tile(F, 128, LANES), _tile(T, 4096, HALO)
    nF, n_i = F // tc, T // tr

    def body(ac, ap, an, bc, bp, bn, dc, dp, dn, wa, wb, ba, bb, o_ref, wga_ref, wgb_ref):
        i = pl.program_id(1)
        xa = _ext(ac, ap, an, i, n_i)
        xb = _ext(bc, bp, bn, i, n_i)
        da = _ext(dc, dp, dn, i, n_i)
        ua = _conv(xa, wa, ba)
        sig = _sigmoid(ua)
        ga = da * _conv(xb, wb, bb) * (sig * (1.0 + ua * (1.0 - sig)))
        o_ref[0] = _mid(_conv_t(ga, wa), tr).astype(o_ref.dtype)
        _conv_wgrad(wga_ref, ga, xa, tr, i == 0)
        gb = da * (ua * sig)
        o_ref[1] = _mid(_conv_t(gb, wb), tr).astype(o_ref.dtype)
        _conv_wgrad(wgb_ref, gb, xb, tr, i == 0)

    wspec = lambda o: pl.BlockSpec((3, tc), lambda j, i: (0, j + o))
    bspec = lambda o: pl.BlockSpec((1, tc), lambda j, i: (0, j + o))
    wg = pl.BlockSpec((8, tc), lambda j, i: (0, j))
    dup, wga, wgb = pl.pallas_call(
        body, grid=(nF, n_i),
        in_specs=_halo_specs(tr, tc, T, lambda j, i: j) + _halo_specs(tr, tc, T, lambda j, i: j + nF)
        + _halo_specs(tr, tc, T, lambda j, i: j) + [wspec(0), wspec(nF), bspec(0), bspec(nF)],
        out_specs=[pl.BlockSpec((2, tr, tc), lambda j, i: (0, i, j)), wg, wg],
        out_shape=[_sds((2, T, F), BF16), _sds((8, F), F32), _sds((8, F), F32)],
        name=name, compiler_params=_cp("parallel", "arbitrary"),
    )(up, up, up, up, up, up, dact, dact, dact, cw, cw, cb, cb)
    return dup, jnp.concatenate([wga, wgb], axis=1)


def _sc_gate_fwd(name, z, cw, cb):
    T, D3 = z.shape
    D = D3 // 3
    tc, tr = _tile(D, 128, LANES), _tile(T, 4096, HALO)
    nD, n_i = D // tc, T // tr

    def body(uc, up_, un, gb, cc, cp, cn, w, b, o_ref):
        i = pl.program_id(1)
        cu = _ext(cc, cp, cn, i, n_i) * _ext(uc, up_, un, i, n_i)
        o_ref[...] = (gb[...].astype(F32) * _mid(_conv(cu, w, b), tr)).astype(o_ref.dtype)

    return pl.pallas_call(
        body, grid=(nD, n_i),
        in_specs=_halo_specs(tr, tc, T, lambda j, i: j) + [pl.BlockSpec((tr, tc), lambda j, i: (i, j + nD))]
        + _halo_specs(tr, tc, T, lambda j, i: j + 2 * nD)
        + [pl.BlockSpec((3, tc), lambda j, i: (0, j)), pl.BlockSpec((1, tc), lambda j, i: (0, j))],
        out_specs=pl.BlockSpec((tr, tc), lambda j, i: (i, j)), out_shape=_sds((T, D), BF16),
        name=name, compiler_params=_cp("parallel", "parallel"))(z, z, z, z, z, z, z, cw, cb)


def _sc_gate_bwd(name, z, dy, cw, cb):
    T, D3 = z.shape
    D = D3 // 3
    tc, tr = _tile(D, 128, LANES), _tile(T, 4096, HALO)
    nD, n_i = D // tc, T // tr

    def body(uc, up_, un, bc, bp, bn, cc, cp, cn, yc, yp, yn, w, b, o_ref, wg_ref):
        i = pl.program_id(1)
        u = _ext(uc, up_, un, i, n_i)
        gc = _ext(cc, cp, cn, i, n_i)
        cu = gc * u
        g = _ext(yc, yp, yn, i, n_i) * _ext(bc, bp, bn, i, n_i)
        dcu = _mid(_conv_t(g, w), tr)
        o_ref[0] = (dcu * _mid(gc, tr)).astype(o_ref.dtype)
        o_ref[1] = (yc[...].astype(F32) * _mid(_conv(cu, w, b), tr)).astype(o_ref.dtype)
        o_ref[2] = (dcu * _mid(u, tr)).astype(o_ref.dtype)
        _conv_wgrad(wg_ref, g, cu, tr, i == 0)

    hs = lambda o: _halo_specs(tr, tc, T, lambda j, i: j + o)
    return pl.pallas_call(
        body, grid=(nD, n_i),
        in_specs=hs(0) + hs(nD) + hs(2 * nD) + hs(0)
        + [pl.BlockSpec((3, tc), lambda j, i: (0, j)), pl.BlockSpec((1, tc), lambda j, i: (0, j))],
        out_specs=[pl.BlockSpec((3, tr, tc), lambda j, i: (0, i, j)), pl.BlockSpec((8, tc), lambda j, i: (0, j))],
        out_shape=[_sds((3, T, D), BF16), _sds((8, D), F32)],
        name=name, compiler_params=_cp("parallel", "arbitrary"),
    )(z, z, z, z, z, z, z, z, z, dy, dy, dy, cw, cb)


def _slopes(n_heads):
    return jnp.asarray(2.0 ** (-ALIBI_MAX * np.arange(1, n_heads + 1) / n_heads), dtype=F32)


CHAINS = 32


def _nq(L, d):
    return max(1, min(CHAINS if d == 1 else CHAINS // 2, L // LANES // 2))


def _srows(ref, r, start, n, d):
    if d == 1:
        return ref[start:start + n, :]
    return ref[pl.ds(start * d + r, n, stride=d), :]


def _win(p_ref, c_ref, n_ref, r, b, nq):
    lo, hi, top = b * LANES - BAND, b * LANES + LANES + BAND, nq * LANES
    parts = [p_ref[r]] if lo < 0 else []
    parts.append(c_ref[r, max(lo, 0):min(hi, top), :])
    if hi > top:
        parts.append(n_ref[r])
    return parts[0] if len(parts) == 1 else jnp.concatenate(parts, axis=0)


def _nat_win(p_ref, c_ref, n_ref, r, b, nq, d):
    lo, hi, top = b * LANES - BAND, b * LANES + LANES + BAND, nq * LANES
    parts = [_srows(p_ref, r, 0, BAND, d)] if lo < 0 else []
    parts.append(_srows(c_ref, r, max(lo, 0), min(hi, top) - max(lo, 0), d))
    if hi > top:
        parts.append(_srows(n_ref, r, 0, BAND, d))
    return parts[0] if len(parts) == 1 else jnp.concatenate(parts, axis=0)


def _biases(slope, d, n, n_steps, nq, q_rows, k_rows, q0, k0):
    qi = lax.broadcasted_iota(jnp.int32, (q_rows, k_rows), 0) + q0
    kj = lax.broadcasted_iota(jnp.int32, (q_rows, k_rows), 1) + k0
    dist = jnp.abs(kj - qi)
    base = jnp.where(dist <= BAND, -slope * (dist * d).astype(F32), NEG_INF)
    out = []
    for b in range(nq):
        t = base
        if b == 0:
            t = jnp.where((n == 0) & ((kj < 0) | (qi < 0)), NEG_INF, t)
        if b == nq - 1:
            t = jnp.where((n == n_steps - 1) & ((kj >= LANES) | (qi >= LANES)), NEG_INF, t)
        out.append(t)
    return out


def _win_specs(d, H, col, nq, L):
    return [pl.BlockSpec((d, BAND, LANES), lambda h, n: (0, jnp.maximum(2 * nq * n - 1, 0), col * H + h)),
            pl.BlockSpec((d, nq * LANES, LANES), lambda h, n: (0, n, col * H + h)),
            pl.BlockSpec((d, BAND, LANES), lambda h, n: (0, jnp.minimum(2 * nq * (n + 1), L // BAND - 1), col * H + h))]


def _nat_specs(d, nq, L):
    return [pl.BlockSpec((BAND * d, LANES), lambda h, n: (jnp.maximum(2 * nq * n - 1, 0), h)),
            pl.BlockSpec((nq * LANES * d, LANES), lambda h, n: (n, h)),
            pl.BlockSpec((BAND * d, LANES), lambda h, n: (jnp.minimum(2 * nq * (n + 1), L // BAND - 1), h))]


def _over_residues(d, nq, per_r):
    if d == 1:
        per_r(0, 0)
    else:
        lax.fori_loop(0, d, per_r, 0, unroll=min(d, max(1, CHAINS // nq)))


def _attn_fwd(name, qkv, d, H):
    T = qkv.shape[0]
    D = H * HEAD_DIM
    L = T // d
    nq = _nq(L, d)
    n_steps = L // (nq * LANES)
    scale = HEAD_DIM ** -0.5
    q3 = qkv.reshape(d, L, 3 * D)

    def body(s_ref, q_ref, kp, kc, kn, vp, vc, vn, o_ref, l_ref):
        h, n = pl.program_id(0), pl.program_id(1)
        bias = _biases(s_ref[h], d, n, n_steps, nq, LANES, 2 * LANES, 0, -BAND)

        def per_r(r, carry):
            for b in range(nq):
                k, v = _win(kp, kc, kn, r, b, nq), _win(vp, vc, vn, r, b, nq)
                s = lax.dot_general(q_ref[r, b * LANES:(b + 1) * LANES, :], k, NT, preferred_element_type=F32) * scale + bias[b]
                m = jnp.max(s, axis=1, keepdims=True)
                p = jnp.exp(s - m)
                den = jnp.sum(p, axis=1, keepdims=True)
                o = lax.dot_general(p.astype(BF16), v, NN, preferred_element_type=F32) / den
                lse = jnp.broadcast_to(m + jnp.log(den), (LANES, LANES))
                if d == 1:
                    o_ref[b * LANES:(b + 1) * LANES, :] = o
                    l_ref[b * LANES:(b + 1) * LANES, :] = lse
                else:
                    o_ref[pl.ds(b * LANES * d + r, LANES, stride=d), :] = o
                    l_ref[pl.ds(b * LANES * d + r, LANES, stride=d), :] = lse
            return carry

        _over_residues(d, nq, per_r)

    out = pl.BlockSpec((nq * LANES * d, LANES), lambda h, n: (n, h))
    return pl.pallas_call(
        body, grid=(H, n_steps),
        in_specs=[pl.BlockSpec(memory_space=pltpu.SMEM), pl.BlockSpec((d, nq * LANES, LANES), lambda h, n: (0, n, h))]
        + _win_specs(d, H, 1, nq, L) + _win_specs(d, H, 2, nq, L),
        out_specs=[out, out], out_shape=[_sds((T, D), F32), _sds((T, D), F32)],
        name=name, compiler_params=_cp("parallel", "parallel"))(_slopes(H), q3, q3, q3, q3, q3, q3, q3)


def _attn_combine(name, outs, lses):
    T, D = outs[0].shape
    tr, tc = _tile(T, 512, 16), _tile(D, 512, LANES)

    def body(o0, o1, o2, l0, l1, l2, ob_ref, l_ref):
        a0, a1, a2 = l0[...], l1[...], l2[...]
        m = jnp.maximum(jnp.maximum(a0, a1), a2)
        e0, e1, e2 = jnp.exp(a0 - m), jnp.exp(a1 - m), jnp.exp(a2 - m)
        z = e0 + e1 + e2
        ob_ref[...] = ((e0 * o0[...] + e1 * o1[...] + e2 * o2[...]) / z).astype(BF16)
        l_ref[...] = m + jnp.log(z)

    blk = pl.BlockSpec((tr, tc), lambda i, j: (i, j))
    return pl.pallas_call(
        body, grid=(T // tr, D // tc), in_specs=[blk] * 6, out_specs=[blk] * 2,
        out_shape=[_sds((T, D), BF16), _sds((T, D), F32)],
        name=name, compiler_params=_cp("parallel", "parallel"))(*outs, *lses)


def _attn_stats(name, do, o, lse):
    T, D = do.shape
    tr, tc = _tile(T, 512, 16), _tile(D, 1024, LANES)

    def body(a, b, l, o_ref):
        lane = lax.broadcasted_iota(jnp.int32, (tr, LANES), 1)
        for k in range(tc // LANES):
            head = slice(k * LANES, (k + 1) * LANES)
            delta = jnp.sum(a[:, head] * b[:, head].astype(F32), axis=1, keepdims=True)
            o_ref[:, head] = jnp.where(lane < BAND, l[:, head], jnp.broadcast_to(delta, (tr, LANES)))

    blk = pl.BlockSpec((tr, tc), lambda i, j: (i, j))
    return pl.pallas_call(body, grid=(T // tr, D // tc), in_specs=[blk, blk, blk], out_specs=blk,
                          out_shape=_sds((T, D), F32), name=name, compiler_params=_cp("parallel", "parallel"))(do, o, lse)


def _attn_bwd(name, qkv, do, stats, d, H):
    T = qkv.shape[0]
    D = H * HEAD_DIM
    L = T // d
    nq = _nq(L, d)
    n_steps = L // (nq * LANES)
    scale = HEAD_DIM ** -0.5
    q3 = qkv.reshape(d, L, 3 * D)
    mid = slice(BAND, BAND + LANES)

    def body(s_ref, qp, qc, qn, kp, kc, kn, vp, vc, vn, gp, gc, gn, tp, tc_, tn_, o_ref):
        h, n = pl.program_id(0), pl.program_id(1)
        bias_q = _biases(s_ref[h], d, n, n_steps, nq, LANES, 2 * LANES, 0, -BAND)
        bias_k = _biases(s_ref[h], d, n, n_steps, nq, 2 * LANES, LANES, -BAND, 0)

        def per_r(r, carry):
            for b in range(nq):
                rows = slice(b * LANES, (b + 1) * LANES)
                q_w, k_w, v_w = _win(qp, qc, qn, r, b, nq), _win(kp, kc, kn, r, b, nq), _win(vp, vc, vn, r, b, nq)
                g_w = _nat_win(gp, gc, gn, r, b, nq, d)
                t_w = _nat_win(tp, tc_, tn_, r, b, nq, d)
                g_b = g_w.astype(BF16)
                q_c, k_c, v_c, g_c, t_c = q_w[mid], k_w[mid], v_w[mid], g_b[mid], t_w[mid]
                s = lax.dot_general(q_c, k_w, NT, preferred_element_type=F32) * scale + bias_q[b]
                p = jnp.exp(s - t_c[:, 0:1])
                dp = lax.dot_general(g_c, v_w, NT, preferred_element_type=F32)
                ds = p * (dp - t_c[:, BAND:BAND + 1])
                o_ref[0, r, rows, :] = (lax.dot_general(ds.astype(BF16), k_w, NN, preferred_element_type=F32) * scale).astype(BF16)
                s2 = lax.dot_general(q_w, k_c, NT, preferred_element_type=F32) * scale + bias_k[b]
                p2 = jnp.exp(s2 - t_w[:, 0:1])
                o_ref[2, r, rows, :] = lax.dot_general(p2.astype(BF16), g_b, TN, preferred_element_type=F32).astype(BF16)
                dp2 = lax.dot_general(g_b, v_c, NT, preferred_element_type=F32)
                ds2 = p2 * (dp2 - t_w[:, BAND:BAND + 1])
                o_ref[1, r, rows, :] = (lax.dot_general(ds2.astype(BF16), q_w, TN, preferred_element_type=F32) * scale).astype(BF16)
            return carry

        _over_residues(d, nq, per_r)

    dqkv = pl.pallas_call(
        body, grid=(H, n_steps),
        in_specs=[pl.BlockSpec(memory_space=pltpu.SMEM)]
        + _win_specs(d, H, 0, nq, L) + _win_specs(d, H, 1, nq, L) + _win_specs(d, H, 2, nq, L)
        + _nat_specs(d, nq, L) + _nat_specs(d, nq, L),
        out_specs=pl.BlockSpec((3, d, nq * LANES, LANES), lambda h, n: (0, 0, n, h)),
        out_shape=_sds((3, d, L, D), BF16),
        name=name, compiler_params=_cp("parallel", "parallel"),
    )(_slopes(H), *([q3] * 9), *([do] * 3), *([stats] * 3))
    return dqkv.reshape(3, T, D)


def _to_group_order(a, d):
    if d == 1:
        return a
    T, C = a.shape
    return a.reshape(T // d, d, C).swapaxes(0, 1).reshape(T, C)


def _from_group_order(a, d):
    if d == 1:
        return a
    T, C = a.shape
    return a.reshape(d, T // d, C).swapaxes(0, 1).reshape(T, C)


def _fwd_bwd(x, tgt, S, ex):
    T, D = x.shape
    H = D // HEAD_DIM
    G3 = 3 * D

    def mm(fn, *args, rides=(), **kw):
        out, extra = fn(*args, comm=_join([getattr(ex, kind)(keys) for kind, keys in rides]), **kw)
        at = 0
        for kind, keys in rides:
            getattr(ex, kind + "_done")(keys, extra[at:at + len(keys)])
            at += len(keys)
        return out

    def ffn_fwd(l, xin, rides):
        hf = mm(_rmsnorm_fwd, f"ffn_norm{l}", xin, S["ffn_g"][l])
        up = mm(_mm_nn, f"ffn_up{l}", hf, ex.w(f"up{l}"), None, BF16, rides=rides[0])
        act = mm(_ffn_gate_fwd, f"ffn_gate{l}", up, ffn_cw[l], S["ffn_cb"][l][None], rides=rides[1])
        return hf, up, act, mm(_mm_nn, f"ffn_down{l}", act, ex.w(f"dn{l}"), xin, F32, tk_pref=2816, rides=rides[2])

    def ffn_bwd(l, xin, hf, up, act, dxo, dxo_b, rides):
        dact = mm(_mm_nt, f"ffn_down_dx{l}", dxo_b, ex.w(f"dn{l}"), BF16, tn_pref=1408, rides=rides[0])
        ex.grad(f"dn{l}", mm(_mm_tn, f"ffn_down_dw{l}", act, dxo_b, tkr_pref=1408))
        dup, cg = _ffn_gate_bwd(f"ffn_gate_bwd{l}", up, dact, ffn_cw[l], S["ffn_cb"][l][None])
        dhf = mm(_mm_nt, f"ffn_up_dx{l}", dup, ex.w(f"up{l}"), BF16, tk_pref=2816, rides=rides[1])
        ex.grad(f"up{l}", mm(_mm_tn, f"ffn_up_dw{l}", hf, dup, tn_pref=2816, tk_pref=1024, rides=rides[2]))
        dx, dx_b, dg = mm(_rmsnorm_bwd, f"ffn_norm_bwd{l}", xin, S["ffn_g"][l], [dhf], dxo)
        return dx, dx_b, dg, cg

    h0 = mm(_rmsnorm_fwd, "mix_norm0", x, S["mix_g"][0], rides=[("ag", ["in", "scw", "fcw"])])
    sc_cw = ex.w("scw")
    ffn_cw = ex.w("fcw").reshape(S["ffn_cb"].shape[0], 3, -1)
    z = mm(_mm_nn, "sc_in", h0, ex.w("in"), None, BF16, rides=[("ag", ["sco", "up0"])])
    y = _sc_gate_fwd("sc_gate", z, sc_cw, S["sc_cb"][None])
    x1 = mm(_mm_nn, "sc_out", y, ex.w("sco"), x, F32, tm_pref=512, tn_pref=2048, rides=[("ag", ["dn0"])])
    hf0, up0, act0, x2 = ffn_fwd(0, x1, [[("ag", ["qkv"])], [("ag", ["up1"])], [("ag", ["ao", "dn1"])]])
    h1 = mm(_rmsnorm_fwd, "mix_norm1", x2, S["mix_g"][1])
    hd, qkv, outs, lses = [], [], [], []
    for g, d in enumerate(DILATIONS):
        hd.append(_to_group_order(h1, d))
        qkv.append(mm(_mm_nn, f"attn_qkv{g}", hd[g], ex.w("qkv"), None, BF16, col0=g * G3, n_cols=G3))
        o_g, l_g = _attn_fwd(f"attn_fwd{g}", qkv[g], d, H)
        outs.append(o_g)
        lses.append(l_g)
    o_b, lse = _attn_combine("attn_combine", outs, lses)
    x3 = mm(_mm_nn, "attn_out", o_b, ex.w("ao"), x2, F32, tm_pref=512, tn_pref=2048)
    hf1, up1, act1, x4 = ffn_fwd(1, x3, [(), (), ()])
    loss, dx4, dx4_b, dg_fin = _final_loss("final_loss", x4, S["fin_g"], tgt)

    dx3, dx3_b, dg_f1, cg1 = ffn_bwd(1, x3, hf1, up1, act1, dx4, dx4_b,
                                     [(), [("pair", ["dn1"])], [("chip", ["dn1"])]])
    do = mm(_mm_nt, "attn_out_dx", dx3_b, ex.w("ao"), F32, rides=[("pair", ["up1"])])
    ex.grad("ao", mm(_mm_tn, "attn_out_dw", o_b, dx3_b))
    stats = _attn_stats("attn_stats", do, o_b, lse)
    dhs, dw_qkv = [], None
    qkv_rides = [[("chip", ["up1"]), ("pair", ["ao"])], [("chip", ["ao"])], ()]
    for g, d in enumerate(DILATIONS):
        dqkv = _attn_bwd(f"attn_bwd{g}", qkv[g], do, stats, d, H)
        dhs.append(_from_group_order(mm(_mm_nt, f"attn_qkv_dx{g}", dqkv, ex.w("qkv"), BF16, col0=g * G3,
                                        rides=qkv_rides[g]), d))
        dw_qkv = mm(_mm_tn, f"attn_qkv_dw{g}", hd[g], dqkv, n_total=len(DILATIONS) * G3, col0=g * G3, prev=dw_qkv)
    ex.grad("qkv", dw_qkv)
    dx2, dx2_b, dg_m1 = mm(_rmsnorm_bwd, "mix_norm_bwd1", x2, S["mix_g"][1], dhs, dx3)
    dx1, dx1_b, dg_f0, cg0 = ffn_bwd(0, x1, hf0, up0, act0, dx2, dx2_b,
                                     [[("pair", ["qkv"])], [("chip", ["qkv"]), ("pair", ["dn0"])], [("chip", ["dn0"])]])
    dy = mm(_mm_nt, "sc_out_dx", dx1_b, ex.w("sco"), BF16, rides=[("pair", ["up0"])])
    ex.grad("sco", mm(_mm_tn, "sc_out_dw", y, dx1_b))
    dz, cg_sc = _sc_gate_bwd("sc_gate_bwd", z, dy, sc_cw, S["sc_cb"][None])
    ex.grad("in", mm(_mm_tn, "sc_in_dw", h0, dz, rides=[("chip", ["up0"]), ("pair", ["sco"])]))
    dh0 = mm(_mm_nt, "sc_in_dx", dz, ex.w("in"), BF16, rides=[("chip", ["sco"]), ("pair", ["in"])])
    dx0, _, dg_m0 = mm(_rmsnorm_bwd, "mix_norm_bwd0", x, S["mix_g"][0], [dh0], dx1, rides=[("chip", ["in"])])

    dS = {"mix_g": jnp.stack([dg_m0, dg_m1]), "ffn_g": jnp.stack([dg_f0, dg_f1]), "fin_g": dg_fin,
          "sc_cw": cg_sc[0:3], "sc_cb": cg_sc[3], "ffn_cw": jnp.stack([cg0[0:3], cg1[0:3]]),
          "ffn_cb": jnp.stack([cg0[3], cg1[3]])}
    return loss, dx0, dS


def _place():
    x, y, c = lax.axis_index("x"), lax.axis_index("y"), lax.axis_index("c")
    return x, y, c, [(1 - x, y), (x, 1 - y), (1 - x, 1 - y)]


def _block(ref, s, shape, axis):
    R, C = shape
    if axis == 0:
        return ref.at[pl.ds(pl.multiple_of(s * R, HALO), R), :]
    return ref.at[:, pl.ds(pl.multiple_of(s * C, LANES), C)]


def _whole(shape, axis):
    return (shape[0] * N_DEV, shape[1]) if axis == 0 else (shape[0], shape[1] * N_DEV)


def _ag_stage(arrs, axes):
    n = len(arrs)

    def plan(ins, outs, sems):
        send_sems, recv_sems, local_sems = sems
        x, y, c, chips = _place()
        me, sibling = 4 * x + 2 * y + c, (x, y, 1 - c)

        def copy(a, k, blk, to, src=None):
            dst = _block(outs[a], blk, arrs[a].shape, axes[a])
            return pltpu.make_async_remote_copy(src_ref=dst if src is None else src, dst_ref=dst,
                                                send_sem=send_sems.at[a, k], recv_sem=recv_sems.at[a, k],
                                                device_id=to, device_id_type=MESH)

        mine = [pltpu.make_async_copy(ins[a], _block(outs[a], me, arrs[a].shape, axes[a]), local_sems.at[a])
                for a in range(n)]
        first = []
        for a in range(n):
            first.append(copy(a, 0, me, sibling, src=ins[a]))
            first += [copy(a, 1 + j, me, (*chip, c), src=ins[a]) for j, chip in enumerate(chips)]
        return x, y, c, chips, sibling, copy, mine, first

    def start(ins, outs, sems):
        *_, mine, first = plan(ins, outs, sems)
        for cp in mine + first:
            cp.start()

    def finish(ins, outs, sems):
        x, y, c, chips, sibling, copy, mine, first = plan(ins, outs, sems)
        passed = []
        for j, (px, py) in enumerate(chips):
            for a in range(n):
                blk = 4 * px + 2 * py + c
                copy(a, 1 + j, blk, sibling).wait_recv()
                passed.append(copy(a, 4 + j, blk, sibling))
                passed[-1].start()
        for a in range(n):
            copy(a, 0, 4 * x + 2 * y + 1 - c, sibling).wait_recv()
            for j, (px, py) in enumerate(chips):
                copy(a, 4 + j, 4 * px + 2 * py + 1 - c, sibling).wait_recv()
        for cp in first + passed:
            cp.wait_send()
        for cp in mine:
            cp.wait()

    return _Stage(list(arrs), [_sds(_whole(a.shape, ax), a.dtype) for a, ax in zip(arrs, axes)],
                  [pltpu.SemaphoreType.DMA((n, 7)), pltpu.SemaphoreType.DMA((n, 7)), pltpu.SemaphoreType.DMA((n,))],
                  start, finish)


def _pair_stage(dws, shapes, axes):
    n = len(dws)

    def copies(ins, outs, sems):
        send_sems, recv_sems = sems
        x, y, c, _ = _place()
        return [pltpu.make_async_remote_copy(src_ref=_block(ins[a], 2 * k + 1 - c, shapes[a], axes[a]),
                                             dst_ref=outs[a].at[k],
                                             send_sem=send_sems.at[a, k], recv_sem=recv_sems.at[a, k],
                                             device_id=(x, y, 1 - c), device_id_type=MESH)
                for a in range(n) for k in range(4)]

    def start(ins, outs, sems):
        for cp in copies(ins, outs, sems):
            cp.start()

    def finish(ins, outs, sems):
        for cp in copies(ins, outs, sems):
            cp.wait()

    return _Stage(list(dws), [_sds((4,) + tuple(s), a.dtype) for a, s in zip(dws, shapes)],
                  [pltpu.SemaphoreType.DMA((n, 4)), pltpu.SemaphoreType.DMA((n, 4))], start, finish)


def _rs_pair_add(name, dw, got, c_arr, axis):
    _, R, C = got.shape
    tr, tc = _tile(R, 512, 16), _tile(C, 1536, LANES)
    per = R // tr if axis == 0 else C // tc

    def body(c_ref, a_ref, b_ref, o_ref):
        o_ref[...] = (a_ref[...].astype(F32) + b_ref[...].astype(F32)).astype(o_ref.dtype)

    if axis == 0:
        mine = pl.BlockSpec((tr, tc), lambda k, i, j, c_ref: ((2 * k + c_ref[0]) * per + i, j))
    else:
        mine = pl.BlockSpec((tr, tc), lambda k, i, j, c_ref: (i, (2 * k + c_ref[0]) * per + j))
    return pl.pallas_call(
        body,
        grid_spec=pltpu.PrefetchScalarGridSpec(
            num_scalar_prefetch=1, grid=(4, R // tr, C // tc),
            in_specs=[mine, pl.BlockSpec((None, tr, tc), lambda k, i, j, c_ref: (k, i, j))],
            out_specs=pl.BlockSpec((None, tr, tc), lambda k, i, j, c_ref: (k, i, j))),
        out_shape=_sds((4, R, C), BF16), name=name,
        compiler_params=_cp("parallel", "parallel", "parallel"))(c_arr, dw, got)


def _chip_stage(parts):
    n = len(parts)

    def plan(ins, outs, sems):
        send_sems, recv_sems, local_sems = sems
        x, y, c, chips = _place()
        my_chip = 2 * x + y
        mine = [pltpu.make_async_copy(ins[a].at[my_chip], outs[a].at[my_chip], local_sems.at[a]) for a in range(n)]
        sends = [pltpu.make_async_remote_copy(src_ref=ins[a].at[2 * px + py], dst_ref=outs[a].at[my_chip],
                                              send_sem=send_sems.at[a, j], recv_sem=recv_sems.at[a, j],
                                              device_id=(px, py, c), device_id_type=MESH)
                 for a in range(n) for j, (px, py) in enumerate(chips)]
        arrivals = lambda: [pltpu.make_async_remote_copy(src_ref=ins[a].at[my_chip], dst_ref=outs[a].at[2 * px + py],
                                                         send_sem=send_sems.at[a, j], recv_sem=recv_sems.at[a, j],
                                                         device_id=(px, py, c), device_id_type=MESH)
                            for a in range(n) for j, (px, py) in enumerate(chips)]
        return mine, sends, arrivals

    def start(ins, outs, sems):
        mine, sends, _ = plan(ins, outs, sems)
        for cp in mine + sends:
            cp.start()

    def finish(ins, outs, sems):
        mine, sends, arrivals = plan(ins, outs, sems)
        for cp in arrivals():
            cp.wait_recv()
        for cp in sends:
            cp.wait_send()
        for cp in mine:
            cp.wait()

    return _Stage(list(parts), [_sds(a.shape, a.dtype) for a in parts],
                  [pltpu.SemaphoreType.DMA((n, 3)), pltpu.SemaphoreType.DMA((n, 3)), pltpu.SemaphoreType.DMA((n,))],
                  start, finish)


class _Exchange:
    ROW_SHARDED = ("sco", "ao", "dn0", "dn1")

    def __init__(self, shards, c_arr):
        self.sh, self.c_arr = shards, c_arr
        self.W, self.dw, self.parts, self.sums = {}, {}, {}, {}

    def axis(self, key):
        return 0 if key in self.ROW_SHARDED else 1

    def w(self, key):
        return self.W[key]

    def ag(self, keys):
        return _ag_stage([self.sh[k] for k in keys], [self.axis(k) for k in keys])

    def ag_done(self, keys, outs):
        self.W.update(zip(keys, outs))

    def grad(self, key, dw):
        self.dw[key] = dw

    def pair(self, keys):
        return _pair_stage([self.dw[k] for k in keys], [self.sh[k].shape for k in keys], [self.axis(k) for k in keys])

    def pair_done(self, keys, outs):
        for k, got in zip(keys, outs):
            self.parts[k] = _rs_pair_add(f"rs_add_{k}", self.dw[k], got, self.c_arr, self.axis(k))

    def chip(self, keys):
        return _chip_stage([self.parts[k] for k in keys])

    def chip_done(self, keys, outs):
        self.sums.update(zip(keys, outs))


def _sum_slots(name, a):
    _, rows, _ = a.shape

    def body(a_ref, o_ref):
        s = a_ref[0]
        for k in range(1, N_DEV):
            s = s + a_ref[k]
        o_ref[...] = s

    return pl.pallas_call(body, out_shape=_sds((rows, LANES), F32), name=name)(a)


def _cast_bf16(name, w3, l):
    _, R, C = w3.shape
    tr, tc = _tile(R, 512, 16), _tile(C, 1536, LANES)

    def body(w_ref, o_ref):
        o_ref[...] = w_ref[...].astype(BF16)

    return pl.pallas_call(
        body, grid=(R // tr, C // tc), in_specs=[pl.BlockSpec((None, tr, tc), lambda i, j: (l, i, j))],
        out_specs=pl.BlockSpec((tr, tc), lambda i, j: (i, j)), out_shape=_sds((R, C), BF16),
        name=name, compiler_params=_cp("parallel", "parallel"))(w3)


def _adamw(name, g_slots, w3, m3, v3, l, prev):
    n_slots, R, C = g_slots.shape
    tr, tc = _tile(R, 256, 8), _tile(C, 1536, LANES)
    c1, c2 = 1.0 - ADAM_B1 ** ADAM_STEP, 1.0 - ADAM_B2 ** ADAM_STEP

    def body(g_ref, w_ref, m_ref, v_ref, *rest):
        og, od, om, ov = rest[-4:]
        g = g_ref[0].astype(F32)
        for k in range(1, n_slots):
            g = g + g_ref[k].astype(F32)
        m = ADAM_B1 * m_ref[...] + (1.0 - ADAM_B1) * g
        v = ADAM_B2 * v_ref[...] + (1.0 - ADAM_B2) * (g * g)
        og[...] = g
        om[...] = m
        ov[...] = v
        od[...] = -ADAM_LR * ((m / c1) / (jnp.sqrt(v / c2) + ADAM_EPS) + ADAM_WD * w_ref[...])

    lay = pl.BlockSpec((None, tr, tc), lambda i, j: (l, i, j))
    ops = [g_slots, w3, m3, v3]
    specs = [pl.BlockSpec((n_slots, tr, tc), lambda i, j: (0, i, j)), lay, lay, lay]
    aliases = {}
    if prev is not None:
        ops += list(prev)
        specs += _any_specs(4)
        aliases = {4 + k: k for k in range(4)}
    return pl.pallas_call(
        body, grid=(R // tr, C // tc), in_specs=specs, out_specs=[lay] * 4,
        out_shape=[_sds(w3.shape, F32)] * 4, input_output_aliases=aliases,
        name=name, compiler_params=_cp("parallel", "parallel"))(*ops)


def _pack(parts):
    flat = jnp.concatenate([p.reshape(-1) for p in parts])
    pad = (-flat.shape[0]) % (HALO * LANES)
    return jnp.pad(flat, (0, pad)).reshape(-1, LANES)


def _unpack(packed, shapes):
    flat = packed.reshape(-1)
    out, at = [], 0
    for s in shapes:
        n = int(np.prod(s))
        out.append(flat[at:at + n].reshape(s))
        at += n
    return out


def kernel(x, mix_norm_g, ffn_norm_g, final_norm_g, sc_w_in, sc_conv_w, sc_conv_b, sc_w_out, attn_w_qkv, attn_w_out, ffn_w_up, ffn_conv_w, ffn_conv_b, ffn_w_down, loss_target, m_mix_norm_g, m_ffn_norm_g, m_final_norm_g, m_sc_w_in, m_sc_conv_w, m_sc_conv_b, m_sc_w_out, m_attn_w_qkv, m_attn_w_out, m_ffn_w_up, m_ffn_conv_w, m_ffn_conv_b, m_ffn_w_down, v_mix_norm_g, v_ffn_norm_g, v_final_norm_g, v_sc_w_in, v_sc_conv_w, v_sc_conv_b, v_sc_w_out, v_attn_w_qkv, v_attn_w_out, v_ffn_w_up, v_ffn_conv_w, v_ffn_conv_b, v_ffn_w_down):
    n_layers = ffn_w_up.shape[0]
    me = 4 * lax.axis_index("x") + 2 * lax.axis_index("y") + lax.axis_index("c")
    c_arr = lax.axis_index("c").astype(jnp.int32).reshape(1)

    big = [("in", sc_w_in, 0), ("sco", sc_w_out, 0), ("qkv", attn_w_qkv, 0), ("ao", attn_w_out, 0)]
    big += [(f"up{l}", ffn_w_up, l) for l in range(n_layers)] + [(f"dn{l}", ffn_w_down, l) for l in range(n_layers)]
    shards = {nm: _cast_bf16(f"cast_{nm}", w, l) for nm, w, l in big}
    shards["scw"] = sc_conv_w.reshape(-1, sc_conv_w.shape[-1])
    shards["fcw"] = ffn_conv_w.reshape(-1, ffn_conv_w.shape[-1])
    ex = _Exchange(shards, c_arr)
    S = {"mix_g": mix_norm_g, "ffn_g": ffn_norm_g, "fin_g": final_norm_g, "sc_cb": sc_conv_b[0], "ffn_cb": ffn_conv_b}

    loss_part, grad_x, dS = _fwd_bwd(x[0], loss_target[0], S, ex)

    small_names = ["mix_g", "ffn_g", "fin_g", "sc_cb", "ffn_cb", "sc_cw", "ffn_cw"]
    small_parts = [dS[k] for k in small_names] + [loss_part.reshape(1)]
    small_mine = _pack(small_parts)
    small_all, = _run_stage("gather_small", _ag_stage([small_mine], [0]))
    small_sum = _sum_slots("sum_small", small_all.reshape((N_DEV,) + small_mine.shape))
    g_mix, g_ffn, g_fin, g_scb, g_fcb, g_scw, g_fcw, loss = _unpack(small_sum, [p.shape for p in small_parts])
    g_scw = lax.dynamic_slice_in_dim(g_scw, me * sc_conv_w.shape[-1], sc_conv_w.shape[-1], axis=-1)[None]
    g_fcw = lax.dynamic_slice_in_dim(g_fcw, me * ffn_conv_w.shape[-1], ffn_conv_w.shape[-1], axis=-1)
    g_scb = g_scb[None]
    small_g = [g_mix, g_ffn, g_fin, g_scw, g_scb, g_fcw, g_fcb]
    small_w = [mix_norm_g, ffn_norm_g, final_norm_g, sc_conv_w, sc_conv_b, ffn_conv_w, ffn_conv_b]
    small_m = [m_mix_norm_g, m_ffn_norm_g, m_final_norm_g, m_sc_conv_w, m_sc_conv_b, m_ffn_conv_w, m_ffn_conv_b]
    small_v = [v_mix_norm_g, v_ffn_norm_g, v_final_norm_g, v_sc_conv_w, v_sc_conv_b, v_ffn_conv_w, v_ffn_conv_b]
    small_out = _adamw("adamw_small", _pack(small_g)[None], _pack(small_w)[None], _pack(small_m)[None],
                       _pack(small_v)[None], 0, None)
    small_shapes = [w.shape for w in small_w]
    sg, sd, sm, sv = [_unpack(o[0], small_shapes) for o in small_out]

    moments = {"in": (m_sc_w_in, v_sc_w_in), "sco": (m_sc_w_out, v_sc_w_out), "qkv": (m_attn_w_qkv, v_attn_w_qkv),
               "ao": (m_attn_w_out, v_attn_w_out), "up": (m_ffn_w_up, v_ffn_w_up), "dn": (m_ffn_w_down, v_ffn_w_down)}
    upd = {}
    for nm, w, l in big:
        key = nm.rstrip("0123456789")
        upd[key] = _adamw(f"adamw_{nm}", ex.sums[nm], w, moments[key][0], moments[key][1], l, upd.get(key))

    def leaves(k):
        return [sg, sd, sm, sv][k][0:3] + [upd["in"][k], [sg, sd, sm, sv][k][3], [sg, sd, sm, sv][k][4], upd["sco"][k],
                                          upd["qkv"][k], upd["ao"][k], upd["up"][k], [sg, sd, sm, sv][k][5],
                                          [sg, sd, sm, sv][k][6], upd["dn"][k]]

    return (loss.reshape(()), grad_x[None], *leaves(0), *leaves(1), *leaves(2), *leaves(3))
```

```python
import math

import numpy as np
import jax
import jax.numpy as jnp
from jax import lax
from jax.experimental import pallas as pl
from jax.experimental.pallas import tpu as pltpu

F32 = jnp.float32
BF16 = jnp.bfloat16
MESH = pl.DeviceIdType.MESH

HEAD_DIM = 128
DILATED_GROUPS = ((128, 1), (512, 4), (2048, 16))
DILATIONS = tuple(d for _, d in DILATED_GROUPS)
BAND = (DILATED_GROUPS[0][0] // 2) // DILATED_GROUPS[0][1]
assert all((w // 2) // d == BAND for w, d in DILATED_GROUPS)
NORM_EPS = 1e-5
ALIBI_MAX = 8.0
NEG_INF = -1e30
ADAM_LR, ADAM_B1, ADAM_B2, ADAM_EPS, ADAM_WD, ADAM_STEP = 0.001, 0.9, 0.999, 1e-08, 0.01, 10

N_DEV = 8
LANES = 128
HALO = 16
VMEM_LIMIT = 56 * 1024 * 1024


def _cp(*sem):
    return pltpu.CompilerParams(dimension_semantics=sem, vmem_limit_bytes=VMEM_LIMIT)


def _tile(n, pref, mult):
    t = (min(n, pref) // mult) * mult
    while t >= mult:
        if n % t == 0:
            return t
        t -= mult
    return n


def _sds(shape, dtype):
    return jax.ShapeDtypeStruct(shape, dtype)


def _any_specs(n):
    return [pl.BlockSpec(memory_space=pl.ANY)] * n


class _Stage:
    def __init__(self, arrays, out_shapes, sems, start, finish):
        self.arrays, self.out_shapes, self.sems, self.start, self.finish = arrays, out_shapes, sems, start, finish


def _join(stages):
    stages = [s for s in stages if s is not None]
    if not stages:
        return None

    def split(refs, count):
        out, at = [], 0
        for s in stages:
            out.append(refs[at:at + count(s)])
            at += count(s)
        return out

    def each(which):
        def run(ins, outs, sems):
            parts = zip(split(ins, lambda s: len(s.arrays)), split(outs, lambda s: len(s.out_shapes)),
                        split(sems, lambda s: len(s.sems)))
            for s, (i, o, m) in zip(stages, parts):
                getattr(s, which)(i, o, m)
        return run

    return _Stage(sum([s.arrays for s in stages], []), sum([s.out_shapes for s in stages], []),
                  sum([s.sems for s in stages], []), each("start"), each("finish"))


def _run_stage(name, st):
    n, m = len(st.arrays), len(st.out_shapes)

    def body(*refs):
        ins, outs, sems = refs[:n], refs[n:n + m], refs[n + m:]
        st.start(ins, outs, sems)
        st.finish(ins, outs, sems)

    return pl.pallas_call(body, in_specs=_any_specs(n), out_specs=_any_specs(m), out_shape=st.out_shapes,
                          scratch_shapes=st.sems, name=name)(*st.arrays)


NN = (((1,), (0,)), ((), ()))
NT = (((1,), (1,)), ((), ()))
TN = (((0,), (0,)), ((), ()))


def _mm(name, operands, in_specs, out_sds, o_spec, grid, dims, acc_shape, has_res=False, aliases=None, comm=None):
    nk = grid[2]
    n_in = len(operands)
    n_ci, n_co = (len(comm.arrays), len(comm.out_shapes)) if comm else (0, 0)

    def body(*refs):
        a_ref, b_ref = refs[0], refs[1]
        r_ref = refs[2] if has_res else None
        o_ref = refs[n_in + n_ci]
        acc = refs[n_in + n_ci + 1 + n_co]
        c_refs = (refs[n_in:n_in + n_ci], refs[n_in + n_ci + 1:n_in + n_ci + 1 + n_co], refs[n_in + n_ci + 2 + n_co:])
        ids = [pl.program_id(q) for q in range(3)]
        if comm:
            @pl.when((ids[0] == 0) & (ids[1] == 0) & (ids[2] == 0))
            def _():
                comm.start(*c_refs)

        def finish(total):
            if has_res:
                total = total + r_ref[...]
            o_ref[...] = total.astype(o_ref.dtype)

        if nk == 1:
            finish(lax.dot_general(a_ref[...], b_ref[...], dims, preferred_element_type=F32))
        else:
            k = ids[2]

            @pl.when(k == 0)
            def _():
                acc[...] = jnp.zeros_like(acc)

            acc[...] += lax.dot_general(a_ref[...], b_ref[...], dims, preferred_element_type=F32)

            @pl.when(k == nk - 1)
            def _():
                finish(acc[...])

        if comm:
            @pl.when((ids[0] == grid[0] - 1) & (ids[1] == grid[1] - 1) & (ids[2] == nk - 1))
            def _():
                comm.finish(*c_refs)

    scratch = [pltpu.VMEM(acc_shape if nk > 1 else (8, LANES), F32)]
    if not comm:
        out = pl.pallas_call(
            body, grid=grid, in_specs=in_specs, out_specs=o_spec, out_shape=out_sds, scratch_shapes=scratch,
            input_output_aliases=aliases or {}, name=name,
            compiler_params=_cp("parallel", "parallel", "arbitrary"))(*operands)
        return out, []
    outs = pl.pallas_call(
        body, grid=grid, in_specs=list(in_specs) + _any_specs(n_ci), out_specs=[o_spec] + _any_specs(n_co),
        out_shape=[out_sds] + comm.out_shapes, scratch_shapes=scratch + comm.sems,
        input_output_aliases=aliases or {}, name=name,
        compiler_params=_cp("arbitrary", "arbitrary", "arbitrary"))(*operands, *comm.arrays)
    return outs[0], list(outs[1:])


def _stack(a):
    return a if a.ndim == 3 else a[None]


def _mm_nn(name, a, w, res, out_dtype, col0=0, n_cols=None, tm_pref=1024, tn_pref=1024, tk_pref=2048, comm=None):
    M, K = a.shape
    N = n_cols or w.shape[1]
    tm, tn, tk = _tile(M, tm_pref, 16), _tile(N, tn_pref, LANES), _tile(K, tk_pref, LANES)
    c0 = col0 // tn
    ops = [a, w]
    specs = [pl.BlockSpec((tm, tk), lambda i, j, k: (i, k)), pl.BlockSpec((tk, tn), lambda i, j, k: (k, c0 + j))]
    if res is not None:
        ops.append(res)
        specs.append(pl.BlockSpec((tm, tn), lambda i, j, k: (i, j)))
    return _mm(name, tuple(ops), specs, _sds((M, N), out_dtype), pl.BlockSpec((tm, tn), lambda i, j, k: (i, j)),
               (M // tm, N // tn, K // tk), NN, (tm, tn), has_res=res is not None, comm=comm)


def _mm_nt(name, dy, w, out_dtype, col0=0, tn_pref=1024, tk_pref=2048, comm=None):
    dy = _stack(dy)
    _, M, Np = dy.shape
    Kw = w.shape[0]
    tm, tn, tk = _tile(M, 1024, 16), _tile(Kw, tn_pref, LANES), _tile(Np, tk_pref, LANES)
    per, c0 = Np // tk, col0 // tk
    return _mm(name, (dy, w),
               [pl.BlockSpec((None, tm, tk), lambda i, j, k: (k // per, i, k % per)),
                pl.BlockSpec((tn, tk), lambda i, j, k: (j, c0 + k))],
               _sds((M, Kw), out_dtype), pl.BlockSpec((tm, tn), lambda i, j, k: (i, j)),
               (M // tm, Kw // tn, dy.shape[0] * per), NT, (tm, tn), comm=comm)


def _mm_tn(name, a, dy, n_total=None, col0=0, tkr_pref=1024, tn_pref=1024, tk_pref=2048, prev=None, comm=None):
    dy = _stack(dy)
    P, M, Np = dy.shape
    Kw = a.shape[1]
    tkr, tn, tk = _tile(Kw, tkr_pref, LANES), _tile(Np, tn_pref, LANES), _tile(M, tk_pref, 16)
    per, c0 = Np // tn, col0 // tn
    ops = [a, dy]
    specs = [pl.BlockSpec((tk, tkr), lambda i, j, k: (k, i)),
             pl.BlockSpec((None, tk, tn), lambda i, j, k: (j // per, k, j % per))]
    aliases = None
    if prev is not None:
        ops.append(prev)
        specs.append(pl.BlockSpec(memory_space=pl.ANY))
        aliases = {2: 0}
    return _mm(name, tuple(ops), specs, _sds((Kw, n_total or P * Np), BF16),
               pl.BlockSpec((tkr, tn), lambda i, j, k: (i, c0 + j)),
               (Kw // tkr, P * per, M // tk), TN, (tkr, tn), aliases=aliases, comm=comm)


def _rmsnorm_fwd(name, x, g, comm=None):
    T, D = x.shape
    tr = _tile(T, 512, 16)
    n_ci, n_co = (len(comm.arrays), len(comm.out_shapes)) if comm else (0, 0)

    def body(*refs):
        x_ref, g_ref, h_ref = refs[0], refs[1], refs[2 + n_ci]
        c_refs = (refs[2:2 + n_ci], refs[3 + n_ci:3 + n_ci + n_co], refs[3 + n_ci + n_co:])
        if comm:
            @pl.when(pl.program_id(0) == 0)
            def _():
                comm.start(*c_refs)

        xf = x_ref[...]
        r = lax.rsqrt(jnp.mean(xf * xf, axis=-1, keepdims=True) + NORM_EPS)
        h_ref[...] = (xf * r * g_ref[...]).astype(h_ref.dtype)

        if comm:
            @pl.when(pl.program_id(0) == T // tr - 1)
            def _():
                comm.finish(*c_refs)

    outs = pl.pallas_call(
        body, grid=(T // tr,),
        in_specs=[pl.BlockSpec((tr, D), lambda i: (i, 0)), pl.BlockSpec((1, D), lambda i: (0, 0))] + _any_specs(n_ci),
        out_specs=[pl.BlockSpec((tr, D), lambda i: (i, 0))] + _any_specs(n_co),
        out_shape=[_sds((T, D), BF16)] + (comm.out_shapes if comm else []),
        scratch_shapes=comm.sems if comm else [],
        name=name, compiler_params=_cp("arbitrary" if comm else "parallel"))(x, g.reshape(1, D), *(comm.arrays if comm else []))
    return outs[0], list(outs[1:])


def _rmsnorm_bwd(name, x, g, dhs, dres, comm=None):
    T, D = x.shape
    tr = _tile(T, 256, 16)
    n_dh = len(dhs)
    n_in = 3 + n_dh
    n_ci, n_co = (len(comm.arrays), len(comm.out_shapes)) if comm else (0, 0)

    def body(*refs):
        x_ref, g_ref = refs[0], refs[1]
        dh_refs = refs[2:2 + n_dh]
        dres_ref = refs[2 + n_dh]
        dx_ref, dxb_ref, dg_ref = refs[n_in + n_ci:n_in + n_ci + 3]
        c_refs = (refs[n_in:n_in + n_ci], refs[n_in + n_ci + 3:n_in + n_ci + 3 + n_co], refs[n_in + n_ci + 3 + n_co:])
        if comm:
            @pl.when(pl.program_id(0) == 0)
            def _():
                comm.start(*c_refs)

        xf = x_ref[...]
        r = lax.rsqrt(jnp.mean(xf * xf, axis=-1, keepdims=True) + NORM_EPS)
        xhat = xf * r
        dh = dh_refs[0][...].astype(F32)
        for q in dh_refs[1:]:
            dh = dh + q[...].astype(F32)
        dy = dh * g_ref[...]
        c = jnp.mean(dy * xhat, axis=-1, keepdims=True)
        dx = dres_ref[...] + r * (dy - xhat * c)
        dx_ref[...] = dx
        dxb_ref[...] = dx.astype(BF16)

        @pl.when(pl.program_id(0) == 0)
        def _():
            dg_ref[...] = jnp.zeros_like(dg_ref)

        dg_ref[...] += jnp.sum(dh * xhat, axis=0, keepdims=True)

        if comm:
            @pl.when(pl.program_id(0) == T // tr - 1)
            def _():
                comm.finish(*c_refs)

    row = pl.BlockSpec((tr, D), lambda i: (i, 0))
    vec = pl.BlockSpec((1, D), lambda i: (0, 0))
    outs = pl.pallas_call(
        body, grid=(T // tr,), in_specs=[row, vec] + [row] * n_dh + [row] + _any_specs(n_ci),
        out_specs=[row, row, vec] + _any_specs(n_co),
        out_shape=[_sds((T, D), F32), _sds((T, D), BF16), _sds((1, D), F32)] + (comm.out_shapes if comm else []),
        scratch_shapes=comm.sems if comm else [],
        name=name, compiler_params=_cp("arbitrary"))(x, g.reshape(1, D), *dhs, dres, *(comm.arrays if comm else []))
    return (outs[0], outs[1], outs[2][0]), list(outs[3:])


def _final_loss(name, x, g, tgt):
    T, D = x.shape
    tr = _tile(T, 256, 16)

    def body(x_ref, g_ref, t_ref, dx_ref, dxb_ref, dg_ref, loss_ref):
        xf = x_ref[...]
        r = lax.rsqrt(jnp.mean(xf * xf, axis=-1, keepdims=True) + NORM_EPS)
        xhat = xf * r
        err = xhat * g_ref[...] - t_ref[...]
        dy = err * (1.0 / D)
        dxh = dy * g_ref[...]
        c = jnp.mean(dxh * xhat, axis=-1, keepdims=True)
        dx = r * (dxh - xhat * c)
        dx_ref[...] = dx
        dxb_ref[...] = dx.astype(BF16)

        @pl.when(pl.program_id(0) == 0)
        def _():
            dg_ref[...] = jnp.zeros_like(dg_ref)
            loss_ref[...] = jnp.zeros_like(loss_ref)

        dg_ref[...] += jnp.sum(dy * xhat, axis=0, keepdims=True)
        loss_ref[...] += 0.5 * jnp.sum(jnp.mean(err * err, axis=-1, keepdims=True), axis=0, keepdims=True)

    row = pl.BlockSpec((tr, D), lambda i: (i, 0))
    vec = pl.BlockSpec((1, D), lambda i: (0, 0))
    dx, dx_b, dg, loss = pl.pallas_call(
        body, grid=(T // tr,), in_specs=[row, vec, row],
        out_specs=[row, row, vec, pl.BlockSpec((1, 1), lambda i: (0, 0))],
        out_shape=[_sds((T, D), F32), _sds((T, D), BF16), _sds((1, D), F32), _sds((1, 1), F32)],
        name=name, compiler_params=_cp("arbitrary"))(x, g.reshape(1, D), tgt)
    return loss[0, 0], dx, dx_b, dg[0]


def _halo_specs(tr, tc, n_rows, col):
    rb = tr // HALO
    last = n_rows // HALO - 1
    return [pl.BlockSpec((tr, tc), lambda *g: (g[-1], col(*g))),
            pl.BlockSpec((HALO, tc), lambda *g: (jnp.maximum(g[-1] * rb - 1, 0), col(*g))),
            pl.BlockSpec((HALO, tc), lambda *g: (jnp.minimum((g[-1] + 1) * rb, last), col(*g)))]


def _ext(cur_ref, prev_ref, next_ref, i, n_i):
    p = prev_ref[...].astype(F32) * (i > 0).astype(F32)
    n = next_ref[...].astype(F32) * (i < n_i - 1).astype(F32)
    return jnp.concatenate([p, cur_ref[...].astype(F32), n], axis=0)


def _shift_dn(x):
    return pltpu.roll(x, 1, axis=0)


def _shift_up(x):
    return pltpu.roll(x, x.shape[0] - 1, axis=0)


def _conv(x, w_ref, b_ref):
    return w_ref[0:1, :] * _shift_dn(x) + w_ref[1:2, :] * x + w_ref[2:3, :] * _shift_up(x) + b_ref[...]


def _mid(x, tr):
    return x[HALO:HALO + tr, :]


def _conv_t(g, w_ref):
    return w_ref[0:1, :] * _shift_up(g) + w_ref[1:2, :] * g + w_ref[2:3, :] * _shift_dn(g)


def _conv_wgrad(acc_ref, g, x, tr, first):
    gm = _mid(g, tr)

    @pl.when(first)
    def _():
        acc_ref[...] = jnp.zeros_like(acc_ref)

    acc_ref[0:1, :] += jnp.sum(gm * _mid(_shift_dn(x), tr), axis=0, keepdims=True)
    acc_ref[1:2, :] += jnp.sum(gm * _mid(x, tr), axis=0, keepdims=True)
    acc_ref[2:3, :] += jnp.sum(gm * _mid(_shift_up(x), tr), axis=0, keepdims=True)
    acc_ref[3:4, :] += jnp.sum(gm, axis=0, keepdims=True)


def _sigmoid(a):
    return 1.0 / (1.0 + jnp.exp(-a))


def _ffn_gate_fwd(name, up, cw, cb, comm=None):
    T, F2 = up.shape
    F = F2 // 2
    tc, tr = _tile(F, 128, LANES), _tile(T, 4096, HALO)
    nF, n_i = F // tc, T // tr

    n_ci, n_co = (len(comm.arrays), len(comm.out_shapes)) if comm else (0, 0)

    def body(*refs):
        ac, ap, an, bc, bp, bn, wa, wb, ba, bb = refs[:10]
        o_ref = refs[10 + n_ci]
        c_refs = (refs[10:10 + n_ci], refs[11 + n_ci:11 + n_ci + n_co], refs[11 + n_ci + n_co:])
        j, i = pl.program_id(0), pl.program_id(1)
        if comm:
            @pl.when((j == 0) & (i == 0))
            def _():
                comm.start(*c_refs)

        ua = _mid(_conv(_ext(ac, ap, an, i, n_i), wa, ba), tr)
        ub = _mid(_conv(_ext(bc, bp, bn, i, n_i), wb, bb), tr)
        o_ref[...] = (ua * _sigmoid(ua) * ub).astype(o_ref.dtype)

        if comm:
            @pl.when((j == nF - 1) & (i == n_i - 1))
            def _():
                comm.finish(*c_refs)

    wspec = lambda o: pl.BlockSpec((3, tc), lambda j, i: (0, j + o))
    bspec = lambda o: pl.BlockSpec((1, tc), lambda j, i: (0, j + o))
    sem = ("arbitrary", "arbitrary") if comm else ("parallel", "parallel")
    outs = pl.pallas_call(
        body, grid=(nF, n_i),
        in_specs=_halo_specs(tr, tc, T, lambda j, i: j) + _halo_specs(tr, tc, T, lambda j, i: j + nF)
        + [wspec(0), wspec(nF), bspec(0), bspec(nF)] + _any_specs(n_ci),
        out_specs=[pl.BlockSpec((tr, tc), lambda j, i: (i, j))] + _any_specs(n_co),
        out_shape=[_sds((T, F), BF16)] + (comm.out_shapes if comm else []),
        scratch_shapes=comm.sems if comm else [],
        name=name, compiler_params=_cp(*sem))(up, up, up, up, up, up, cw, cw, cb, cb, *(comm.arrays if comm else []))
    return outs[0], list(outs[1:])


def _ffn_gate_bwd(name, up, dact, cw, cb):
    T, F2 = up.shape
    F = F2 // 2
    tc, tr = _tile(F, 128, LANES), _tile(T, 4096, HALO)
    nF, n_i = F // tc, T // tr

    def body(ac, ap, an, bc, bp, bn, dc, dp, dn, wa, wb, ba, bb, o_ref, wga_ref, wgb_ref):
        i = pl.program_id(1)
        xa = _ext(ac, ap, an, i, n_i)
        xb = _ext(bc, bp, bn, i, n_i)
        da = _ext(dc, dp, dn, i, n_i)
        ua = _conv(xa, wa, ba)
        sig = _sigmoid(ua)
        ga = da * _conv(xb, wb, bb) * (sig * (1.0 + ua * (1.0 - sig)))
        o_ref[0] = _mid(_conv_t(ga, wa), tr).astype(o_ref.dtype)
        _conv_wgrad(wga_ref, ga, xa, tr, i == 0)
        gb = da * (ua * sig)
        o_ref[1] = _mid(_conv_t(gb, wb), tr).astype(o_ref.dtype)
        _conv_wgrad(wgb_ref, gb, xb, tr, i == 0)

    wspec = lambda o: pl.BlockSpec((3, tc), lambda j, i: (0, j + o))
    bspec = lambda o: pl.BlockSpec((1, tc), lambda j, i: (0, j + o))
    wg = pl.BlockSpec((8, tc), lambda j, i: (0, j))
    dup, wga, wgb = pl.pallas_call(
        body, grid=(nF, n_i),
        in_specs=_halo_specs(tr, tc, T, lambda j, i: j) + _halo_specs(tr, tc, T, lambda j, i: j + nF)
        + _halo_specs(tr, tc, T, lambda j, i: j) + [wspec(0), wspec(nF), bspec(0), bspec(nF)],
        out_specs=[pl.BlockSpec((2, tr, tc), lambda j, i: (0, i, j)), wg, wg],
        out_shape=[_sds((2, T, F), BF16), _sds((8, F), F32), _sds((8, F), F32)],
        name=name, compiler_params=_cp("parallel", "arbitrary"),
    )(up, up, up, up, up, up, dact, dact, dact, cw, cw, cb, cb)
    return dup, jnp.concatenate([wga, wgb], axis=1)


def _sc_gate_fwd(name, z, cw, cb):
    T, D3 = z.shape
    D = D3 // 3
    tc, tr = _tile(D, 128, LANES), _tile(T, 4096, HALO)
    nD, n_i = D // tc, T // tr

    def body(uc, up_, un, gb, cc, cp, cn, w, b, o_ref):
        i = pl.program_id(1)
        cu = _ext(cc, cp, cn, i, n_i) * _ext(uc, up_, un, i, n_i)
        o_ref[...] = (gb[...].astype(F32) * _mid(_conv(cu, w, b), tr)).astype(o_ref.dtype)

    return pl.pallas_call(
        body, grid=(nD, n_i),
        in_specs=_halo_specs(tr, tc, T, lambda j, i: j) + [pl.BlockSpec((tr, tc), lambda j, i: (i, j + nD))]
        + _halo_specs(tr, tc, T, lambda j, i: j + 2 * nD)
        + [pl.BlockSpec((3, tc), lambda j, i: (0, j)), pl.BlockSpec((1, tc), lambda j, i: (0, j))],
        out_specs=pl.BlockSpec((tr, tc), lambda j, i: (i, j)), out_shape=_sds((T, D), BF16),
        name=name, compiler_params=_cp("parallel", "parallel"))(z, z, z, z, z, z, z, cw, cb)


def _sc_gate_bwd(name, z, dy, cw, cb):
    T, D3 = z.shape
    D = D3 // 3
    tc, tr = _tile(D, 128, LANES), _tile(T, 4096, HALO)
    nD, n_i = D // tc, T // tr

    def body(uc, up_, un, bc, bp, bn, cc, cp, cn, yc, yp, yn, w, b, o_ref, wg_ref):
        i = pl.program_id(1)
        u = _ext(uc, up_, un, i, n_i)
        gc = _ext(cc, cp, cn, i, n_i)
        cu = gc * u
        g = _ext(yc, yp, yn, i, n_i) * _ext(bc, bp, bn, i, n_i)
        dcu = _mid(_conv_t(g, w), tr)
        o_ref[0] = (dcu * _mid(gc, tr)).astype(o_ref.dtype)
        o_ref[1] = (yc[...].astype(F32) * _mid(_conv(cu, w, b), tr)).astype(o_ref.dtype)
        o_ref[2] = (dcu * _mid(u, tr)).astype(o_ref.dtype)
        _conv_wgrad(wg_ref, g, cu, tr, i == 0)

    hs = lambda o: _halo_specs(tr, tc, T, lambda j, i: j + o)
    return pl.pallas_call(
        body, grid=(nD, n_i),
        in_specs=hs(0) + hs(nD) + hs(2 * nD) + hs(0)
        + [pl.BlockSpec((3, tc), lambda j, i: (0, j)), pl.BlockSpec((1, tc), lambda j, i: (0, j))],
        out_specs=[pl.BlockSpec((3, tr, tc), lambda j, i: (0, i, j)), pl.BlockSpec((8, tc), lambda j, i: (0, j))],
        out_shape=[_sds((3, T, D), BF16), _sds((8, D), F32)],
        name=name, compiler_params=_cp("parallel", "arbitrary"),
    )(z, z, z, z, z, z, z, z, z, dy, dy, dy, cw, cb)


def _slopes(n_heads):
    return jnp.asarray(2.0 ** (-ALIBI_MAX * np.arange(1, n_heads + 1) / n_heads), dtype=F32)


CHAINS = 32


def _nq(L, d):
    return max(1, min(CHAINS if d == 1 else CHAINS // 2, L // LANES // 2))


def _srows(ref, r, start, n, d):
    if d == 1:
        return ref[start:start + n, :]
    return ref[pl.ds(start * d + r, n, stride=d), :]


def _win(p_ref, c_ref, n_ref, r, b, nq):
    lo, hi, top = b * LANES - BAND, b * LANES + LANES + BAND, nq * LANES
    parts = [p_ref[r]] if lo < 0 else []
    parts.append(c_ref[r, max(lo, 0):min(hi, top), :])
    if hi > top:
        parts.append(n_ref[r])
    return parts[0] if len(parts) == 1 else jnp.concatenate(parts, axis=0)


def _nat_win(p_ref, c_ref, n_ref, r, b, nq, d):
    lo, hi, top = b * LANES - BAND, b * LANES + LANES + BAND, nq * LANES
    parts = [_srows(p_ref, r, 0, BAND, d)] if lo < 0 else []
    parts.append(_srows(c_ref, r, max(lo, 0), min(hi, top) - max(lo, 0), d))
    if hi > top:
        parts.append(_srows(n_ref, r, 0, BAND, d))
    return parts[0] if len(parts) == 1 else jnp.concatenate(parts, axis=0)


def _biases(slope, d, n, n_steps, nq, q_rows, k_rows, q0, k0):
    qi = lax.broadcasted_iota(jnp.int32, (q_rows, k_rows), 0) + q0
    kj = lax.broadcasted_iota(jnp.int32, (q_rows, k_rows), 1) + k0
    dist = jnp.abs(kj - qi)
    base = jnp.where(dist <= BAND, -slope * (dist * d).astype(F32), NEG_INF)
    out = []
    for b in range(nq):
        t = base
        if b == 0:
            t = jnp.where((n == 0) & ((kj < 0) | (qi < 0)), NEG_INF, t)
        if b == nq - 1:
            t = jnp.where((n == n_steps - 1) & ((kj >= LANES) | (qi >= LANES)), NEG_INF, t)
        out.append(t)
    return out


def _win_specs(d, H, col, nq, L):
    return [pl.BlockSpec((d, BAND, LANES), lambda h, n: (0, jnp.maximum(2 * nq * n - 1, 0), col * H + h)),
            pl.BlockSpec((d, nq * LANES, LANES), lambda h, n: (0, n, col * H + h)),
            pl.BlockSpec((d, BAND, LANES), lambda h, n: (0, jnp.minimum(2 * nq * (n + 1), L // BAND - 1), col * H + h))]


def _nat_specs(d, nq, L):
    return [pl.BlockSpec((BAND * d, LANES), lambda h, n: (jnp.maximum(2 * nq * n - 1, 0), h)),
            pl.BlockSpec((nq * LANES * d, LANES), lambda h, n: (n, h)),
            pl.BlockSpec((BAND * d, LANES), lambda h, n: (jnp.minimum(2 * nq * (n + 1), L // BAND - 1), h))]


def _over_residues(d, nq, per_r):
    if d == 1:
        per_r(0, 0)
    else:
        lax.fori_loop(0, d, per_r, 0, unroll=min(d, max(1, CHAINS // nq)))


def _attn_fwd(name, qkv, d, H):
    T = qkv.shape[0]
    D = H * HEAD_DIM
    L = T // d
    nq = _nq(L, d)
    n_steps = L // (nq * LANES)
    scale = HEAD_DIM ** -0.5
    q3 = qkv.reshape(d, L, 3 * D)

    def body(s_ref, q_ref, kp, kc, kn, vp, vc, vn, o_ref, l_ref):
        h, n = pl.program_id(0), pl.program_id(1)
        bias = _biases(s_ref[h], d, n, n_steps, nq, LANES, 2 * LANES, 0, -BAND)

        def per_r(r, carry):
            for b in range(nq):
                k, v = _win(kp, kc, kn, r, b, nq), _win(vp, vc, vn, r, b, nq)
                s = lax.dot_general(q_ref[r, b * LANES:(b + 1) * LANES, :], k, NT, preferred_element_type=F32) * scale + bias[b]
                m = jnp.max(s, axis=1, keepdims=True)
                p = jnp.exp(s - m)
                den = jnp.sum(p, axis=1, keepdims=True)
                o = lax.dot_general(p.astype(BF16), v, NN, preferred_element_type=F32) / den
                lse = jnp.broadcast_to(m + jnp.log(den), (LANES, LANES))
                if d == 1:
                    o_ref[b * LANES:(b + 1) * LANES, :] = o
                    l_ref[b * LANES:(b + 1) * LANES, :] = lse
                else:
                    o_ref[pl.ds(b * LANES * d + r, LANES, stride=d), :] = o
                    l_ref[pl.ds(b * LANES * d + r, LANES, stride=d), :] = lse
            return carry

        _over_residues(d, nq, per_r)

    out = pl.BlockSpec((nq * LANES * d, LANES), lambda h, n: (n, h))
    return pl.pallas_call(
        body, grid=(H, n_steps),
        in_specs=[pl.BlockSpec(memory_space=pltpu.SMEM), pl.BlockSpec((d, nq * LANES, LANES), lambda h, n: (0, n, h))]
        + _win_specs(d, H, 1, nq, L) + _win_specs(d, H, 2, nq, L),
        out_specs=[out, out], out_shape=[_sds((T, D), F32), _sds((T, D), F32)],
        name=name, compiler_params=_cp("parallel", "parallel"))(_slopes(H), q3, q3, q3, q3, q3, q3, q3)


def _attn_combine(name, outs, lses):
    T, D = outs[0].shape
    tr, tc = _tile(T, 512, 16), _tile(D, 512, LANES)

    def body(o0, o1, o2, l0, l1, l2, ob_ref, l_ref):
        a0, a1, a2 = l0[...], l1[...], l2[...]
        m = jnp.maximum(jnp.maximum(a0, a1), a2)
        e0, e1, e2 = jnp.exp(a0 - m), jnp.exp(a1 - m), jnp.exp(a2 - m)
        z = e0 + e1 + e2
        ob_ref[...] = ((e0 * o0[...] + e1 * o1[...] + e2 * o2[...]) / z).astype(BF16)
        l_ref[...] = m + jnp.log(z)

    blk = pl.BlockSpec((tr, tc), lambda i, j: (i, j))
    return pl.pallas_call(
        body, grid=(T // tr, D // tc), in_specs=[blk] * 6, out_specs=[blk] * 2,
        out_shape=[_sds((T, D), BF16), _sds((T, D), F32)],
        name=name, compiler_params=_cp("parallel", "parallel"))(*outs, *lses)


def _attn_stats(name, do, o, lse):
    T, D = do.shape
    tr, tc = _tile(T, 512, 16), _tile(D, 1024, LANES)

    def body(a, b, l, o_ref):
        lane = lax.broadcasted_iota(jnp.int32, (tr, LANES), 1)
        for k in range(tc // LANES):
            head = slice(k * LANES, (k + 1) * LANES)
            delta = jnp.sum(a[:, head] * b[:, head].astype(F32), axis=1, keepdims=True)
            o_ref[:, head] = jnp.where(lane < BAND, l[:, head], jnp.broadcast_to(delta, (tr, LANES)))

    blk = pl.BlockSpec((tr, tc), lambda i, j: (i, j))
    return pl.pallas_call(body, grid=(T // tr, D // tc), in_specs=[blk, blk, blk], out_specs=blk,
                          out_shape=_sds((T, D), F32), name=name, compiler_params=_cp("parallel", "parallel"))(do, o, lse)


def _attn_bwd(name, qkv, do, stats, d, H):
    T = qkv.shape[0]
    D = H * HEAD_DIM
    L = T // d
    nq = _nq(L, d)
    n_steps = L // (nq * LANES)
    scale = HEAD_DIM ** -0.5
    q3 = qkv.reshape(d, L, 3 * D)
    mid = slice(BAND, BAND + LANES)

    def body(s_ref, qp, qc, qn, kp, kc, kn, vp, vc, vn, gp, gc, gn, tp, tc_, tn_, o_ref):
        h, n = pl.program_id(0), pl.program_id(1)
        bias_q = _biases(s_ref[h], d, n, n_steps, nq, LANES, 2 * LANES, 0, -BAND)
        bias_k = _biases(s_ref[h], d, n, n_steps, nq, 2 * LANES, LANES, -BAND, 0)

        def per_r(r, carry):
            for b in range(nq):
                rows = slice(b * LANES, (b + 1) * LANES)
                q_w, k_w, v_w = _win(qp, qc, qn, r, b, nq), _win(kp, kc, kn, r, b, nq), _win(vp, vc, vn, r, b, nq)
                g_w = _nat_win(gp, gc, gn, r, b, nq, d)
                t_w = _nat_win(tp, tc_, tn_, r, b, nq, d)
                g_b = g_w.astype(BF16)
                q_c, k_c, v_c, g_c, t_c = q_w[mid], k_w[mid], v_w[mid], g_b[mid], t_w[mid]
                s = lax.dot_general(q_c, k_w, NT, preferred_element_type=F32) * scale + bias_q[b]
                p = jnp.exp(s - t_c[:, 0:1])
                dp = lax.dot_general(g_c, v_w, NT, preferred_element_type=F32)
                ds = p * (dp - t_c[:, BAND:BAND + 1])
                o_ref[0, r, rows, :] = (lax.dot_general(ds.astype(BF16), k_w, NN, preferred_element_type=F32) * scale).astype(BF16)
                s2 = lax.dot_general(q_w, k_c, NT, preferred_element_type=F32) * scale + bias_k[b]
                p2 = jnp.exp(s2 - t_w[:, 0:1])
                o_ref[2, r, rows, :] = lax.dot_general(p2.astype(BF16), g_b, TN, preferred_element_type=F32).astype(BF16)
                dp2 = lax.dot_general(g_b, v_c, NT, preferred_element_type=F32)
                ds2 = p2 * (dp2 - t_w[:, BAND:BAND + 1])
                o_ref[1, r, rows, :] = (lax.dot_general(ds2.astype(BF16), q_w, TN, preferred_element_type=F32) * scale).astype(BF16)
            return carry

        _over_residues(d, nq, per_r)

    dqkv = pl.pallas_call(
        body, grid=(H, n_steps),
        in_specs=[pl.BlockSpec(memory_space=pltpu.SMEM)]
        + _win_specs(d, H, 0, nq, L) + _win_specs(d, H, 1, nq, L) + _win_specs(d, H, 2, nq, L)
        + _nat_specs(d, nq, L) + _nat_specs(d, nq, L),
        out_specs=pl.BlockSpec((3, d, nq * LANES, LANES), lambda h, n: (0, 0, n, h)),
        out_shape=_sds((3, d, L, D), BF16),
        name=name, compiler_params=_cp("parallel", "parallel"),
    )(_slopes(H), *([q3] * 9), *([do] * 3), *([stats] * 3))
    return dqkv.reshape(3, T, D)


def _to_group_order(a, d):
    if d == 1:
        return a
    T, C = a.shape
    return a.reshape(T // d, d, C).swapaxes(0, 1).reshape(T, C)


def _from_group_order(a, d):
    if d == 1:
        return a
    T, C = a.shape
    return a.reshape(d, T // d, C).swapaxes(0, 1).reshape(T, C)


def _fwd_bwd(x, tgt, S, ex):
    T, D = x.shape
    H = D // HEAD_DIM
    G3 = 3 * D

    def mm(fn, *args, rides=(), **kw):
        out, extra = fn(*args, comm=_join([getattr(ex, kind)(keys) for kind, keys in rides]), **kw)
        at = 0
        for kind, keys in rides:
            getattr(ex, kind + "_done")(keys, extra[at:at + len(keys)])
            at += len(keys)
        return out

    def ffn_fwd(l, xin, rides):
        hf = mm(_rmsnorm_fwd, f"ffn_norm{l}", xin, S["ffn_g"][l])
        up = mm(_mm_nn, f"ffn_up{l}", hf, ex.w(f"up{l}"), None, BF16, rides=rides[0])
        act = mm(_ffn_gate_fwd, f"ffn_gate{l}", up, ffn_cw[l], S["ffn_cb"][l][None], rides=rides[1])
        return hf, up, act, mm(_mm_nn, f"ffn_down{l}", act, ex.w(f"dn{l}"), xin, F32, tk_pref=2816, rides=rides[2])

    def ffn_bwd(l, xin, hf, up, act, dxo, dxo_b, rides):
        dact = mm(_mm_nt, f"ffn_down_dx{l}", dxo_b, ex.w(f"dn{l}"), BF16, tn_pref=1408, rides=rides[0])
        ex.grad(f"dn{l}", mm(_mm_tn, f"ffn_down_dw{l}", act, dxo_b, tkr_pref=1408))
        dup, cg = _ffn_gate_bwd(f"ffn_gate_bwd{l}", up, dact, ffn_cw[l], S["ffn_cb"][l][None])
        dhf = mm(_mm_nt, f"ffn_up_dx{l}", dup, ex.w(f"up{l}"), BF16, tk_pref=2816, rides=rides[1])
        ex.grad(f"up{l}", mm(_mm_tn, f"ffn_up_dw{l}", hf, dup, tn_pref=2816, tk_pref=1024, rides=rides[2]))
        dx, dx_b, dg = mm(_rmsnorm_bwd, f"ffn_norm_bwd{l}", xin, S["ffn_g"][l], [dhf], dxo)
        return dx, dx_b, dg, cg

    h0 = mm(_rmsnorm_fwd, "mix_norm0", x, S["mix_g"][0], rides=[("ag", ["in", "scw", "fcw", "sco"])])
    sc_cw = ex.w("scw")
    ffn_cw = ex.w("fcw").reshape(S["ffn_cb"].shape[0], 3, -1)
    z = mm(_mm_nn, "sc_in", h0, ex.w("in"), None, BF16, rides=[("ag", ["up0"])])
    y = _sc_gate_fwd("sc_gate", z, sc_cw, S["sc_cb"][None])
    x1 = mm(_mm_nn, "sc_out", y, ex.w("sco"), x, F32, tm_pref=512, tn_pref=2048, rides=[("ag", ["dn0"])])
    hf0, up0, act0, x2 = ffn_fwd(0, x1, [[("ag", ["qkv"])], [("ag", ["up1"])], [("ag", ["ao", "dn1"])]])
    h1 = mm(_rmsnorm_fwd, "mix_norm1", x2, S["mix_g"][1])
    hd, qkv, outs, lses = [], [], [], []
    for g, d in enumerate(DILATIONS):
        hd.append(_to_group_order(h1, d))
        qkv.append(mm(_mm_nn, f"attn_qkv{g}", hd[g], ex.w("qkv"), None, BF16, col0=g * G3, n_cols=G3))
        o_g, l_g = _attn_fwd(f"attn_fwd{g}", qkv[g], d, H)
        outs.append(o_g)
        lses.append(l_g)
    o_b, lse = _attn_combine("attn_combine", outs, lses)
    x3 = mm(_mm_nn, "attn_out", o_b, ex.w("ao"), x2, F32, tm_pref=512, tn_pref=2048)
    hf1, up1, act1, x4 = ffn_fwd(1, x3, [(), (), ()])
    loss, dx4, dx4_b, dg_fin = _final_loss("final_loss", x4, S["fin_g"], tgt)

    dx3, dx3_b, dg_f1, cg1 = ffn_bwd(1, x3, hf1, up1, act1, dx4, dx4_b,
                                     [(), [("pair", ["dn1"])], [("chip", ["dn1"])]])
    do = mm(_mm_nt, "attn_out_dx", dx3_b, ex.w("ao"), F32, rides=[("pair", ["up1"])])
    ex.grad("ao", mm(_mm_tn, "attn_out_dw", o_b, dx3_b))
    stats = _attn_stats("attn_stats", do, o_b, lse)
    dhs, dw_qkv = [], None
    qkv_rides = [[("chip", ["up1"]), ("pair", ["ao"])], [("chip", ["ao"])], ()]
    for g, d in enumerate(DILATIONS):
        dqkv = _attn_bwd(f"attn_bwd{g}", qkv[g], do, stats, d, H)
        dhs.append(_from_group_order(mm(_mm_nt, f"attn_qkv_dx{g}", dqkv, ex.w("qkv"), BF16, col0=g * G3,
                                        rides=qkv_rides[g]), d))
        dw_qkv = mm(_mm_tn, f"attn_qkv_dw{g}", hd[g], dqkv, n_total=len(DILATIONS) * G3, col0=g * G3, prev=dw_qkv)
    ex.grad("qkv", dw_qkv)
    dx2, dx2_b, dg_m1 = mm(_rmsnorm_bwd, "mix_norm_bwd1", x2, S["mix_g"][1], dhs, dx3)
    dx1, dx1_b, dg_f0, cg0 = ffn_bwd(0, x1, hf0, up0, act0, dx2, dx2_b,
                                     [[("pair", ["qkv"])], [("chip", ["qkv"]), ("pair", ["dn0"])], [("chip", ["dn0"])]])
    dy = mm(_mm_nt, "sc_out_dx", dx1_b, ex.w("sco"), BF16, rides=[("pair", ["up0"])])
    ex.grad("sco", mm(_mm_tn, "sc_out_dw", y, dx1_b))
    dz, cg_sc = _sc_gate_bwd("sc_gate_bwd", z, dy, sc_cw, S["sc_cb"][None])
    ex.grad("in", mm(_mm_tn, "sc_in_dw", h0, dz, rides=[("chip", ["up0"]), ("pair", ["sco"])]))
    dh0 = mm(_mm_nt, "sc_in_dx", dz, ex.w("in"), BF16, rides=[("chip", ["sco"]), ("pair", ["in"])])
    dx0, _, dg_m0 = mm(_rmsnorm_bwd, "mix_norm_bwd0", x, S["mix_g"][0], [dh0], dx1, rides=[("chip", ["in"])])

    dS = {"mix_g": jnp.stack([dg_m0, dg_m1]), "ffn_g": jnp.stack([dg_f0, dg_f1]), "fin_g": dg_fin,
          "sc_cw": cg_sc[0:3], "sc_cb": cg_sc[3], "ffn_cw": jnp.stack([cg0[0:3], cg1[0:3]]),
          "ffn_cb": jnp.stack([cg0[3], cg1[3]])}
    return loss, dx0, dS


def _place():
    x, y, c = lax.axis_index("x"), lax.axis_index("y"), lax.axis_index("c")
    return x, y, c, [(1 - x, y), (x, 1 - y), (1 - x, 1 - y)]


def _block(ref, s, shape, axis):
    R, C = shape
    if axis == 0:
        return ref.at[pl.ds(pl.multiple_of(s * R, HALO), R), :]
    return ref.at[:, pl.ds(pl.multiple_of(s * C, LANES), C)]


def _whole(shape, axis):
    return (shape[0] * N_DEV, shape[1]) if axis == 0 else (shape[0], shape[1] * N_DEV)


def _ag_stage(arrs, axes):
    n = len(arrs)

    def plan(ins, outs, sems):
        send_sems, recv_sems, local_sems = sems
        x, y, c, chips = _place()
        me, sibling = 4 * x + 2 * y + c, (x, y, 1 - c)

        def copy(a, k, blk, to, src=None):
            dst = _block(outs[a], blk, arrs[a].shape, axes[a])
            return pltpu.make_async_remote_copy(src_ref=dst if src is None else src, dst_ref=dst,
                                                send_sem=send_sems.at[a, k], recv_sem=recv_sems.at[a, k],
                                                device_id=to, device_id_type=MESH)

        mine = [pltpu.make_async_copy(ins[a], _block(outs[a], me, arrs[a].shape, axes[a]), local_sems.at[a])
                for a in range(n)]
        first = []
        for a in range(n):
            first.append(copy(a, 0, me, sibling, src=ins[a]))
            first += [copy(a, 1 + j, me, (*chip, c), src=ins[a]) for j, chip in enumerate(chips)]
        return x, y, c, chips, sibling, copy, mine, first

    def start(ins, outs, sems):
        *_, mine, first = plan(ins, outs, sems)
        for cp in mine + first:
            cp.start()

    def finish(ins, outs, sems):
        x, y, c, chips, sibling, copy, mine, first = plan(ins, outs, sems)
        passed = []
        for j, (px, py) in enumerate(chips):
            for a in range(n):
                blk = 4 * px + 2 * py + c
                copy(a, 1 + j, blk, sibling).wait_recv()
                passed.append(copy(a, 4 + j, blk, sibling))
                passed[-1].start()
        for a in range(n):
            copy(a, 0, 4 * x + 2 * y + 1 - c, sibling).wait_recv()
            for j, (px, py) in enumerate(chips):
                copy(a, 4 + j, 4 * px + 2 * py + 1 - c, sibling).wait_recv()
        for cp in first + passed:
            cp.wait_send()
        for cp in mine:
            cp.wait()

    return _Stage(list(arrs), [_sds(_whole(a.shape, ax), a.dtype) for a, ax in zip(arrs, axes)],
                  [pltpu.SemaphoreType.DMA((n, 7)), pltpu.SemaphoreType.DMA((n, 7)), pltpu.SemaphoreType.DMA((n,))],
                  start, finish)


def _pair_stage(dws, shapes, axes):
    n = len(dws)

    def copies(ins, outs, sems):
        send_sems, recv_sems = sems
        x, y, c, _ = _place()
        return [pltpu.make_async_remote_copy(src_ref=_block(ins[a], 2 * k + 1 - c, shapes[a], axes[a]),
                                             dst_ref=outs[a].at[k],
                                             send_sem=send_sems.at[a, k], recv_sem=recv_sems.at[a, k],
                                             device_id=(x, y, 1 - c), device_id_type=MESH)
                for a in range(n) for k in range(4)]

    def start(ins, outs, sems):
        for cp in copies(ins, outs, sems):
            cp.start()

    def finish(ins, outs, sems):
        for cp in copies(ins, outs, sems):
            cp.wait()

    return _Stage(list(dws), [_sds((4,) + tuple(s), a.dtype) for a, s in zip(dws, shapes)],
                  [pltpu.SemaphoreType.DMA((n, 4)), pltpu.SemaphoreType.DMA((n, 4))], start, finish)


def _rs_pair_add(name, dw, got, c_arr, axis):
    _, R, C = got.shape
    tr, tc = _tile(R, 512, 16), _tile(C, 1536, LANES)
    per = R // tr if axis == 0 else C // tc

    def body(c_ref, a_ref, b_ref, o_ref):
        o_ref[...] = (a_ref[...].astype(F32) + b_ref[...].astype(F32)).astype(o_ref.dtype)

    if axis == 0:
        mine = pl.BlockSpec((tr, tc), lambda k, i, j, c_ref: ((2 * k + c_ref[0]) * per + i, j))
    else:
        mine = pl.BlockSpec((tr, tc), lambda k, i, j, c_ref: (i, (2 * k + c_ref[0]) * per + j))
    return pl.pallas_call(
        body,
        grid_spec=pltpu.PrefetchScalarGridSpec(
            num_scalar_prefetch=1, grid=(4, R // tr, C // tc),
            in_specs=[mine, pl.BlockSpec((None, tr, tc), lambda k, i, j, c_ref: (k, i, j))],
            out_specs=pl.BlockSpec((None, tr, tc), lambda k, i, j, c_ref: (k, i, j))),
        out_shape=_sds((4, R, C), BF16), name=name,
        compiler_params=_cp("parallel", "parallel", "parallel"))(c_arr, dw, got)


def _chip_stage(parts):
    n = len(parts)

    def plan(ins, outs, sems):
        send_sems, recv_sems, local_sems = sems
        x, y, c, chips = _place()
        my_chip = 2 * x + y
        mine = [pltpu.make_async_copy(ins[a].at[my_chip], outs[a].at[my_chip], local_sems.at[a]) for a in range(n)]
        sends = [pltpu.make_async_remote_copy(src_ref=ins[a].at[2 * px + py], dst_ref=outs[a].at[my_chip],
                                              send_sem=send_sems.at[a, j], recv_sem=recv_sems.at[a, j],
                                              device_id=(px, py, c), device_id_type=MESH)
                 for a in range(n) for j, (px, py) in enumerate(chips)]
        arrivals = lambda: [pltpu.make_async_remote_copy(src_ref=ins[a].at[my_chip], dst_ref=outs[a].at[2 * px + py],
                                                         send_sem=send_sems.at[a, j], recv_sem=recv_sems.at[a, j],
                                                         device_id=(px, py, c), device_id_type=MESH)
                            for a in range(n) for j, (px, py) in enumerate(chips)]
        return mine, sends, arrivals

    def start(ins, outs, sems):
        mine, sends, _ = plan(ins, outs, sems)
        for cp in mine + sends:
            cp.start()

    def finish(ins, outs, sems):
        mine, sends, arrivals = plan(ins, outs, sems)
        for cp in arrivals():
            cp.wait_recv()
        for cp in sends:
            cp.wait_send()
        for cp in mine:
            cp.wait()

    return _Stage(list(parts), [_sds(a.shape, a.dtype) for a in parts],
                  [pltpu.SemaphoreType.DMA((n, 3)), pltpu.SemaphoreType.DMA((n, 3)), pltpu.SemaphoreType.DMA((n,))],
                  start, finish)


class _Exchange:
    ROW_SHARDED = ("sco", "ao", "dn0", "dn1")

    def __init__(self, shards, c_arr):
        self.sh, self.c_arr = shards, c_arr
        self.W, self.dw, self.parts, self.sums = {}, {}, {}, {}

    def axis(self, key):
        return 0 if key in self.ROW_SHARDED else 1

    def w(self, key):
        return self.W[key]

    def ag(self, keys):
        return _ag_stage([self.sh[k] for k in keys], [self.axis(k) for k in keys])

    def ag_done(self, keys, outs):
        self.W.update(zip(keys, outs))

    def grad(self, key, dw):
        self.dw[key] = dw

    def pair(self, keys):
        return _pair_stage([self.dw[k] for k in keys], [self.sh[k].shape for k in keys], [self.axis(k) for k in keys])

    def pair_done(self, keys, outs):
        for k, got in zip(keys, outs):
            self.parts[k] = _rs_pair_add(f"rs_add_{k}", self.dw[k], got, self.c_arr, self.axis(k))

    def chip(self, keys):
        return _chip_stage([self.parts[k] for k in keys])

    def chip_done(self, keys, outs):
        self.sums.update(zip(keys, outs))


def _sum_slots(name, a):
    _, rows, _ = a.shape

    def body(a_ref, o_ref):
        s = a_ref[0]
        for k in range(1, N_DEV):
            s = s + a_ref[k]
        o_ref[...] = s

    return pl.pallas_call(body, out_shape=_sds((rows, LANES), F32), name=name)(a)


def _cast_bf16(name, w3, l):
    _, R, C = w3.shape
    tr, tc = _tile(R, 512, 16), _tile(C, 1536, LANES)

    def body(w_ref, o_ref):
        o_ref[...] = w_ref[...].astype(BF16)

    return pl.pallas_call(
        body, grid=(R // tr, C // tc), in_specs=[pl.BlockSpec((None, tr, tc), lambda i, j: (l, i, j))],
        out_specs=pl.BlockSpec((tr, tc), lambda i, j: (i, j)), out_shape=_sds((R, C), BF16),
        name=name, compiler_params=_cp("parallel", "parallel"))(w3)


def _adamw(name, g_slots, w3, m3, v3, l, prev):
    n_slots, R, C = g_slots.shape
    tr, tc = _tile(R, 256, 8), _tile(C, 1536, LANES)
    c1, c2 = 1.0 - ADAM_B1 ** ADAM_STEP, 1.0 - ADAM_B2 ** ADAM_STEP

    def body(g_ref, w_ref, m_ref, v_ref, *rest):
        og, od, om, ov = rest[-4:]
        g = g_ref[0].astype(F32)
        for k in range(1, n_slots):
            g = g + g_ref[k].astype(F32)
        m = ADAM_B1 * m_ref[...] + (1.0 - ADAM_B1) * g
        v = ADAM_B2 * v_ref[...] + (1.0 - ADAM_B2) * (g * g)
        og[...] = g
        om[...] = m
        ov[...] = v
        od[...] = -ADAM_LR * ((m / c1) / (jnp.sqrt(v / c2) + ADAM_EPS) + ADAM_WD * w_ref[...])

    lay = pl.BlockSpec((None, tr, tc), lambda i, j: (l, i, j))
    ops = [g_slots, w3, m3, v3]
    specs = [pl.BlockSpec((n_slots, tr, tc), lambda i, j: (0, i, j)), lay, lay, lay]
    aliases = {}
    if prev is not None:
        ops += list(prev)
        specs += _any_specs(4)
        aliases = {4 + k: k for k in range(4)}
    return pl.pallas_call(
        body, grid=(R // tr, C // tc), in_specs=specs, out_specs=[lay] * 4,
        out_shape=[_sds(w3.shape, F32)] * 4, input_output_aliases=aliases,
        name=name, compiler_params=_cp("parallel", "parallel"))(*ops)


def _pack(parts):
    flat = jnp.concatenate([p.reshape(-1) for p in parts])
    pad = (-flat.shape[0]) % (HALO * LANES)
    return jnp.pad(flat, (0, pad)).reshape(-1, LANES)


def _unpack(packed, shapes):
    flat = packed.reshape(-1)
    out, at = [], 0
    for s in shapes:
        n = int(np.prod(s))
        out.append(flat[at:at + n].reshape(s))
        at += n
    return out


def kernel(x, mix_norm_g, ffn_norm_g, final_norm_g, sc_w_in, sc_conv_w, sc_conv_b, sc_w_out, attn_w_qkv, attn_w_out, ffn_w_up, ffn_conv_w, ffn_conv_b, ffn_w_down, loss_target, m_mix_norm_g, m_ffn_norm_g, m_final_norm_g, m_sc_w_in, m_sc_conv_w, m_sc_conv_b, m_sc_w_out, m_attn_w_qkv, m_attn_w_out, m_ffn_w_up, m_ffn_conv_w, m_ffn_conv_b, m_ffn_w_down, v_mix_norm_g, v_ffn_norm_g, v_final_norm_g, v_sc_w_in, v_sc_conv_w, v_sc_conv_b, v_sc_w_out, v_attn_w_qkv, v_attn_w_out, v_ffn_w_up, v_ffn_conv_w, v_ffn_conv_b, v_ffn_w_down):
    n_layers = ffn_w_up.shape[0]
    me = 4 * lax.axis_index("x") + 2 * lax.axis_index("y") + lax.axis_index("c")
    c_arr = lax.axis_index("c").astype(jnp.int32).reshape(1)

    big = [("in", sc_w_in, 0), ("sco", sc_w_out, 0), ("qkv", attn_w_qkv, 0), ("ao", attn_w_out, 0)]
    big += [(f"up{l}", ffn_w_up, l) for l in range(n_layers)] + [(f"dn{l}", ffn_w_down, l) for l in range(n_layers)]
    shards = {nm: _cast_bf16(f"cast_{nm}", w, l) for nm, w, l in big}
    shards["scw"] = sc_conv_w.reshape(-1, sc_conv_w.shape[-1])
    shards["fcw"] = ffn_conv_w.reshape(-1, ffn_conv_w.shape[-1])
    ex = _Exchange(shards, c_arr)
    S = {"mix_g": mix_norm_g, "ffn_g": ffn_norm_g, "fin_g": final_norm_g, "sc_cb": sc_conv_b[0], "ffn_cb": ffn_conv_b}

    loss_part, grad_x, dS = _fwd_bwd(x[0], loss_target[0], S, ex)

    small_names = ["mix_g", "ffn_g", "fin_g", "sc_cb", "ffn_cb", "sc_cw", "ffn_cw"]
    small_parts = [dS[k] for k in small_names] + [loss_part.reshape(1)]
    small_mine = _pack(small_parts)
    small_all, = _run_stage("gather_small", _ag_stage([small_mine], [0]))
    small_sum = _sum_slots("sum_small", small_all.reshape((N_DEV,) + small_mine.shape))
    g_mix, g_ffn, g_fin, g_scb, g_fcb, g_scw, g_fcw, loss = _unpack(small_sum, [p.shape for p in small_parts])
    g_scw = lax.dynamic_slice_in_dim(g_scw, me * sc_conv_w.shape[-1], sc_conv_w.shape[-1], axis=-1)[None]
    g_fcw = lax.dynamic_slice_in_dim(g_fcw, me * ffn_conv_w.shape[-1], ffn_conv_w.shape[-1], axis=-1)
    g_scb = g_scb[None]
    small_g = [g_mix, g_ffn, g_fin, g_scw, g_scb, g_fcw, g_fcb]
    small_w = [mix_norm_g, ffn_norm_g, final_norm_g, sc_conv_w, sc_conv_b, ffn_conv_w, ffn_conv_b]
    small_m = [m_mix_norm_g, m_ffn_norm_g, m_final_norm_g, m_sc_conv_w, m_sc_conv_b, m_ffn_conv_w, m_ffn_conv_b]
    small_v = [v_mix_norm_g, v_ffn_norm_g, v_final_norm_g, v_sc_conv_w, v_sc_conv_b, v_ffn_conv_w, v_ffn_conv_b]
    small_out = _adamw("adamw_small", _pack(small_g)[None], _pack(small_w)[None], _pack(small_m)[None],
                       _pack(small_v)[None], 0, None)
    small_shapes = [w.shape for w in small_w]
    sg, sd, sm, sv = [_unpack(o[0], small_shapes) for o in small_out]

    moments = {"in": (m_sc_w_in, v_sc_w_in), "sco": (m_sc_w_out, v_sc_w_out), "qkv": (m_attn_w_qkv, v_attn_w_qkv),
               "ao": (m_attn_w_out, v_attn_w_out), "up": (m_ffn_w_up, v_ffn_w_up), "dn": (m_ffn_w_down, v_ffn_w_down)}
    upd = {}
    for nm, w, l in big:
        key = nm.rstrip("0123456789")
        upd[key] = _adamw(f"adamw_{nm}", ex.sums[nm], w, moments[key][0], moments[key][1], l, upd.get(key))

    def leaves(k):
        return [sg, sd, sm, sv][k][0:3] + [upd["in"][k], [sg, sd, sm, sv][k][3], [sg, sd, sm, sv][k][4], upd["sco"][k],
                                          upd["qkv"][k], upd["ao"][k], upd["up"][k], [sg, sd, sm, sv][k][5],
                                          [sg, sd, sm, sv][k][6], upd["dn"][k]]

    return (loss.reshape(()), grad_x[None], *leaves(0), *leaves(1), *leaves(2), *leaves(3))
```
